```python
import math
import jax
import jax.numpy as jnp
from jax import lax
import numpy as np

D_MODEL = 4096
BATCH = 2
SEQ = 4096
DEPTH = 2

GRID_W = 64
CTX_LEN = 256
MIX_W = D_MODEL // 4
MIX_TOTAL = 4 * MIX_W
CHUNK = 64
CONV_W = 4
CONV_PAD_L = 2
CONV_PAD_R = 1
EPS = 1e-6

S5_CH = 16
S5_G = MIX_W // S5_CH
S5_P = 64

DN_HD = 128
DN_H = MIX_W // DN_HD

ML_H = 4
ML_DV = MIX_W // ML_H
ML_DQK = ML_DV // 2

LRU_BLOCKS = 8
LRU_BD = MIX_W // LRU_BLOCKS
LRU_C = 8.0

N_EXPERTS = 16
N_GROUPS = 4
E_PER_G = N_EXPERTS // N_GROUPS
TOP_K = 2
D_FF = D_MODEL // 4

PROJ_SIZES = (MIX_W, MIX_W, MIX_W, MIX_W, MIX_W, 2 * DN_H, 2 * DN_H, ML_H * ML_DQK, ML_H * ML_DQK, MIX_W, MIX_W, 2 * ML_H, 2 * ML_H, MIX_W, MIX_W)
PROJ_W = sum(PROJ_SIZES)

kernel_name = 'hybrid_parallel_group_dit_block'


def rmsnorm(x, g):
    xf = x.astype(jnp.float32)
    y = xf * lax.rsqrt(jnp.mean(xf * xf, axis=-1, keepdims=True) + EPS)
    return (y * g.astype(jnp.float32)).astype(x.dtype)


def l2norm(t):
    return t * lax.rsqrt(jnp.sum(t * t, axis=-1, keepdims=True) + EPS)


def split_proj(p):
    idx = [int(i) for i in np.cumsum(PROJ_SIZES)[:-1]]
    return jnp.split(p, idx, axis=-1)


def to_col_major(t):
    b, l = t.shape[:2]
    rows = l // GRID_W
    return jnp.swapaxes(t.reshape(b, rows, GRID_W, -1), 1, 2).reshape(t.shape)


def to_row_major(t):
    b, l = t.shape[:2]
    rows = l // GRID_W
    return jnp.swapaxes(t.reshape(b, GRID_W, rows, -1), 1, 2).reshape(t.shape)


def centred_dwconv(x, w):
    l = x.shape[1]
    xp = jnp.pad(x, ((0, 0), (CONV_PAD_L, CONV_PAD_R), (0, 0)))
    out = xp[:, 0:l] * w[0]
    for j in range(1, CONV_W):
        out = out + xp[:, j:j + l] * w[j]
    return out


def to_chunks(t):
    b, l = t.shape[:2]
    t = t.reshape(b, l // CHUNK, CHUNK, *t.shape[2:])
    return jnp.moveaxis(t, (1, 3), (0, 2))


def from_chunks(t):
    t = jnp.moveaxis(t, (0, 2), (1, 3))
    return t.reshape(t.shape[0], -1, *t.shape[3:])


def _lin_combine(e1, e2):
    a1, b1 = e1
    a2, b2 = e2
    return a1 * a2, a2 * b1 + b2


def linear_scan(a, b, h0):
    acum, bcum = lax.associative_scan(_lin_combine, (a, b), axis=1)
    h = bcum + acum * h0[:, None]
    return h, h[:, -1]


def _clin_combine(e1, e2):
    ar1, ai1, br1, bi1 = e1
    ar2, ai2, br2, bi2 = e2
    ar = ar2 * ar1 - ai2 * ai1
    ai = ar2 * ai1 + ai2 * ar1
    br = ar2 * br1 - ai2 * bi1 + br2
    bi = ar2 * bi1 + ai2 * br1 + bi2
    return ar, ai, br, bi


def complex_linear_scan(ar, ai, br, bi, h0r, h0i):
    acr, aci, bcr, bci = lax.associative_scan(_clin_combine, (ar, ai, br, bi), axis=1)
    hr = bcr + acr * h0r[:, None] - aci * h0i[:, None]
    hi = bci + acr * h0i[:, None] + aci * h0r[:, None]
    return hr, hi, hr[:, -1], hi[:, -1]


def run_bidir(step, p_f, p_b, ctx_f, lat_f, ctx_b, lat_b, state0):
    rev = lambda ts: tuple(jnp.flip(t, 1) for t in ts)
    oc_f, s_f = step(p_f, ctx_f, state0)
    ol_f, _ = step(p_f, lat_f, s_f)
    oc_b, s_b = step(p_b, rev(ctx_b), state0)
    ol_b, _ = step(p_b, rev(lat_b), s_b)
    return oc_f + jnp.flip(oc_b, 1), ol_f + jnp.flip(ol_b, 1)


def s5_mixer(u_c, u_l, lam_re, lam_im, log_step, b_re, b_im, c_re, c_im, d_skip, w_glu, b_glu):
    def disc(d):
        dt = jnp.exp(log_step[d])[:, None]
        lr, li = lam_re[d], lam_im[d]
        er = jnp.exp(lr * dt)
        abr, abi = er * jnp.cos(li * dt), er * jnp.sin(li * dt)
        nr, ni = abr - 1.0, abi
        den = lr * lr + li * li
        fr = (nr * lr + ni * li) / den
        fi = (ni * lr - nr * li) / den
        bbr = fr[..., None] * b_re[d] - fi[..., None] * b_im[d]
        bbi = fr[..., None] * b_im[d] + fi[..., None] * b_re[d]
        return (abr, abi, bbr, bbi, c_re[d], c_im[d])

    def step(p, seq, h0):
        abr, abi, bbr, bbi, cr, ci = p
        u = seq[0]
        br = jnp.einsum('blgh,gph->blgp', u, bbr)
        bi = jnp.einsum('blgh,gph->blgp', u, bbi)
        hr, hi, fr, fi = complex_linear_scan(jnp.broadcast_to(abr, br.shape), jnp.broadcast_to(abi, bi.shape), br, bi, h0[0], h0[1])
        y = jnp.einsum('blgp,ghp->blgh', hr, cr) - jnp.einsum('blgp,ghp->blgh', hi, ci)
        return y, (fr, fi)

    grp = lambda u: u.reshape(u.shape[0], u.shape[1], S5_G, S5_CH)
    gc, gl = grp(u_c), grp(u_l)
    z = jnp.zeros((u_c.shape[0], S5_G, S5_P), jnp.float32)
    yc, yl = run_bidir(step, disc(0), disc(1), (gc,), (gl,), (gc,), (gl,), (z, z))

    def out(y, u):
        y = y.reshape(u.shape) + d_skip * u
        g = jax.nn.gelu(y)
        return g * jax.nn.sigmoid(g @ w_glu + b_glu)
    return out(yc, u_c), out(yl, u_l)


def gated_delta_chunked(q, k, v, g, beta, s0):
    dk = q.shape[-1]
    q, k, v = to_chunks(q) * dk ** -0.5, to_chunks(k), to_chunks(v)
    g, beta = to_chunks(g), to_chunks(beta)
    dv = v.shape[-1]
    gc = jnp.cumsum(g, axis=-1)
    tril = jnp.tril(jnp.ones((CHUNK, CHUNK), bool))
    strict = jnp.tril(jnp.ones((CHUNK, CHUNK), bool), -1)
    decay = jnp.exp(jnp.where(tril, gc[..., :, None] - gc[..., None, :], -jnp.inf))
    kb = k * beta[..., None]
    lower = jnp.where(strict, jnp.einsum('nbhid,nbhjd->nbhij', kb, k) * decay, 0.0)
    eye = jnp.eye(CHUNK, dtype=lower.dtype)
    rhs = jnp.concatenate([v * beta[..., None], kb * jnp.exp(gc)[..., None]], axis=-1)
    sol = lax.linalg.triangular_solve(lower + eye, rhs, left_side=True, lower=True, unit_diagonal=True)
    u, w = sol[..., :dv], sol[..., dv:]
    attn = jnp.einsum('nbhid,nbhjd->nbhij', q, k) * decay
    qg = q * jnp.exp(gc)[..., None]
    kd = k * jnp.exp(gc[..., -1:] - gc)[..., None]
    g_last = jnp.exp(gc[..., -1])

    def body(s, xs):
        u_i, w_i, a_i, qg_i, kd_i, gl_i = xs
        v_new = u_i - jnp.einsum('bhck,bhkv->bhcv', w_i, s)
        o = jnp.einsum('bhck,bhkv->bhcv', qg_i, s) + jnp.einsum('bhij,bhjv->bhiv', a_i, v_new)
        s = s * gl_i[..., None, None] + jnp.einsum('bhck,bhcv->bhkv', kd_i, v_new)
        return s, o
    s_fin, o = lax.scan(body, s0, (u, w, attn, qg, kd, g_last))
    return from_chunks(o), s_fin


def deltanet_mixer(seg_c, seg_l, conv_w, a_log, dt_bias, norm_w):
    def prep(q, k, v, z, a, b):
        bs, l = q.shape[:2]
        qkv = jax.nn.silu(centred_dwconv(jnp.concatenate([q, k, v], axis=-1), conv_w))
        q, k, v = jnp.split(qkv, 3, axis=-1)
        q = l2norm(q.reshape(bs, l, DN_H, DN_HD))
        k = l2norm(k.reshape(bs, l, DN_H, DN_HD))
        v = v.reshape(bs, l, DN_H, DN_HD)
        g = -jnp.exp(a_log) * jax.nn.softplus(a.reshape(bs, l, 2, DN_H) + dt_bias)
        beta = jax.nn.sigmoid(b.reshape(bs, l, 2, DN_H))
        return (q, k, v, g[:, :, 0], beta[:, :, 0]), (q, k, v, g[:, :, 1], beta[:, :, 1])

    cf, cb = prep(*seg_c)
    lf, lb = prep(*seg_l)
    s0 = jnp.zeros((seg_c[0].shape[0], DN_H, DN_HD, DN_HD), jnp.float32)
    step = lambda p, seq, s: gated_delta_chunked(*seq, s)
    oc, ol = run_bidir(step, None, None, cf, lf, cb, lb, s0)

    def out(o, z):
        bs, l = z.shape[:2]
        o = rmsnorm(o, norm_w) * jax.nn.silu(z.reshape(bs, l, DN_H, DN_HD))
        return o.reshape(bs, l, MIX_W)
    return out(oc, seg_c[3]), out(ol, seg_l[3])


def mlstm_chunked(q, k, v, ig, lf, state0):
    dqk = q.shape[-1]
    q, k, v = to_chunks(q) * dqk ** -0.5, to_chunks(k), to_chunks(v)
    ig, lf = to_chunks(ig), to_chunks(lf)
    b = jnp.cumsum(lf, axis=-1)
    tril = jnp.tril(jnp.ones((CHUNK, CHUNK), bool))
    dmat = jnp.where(tril, b[..., :, None] - b[..., None, :] + ig[..., None, :], -jnp.inf)
    dmax = jnp.max(dmat, axis=-1)
    sqk = jnp.einsum('nbhid,nbhjd->nbhij', q, k)
    b_last = b[..., -1]
    logw = b_last[..., None] - b + ig
    logw_max = jnp.max(logw, axis=-1)

    def body(carry, xs):
        c_s, n_s, m = carry
        q_i, k_i, v_i, b_i, d_i, dm_i, s_i, lw_i, lwm_i, bl_i = xs
        inter = b_i + m[..., None]
        m_t = jnp.maximum(inter, dm_i)
        w_inter = jnp.exp(inter - m_t)
        wmat = jnp.exp(d_i - m_t[..., None]) * s_i
        num = w_inter[..., None] * jnp.einsum('bhck,bhkv->bhcv', q_i, c_s) + jnp.einsum('bhij,bhjv->bhiv', wmat, v_i)
        den = w_inter * jnp.einsum('bhck,bhk->bhc', q_i, n_s) + jnp.sum(wmat, axis=-1)
        h = num / jnp.maximum(jnp.abs(den), jnp.exp(-m_t))[..., None]
        m_new = jnp.maximum(bl_i + m, lwm_i)
        keep = jnp.exp(bl_i + m - m_new)
        kw = k_i * jnp.exp(lw_i - m_new[..., None])[..., None]
        c_s = keep[..., None, None] * c_s + jnp.einsum('bhck,bhcv->bhkv', kw, v_i)
        n_s = keep[..., None] * n_s + jnp.sum(kw, axis=2)
        return (c_s, n_s, m_new), h
    state, h = lax.scan(body, state0, (q, k, v, b, dmat, dmax, sqk, logw, logw_max, b_last))
    return from_chunks(h), state


def mlstm_mixer(seg_c, seg_l, i_bias, f_bias, norm_w):
    def prep(q, k, v, o, ip, fp):
        bs, l = q.shape[:2]
        q = q.reshape(bs, l, ML_H, ML_DQK)
        k = k.reshape(bs, l, ML_H, ML_DQK)
        v = v.reshape(bs, l, ML_H, ML_DV)
        ig = ip.reshape(bs, l, 2, ML_H) + i_bias
        lf = jax.nn.log_sigmoid(fp.reshape(bs, l, 2, ML_H) + f_bias)
        return (q, k, v, ig[:, :, 0], lf[:, :, 0]), (q, k, v, ig[:, :, 1], lf[:, :, 1])

    cf, cb = prep(*seg_c)
    lf_, lb = prep(*seg_l)
    bs = seg_c[0].shape[0]
    state0 = (jnp.zeros((bs, ML_H, ML_DQK, ML_DV), jnp.float32), jnp.zeros((bs, ML_H, ML_DQK), jnp.float32), jnp.zeros((bs, ML_H), jnp.float32))
    step = lambda p, seq, s: mlstm_chunked(*seq, s)
    hc, hl = run_bidir(step, None, None, cf, lf_, cb, lb, state0)

    def out(h, o):
        bs_, l = o.shape[:2]
        h = rmsnorm(h, norm_w.reshape(ML_H, ML_DV)).reshape(bs_, l, MIX_W)
        return h * jax.nn.sigmoid(o)
    return out(hc, seg_c[3]), out(hl, seg_l[3])


def blockdiag(x, w):
    bs, l, _ = x.shape
    return jnp.einsum('blni,nij->blnj', x.reshape(bs, l, LRU_BLOCKS, LRU_BD), w).reshape(bs, l, -1)


def lru_mixer(seg_c, seg_l, conv_w, conv_b, w_a, b_a, w_i, b_i, lam):
    def prep(xb):
        xb = centred_dwconv(xb, conv_w) + conv_b
        seqs = []
        for d in range(2):
            r = jax.nn.sigmoid(blockdiag(xb, w_a[d]) + b_a[d])
            i = jax.nn.sigmoid(blockdiag(xb, w_i[d]) + b_i[d])
            log_a = -LRU_C * r * jax.nn.softplus(-lam[d])
            seqs.append((jnp.exp(log_a), jnp.sqrt(-jnp.expm1(2.0 * log_a)) * (i * xb)))
        return seqs

    yc, xc = seg_c
    yl, xl = seg_l
    cf, cb = prep(xc)
    lf, lb = prep(xl)
    h0 = jnp.zeros((xc.shape[0], MIX_W), jnp.float32)
    step = lambda p, seq, h: linear_scan(seq[0], seq[1], h)
    hc, hl = run_bidir(step, None, None, cf, lf, cb, lb, h0)
    return jax.nn.gelu(yc) * hc, jax.nn.gelu(yl) * hl


def hybrid_mixer(pc, px, s5_p, dn_p, ml_p, lru_p):
    out_dtype = px.dtype
    sc = split_proj(pc.astype(jnp.float32))
    sx = split_proj(px.astype(jnp.float32))
    a_c, a_l = s5_mixer(sc[0], to_col_major(sx[0]), *s5_p)
    a_l = to_row_major(a_l)
    b_c, b_l = deltanet_mixer(tuple(sc[1:7]), tuple(sx[1:7]), *dn_p)
    c_c, c_l = mlstm_mixer(tuple(sc[7:13]), tuple(sx[7:13]), *ml_p)
    d_c, d_l = lru_mixer(tuple(sc[13:15]), tuple(to_col_major(t) for t in sx[13:15]), *lru_p)
    d_l = to_row_major(d_l)
    mc = jnp.concatenate([a_c, b_c, c_c, d_c], axis=-1).astype(out_dtype)
    mx = jnp.concatenate([a_l, b_l, c_l, d_l], axis=-1).astype(out_dtype)
    return mc, mx


def moe_ffn(h, router_w, router_bias, w1, w3, w2):
    bs, l, d = h.shape
    t = h.reshape(-1, d)
    scores = jax.nn.sigmoid((t @ router_w).astype(jnp.float32))
    biased = scores + router_bias.astype(jnp.float32)
    gscore = jnp.sum(lax.top_k(biased.reshape(-1, N_GROUPS, E_PER_G), TOP_K)[0], axis=-1)
    gsel = jnp.argmax(gscore, axis=-1)
    in_group = (jnp.arange(N_EXPERTS) // E_PER_G)[None, :] == gsel[:, None]
    _, idx = lax.top_k(jnp.where(in_group, biased, -jnp.inf), TOP_K)
    wsel = jnp.take_along_axis(scores, idx, axis=-1)
    wsel = wsel / jnp.sum(wsel, axis=-1, keepdims=True)
    gates = jnp.sum(jax.nn.one_hot(idx, N_EXPERTS, dtype=jnp.float32) * wsel[..., None], axis=1)
    hg = jnp.einsum('nd,edf->nef', t, w1)
    hu = jnp.einsum('nd,edf->nef', t, w3)
    act = jax.nn.silu(hg) * hu * gates[..., None].astype(hg.dtype)
    return jnp.einsum('nef,efd->nd', act, w2).reshape(bs, l, d)


def setup_inputs(seed: int = 0) -> dict:
    key = jax.random.key(seed)
    f32 = jnp.float32
    kf = lambda i: jax.random.fold_in(key, i)
    nrm = lambda i, shape, scale: jax.random.normal(kf(i), shape, f32) * scale
    unif = lambda i, shape, lo, hi: jax.random.uniform(kf(i), shape, f32, lo, hi)
    d = D_MODEL
    n_idx = jnp.arange(S5_P, dtype=f32)
    dt = jnp.exp(unif(23, (DEPTH, 2, DN_H), math.log(1e-3), math.log(1e-1)))
    a0 = unif(34, (DEPTH, 2, MIX_W), 0.9, 0.999)
    s_lam = a0 ** (1.0 / LRU_C)
    return {
        'x': nrm(0, (BATCH, SEQ, d), 1.0),
        'c': nrm(1, (BATCH, d), 1.0),
        'ctx': nrm(2, (BATCH, CTX_LEN, d), 1.0),
        'c_ctx': nrm(3, (d,), 1.0),
        'w_ada': nrm(4, (DEPTH, d, 6 * d), 0.5 * d ** -0.5),
        'b_ada': nrm(5, (DEPTH, 6 * d), 0.02),
        'norm1': 1.0 + nrm(6, (DEPTH, d), 0.05),
        'norm2': 1.0 + nrm(7, (DEPTH, d), 0.05),
        'norm_f': 1.0 + nrm(8, (d,), 0.05),
        'w_in': nrm(9, (DEPTH, d, PROJ_W), d ** -0.5),
        'w_out': nrm(10, (DEPTH, MIX_TOTAL, d), MIX_TOTAL ** -0.5),
        's5_lam_re': -0.5 + nrm(11, (DEPTH, 2, S5_G, S5_P), 0.01),
        's5_lam_im': math.pi * n_idx + nrm(12, (DEPTH, 2, S5_G, S5_P), 0.01),
        's5_log_step': unif(13, (DEPTH, 2, S5_G), math.log(1e-3), math.log(1e-1)),
        's5_b_re': nrm(14, (DEPTH, 2, S5_G, S5_P, S5_CH), (2 * S5_CH) ** -0.5),
        's5_b_im': nrm(15, (DEPTH, 2, S5_G, S5_P, S5_CH), (2 * S5_CH) ** -0.5),
        's5_c_re': nrm(16, (DEPTH, 2, S5_G, S5_CH, S5_P), S5_P ** -0.5),
        's5_c_im': nrm(17, (DEPTH, 2, S5_G, S5_CH, S5_P), S5_P ** -0.5),
        's5_d': nrm(18, (DEPTH, MIX_W), 0.5),
        's5_w_glu': nrm(19, (DEPTH, MIX_W, MIX_W), MIX_W ** -0.5),
        's5_b_glu': nrm(20, (DEPTH, MIX_W), 0.02),
        'dn_conv': nrm(21, (DEPTH, CONV_W, 3 * MIX_W), CONV_W ** -0.5),
        'dn_a_log': jnp.log(unif(22, (DEPTH, 2, DN_H), 1.0, 16.0)),
        'dn_dt_bias': dt + jnp.log(-jnp.expm1(-dt)),
        'dn_norm': 1.0 + nrm(24, (DEPTH, DN_HD), 0.05),
        'ml_i_bias': nrm(25, (DEPTH, 2, ML_H), 0.1),
        'ml_f_bias': jnp.linspace(3.0, 6.0, ML_H, dtype=f32) + nrm(26, (DEPTH, 2, ML_H), 0.1),
        'ml_norm': 1.0 + nrm(27, (DEPTH, MIX_W), 0.05),
        'lru_conv_w': nrm(28, (DEPTH, CONV_W, MIX_W), CONV_W ** -0.5),
        'lru_conv_b': nrm(29, (DEPTH, MIX_W), 0.02),
        'lru_w_a': nrm(30, (DEPTH, 2, LRU_BLOCKS, LRU_BD, LRU_BD), LRU_BD ** -0.5),
        'lru_b_a': nrm(31, (DEPTH, 2, MIX_W), 0.02),
        'lru_w_i': nrm(32, (DEPTH, 2, LRU_BLOCKS, LRU_BD, LRU_BD), LRU_BD ** -0.5),
        'lru_b_i': nrm(33, (DEPTH, 2, MIX_W), 0.02),
        'lru_lam': jnp.log(s_lam) - jnp.log1p(-s_lam),
        'router_w': nrm(35, (d, N_EXPERTS), d ** -0.5),
        'router_bias': nrm(36, (N_EXPERTS,), 0.01),
        'moe_w1': nrm(37, (DEPTH, N_EXPERTS, d, D_FF), d ** -0.5),
        'moe_w3': nrm(38, (DEPTH, N_EXPERTS, d, D_FF), d ** -0.5),
        'moe_w2': nrm(39, (DEPTH, N_EXPERTS, D_FF, d), D_FF ** -0.5),
    }


def reference(x, c, ctx, c_ctx, w_ada, b_ada, norm1, norm2, norm_f, w_in, w_out,
              s5_lam_re, s5_lam_im, s5_log_step, s5_b_re, s5_b_im, s5_c_re, s5_c_im, s5_d, s5_w_glu, s5_b_glu,
              dn_conv, dn_a_log, dn_dt_bias, dn_norm,
              ml_i_bias, ml_f_bias, ml_norm,
              lru_conv_w, lru_conv_b, lru_w_a, lru_b_a, lru_w_i, lru_b_i, lru_lam,
              router_w, router_bias, moe_w1, moe_w3, moe_w2):
    sc = jax.nn.silu(c)
    scc = jax.nn.silu(c_ctx)
    bs = x.shape[0]
    for l in range(DEPTH):
        last = l == DEPTH - 1
        mod_x = (sc @ w_ada[l] + b_ada[l]).reshape(bs, 6, 1, D_MODEL)
        mod_c = (scc @ w_ada[l] + b_ada[l]).reshape(6, 1, D_MODEL)
        hx = rmsnorm(x, norm1[l]) * (1.0 + mod_x[:, 1]) + mod_x[:, 0]
        hc = rmsnorm(ctx, norm1[l]) * (1.0 + mod_c[1]) + mod_c[0]
        mix_c, mix_x = hybrid_mixer(
            hc @ w_in[l], hx @ w_in[l],
            (s5_lam_re[l], s5_lam_im[l], s5_log_step[l], s5_b_re[l], s5_b_im[l], s5_c_re[l], s5_c_im[l], s5_d[l], s5_w_glu[l], s5_b_glu[l]),
            (dn_conv[l], dn_a_log[l], dn_dt_bias[l], dn_norm[l]),
            (ml_i_bias[l], ml_f_bias[l], ml_norm[l]),
            (lru_conv_w[l], lru_conv_b[l], lru_w_a[l], lru_b_a[l], lru_w_i[l], lru_b_i[l], lru_lam[l]))
        x = x + mod_x[:, 2] * (mix_x @ w_out[l])
        hx = rmsnorm(x, norm2[l]) * (1.0 + mod_x[:, 4]) + mod_x[:, 3]
        x = x + mod_x[:, 5] * moe_ffn(hx, router_w, router_bias, moe_w1[l], moe_w3[l], moe_w2[l])
        if not last:
            ctx = ctx + mod_c[2] * (mix_c @ w_out[l])
            hc = rmsnorm(ctx, norm2[l]) * (1.0 + mod_c[4]) + mod_c[3]
            ctx = ctx + mod_c[5] * moe_ffn(hc, router_w, router_bias, moe_w1[l], moe_w3[l], moe_w2[l])
    return rmsnorm(x, norm_f)
```

```python
import functools
import math

import numpy as np
import jax
import jax.numpy as jnp
from jax import lax
from jax.experimental import pallas as pl
from jax.experimental.pallas import tpu as pltpu

D_MODEL = 4096
BATCH = 2
SEQ = 4096
DEPTH = 2
GRID_W = 64
CTX_LEN = 256
MIX_W = D_MODEL // 4
MIX_TOTAL = 4 * MIX_W
CHUNK = 64
CONV_W = 4
CONV_PAD_L = 2
CONV_PAD_R = 1
EPS = 1e-6
S5_CH = 16
S5_G = MIX_W // S5_CH
S5_P = 64
DN_HD = 128
DN_H = MIX_W // DN_HD
ML_H = 4
ML_DV = MIX_W // ML_H
ML_DQK = ML_DV // 2
LRU_BLOCKS = 8
LRU_BD = MIX_W // LRU_BLOCKS
LRU_C = 8.0
N_EXPERTS = 16
N_GROUPS = 4
E_PER_G = N_EXPERTS // N_GROUPS
TOP_K = 2
D_FF = D_MODEL // 4
PROJ_SIZES = (MIX_W, MIX_W, MIX_W, MIX_W, MIX_W, 2 * DN_H, 2 * DN_H, ML_H * ML_DQK, ML_H * ML_DQK, MIX_W, MIX_W,
              2 * ML_H, 2 * ML_H, MIX_W, MIX_W)
PROJ_W = sum(PROJ_SIZES)

N_LAT = BATCH * SEQ
N_CTX = BATCH * CTX_LEN
VMEM_LIMIT = 56 * 1024 * 1024

F32 = jnp.float32
BF16 = jnp.bfloat16


def _cparams(sem):
    return pltpu.CompilerParams(dimension_semantics=sem, vmem_limit_bytes=VMEM_LIMIT)


def _mm_kernel(a_ref, b_ref, o_ref):
    o_ref[...] = jnp.dot(a_ref[...], b_ref[...], preferred_element_type=F32).astype(o_ref.dtype)


def matmul(a, b, tm, tn, out_dtype=F32):
    m, k = a.shape
    n = b.shape[1]
    assert m % tm == 0 and n % tn == 0
    return pl.pallas_call(
        _mm_kernel,
        grid=(m // tm, n // tn),
        in_specs=[pl.BlockSpec((tm, k), lambda i, j: (i, 0)),
                  pl.BlockSpec((k, tn), lambda i, j: (0, j))],
        out_specs=pl.BlockSpec((tm, tn), lambda i, j: (i, j)),
        out_shape=jax.ShapeDtypeStruct((m, n), out_dtype),
        compiler_params=_cparams(("parallel", "arbitrary")),
        name="mm",
    )(a, b)


def _mm_resid_kernel(a_ref, b_ref, r_ref, g_ref, o_ref):
    acc = jnp.dot(a_ref[...], b_ref[...], preferred_element_type=F32)
    o_ref[...] = r_ref[...] + g_ref[...] * acc


def matmul_gated_residual(a, b, resid, gate, rows_per_gate, tm, tn):
    m, k = a.shape
    n = b.shape[1]
    assert m % tm == 0 and n % tn == 0 and rows_per_gate % tm == 0
    return pl.pallas_call(
        _mm_resid_kernel,
        grid=(m // tm, n // tn),
        in_specs=[pl.BlockSpec((tm, k), lambda i, j: (i, 0)),
                  pl.BlockSpec((k, tn), lambda i, j: (0, j)),
                  pl.BlockSpec((tm, tn), lambda i, j: (i, j)),
                  pl.BlockSpec((None, 1, tn), lambda i, j: ((i * tm) // rows_per_gate, 0, j))],
        out_specs=pl.BlockSpec((tm, tn), lambda i, j: (i, j)),
        out_shape=jax.ShapeDtypeStruct((m, n), F32),
        compiler_params=_cparams(("parallel", "arbitrary")),
        name="mm_resid",
    )(a, b, resid, gate)


def _ada_kernel(a_ref, w_ref, b_ref, o_ref):
    o_ref[...] = jnp.dot(a_ref[...], w_ref[...].astype(BF16), preferred_element_type=F32) + b_ref[...]


def ada_matmul(a, w, bias, tn):
    m, k = a.shape
    n = w.shape[1]
    return pl.pallas_call(
        _ada_kernel,
        grid=(n // tn,),
        in_specs=[pl.BlockSpec((m, k), lambda j: (0, 0)),
                  pl.BlockSpec((k, tn), lambda j: (0, j)),
                  pl.BlockSpec((1, tn), lambda j: (0, j))],
        out_specs=pl.BlockSpec((m, tn), lambda j: (0, j)),
        out_shape=jax.ShapeDtypeStruct((m, n), F32),
        compiler_params=_cparams(("arbitrary",)),
        name="ada",
    )(a, w, bias)


def _modulated_norm(x, g, sc, sh):
    ms = jnp.mean(x * x, axis=-1, keepdims=True)
    y = x * lax.rsqrt(ms + EPS) * g
    return y * (1.0 + sc) + sh


def _norm_mod_kernel(x_ref, g_ref, sc_ref, sh_ref, o_ref):
    o_ref[...] = _modulated_norm(x_ref[...], g_ref[...], sc_ref[...], sh_ref[...]).astype(o_ref.dtype)


def _pair_top2_sum(v):
    a, b, c, d = v
    return jnp.maximum(jnp.maximum(jnp.maximum(a + b, a + c), jnp.maximum(a + d, b + c)),
                       jnp.maximum(b + d, c + d))


def _norm_mod_router_kernel(x_ref, g_ref, sc_ref, sh_ref, rwt_ref, rb_ref, o_ref, idx_ref, w_ref):
    h = _modulated_norm(x_ref[...], g_ref[...], sc_ref[...], sh_ref[...])
    o_ref[...] = h.astype(o_ref.dtype)
    logits = lax.dot_general(rwt_ref[...], h, (((1,), (1,)), ((), ())),
                             precision=lax.Precision.HIGHEST, preferred_element_type=F32)
    scores = jax.nn.sigmoid(logits)
    biased = scores + rb_ref[...]
    s = [scores[e:e + 1, :] for e in range(N_EXPERTS)]
    b = [biased[e:e + 1, :] for e in range(N_EXPERTS)]
    gs = [_pair_top2_sum(b[E_PER_G * g:E_PER_G * (g + 1)]) for g in range(N_GROUPS)]
    best, gsel = gs[0], jnp.zeros_like(gs[0], dtype=jnp.int32)
    for g in range(1, N_GROUPS):
        better = gs[g] > best
        gsel = jnp.where(better, g, gsel)
        best = jnp.where(better, gs[g], best)
    vb, vs = [], []
    for k in range(E_PER_G):
        bk, sk = b[k], s[k]
        for g in range(1, N_GROUPS):
            bk = jnp.where(gsel == g, b[E_PER_G * g + k], bk)
            sk = jnp.where(gsel == g, s[E_PER_G * g + k], sk)
        vb.append(bk)
        vs.append(sk)
    m1, i1, w1 = vb[0], jnp.zeros_like(gsel), vs[0]
    for k in range(1, E_PER_G):
        better = vb[k] > m1
        i1 = jnp.where(better, k, i1)
        w1 = jnp.where(better, vs[k], w1)
        m1 = jnp.where(better, vb[k], m1)
    m2, i2, w2, have = vb[0], jnp.zeros_like(gsel), vs[0], i1 != 0
    for k in range(1, E_PER_G):
        valid = i1 != k
        better = valid & (jnp.logical_not(have) | (vb[k] > m2))
        i2 = jnp.where(better, k, i2)
        w2 = jnp.where(better, vs[k], w2)
        m2 = jnp.where(better, vb[k], m2)
        have = have | valid
    tot = w1 + w2
    idx_ref[0:1, :] = gsel * E_PER_G + i1
    idx_ref[1:2, :] = gsel * E_PER_G + i2
    w_ref[0:1, :] = w1 / tot
    w_ref[1:2, :] = w2 / tot


def norm_mod(x, g, sc, sh, rows_per_mod, tm, router=None):
    m, d = x.shape
    assert m % tm == 0 and rows_per_mod % tm == 0
    mod_spec = pl.BlockSpec((None, 1, d), lambda i: ((i * tm) // rows_per_mod, 0, 0))
    in_specs = [pl.BlockSpec((tm, d), lambda i: (i, 0)), pl.BlockSpec((1, d), lambda i: (0, 0)), mod_spec, mod_spec]
    o_spec = pl.BlockSpec((tm, d), lambda i: (i, 0))
    o_shape = jax.ShapeDtypeStruct((m, d), BF16)
    if router is None:
        return pl.pallas_call(
            _norm_mod_kernel, grid=(m // tm,), in_specs=in_specs, out_specs=o_spec, out_shape=o_shape,
            compiler_params=_cparams(("parallel",)), name="norm_mod",
        )(x, g, sc, sh)
    rwt, rb = router
    in_specs += [pl.BlockSpec((N_EXPERTS, d), lambda i: (0, 0)), pl.BlockSpec((N_EXPERTS, 1), lambda i: (0, 0))]
    sel_spec = pl.BlockSpec((TOP_K, tm), lambda i: (0, i))
    return pl.pallas_call(
        _norm_mod_router_kernel, grid=(m // tm,), in_specs=in_specs,
        out_specs=[o_spec, sel_spec, sel_spec],
        out_shape=[o_shape, jax.ShapeDtypeStruct((TOP_K, m), jnp.int32), jax.ShapeDtypeStruct((TOP_K, m), F32)],
        compiler_params=_cparams(("parallel",)), name="norm_mod_router",
    )(x, g, sc, sh, rwt, rb)


def _final_norm_kernel(x_ref, g_ref, o_ref):
    x = x_ref[...]
    ms = jnp.mean(x * x, axis=-1, keepdims=True)
    o_ref[...] = x * lax.rsqrt(ms + EPS) * g_ref[...]


def final_norm(x, g, tm):
    m, d = x.shape
    return pl.pallas_call(
        _final_norm_kernel, grid=(m // tm,),
        in_specs=[pl.BlockSpec((tm, d), lambda i: (i, 0)), pl.BlockSpec((1, d), lambda i: (0, 0))],
        out_specs=pl.BlockSpec((tm, d), lambda i: (i, 0)),
        out_shape=jax.ShapeDtypeStruct((m, d), F32),
        compiler_params=_cparams(("parallel",)), name="final_norm",
    )(x, g)


MOE_TM = 512
MOE_FC = 256


def _moe_kernel(te_ref, nu_ref, x_ref, w1_ref, w3_ref, w2_ref, g_ref, o_ref):
    t = pl.program_id(0)
    f = pl.program_id(1)
    used = t < nu_ref[0]

    @pl.when(used)
    def _():
        x = x_ref[...]
        hg = jnp.dot(x, w1_ref[...], preferred_element_type=F32)
        hu = jnp.dot(x, w3_ref[...], preferred_element_type=F32)
        act = (hg * jax.nn.sigmoid(hg)) * hu * g_ref[...]
        y = jnp.dot(act.astype(BF16), w2_ref[...], preferred_element_type=F32)

        @pl.when(f == 0)
        def _():
            o_ref[...] = y

        @pl.when(f > 0)
        def _():
            o_ref[...] += y

    @pl.when(jnp.logical_not(used) & (f == 0))
    def _():
        o_ref[...] = jnp.zeros_like(o_ref)


def moe_grouped(tile_expert, n_used, xs, w1, w3, w2, gs):
    r, d = xs.shape
    nf = D_FF // MOE_FC

    def fidx(t, f, nu):
        return jnp.where(t < nu[0], f, nf - 1)

    grid_spec = pltpu.PrefetchScalarGridSpec(
        num_scalar_prefetch=2,
        grid=(r // MOE_TM, nf),
        in_specs=[pl.BlockSpec((MOE_TM, d), lambda t, f, te, nu: (t, 0)),
                  pl.BlockSpec((None, d, MOE_FC), lambda t, f, te, nu: (te[t], 0, fidx(t, f, nu))),
                  pl.BlockSpec((None, d, MOE_FC), lambda t, f, te, nu: (te[t], 0, fidx(t, f, nu))),
                  pl.BlockSpec((None, MOE_FC, d), lambda t, f, te, nu: (te[t], fidx(t, f, nu), 0)),
                  pl.BlockSpec((MOE_TM, 1), lambda t, f, te, nu: (t, 0))],
        out_specs=pl.BlockSpec((MOE_TM, d), lambda t, f, te, nu: (t, 0)),
    )
    return pl.pallas_call(
        _moe_kernel, grid_spec=grid_spec,
        out_shape=jax.ShapeDtypeStruct((r, d), F32),
        compiler_params=_cparams(("arbitrary", "arbitrary")), name="moe",
    )(tile_expert, n_used, xs, w1, w3, w2, gs)


def moe_dispatch(idx, wsel):
    n = idx.shape[1]
    e_flat = idx.reshape(-1)
    tok = jnp.tile(jnp.arange(n, dtype=jnp.int32), TOP_K)
    onehot = (e_flat[:, None] == jnp.arange(N_EXPERTS, dtype=jnp.int32)[None, :]).astype(jnp.int32)
    csum = jnp.cumsum(onehot, axis=0)
    counts = csum[-1]
    rank = jnp.take_along_axis(csum, e_flat[:, None], axis=1)[:, 0] - 1
    padded = ((counts + MOE_TM - 1) // MOE_TM) * MOE_TM
    seg_end = jnp.cumsum(padded)
    seg_start = seg_end - padded
    dest = seg_start[e_flat] + rank
    rows = TOP_K * n + N_EXPERTS * MOE_TM
    src_tok = jnp.zeros((rows,), jnp.int32).at[dest].set(tok)
    gate = jnp.zeros((rows,), F32).at[dest].set(wsel.reshape(-1))
    n_tiles = rows // MOE_TM
    n_used = (seg_end[-1] // MOE_TM).astype(jnp.int32)
    tile_start = jnp.arange(n_tiles, dtype=jnp.int32) * MOE_TM
    te = jnp.minimum(jnp.searchsorted(seg_end, tile_start, side='right'), N_EXPERTS - 1).astype(jnp.int32)
    te = jnp.where(jnp.arange(n_tiles) < n_used, te, te[jnp.maximum(n_used - 1, 0)])
    return dest.reshape(TOP_K, n), src_tok, gate[:, None], te, n_used.reshape(1)


def moe_ffn(hb, idx, wsel, w1, w3, w2):
    dest, src_tok, gate, te, n_used = moe_dispatch(idx, wsel)
    xs = jnp.take(hb, src_tok, axis=0)
    ys = moe_grouped(te, n_used, xs, w1, w3, w2, gate)
    return jnp.take(ys, dest[0], axis=0) + jnp.take(ys, dest[1], axis=0)


def rmsnorm(x, g):
    xf = x.astype(F32)
    y = xf * lax.rsqrt(jnp.mean(xf * xf, axis=-1, keepdims=True) + EPS)
    return (y * g.astype(F32)).astype(x.dtype)


def l2norm(t):
    return t * lax.rsqrt(jnp.sum(t * t, axis=-1, keepdims=True) + EPS)


def split_proj(p):
    idx = [int(i) for i in np.cumsum(PROJ_SIZES)[:-1]]
    return jnp.split(p, idx, axis=-1)


def to_col_major(t):
    b, l = t.shape[:2]
    rows = l // GRID_W
    return jnp.swapaxes(t.reshape(b, rows, GRID_W, -1), 1, 2).reshape(t.shape)


def to_row_major(t):
    b, l = t.shape[:2]
    rows = l // GRID_W
    return jnp.swapaxes(t.reshape(b, GRID_W, rows, -1), 1, 2).reshape(t.shape)


def centred_dwconv(x, w):
    l = x.shape[1]
    xp = jnp.pad(x, ((0, 0), (CONV_PAD_L, CONV_PAD_R), (0, 0)))
    out = xp[:, 0:l] * w[0]
    for j in range(1, CONV_W):
        out = out + xp[:, j:j + l] * w[j]
    return out


def to_chunks(t):
    b, l = t.shape[:2]
    t = t.reshape(b, l // CHUNK, CHUNK, *t.shape[2:])
    return jnp.moveaxis(t, (1, 3), (0, 2))


def from_chunks(t):
    t = jnp.moveaxis(t, (0, 2), (1, 3))
    return t.reshape(t.shape[0], -1, *t.shape[3:])


def _lin_combine(e1, e2):
    a1, b1 = e1
    a2, b2 = e2
    return a1 * a2, a2 * b1 + b2


def linear_scan(a, b, h0):
    acum, bcum = lax.associative_scan(_lin_combine, (a, b), axis=1)
    h = bcum + acum * h0[:, None]
    return h, h[:, -1]


def _clin_combine(e1, e2):
    ar1, ai1, br1, bi1 = e1
    ar2, ai2, br2, bi2 = e2
    ar = ar2 * ar1 - ai2 * ai1
    ai = ar2 * ai1 + ai2 * ar1
    br = ar2 * br1 - ai2 * bi1 + br2
    bi = ar2 * bi1 + ai2 * br1 + bi2
    return ar, ai, br, bi


def complex_linear_scan(ar, ai, br, bi, h0r, h0i):
    acr, aci, bcr, bci = lax.associative_scan(_clin_combine, (ar, ai, br, bi), axis=1)
    hr = bcr + acr * h0r[:, None] - aci * h0i[:, None]
    hi = bci + acr * h0i[:, None] + aci * h0r[:, None]
    return hr, hi, hr[:, -1], hi[:, -1]


def run_bidir(step, p_f, p_b, ctx_f, lat_f, ctx_b, lat_b, state0):
    rev = lambda ts: tuple(jnp.flip(t, 1) for t in ts)
    oc_f, s_f = step(p_f, ctx_f, state0)
    ol_f, _ = step(p_f, lat_f, s_f)
    oc_b, s_b = step(p_b, rev(ctx_b), state0)
    ol_b, _ = step(p_b, rev(lat_b), s_b)
    return oc_f + jnp.flip(oc_b, 1), ol_f + jnp.flip(ol_b, 1)


def s5_mixer(u_c, u_l, lam_re, lam_im, log_step, b_re, b_im, c_re, c_im, d_skip, w_glu, b_glu):
    def disc(d):
        dt = jnp.exp(log_step[d])[:, None]
        lr, li = lam_re[d], lam_im[d]
        er = jnp.exp(lr * dt)
        abr, abi = er * jnp.cos(li * dt), er * jnp.sin(li * dt)
        nr, ni = abr - 1.0, abi
        den = lr * lr + li * li
        fr = (nr * lr + ni * li) / den
        fi = (ni * lr - nr * li) / den
        bbr = fr[..., None] * b_re[d] - fi[..., None] * b_im[d]
        bbi = fr[..., None] * b_im[d] + fi[..., None] * b_re[d]
        return (abr, abi, bbr, bbi, c_re[d], c_im[d])

    def step(p, seq, h0):
        abr, abi, bbr, bbi, cr, ci = p
        u = seq[0]
        br = jnp.einsum('blgh,gph->blgp', u, bbr)
        bi = jnp.einsum('blgh,gph->blgp', u, bbi)
        hr, hi, fr, fi = complex_linear_scan(jnp.broadcast_to(abr, br.shape), jnp.broadcast_to(abi, bi.shape),
                                             br, bi, h0[0], h0[1])
        y = jnp.einsum('blgp,ghp->blgh', hr, cr) - jnp.einsum('blgp,ghp->blgh', hi, ci)
        return y, (fr, fi)

    grp = lambda u: u.reshape(u.shape[0], u.shape[1], S5_G, S5_CH)
    gc, gl = grp(u_c), grp(u_l)
    z = jnp.zeros((u_c.shape[0], S5_G, S5_P), F32)
    yc, yl = run_bidir(step, disc(0), disc(1), (gc,), (gl,), (gc,), (gl,), (z, z))

    def out(y, u):
        y = y.reshape(u.shape) + d_skip * u
        g = jax.nn.gelu(y)
        return g * jax.nn.sigmoid(g @ w_glu + b_glu)
    return out(yc, u_c), out(yl, u_l)


def gated_delta_chunked(q, k, v, g, beta, s0):
    dk = q.shape[-1]
    q, k, v = to_chunks(q) * dk ** -0.5, to_chunks(k), to_chunks(v)
    g, beta = to_chunks(g), to_chunks(beta)
    dv = v.shape[-1]
    gc = jnp.cumsum(g, axis=-1)
    tril = jnp.tril(jnp.ones((CHUNK, CHUNK), bool))
    strict = jnp.tril(jnp.ones((CHUNK, CHUNK), bool), -1)
    decay = jnp.exp(jnp.where(tril, gc[..., :, None] - gc[..., None, :], -jnp.inf))
    kb = k * beta[..., None]
    lower = jnp.where(strict, jnp.einsum('nbhid,nbhjd->nbhij', kb, k) * decay, 0.0)
    eye = jnp.eye(CHUNK, dtype=lower.dtype)
    rhs = jnp.concatenate([v * beta[..., None], kb * jnp.exp(gc)[..., None]], axis=-1)
    sol = lax.linalg.triangular_solve(lower + eye, rhs, left_side=True, lower=True, unit_diagonal=True)
    u, w = sol[..., :dv], sol[..., dv:]
    attn = jnp.einsum('nbhid,nbhjd->nbhij', q, k) * decay
    qg = q * jnp.exp(gc)[..., None]
    kd = k * jnp.exp(gc[..., -1:] - gc)[..., None]
    g_last = jnp.exp(gc[..., -1])

    def body(s, xs):
        u_i, w_i, a_i, qg_i, kd_i, gl_i = xs
        v_new = u_i - jnp.einsum('bhck,bhkv->bhcv', w_i, s)
        o = jnp.einsum('bhck,bhkv->bhcv', qg_i, s) + jnp.einsum('bhij,bhjv->bhiv', a_i, v_new)
        s = s * gl_i[..., None, None] + jnp.einsum('bhck,bhcv->bhkv', kd_i, v_new)
        return s, o
    s_fin, o = lax.scan(body, s0, (u, w, attn, qg, kd, g_last))
    return from_chunks(o), s_fin


def deltanet_mixer(seg_c, seg_l, conv_w, a_log, dt_bias, norm_w):
    def prep(q, k, v, z, a, b):
        bs, l = q.shape[:2]
        qkv = jax.nn.silu(centred_dwconv(jnp.concatenate([q, k, v], axis=-1), conv_w))
        q, k, v = jnp.split(qkv, 3, axis=-1)
        q = l2norm(q.reshape(bs, l, DN_H, DN_HD))
        k = l2norm(k.reshape(bs, l, DN_H, DN_HD))
        v = v.reshape(bs, l, DN_H, DN_HD)
        g = -jnp.exp(a_log) * jax.nn.softplus(a.reshape(bs, l, 2, DN_H) + dt_bias)
        beta = jax.nn.sigmoid(b.reshape(bs, l, 2, DN_H))
        return (q, k, v, g[:, :, 0], beta[:, :, 0]), (q, k, v, g[:, :, 1], beta[:, :, 1])

    cf, cb = prep(*seg_c)
    lf, lb = prep(*seg_l)
    s0 = jnp.zeros((seg_c[0].shape[0], DN_H, DN_HD, DN_HD), F32)
    step = lambda p, seq, s: gated_delta_chunked(*seq, s)
    oc, ol = run_bidir(step, None, None, cf, lf, cb, lb, s0)

    def out(o, z):
        bs, l = z.shape[:2]
        o = rmsnorm(o, norm_w) * jax.nn.silu(z.reshape(bs, l, DN_H, DN_HD))
        return o.reshape(bs, l, MIX_W)
    return out(oc, seg_c[3]), out(ol, seg_l[3])


def mlstm_chunked(q, k, v, ig, lf, state0):
    dqk = q.shape[-1]
    q, k, v = to_chunks(q) * dqk ** -0.5, to_chunks(k), to_chunks(v)
    ig, lf = to_chunks(ig), to_chunks(lf)
    b = jnp.cumsum(lf, axis=-1)
    tril = jnp.tril(jnp.ones((CHUNK, CHUNK), bool))
    dmat = jnp.where(tril, b[..., :, None] - b[..., None, :] + ig[..., None, :], -jnp.inf)
    dmax = jnp.max(dmat, axis=-1)
    sqk = jnp.einsum('nbhid,nbhjd->nbhij', q, k)
    b_last = b[..., -1]
    logw = b_last[..., None] - b + ig
    logw_max = jnp.max(logw, axis=-1)

    def body(carry, xs):
        c_s, n_s, m = carry
        q_i, k_i, v_i, b_i, d_i, dm_i, s_i, lw_i, lwm_i, bl_i = xs
        inter = b_i + m[..., None]
        m_t = jnp.maximum(inter, dm_i)
        w_inter = jnp.exp(inter - m_t)
        wmat = jnp.exp(d_i - m_t[..., None]) * s_i
        num = w_inter[..., None] * jnp.einsum('bhck,bhkv->bhcv', q_i, c_s) + jnp.einsum('bhij,bhjv->bhiv', wmat, v_i)
        den = w_inter * jnp.einsum('bhck,bhk->bhc', q_i, n_s) + jnp.sum(wmat, axis=-1)
        h = num / jnp.maximum(jnp.abs(den), jnp.exp(-m_t))[..., None]
        m_new = jnp.maximum(bl_i + m, lwm_i)
        keep = jnp.exp(bl_i + m - m_new)
        kw = k_i * jnp.exp(lw_i - m_new[..., None])[..., None]
        c_s = keep[..., None, None] * c_s + jnp.einsum('bhck,bhcv->bhkv', kw, v_i)
        n_s = keep[..., None] * n_s + jnp.sum(kw, axis=2)
        return (c_s, n_s, m_new), h
    state, h = lax.scan(body, state0, (q, k, v, b, dmat, dmax, sqk, logw, logw_max, b_last))
    return from_chunks(h), state


def mlstm_mixer(seg_c, seg_l, i_bias, f_bias, norm_w):
    def prep(q, k, v, o, ip, fp):
        bs, l = q.shape[:2]
        q = q.reshape(bs, l, ML_H, ML_DQK)
        k = k.reshape(bs, l, ML_H, ML_DQK)
        v = v.reshape(bs, l, ML_H, ML_DV)
        ig = ip.reshape(bs, l, 2, ML_H) + i_bias
        lf = jax.nn.log_sigmoid(fp.reshape(bs, l, 2, ML_H) + f_bias)
        return (q, k, v, ig[:, :, 0], lf[:, :, 0]), (q, k, v, ig[:, :, 1], lf[:, :, 1])

    cf, cb = prep(*seg_c)
    lf_, lb = prep(*seg_l)
    bs = seg_c[0].shape[0]
    state0 = (jnp.zeros((bs, ML_H, ML_DQK, ML_DV), F32), jnp.zeros((bs, ML_H, ML_DQK), F32),
              jnp.zeros((bs, ML_H), F32))
    step = lambda p, seq, s: mlstm_chunked(*seq, s)
    hc, hl = run_bidir(step, None, None, cf, lf_, cb, lb, state0)

    def out(h, o):
        bs_, l = o.shape[:2]
        h = rmsnorm(h, norm_w.reshape(ML_H, ML_DV)).reshape(bs_, l, MIX_W)
        return h * jax.nn.sigmoid(o)
    return out(hc, seg_c[3]), out(hl, seg_l[3])


def blockdiag(x, w):
    bs, l, _ = x.shape
    return jnp.einsum('blni,nij->blnj', x.reshape(bs, l, LRU_BLOCKS, LRU_BD), w).reshape(bs, l, -1)


def lru_mixer(seg_c, seg_l, conv_w, conv_b, w_a, b_a, w_i, b_i, lam):
    def prep(xb):
        xb = centred_dwconv(xb, conv_w) + conv_b
        seqs = []
        for d in range(2):
            r = jax.nn.sigmoid(blockdiag(xb, w_a[d]) + b_a[d])
            i = jax.nn.sigmoid(blockdiag(xb, w_i[d]) + b_i[d])
            log_a = -LRU_C * r * jax.nn.softplus(-lam[d])
            seqs.append((jnp.exp(log_a), jnp.sqrt(-jnp.expm1(2.0 * log_a)) * (i * xb)))
        return seqs

    yc, xc = seg_c
    yl, xl = seg_l
    cf, cb = prep(xc)
    lf, lb = prep(xl)
    h0 = jnp.zeros((xc.shape[0], MIX_W), F32)
    step = lambda p, seq, h: linear_scan(seq[0], seq[1], h)
    hc, hl = run_bidir(step, None, None, cf, lf, cb, lb, h0)
    return jax.nn.gelu(yc) * hc, jax.nn.gelu(yl) * hl


def hybrid_mixer(pc, px, s5_p, dn_p, ml_p, lru_p):
    sc = split_proj(pc)
    sx = split_proj(px)
    a_c, a_l = s5_mixer(sc[0], to_col_major(sx[0]), *s5_p)
    a_l = to_row_major(a_l)
    b_c, b_l = deltanet_mixer(tuple(sc[1:7]), tuple(sx[1:7]), *dn_p)
    c_c, c_l = mlstm_mixer(tuple(sc[7:13]), tuple(sx[7:13]), *ml_p)
    d_c, d_l = lru_mixer(tuple(sc[13:15]), tuple(to_col_major(t) for t in sx[13:15]), *lru_p)
    d_l = to_row_major(d_l)
    mc = jnp.concatenate([a_c, b_c, c_c, d_c], axis=-1)
    mx = jnp.concatenate([a_l, b_l, c_l, d_l], axis=-1)
    return mc, mx


PROJ_TN = 512
PROJ_PAD = -(-PROJ_W // PROJ_TN) * PROJ_TN


def kernel(x, c, ctx, c_ctx, w_ada, b_ada, norm1, norm2, norm_f, w_in, w_out, s5_lam_re, s5_lam_im, s5_log_step,
           s5_b_re, s5_b_im, s5_c_re, s5_c_im, s5_d, s5_w_glu, s5_b_glu, dn_conv, dn_a_log, dn_dt_bias, dn_norm,
           ml_i_bias, ml_f_bias, ml_norm, lru_conv_w, lru_conv_b, lru_w_a, lru_b_a, lru_w_i, lru_b_i, lru_lam,
           router_w, router_bias, moe_w1, moe_w3, moe_w2):
    d = D_MODEL
    xa = jnp.concatenate([x.reshape(N_LAT, d), ctx.reshape(N_CTX, d)], axis=0)
    cond = jnp.concatenate([jax.nn.silu(c), jax.nn.silu(c_ctx)[None], jnp.zeros((16 - BATCH - 1, d), F32)], axis=0)
    cond = cond.astype(BF16)
    router = (router_w.T, router_bias.reshape(N_EXPERTS, 1))

    for l in range(DEPTH):
        last = l == DEPTH - 1
        mod = ada_matmul(cond, w_ada[l], b_ada[l][None], 512)[:BATCH + 1].reshape(BATCH + 1, 6, 1, d)
        shift1, scale1, gate1, shift2, scale2, gate2 = (mod[:, k] for k in range(6))

        h1 = norm_mod(xa, norm1[l][None], scale1, shift1, SEQ, 256)
        w_in_b = jnp.pad(w_in[l].astype(BF16), ((0, 0), (0, PROJ_PAD - PROJ_W)))
        proj = matmul(h1, w_in_b, 512, PROJ_TN)[:, :PROJ_W]
        px = proj[:N_LAT].reshape(BATCH, SEQ, PROJ_W)
        pc = proj[N_LAT:].reshape(BATCH, CTX_LEN, PROJ_W)
        mc, mx = hybrid_mixer(
            pc, px,
            (s5_lam_re[l], s5_lam_im[l], s5_log_step[l], s5_b_re[l], s5_b_im[l], s5_c_re[l], s5_c_im[l], s5_d[l],
             s5_w_glu[l], s5_b_glu[l]),
            (dn_conv[l], dn_a_log[l], dn_dt_bias[l], dn_norm[l]),
            (ml_i_bias[l], ml_f_bias[l], ml_norm[l]),
            (lru_conv_w[l], lru_conv_b[l], lru_w_a[l], lru_b_a[l], lru_w_i[l], lru_b_i[l], lru_lam[l]))
        mix = jnp.concatenate([mx.reshape(N_LAT, MIX_TOTAL), mc.reshape(N_CTX, MIX_TOTAL)], axis=0).astype(BF16)
        if last:
            xa, mix = xa[:N_LAT], mix[:N_LAT]
        xa = matmul_gated_residual(mix, w_out[l].astype(BF16), xa, gate1, SEQ, 512, 512)

        h2, idx, wsel = norm_mod(xa, norm2[l][None], scale2, shift2, SEQ, 256, router=router)
        y = moe_ffn(h2, idx, wsel, moe_w1[l].astype(BF16), moe_w3[l].astype(BF16), moe_w2[l].astype(BF16))
        upd = (xa[:N_LAT].reshape(BATCH, SEQ, d) + gate2[:BATCH] * y[:N_LAT].reshape(BATCH, SEQ, d)).reshape(N_LAT, d)
        if last:
            xa = upd
        else:
            xa = jnp.concatenate([upd, xa[N_LAT:] + gate2[BATCH] * y[N_LAT:]], axis=0)

    return final_norm(xa[:N_LAT], norm_f[None], 256).reshape(BATCH, SEQ, d)
```

```python
import functools
import math

import numpy as np
import jax
import jax.numpy as jnp
from jax import lax
from jax.experimental import pallas as pl
from jax.experimental.pallas import tpu as pltpu

D_MODEL = 4096
BATCH = 2
SEQ = 4096
DEPTH = 2
GRID_W = 64
CTX_LEN = 256
MIX_W = D_MODEL // 4
MIX_TOTAL = 4 * MIX_W
CHUNK = 64
CONV_W = 4
CONV_PAD_L = 2
CONV_PAD_R = 1
EPS = 1e-6
S5_CH = 16
S5_G = MIX_W // S5_CH
S5_P = 64
DN_HD = 128
DN_H = MIX_W // DN_HD
ML_H = 4
ML_DV = MIX_W // ML_H
ML_DQK = ML_DV // 2
LRU_BLOCKS = 8
LRU_BD = MIX_W // LRU_BLOCKS
LRU_C = 8.0
N_EXPERTS = 16
N_GROUPS = 4
E_PER_G = N_EXPERTS // N_GROUPS
TOP_K = 2
D_FF = D_MODEL // 4
PROJ_SIZES = (MIX_W, MIX_W, MIX_W, MIX_W, MIX_W, 2 * DN_H, 2 * DN_H, ML_H * ML_DQK, ML_H * ML_DQK, MIX_W, MIX_W,
              2 * ML_H, 2 * ML_H, MIX_W, MIX_W)
PROJ_W = sum(PROJ_SIZES)

N_LAT = BATCH * SEQ
N_CTX = BATCH * CTX_LEN
VMEM_LIMIT = 56 * 1024 * 1024

F32 = jnp.float32
BF16 = jnp.bfloat16


def _cparams(sem):
    return pltpu.CompilerParams(dimension_semantics=sem, vmem_limit_bytes=VMEM_LIMIT)


def _mm_kernel(a_ref, b_ref, o_ref):
    o_ref[...] = jnp.dot(a_ref[...], b_ref[...], preferred_element_type=F32).astype(o_ref.dtype)


def matmul(a, b, tm, tn, out_dtype=F32):
    m, k = a.shape
    n = b.shape[1]
    assert m % tm == 0 and n % tn == 0
    return pl.pallas_call(
        _mm_kernel,
        grid=(m // tm, n // tn),
        in_specs=[pl.BlockSpec((tm, k), lambda i, j: (i, 0)),
                  pl.BlockSpec((k, tn), lambda i, j: (0, j))],
        out_specs=pl.BlockSpec((tm, tn), lambda i, j: (i, j)),
        out_shape=jax.ShapeDtypeStruct((m, n), out_dtype),
        compiler_params=_cparams(("parallel", "arbitrary")),
        name="mm",
    )(a, b)


def _mm_resid_kernel(a_ref, b_ref, r_ref, g_ref, o_ref):
    acc = jnp.dot(a_ref[...], b_ref[...], preferred_element_type=F32)
    o_ref[...] = r_ref[...] + g_ref[...] * acc


def matmul_gated_residual(a, b, resid, gate, rows_per_gate, tm, tn):
    m, k = a.shape
    n = b.shape[1]
    assert m % tm == 0 and n % tn == 0 and rows_per_gate % tm == 0
    return pl.pallas_call(
        _mm_resid_kernel,
        grid=(m // tm, n // tn),
        in_specs=[pl.BlockSpec((tm, k), lambda i, j: (i, 0)),
                  pl.BlockSpec((k, tn), lambda i, j: (0, j)),
                  pl.BlockSpec((tm, tn), lambda i, j: (i, j)),
                  pl.BlockSpec((None, 1, tn), lambda i, j: ((i * tm) // rows_per_gate, 0, j))],
        out_specs=pl.BlockSpec((tm, tn), lambda i, j: (i, j)),
        out_shape=jax.ShapeDtypeStruct((m, n), F32),
        compiler_params=_cparams(("parallel", "arbitrary")),
        name="mm_resid",
    )(a, b, resid, gate)


def _ada_kernel(a_ref, w_ref, b_ref, o_ref):
    o_ref[...] = jnp.dot(a_ref[...], w_ref[...].astype(BF16), preferred_element_type=F32) + b_ref[...]


def ada_matmul(a, w, bias, tn):
    m, k = a.shape
    n = w.shape[1]
    return pl.pallas_call(
        _ada_kernel,
        grid=(n // tn,),
        in_specs=[pl.BlockSpec((m, k), lambda j: (0, 0)),
                  pl.BlockSpec((k, tn), lambda j: (0, j)),
                  pl.BlockSpec((1, tn), lambda j: (0, j))],
        out_specs=pl.BlockSpec((m, tn), lambda j: (0, j)),
        out_shape=jax.ShapeDtypeStruct((m, n), F32),
        compiler_params=_cparams(("arbitrary",)),
        name="ada",
    )(a, w, bias)


def _modulated_norm(x, g, sc, sh):
    ms = jnp.mean(x * x, axis=-1, keepdims=True)
    y = x * lax.rsqrt(ms + EPS) * g
    return y * (1.0 + sc) + sh


def _norm_mod_kernel(x_ref, g_ref, sc_ref, sh_ref, o_ref):
    o_ref[...] = _modulated_norm(x_ref[...], g_ref[...], sc_ref[...], sh_ref[...]).astype(o_ref.dtype)


def _pair_top2_sum(v):
    a, b, c, d = v
    return jnp.maximum(jnp.maximum(jnp.maximum(a + b, a + c), jnp.maximum(a + d, b + c)),
                       jnp.maximum(b + d, c + d))


def _norm_mod_router_kernel(x_ref, g_ref, sc_ref, sh_ref, rwt_ref, rb_ref, o_ref, idx_ref, w_ref):
    h = _modulated_norm(x_ref[...], g_ref[...], sc_ref[...], sh_ref[...])
    o_ref[...] = h.astype(o_ref.dtype)
    logits = lax.dot_general(rwt_ref[...], h, (((1,), (1,)), ((), ())),
                             precision=lax.Precision.HIGHEST, preferred_element_type=F32)
    scores = jax.nn.sigmoid(logits)
    biased = scores + rb_ref[...]
    s = [scores[e:e + 1, :] for e in range(N_EXPERTS)]
    b = [biased[e:e + 1, :] for e in range(N_EXPERTS)]
    gs = [_pair_top2_sum(b[E_PER_G * g:E_PER_G * (g + 1)]) for g in range(N_GROUPS)]
    best, gsel = gs[0], jnp.zeros_like(gs[0], dtype=jnp.int32)
    for g in range(1, N_GROUPS):
        better = gs[g] > best
        gsel = jnp.where(better, g, gsel)
        best = jnp.where(better, gs[g], best)
    vb, vs = [], []
    for k in range(E_PER_G):
        bk, sk = b[k], s[k]
        for g in range(1, N_GROUPS):
            bk = jnp.where(gsel == g, b[E_PER_G * g + k], bk)
            sk = jnp.where(gsel == g, s[E_PER_G * g + k], sk)
        vb.append(bk)
        vs.append(sk)
    m1, i1, w1 = vb[0], jnp.zeros_like(gsel), vs[0]
    for k in range(1, E_PER_G):
        better = vb[k] > m1
        i1 = jnp.where(better, k, i1)
        w1 = jnp.where(better, vs[k], w1)
        m1 = jnp.where(better, vb[k], m1)
    m2, i2, w2, have = vb[0], jnp.zeros_like(gsel), vs[0], i1 != 0
    for k in range(1, E_PER_G):
        valid = i1 != k
        better = valid & (jnp.logical_not(have) | (vb[k] > m2))
        i2 = jnp.where(better, k, i2)
        w2 = jnp.where(better, vs[k], w2)
        m2 = jnp.where(better, vb[k], m2)
        have = have | valid
    tot = w1 + w2
    idx_ref[0:1, :] = gsel * E_PER_G + i1
    idx_ref[1:2, :] = gsel * E_PER_G + i2
    w_ref[0:1, :] = w1 / tot
    w_ref[1:2, :] = w2 / tot


def norm_mod(x, g, sc, sh, rows_per_mod, tm, router=None):
    m, d = x.shape
    assert m % tm == 0 and rows_per_mod % tm == 0
    mod_spec = pl.BlockSpec((None, 1, d), lambda i: ((i * tm) // rows_per_mod, 0, 0))
    in_specs = [pl.BlockSpec((tm, d), lambda i: (i, 0)), pl.BlockSpec((1, d), lambda i: (0, 0)), mod_spec, mod_spec]
    o_spec = pl.BlockSpec((tm, d), lambda i: (i, 0))
    o_shape = jax.ShapeDtypeStruct((m, d), BF16)
    if router is None:
        return pl.pallas_call(
            _norm_mod_kernel, grid=(m // tm,), in_specs=in_specs, out_specs=o_spec, out_shape=o_shape,
            compiler_params=_cparams(("parallel",)), name="norm_mod",
        )(x, g, sc, sh)
    rwt, rb = router
    in_specs += [pl.BlockSpec((N_EXPERTS, d), lambda i: (0, 0)), pl.BlockSpec((N_EXPERTS, 1), lambda i: (0, 0))]
    sel_spec = pl.BlockSpec((TOP_K, tm), lambda i: (0, i))
    return pl.pallas_call(
        _norm_mod_router_kernel, grid=(m // tm,), in_specs=in_specs,
        out_specs=[o_spec, sel_spec, sel_spec],
        out_shape=[o_shape, jax.ShapeDtypeStruct((TOP_K, m), jnp.int32), jax.ShapeDtypeStruct((TOP_K, m), F32)],
        compiler_params=_cparams(("parallel",)), name="norm_mod_router",
    )(x, g, sc, sh, rwt, rb)


def _final_norm_kernel(x_ref, g_ref, o_ref):
    x = x_ref[...]
    ms = jnp.mean(x * x, axis=-1, keepdims=True)
    o_ref[...] = x * lax.rsqrt(ms + EPS) * g_ref[...]


def final_norm(x, g, tm):
    m, d = x.shape
    return pl.pallas_call(
        _final_norm_kernel, grid=(m // tm,),
        in_specs=[pl.BlockSpec((tm, d), lambda i: (i, 0)), pl.BlockSpec((1, d), lambda i: (0, 0))],
        out_specs=pl.BlockSpec((tm, d), lambda i: (i, 0)),
        out_shape=jax.ShapeDtypeStruct((m, d), F32),
        compiler_params=_cparams(("parallel",)), name="final_norm",
    )(x, g)


MOE_TM = 512
MOE_FC = 256


def _moe_kernel(te_ref, nu_ref, x_ref, w1_ref, w3_ref, w2_ref, g_ref, o_ref):
    t = pl.program_id(0)
    f = pl.program_id(1)
    used = t < nu_ref[0]

    @pl.when(used)
    def _():
        x = x_ref[...]
        hg = jnp.dot(x, w1_ref[...], preferred_element_type=F32)
        hu = jnp.dot(x, w3_ref[...], preferred_element_type=F32)
        act = (hg * jax.nn.sigmoid(hg)) * hu * g_ref[...]
        y = jnp.dot(act.astype(BF16), w2_ref[...], preferred_element_type=F32)

        @pl.when(f == 0)
        def _():
            o_ref[...] = y

        @pl.when(f > 0)
        def _():
            o_ref[...] += y

    @pl.when(jnp.logical_not(used) & (f == 0))
    def _():
        o_ref[...] = jnp.zeros_like(o_ref)


def moe_grouped(tile_expert, n_used, xs, w1, w3, w2, gs):
    r, d = xs.shape
    nf = D_FF // MOE_FC

    def fidx(t, f, nu):
        return jnp.where(t < nu[0], f, nf - 1)

    grid_spec = pltpu.PrefetchScalarGridSpec(
        num_scalar_prefetch=2,
        grid=(r // MOE_TM, nf),
        in_specs=[pl.BlockSpec((MOE_TM, d), lambda t, f, te, nu: (t, 0)),
                  pl.BlockSpec((None, d, MOE_FC), lambda t, f, te, nu: (te[t], 0, fidx(t, f, nu))),
                  pl.BlockSpec((None, d, MOE_FC), lambda t, f, te, nu: (te[t], 0, fidx(t, f, nu))),
                  pl.BlockSpec((None, MOE_FC, d), lambda t, f, te, nu: (te[t], fidx(t, f, nu), 0)),
                  pl.BlockSpec((MOE_TM, 1), lambda t, f, te, nu: (t, 0))],
        out_specs=pl.BlockSpec((MOE_TM, d), lambda t, f, te, nu: (t, 0)),
    )
    return pl.pallas_call(
        _moe_kernel, grid_spec=grid_spec,
        out_shape=jax.ShapeDtypeStruct((r, d), F32),
        compiler_params=_cparams(("arbitrary", "arbitrary")), name="moe",
    )(tile_expert, n_used, xs, w1, w3, w2, gs)


def moe_dispatch(idx, wsel):
    n = idx.shape[1]
    e_flat = idx.reshape(-1)
    tok = jnp.tile(jnp.arange(n, dtype=jnp.int32), TOP_K)
    onehot = (e_flat[:, None] == jnp.arange(N_EXPERTS, dtype=jnp.int32)[None, :]).astype(jnp.int32)
    csum = jnp.cumsum(onehot, axis=0)
    counts = csum[-1]
    rank = jnp.take_along_axis(csum, e_flat[:, None], axis=1)[:, 0] - 1
    padded = ((counts + MOE_TM - 1) // MOE_TM) * MOE_TM
    seg_end = jnp.cumsum(padded)
    seg_start = seg_end - padded
    dest = seg_start[e_flat] + rank
    rows = TOP_K * n + N_EXPERTS * MOE_TM
    src_tok = jnp.zeros((rows,), jnp.int32).at[dest].set(tok)
    gate = jnp.zeros((rows,), F32).at[dest].set(wsel.reshape(-1))
    n_tiles = rows // MOE_TM
    n_used = (seg_end[-1] // MOE_TM).astype(jnp.int32)
    tile_start = jnp.arange(n_tiles, dtype=jnp.int32) * MOE_TM
    te = jnp.minimum(jnp.searchsorted(seg_end, tile_start, side='right'), N_EXPERTS - 1).astype(jnp.int32)
    te = jnp.where(jnp.arange(n_tiles) < n_used, te, te[jnp.maximum(n_used - 1, 0)])
    return dest.reshape(TOP_K, n), src_tok, gate[:, None], te, n_used.reshape(1)


def moe_ffn(hb, idx, wsel, w1, w3, w2):
    dest, src_tok, gate, te, n_used = moe_dispatch(idx, wsel)
    xs = jnp.take(hb, src_tok, axis=0)
    ys = moe_grouped(te, n_used, xs, w1, w3, w2, gate)
    return jnp.take(ys, dest[0], axis=0) + jnp.take(ys, dest[1], axis=0)


SEQ_ALL = CTX_LEN + SEQ
S5_T = 16
S5_NC = SEQ_ALL // S5_T
S5_NCC = CTX_LEN // S5_T
S5_W = S5_T * S5_CH


def _gelu_tanh(x):
    return x * (0.5 * (1.0 + jnp.tanh(math.sqrt(2.0 / math.pi) * (x + 0.044715 * (x * x * x)))))


def s5_chunk_operators(lam_re, lam_im, log_step, b_re, b_im, c_re, c_im):
    hp = lax.Precision.HIGHEST
    jj = jnp.arange(S5_T)[:, None]
    ii = jnp.arange(S5_T)[None, :]
    ms, es, fs, a16 = [], [], [], []
    for d in range(2):
        dt = jnp.exp(log_step[d])[:, None]
        lr, li = lam_re[d], lam_im[d]
        er = jnp.exp(lr * dt)
        abr, abi = er * jnp.cos(li * dt), er * jnp.sin(li * dt)
        nr, ni = abr - 1.0, abi
        den = lr * lr + li * li
        fr = (nr * lr + ni * li) / den
        fi = (ni * lr - nr * li) / den
        bbr = fr[..., None] * b_re[d] - fi[..., None] * b_im[d]
        bbi = fr[..., None] * b_im[d] + fi[..., None] * b_re[d]
        k = jnp.arange(S5_T + 1, dtype=F32)[:, None, None]
        mag, ang = jnp.exp(k * (lr * dt)), k * (li * dt)
        pr, pi = mag * jnp.cos(ang), mag * jnp.sin(ang)
        pbr = pr[..., None] * bbr - pi[..., None] * bbi
        pbi = pr[..., None] * bbi + pi[..., None] * bbr
        cr, ci = c_re[d], c_im[d]
        kk = (jnp.einsum('gop,lgpc->lgoc', cr, pbr[:S5_T], precision=hp)
              - jnp.einsum('gop,lgpc->lgoc', ci, pbi[:S5_T], precision=hp))
        cpr = cr[None] * pr[:, :, None, :] - ci[None] * pi[:, :, None, :]
        cpi = -(cr[None] * pi[:, :, None, :] + ci[None] * pr[:, :, None, :])
        if d == 0:
            lag, mask = ii - jj, ii >= jj
            e_pow = S5_T - 1 - jnp.arange(S5_T)
            f_pow = jnp.arange(S5_T) + 1
        else:
            lag, mask = jj - ii, jj >= ii
            e_pow = jnp.arange(S5_T)
            f_pow = S5_T - jnp.arange(S5_T)
        mm = jnp.where(mask[:, :, None, None, None], kk[jnp.clip(lag, 0, S5_T - 1)], 0.0)
        ms.append(mm.transpose(2, 0, 4, 1, 3).reshape(S5_G, S5_W, S5_W))
        for pb in (pbr, pbi):
            es.append(pb[e_pow].transpose(1, 0, 3, 2).reshape(S5_G, S5_W, S5_P))
        for cp in (cpr, cpi):
            fs.append(cp[f_pow].transpose(1, 3, 0, 2).reshape(S5_G, S5_P, S5_W))
        a16 += [pr[S5_T][:, None, :], pi[S5_T][:, None, :]]
    return (ms[0] + ms[1]).astype(BF16), jnp.stack(es).astype(BF16), jnp.stack(fs).astype(BF16), jnp.stack(a16)


def _s5_kernel(u_ref, m_ref, e_ref, f_ref, a_ref, o_ref, e_scr, s_scr):
    u = u_ref[...]
    for k in range(4):
        e_scr[k] = jnp.dot(u, e_ref[k], preferred_element_type=F32)
    arf, aif, arb, aib = (a_ref[k] for k in range(4))

    def segment(start, n, carry):
        def body(k, carry):
            out = []
            for b in range(BATCH):
                sfr, sfi, sbr, sbi = carry[4 * b:4 * b + 4]
                rf = b * S5_NC + start + k
                rb = b * S5_NC + start + n - 1 - k
                s_scr[0, pl.ds(rf, 1), :] = sfr
                s_scr[1, pl.ds(rf, 1), :] = sfi
                s_scr[2, pl.ds(rb, 1), :] = sbr
                s_scr[3, pl.ds(rb, 1), :] = sbi
                nfr = arf * sfr - aif * sfi + e_scr[0, pl.ds(rf, 1), :]
                nfi = arf * sfi + aif * sfr + e_scr[1, pl.ds(rf, 1), :]
                nbr = arb * sbr - aib * sbi + e_scr[2, pl.ds(rb, 1), :]
                nbi = arb * sbi + aib * sbr + e_scr[3, pl.ds(rb, 1), :]
                out += [nfr, nfi, nbr, nbi]
            return tuple(out)
        return lax.fori_loop(0, n, body, carry, unroll=4)

    z = jnp.zeros((1, S5_P), F32)
    carry = segment(0, S5_NCC, (z,) * (4 * BATCH))
    segment(S5_NCC, S5_NC - S5_NCC, carry)
    y = jnp.dot(u, m_ref[...], preferred_element_type=F32)
    for k in range(4):
        y += jnp.dot(s_scr[k].astype(BF16), f_ref[k], preferred_element_type=F32)
    o_ref[...] = y


def s5_scan(u, m, e, f, a16):
    g, r, w = u.shape
    return pl.pallas_call(
        _s5_kernel, grid=(g,),
        in_specs=[pl.BlockSpec((None, r, w), lambda i: (i, 0, 0)),
                  pl.BlockSpec((None, w, w), lambda i: (i, 0, 0)),
                  pl.BlockSpec((4, None, w, S5_P), lambda i: (0, i, 0, 0)),
                  pl.BlockSpec((4, None, S5_P, w), lambda i: (0, i, 0, 0)),
                  pl.BlockSpec((4, None, 1, S5_P), lambda i: (0, i, 0, 0))],
        out_specs=pl.BlockSpec((None, r, w), lambda i: (i, 0, 0)),
        out_shape=jax.ShapeDtypeStruct((g, r, w), F32),
        scratch_shapes=[pltpu.VMEM((4, r, S5_P), F32), pltpu.VMEM((4, r, S5_P), F32)],
        compiler_params=_cparams(("parallel",)), name="s5_scan",
    )(u, m, e, f, a16)


def _s5_out_kernel(y_ref, u_ref, d_ref, w_ref, b_ref, o_ref):
    g = _gelu_tanh(y_ref[...] + d_ref[...] * u_ref[...])
    z = jnp.dot(g.astype(BF16), w_ref[...], preferred_element_type=F32) + b_ref[...]
    o_ref[...] = g * jax.nn.sigmoid(z)


def s5_out(y, u, d_skip, w_glu, b_glu, tm):
    m, c = y.shape
    row = pl.BlockSpec((tm, c), lambda i: (i, 0))
    vec = pl.BlockSpec((1, c), lambda i: (0, 0))
    return pl.pallas_call(
        _s5_out_kernel, grid=(m // tm,),
        in_specs=[row, row, vec, pl.BlockSpec((c, c), lambda i: (0, 0)), vec],
        out_specs=row, out_shape=jax.ShapeDtypeStruct((m, c), F32),
        compiler_params=_cparams(("parallel",)), name="s5_out",
    )(y, u, d_skip, w_glu, b_glu)


def s5_mixer(u_c, u_l, lam_re, lam_im, log_step, b_re, b_im, c_re, c_im, d_skip, w_glu, b_glu):
    bs = u_c.shape[0]
    u_all = jnp.concatenate([u_c, u_l], axis=1)
    uc = u_all.astype(BF16).reshape(bs, S5_NC, S5_T, S5_G, S5_CH).transpose(3, 0, 1, 2, 4)
    uc = uc.reshape(S5_G, bs * S5_NC, S5_W)
    y = s5_scan(uc, *s5_chunk_operators(lam_re, lam_im, log_step, b_re, b_im, c_re, c_im))
    y = y.reshape(S5_G, bs, S5_NC, S5_T, S5_CH).transpose(1, 2, 3, 0, 4).reshape(bs * SEQ_ALL, MIX_W)
    out = s5_out(y, u_all.reshape(bs * SEQ_ALL, MIX_W), d_skip[None], w_glu.astype(BF16), b_glu[None], 544)
    out = out.reshape(bs, SEQ_ALL, MIX_W)
    return out[:, :CTX_LEN], out[:, CTX_LEN:]


LRU_CB = 256
LRU_TB = 256
LRU_HALO = 8
LRU_ROWS = SEQ_ALL + 3 * LRU_HALO


def _lru_kernel(xc_ref, xl_ref, yc_ref, yl_ref, cw_ref, cb_ref, wa_ref, ba_ref, wi_ref, bi_ref, sp_ref,
                oc_ref, ol_ref, xpad, a_f, b_f, a_b, b_b):
    halo = jnp.zeros((LRU_HALO, LRU_CB), F32)
    lat0 = CTX_LEN + 2 * LRU_HALO
    xpad[0:LRU_HALO] = halo
    xpad[LRU_HALO:LRU_HALO + CTX_LEN] = xc_ref[...]
    xpad[LRU_HALO + CTX_LEN:lat0] = halo
    xpad[lat0:lat0 + SEQ] = xl_ref[...]
    xpad[lat0 + SEQ:LRU_ROWS] = halo
    cw = [cw_ref[j:j + 1, :] for j in range(CONV_W)]
    cb = cb_ref[...]

    def gates(p0, r0):
        ext = xpad[pl.ds(p0 - LRU_HALO, LRU_TB + 2 * LRU_HALO), :]
        xb = cb
        for j in range(CONV_W):
            s0 = LRU_HALO - CONV_PAD_L + j
            xb = xb + cw[j] * ext[s0:s0 + LRU_TB]
        for k in range(LRU_CB // LRU_BD):
            lanes = slice(k * LRU_BD, (k + 1) * LRU_BD)
            xk = xb[:, lanes]
            xkb = xk.astype(BF16)
            for d, (a_s, b_s) in enumerate(((a_f, b_f), (a_b, b_b))):
                r = jax.nn.sigmoid(jnp.dot(xkb, wa_ref[d, k], preferred_element_type=F32) + ba_ref[d][:, lanes])
                i = jax.nn.sigmoid(jnp.dot(xkb, wi_ref[d, k], preferred_element_type=F32) + bi_ref[d][:, lanes])
                log_a = -LRU_C * r * sp_ref[d][:, lanes]
                t = jnp.tanh(log_a)
                a_s[pl.ds(r0, LRU_TB), lanes] = jnp.exp(log_a)
                b_s[pl.ds(r0, LRU_TB), lanes] = jnp.sqrt(-2.0 * t / (1.0 - t)) * (i * xk)

    gates(LRU_HALO, 0)

    def lat_gates(k, _):
        gates(pl.multiple_of(lat0 + k * LRU_TB, 8), pl.multiple_of(CTX_LEN + k * LRU_TB, 8))
        return 0
    lax.fori_loop(0, SEQ // LRU_TB, lat_gates, 0)

    def segment(start, n, carry):
        def body(t, carry):
            hf, hb = carry
            rf = start + t
            rb = start + n - 1 - t
            hf = a_f[pl.ds(rf, 1), :] * hf + b_f[pl.ds(rf, 1), :]
            b_f[pl.ds(rf, 1), :] = hf
            hb = a_b[pl.ds(rb, 1), :] * hb + b_b[pl.ds(rb, 1), :]
            b_b[pl.ds(rb, 1), :] = hb
            return hf, hb
        return lax.fori_loop(0, n, body, carry, unroll=8)

    z = jnp.zeros((1, LRU_CB), F32)
    carry = segment(0, CTX_LEN, (z, z))
    segment(CTX_LEN, SEQ, carry)

    oc_ref[...] = _gelu_tanh(yc_ref[...]) * (b_f[0:CTX_LEN] + b_b[0:CTX_LEN])

    def lat_out(k, _):
        r0 = pl.multiple_of(k * LRU_TB, 8)
        rows = pl.ds(pl.multiple_of(CTX_LEN + k * LRU_TB, 8), LRU_TB)
        ol_ref[pl.ds(r0, LRU_TB), :] = _gelu_tanh(yl_ref[pl.ds(r0, LRU_TB), :]) * (b_f[rows, :] + b_b[rows, :])
        return 0
    lax.fori_loop(0, SEQ // LRU_TB, lat_out, 0)


def lru_mixer(y_c, x_c, y_l, x_l, conv_w, conv_b, w_a, b_a, w_i, b_i, lam):
    bs = x_c.shape[0]
    nb = LRU_CB // LRU_BD
    sp = jax.nn.softplus(-lam)[:, None, :]
    ctx_spec = pl.BlockSpec((None, CTX_LEN, LRU_CB), lambda b, c: (b, 0, c))
    lat_spec = pl.BlockSpec((None, SEQ, LRU_CB), lambda b, c: (b, 0, c))
    vec2 = pl.BlockSpec((2, 1, LRU_CB), lambda b, c: (0, 0, c))
    wspec = pl.BlockSpec((2, nb, LRU_BD, LRU_BD), lambda b, c: (0, c, 0, 0))
    scan_buf = pltpu.VMEM((SEQ_ALL, LRU_CB), F32)
    return pl.pallas_call(
        _lru_kernel, grid=(bs, MIX_W // LRU_CB),
        in_specs=[ctx_spec, lat_spec, ctx_spec, lat_spec,
                  pl.BlockSpec((CONV_W, LRU_CB), lambda b, c: (0, c)),
                  pl.BlockSpec((1, LRU_CB), lambda b, c: (0, c)),
                  wspec, vec2, wspec, vec2, vec2],
        out_specs=[ctx_spec, lat_spec],
        out_shape=[jax.ShapeDtypeStruct((bs, CTX_LEN, MIX_W), F32), jax.ShapeDtypeStruct((bs, SEQ, MIX_W), F32)],
        scratch_shapes=[pltpu.VMEM((LRU_ROWS, LRU_CB), F32), scan_buf, scan_buf, scan_buf, scan_buf],
        compiler_params=_cparams(("parallel", "parallel")), name="lru",
    )(x_c, x_l, y_c, y_l, conv_w, conv_b[None], w_a.astype(BF16), b_a[:, None, :], w_i.astype(BF16),
      b_i[:, None, :], sp)


def rmsnorm(x, g):
    xf = x.astype(F32)
    y = xf * lax.rsqrt(jnp.mean(xf * xf, axis=-1, keepdims=True) + EPS)
    return (y * g.astype(F32)).astype(x.dtype)


def l2norm(t):
    return t * lax.rsqrt(jnp.sum(t * t, axis=-1, keepdims=True) + EPS)


def to_col_major(t):
    b, l = t.shape[:2]
    rows = l // GRID_W
    return jnp.swapaxes(t.reshape(b, rows, GRID_W, -1), 1, 2).reshape(t.shape)


def to_row_major(t):
    b, l = t.shape[:2]
    rows = l // GRID_W
    return jnp.swapaxes(t.reshape(b, GRID_W, rows, -1), 1, 2).reshape(t.shape)


def centred_dwconv(x, w):
    l = x.shape[1]
    xp = jnp.pad(x, ((0, 0), (CONV_PAD_L, CONV_PAD_R), (0, 0)))
    out = xp[:, 0:l] * w[0]
    for j in range(1, CONV_W):
        out = out + xp[:, j:j + l] * w[j]
    return out


def to_chunks(t):
    b, l = t.shape[:2]
    t = t.reshape(b, l // CHUNK, CHUNK, *t.shape[2:])
    return jnp.moveaxis(t, (1, 3), (0, 2))


def from_chunks(t):
    t = jnp.moveaxis(t, (0, 2), (1, 3))
    return t.reshape(t.shape[0], -1, *t.shape[3:])


def run_bidir(step, p_f, p_b, ctx_f, lat_f, ctx_b, lat_b, state0):
    rev = lambda ts: tuple(jnp.flip(t, 1) for t in ts)
    oc_f, s_f = step(p_f, ctx_f, state0)
    ol_f, _ = step(p_f, lat_f, s_f)
    oc_b, s_b = step(p_b, rev(ctx_b), state0)
    ol_b, _ = step(p_b, rev(lat_b), s_b)
    return oc_f + jnp.flip(oc_b, 1), ol_f + jnp.flip(ol_b, 1)


def gated_delta_chunked(q, k, v, g, beta, s0):
    dk = q.shape[-1]
    q, k, v = to_chunks(q) * dk ** -0.5, to_chunks(k), to_chunks(v)
    g, beta = to_chunks(g), to_chunks(beta)
    dv = v.shape[-1]
    gc = jnp.cumsum(g, axis=-1)
    tril = jnp.tril(jnp.ones((CHUNK, CHUNK), bool))
    strict = jnp.tril(jnp.ones((CHUNK, CHUNK), bool), -1)
    decay = jnp.exp(jnp.where(tril, gc[..., :, None] - gc[..., None, :], -jnp.inf))
    kb = k * beta[..., None]
    lower = jnp.where(strict, jnp.einsum('nbhid,nbhjd->nbhij', kb, k) * decay, 0.0)
    eye = jnp.eye(CHUNK, dtype=lower.dtype)
    rhs = jnp.concatenate([v * beta[..., None], kb * jnp.exp(gc)[..., None]], axis=-1)
    sol = lax.linalg.triangular_solve(lower + eye, rhs, left_side=True, lower=True, unit_diagonal=True)
    u, w = sol[..., :dv], sol[..., dv:]
    attn = jnp.einsum('nbhid,nbhjd->nbhij', q, k) * decay
    qg = q * jnp.exp(gc)[..., None]
    kd = k * jnp.exp(gc[..., -1:] - gc)[..., None]
    g_last = jnp.exp(gc[..., -1])

    def body(s, xs):
        u_i, w_i, a_i, qg_i, kd_i, gl_i = xs
        v_new = u_i - jnp.einsum('bhck,bhkv->bhcv', w_i, s)
        o = jnp.einsum('bhck,bhkv->bhcv', qg_i, s) + jnp.einsum('bhij,bhjv->bhiv', a_i, v_new)
        s = s * gl_i[..., None, None] + jnp.einsum('bhck,bhcv->bhkv', kd_i, v_new)
        return s, o
    s_fin, o = lax.scan(body, s0, (u, w, attn, qg, kd, g_last))
    return from_chunks(o), s_fin


def deltanet_mixer(seg_c, seg_l, conv_w, a_log, dt_bias, norm_w):
    def prep(q, k, v, z, a, b):
        bs, l = q.shape[:2]
        qkv = jax.nn.silu(centred_dwconv(jnp.concatenate([q, k, v], axis=-1), conv_w))
        q, k, v = jnp.split(qkv, 3, axis=-1)
        q = l2norm(q.reshape(bs, l, DN_H, DN_HD))
        k = l2norm(k.reshape(bs, l, DN_H, DN_HD))
        v = v.reshape(bs, l, DN_H, DN_HD)
        g = -jnp.exp(a_log) * jax.nn.softplus(a.reshape(bs, l, 2, DN_H) + dt_bias)
        beta = jax.nn.sigmoid(b.reshape(bs, l, 2, DN_H))
        return (q, k, v, g[:, :, 0], beta[:, :, 0]), (q, k, v, g[:, :, 1], beta[:, :, 1])

    cf, cb = prep(*seg_c)
    lf, lb = prep(*seg_l)
    s0 = jnp.zeros((seg_c[0].shape[0], DN_H, DN_HD, DN_HD), F32)
    step = lambda p, seq, s: gated_delta_chunked(*seq, s)
    oc, ol = run_bidir(step, None, None, cf, lf, cb, lb, s0)

    def out(o, z):
        bs, l = z.shape[:2]
        o = rmsnorm(o, norm_w) * jax.nn.silu(z.reshape(bs, l, DN_H, DN_HD))
        return o.reshape(bs, l, MIX_W)
    return out(oc, seg_c[3]), out(ol, seg_l[3])


def mlstm_chunked(q, k, v, ig, lf, state0):
    dqk = q.shape[-1]
    q, k, v = to_chunks(q) * dqk ** -0.5, to_chunks(k), to_chunks(v)
    ig, lf = to_chunks(ig), to_chunks(lf)
    b = jnp.cumsum(lf, axis=-1)
    tril = jnp.tril(jnp.ones((CHUNK, CHUNK), bool))
    dmat = jnp.where(tril, b[..., :, None] - b[..., None, :] + ig[..., None, :], -jnp.inf)
    dmax = jnp.max(dmat, axis=-1)
    sqk = jnp.einsum('nbhid,nbhjd->nbhij', q, k)
    b_last = b[..., -1]
    logw = b_last[..., None] - b + ig
    logw_max = jnp.max(logw, axis=-1)

    def body(carry, xs):
        c_s, n_s, m = carry
        q_i, k_i, v_i, b_i, d_i, dm_i, s_i, lw_i, lwm_i, bl_i = xs
        inter = b_i + m[..., None]
        m_t = jnp.maximum(inter, dm_i)
        w_inter = jnp.exp(inter - m_t)
        wmat = jnp.exp(d_i - m_t[..., None]) * s_i
        num = w_inter[..., None] * jnp.einsum('bhck,bhkv->bhcv', q_i, c_s) + jnp.einsum('bhij,bhjv->bhiv', wmat, v_i)
        den = w_inter * jnp.einsum('bhck,bhk->bhc', q_i, n_s) + jnp.sum(wmat, axis=-1)
        h = num / jnp.maximum(jnp.abs(den), jnp.exp(-m_t))[..., None]
        m_new = jnp.maximum(bl_i + m, lwm_i)
        keep = jnp.exp(bl_i + m - m_new)
        kw = k_i * jnp.exp(lw_i - m_new[..., None])[..., None]
        c_s = keep[..., None, None] * c_s + jnp.einsum('bhck,bhcv->bhkv', kw, v_i)
        n_s = keep[..., None] * n_s + jnp.sum(kw, axis=2)
        return (c_s, n_s, m_new), h
    state, h = lax.scan(body, state0, (q, k, v, b, dmat, dmax, sqk, logw, logw_max, b_last))
    return from_chunks(h), state


def mlstm_mixer(seg_c, seg_l, i_bias, f_bias, norm_w):
    def prep(q, k, v, o, ip, fp):
        bs, l = q.shape[:2]
        q = q.reshape(bs, l, ML_H, ML_DQK)
        k = k.reshape(bs, l, ML_H, ML_DQK)
        v = v.reshape(bs, l, ML_H, ML_DV)
        ig = ip.reshape(bs, l, 2, ML_H) + i_bias
        lf = jax.nn.log_sigmoid(fp.reshape(bs, l, 2, ML_H) + f_bias)
        return (q, k, v, ig[:, :, 0], lf[:, :, 0]), (q, k, v, ig[:, :, 1], lf[:, :, 1])

    cf, cb = prep(*seg_c)
    lf_, lb = prep(*seg_l)
    bs = seg_c[0].shape[0]
    state0 = (jnp.zeros((bs, ML_H, ML_DQK, ML_DV), F32), jnp.zeros((bs, ML_H, ML_DQK), F32),
              jnp.zeros((bs, ML_H), F32))
    step = lambda p, seq, s: mlstm_chunked(*seq, s)
    hc, hl = run_bidir(step, None, None, cf, lf_, cb, lb, state0)

    def out(h, o):
        bs_, l = o.shape[:2]
        h = rmsnorm(h, norm_w.reshape(ML_H, ML_DV)).reshape(bs_, l, MIX_W)
        return h * jax.nn.sigmoid(o)
    return out(hc, seg_c[3]), out(hl, seg_l[3])


PROJ_TN = 512
PROJ_PAD = -(-PROJ_W // PROJ_TN) * PROJ_TN
COL_S5, COL_DN_QKVZ, COL_ML_Q, COL_ML_K, COL_ML_V, COL_ML_O, COL_LRU_Y, COL_LRU_X = (
    0, 1024, 5120, 5632, 6144, 7168, 8192, 9216)
COL_DN_A, COL_DN_B, COL_ML_I, COL_ML_F = 10240, 10256, 10272, 10280


def permute_proj_weight(w):
    o = [0] + [int(v) for v in np.cumsum(PROJ_SIZES)]
    parts = [w[:, o[0]:o[5]], w[:, o[7]:o[11]], w[:, o[13]:o[15]], w[:, o[5]:o[7]], w[:, o[11]:o[13]]]
    return jnp.pad(jnp.concatenate(parts, axis=1).astype(BF16), ((0, 0), (0, PROJ_PAD - PROJ_W)))


def hybrid_mixer(proj, s5_p, dn_p, ml_p, lru_p):
    def seg(c0, w):
        blk = proj[:, c0:c0 + w]
        return blk[N_LAT:].reshape(BATCH, CTX_LEN, w), blk[:N_LAT].reshape(BATCH, SEQ, w)

    u_c, u_l = seg(COL_S5, MIX_W)
    a_c, a_l = s5_mixer(u_c, to_col_major(u_l), *s5_p)
    a_l = to_row_major(a_l)
    dn = [seg(COL_DN_QKVZ + k * MIX_W, MIX_W) for k in range(4)] + [seg(COL_DN_A, 2 * DN_H), seg(COL_DN_B, 2 * DN_H)]
    b_c, b_l = deltanet_mixer(tuple(s[0] for s in dn), tuple(s[1] for s in dn), *dn_p)
    ml = [seg(COL_ML_Q, ML_H * ML_DQK), seg(COL_ML_K, ML_H * ML_DQK), seg(COL_ML_V, MIX_W), seg(COL_ML_O, MIX_W),
          seg(COL_ML_I, 2 * ML_H), seg(COL_ML_F, 2 * ML_H)]
    c_c, c_l = mlstm_mixer(tuple(s[0] for s in ml), tuple(s[1] for s in ml), *ml_p)
    y_c, y_l = seg(COL_LRU_Y, MIX_W)
    x_c, x_l = seg(COL_LRU_X, MIX_W)
    d_c, d_l = lru_mixer(y_c, x_c, to_col_major(y_l), to_col_major(x_l), *lru_p)
    d_l = to_row_major(d_l)
    mc = jnp.concatenate([a_c, b_c, c_c, d_c], axis=-1).reshape(N_CTX, MIX_TOTAL)
    mx = jnp.concatenate([a_l, b_l, c_l, d_l], axis=-1).reshape(N_LAT, MIX_TOTAL)
    return jnp.concatenate([mx, mc], axis=0).astype(BF16)


def kernel(x, c, ctx, c_ctx, w_ada, b_ada, norm1, norm2, norm_f, w_in, w_out, s5_lam_re, s5_lam_im, s5_log_step,
           s5_b_re, s5_b_im, s5_c_re, s5_c_im, s5_d, s5_w_glu, s5_b_glu, dn_conv, dn_a_log, dn_dt_bias, dn_norm,
           ml_i_bias, ml_f_bias, ml_norm, lru_conv_w, lru_conv_b, lru_w_a, lru_b_a, lru_w_i, lru_b_i, lru_lam,
           router_w, router_bias, moe_w1, moe_w3, moe_w2):
    d = D_MODEL
    xa = jnp.concatenate([x.reshape(N_LAT, d), ctx.reshape(N_CTX, d)], axis=0)
    cond = jnp.concatenate([jax.nn.silu(c), jax.nn.silu(c_ctx)[None], jnp.zeros((16 - BATCH - 1, d), F32)], axis=0)
    cond = cond.astype(BF16)
    router = (router_w.T, router_bias.reshape(N_EXPERTS, 1))

    for l in range(DEPTH):
        last = l == DEPTH - 1
        mod = ada_matmul(cond, w_ada[l], b_ada[l][None], 512)[:BATCH + 1].reshape(BATCH + 1, 6, 1, d)
        shift1, scale1, gate1, shift2, scale2, gate2 = (mod[:, k] for k in range(6))

        h1 = norm_mod(xa, norm1[l][None], scale1, shift1, SEQ, 256)
        proj = matmul(h1, permute_proj_weight(w_in[l]), 512, PROJ_TN)
        mix = hybrid_mixer(
            proj,
            (s5_lam_re[l], s5_lam_im[l], s5_log_step[l], s5_b_re[l], s5_b_im[l], s5_c_re[l], s5_c_im[l], s5_d[l],
             s5_w_glu[l], s5_b_glu[l]),
            (dn_conv[l], dn_a_log[l], dn_dt_bias[l], dn_norm[l]),
            (ml_i_bias[l], ml_f_bias[l], ml_norm[l]),
            (lru_conv_w[l], lru_conv_b[l], lru_w_a[l], lru_b_a[l], lru_w_i[l], lru_b_i[l], lru_lam[l]))
        if last:
            xa, mix = xa[:N_LAT], mix[:N_LAT]
        xa = matmul_gated_residual(mix, w_out[l].astype(BF16), xa, gate1, SEQ, 512, 512)

        h2, idx, wsel = norm_mod(xa, norm2[l][None], scale2, shift2, SEQ, 256, router=router)
        y = moe_ffn(h2, idx, wsel, moe_w1[l].astype(BF16), moe_w3[l].astype(BF16), moe_w2[l].astype(BF16))
        upd = (xa[:N_LAT].reshape(BATCH, SEQ, d) + gate2[:BATCH] * y[:N_LAT].reshape(BATCH, SEQ, d)).reshape(N_LAT, d)
        if last:
            xa = upd
        else:
            xa = jnp.concatenate([upd, xa[N_LAT:] + gate2[BATCH] * y[N_LAT:]], axis=0)

    return final_norm(xa[:N_LAT], norm_f[None], 256).reshape(BATCH, SEQ, d)
```

```python
import functools
import math

import numpy as np
import jax
import jax.numpy as jnp
from jax import lax
from jax.experimental import pallas as pl
from jax.experimental.pallas import tpu as pltpu

D_MODEL = 4096
BATCH = 2
SEQ = 4096
DEPTH = 2
GRID_W = 64
CTX_LEN = 256
MIX_W = D_MODEL // 4
MIX_TOTAL = 4 * MIX_W
CHUNK = 64
CONV_W = 4
CONV_PAD_L = 2
CONV_PAD_R = 1
EPS = 1e-6
S5_CH = 16
S5_G = MIX_W // S5_CH
S5_P = 64
DN_HD = 128
DN_H = MIX_W // DN_HD
ML_H = 4
ML_DV = MIX_W // ML_H
ML_DQK = ML_DV // 2
LRU_BLOCKS = 8
LRU_BD = MIX_W // LRU_BLOCKS
LRU_C = 8.0
N_EXPERTS = 16
N_GROUPS = 4
E_PER_G = N_EXPERTS // N_GROUPS
TOP_K = 2
D_FF = D_MODEL // 4
PROJ_SIZES = (MIX_W, MIX_W, MIX_W, MIX_W, MIX_W, 2 * DN_H, 2 * DN_H, ML_H * ML_DQK, ML_H * ML_DQK, MIX_W, MIX_W,
              2 * ML_H, 2 * ML_H, MIX_W, MIX_W)
PROJ_W = sum(PROJ_SIZES)

N_LAT = BATCH * SEQ
N_CTX = BATCH * CTX_LEN
VMEM_LIMIT = 56 * 1024 * 1024

F32 = jnp.float32
BF16 = jnp.bfloat16


def _cparams(sem):
    return pltpu.CompilerParams(dimension_semantics=sem, vmem_limit_bytes=VMEM_LIMIT)


def _mm_kernel(a_ref, b_ref, o_ref):
    o_ref[...] = jnp.dot(a_ref[...], b_ref[...], preferred_element_type=F32).astype(o_ref.dtype)


def matmul(a, b, tm, tn, out_dtype=F32):
    m, k = a.shape
    n = b.shape[1]
    assert m % tm == 0 and n % tn == 0
    return pl.pallas_call(
        _mm_kernel,
        grid=(m // tm, n // tn),
        in_specs=[pl.BlockSpec((tm, k), lambda i, j: (i, 0)),
                  pl.BlockSpec((k, tn), lambda i, j: (0, j))],
        out_specs=pl.BlockSpec((tm, tn), lambda i, j: (i, j)),
        out_shape=jax.ShapeDtypeStruct((m, n), out_dtype),
        compiler_params=_cparams(("parallel", "arbitrary")),
        name="mm",
    )(a, b)


def _mm_resid_kernel(a0_ref, a1_ref, a2_ref, a3_ref, b_ref, r_ref, g_ref, o_ref):
    acc = None
    for s, a_ref in enumerate((a0_ref, a1_ref, a2_ref, a3_ref)):
        part = jnp.dot(a_ref[...], b_ref[s * MIX_W:(s + 1) * MIX_W, :], preferred_element_type=F32)
        acc = part if acc is None else acc + part
    o_ref[...] = r_ref[...] + g_ref[...] * acc


def matmul_gated_residual(a_slabs, b, resid, gate, rows_per_gate, tm, tn):
    m = a_slabs[0].shape[0]
    k, n = b.shape
    assert m % tm == 0 and n % tn == 0 and rows_per_gate % tm == 0 and k == 4 * MIX_W
    return pl.pallas_call(
        _mm_resid_kernel,
        grid=(m // tm, n // tn),
        in_specs=[pl.BlockSpec((tm, MIX_W), lambda i, j: (i, 0))] * 4 + [
                  pl.BlockSpec((k, tn), lambda i, j: (0, j)),
                  pl.BlockSpec((tm, tn), lambda i, j: (i, j)),
                  pl.BlockSpec((None, 1, tn), lambda i, j: ((i * tm) // rows_per_gate, 0, j))],
        out_specs=pl.BlockSpec((tm, tn), lambda i, j: (i, j)),
        out_shape=jax.ShapeDtypeStruct((m, n), F32),
        compiler_params=_cparams(("parallel", "arbitrary")),
        name="mm_resid",
    )(*a_slabs, b, resid, gate)


def _ada_kernel(a_ref, w_ref, b_ref, o_ref):
    o_ref[...] = jnp.dot(a_ref[...], w_ref[...].astype(BF16), preferred_element_type=F32) + b_ref[...]


def ada_matmul(a, w, bias, tn):
    m, k = a.shape
    n = w.shape[1]
    return pl.pallas_call(
        _ada_kernel,
        grid=(n // tn,),
        in_specs=[pl.BlockSpec((m, k), lambda j: (0, 0)),
                  pl.BlockSpec((k, tn), lambda j: (0, j)),
                  pl.BlockSpec((1, tn), lambda j: (0, j))],
        out_specs=pl.BlockSpec((m, tn), lambda j: (0, j)),
        out_shape=jax.ShapeDtypeStruct((m, n), F32),
        compiler_params=_cparams(("arbitrary",)),
        name="ada",
    )(a, w, bias)


def _modulated_norm(x, g, sc, sh):
    ms = jnp.mean(x * x, axis=-1, keepdims=True)
    y = x * lax.rsqrt(ms + EPS) * g
    return y * (1.0 + sc) + sh


def _norm_mod_kernel(x_ref, g_ref, sc_ref, sh_ref, o_ref):
    o_ref[...] = _modulated_norm(x_ref[...], g_ref[...], sc_ref[...], sh_ref[...]).astype(o_ref.dtype)


def _pair_top2_sum(v):
    a, b, c, d = v
    return jnp.maximum(jnp.maximum(jnp.maximum(a + b, a + c), jnp.maximum(a + d, b + c)),
                       jnp.maximum(b + d, c + d))


def _norm_mod_router_kernel(x_ref, g_ref, sc_ref, sh_ref, rwt_ref, rb_ref, o_ref, idx_ref, w_ref):
    h = _modulated_norm(x_ref[...], g_ref[...], sc_ref[...], sh_ref[...])
    o_ref[...] = h.astype(o_ref.dtype)
    logits = lax.dot_general(rwt_ref[...], h, (((1,), (1,)), ((), ())),
                             precision=lax.Precision.HIGHEST, preferred_element_type=F32)
    scores = jax.nn.sigmoid(logits)
    biased = scores + rb_ref[...]
    s = [scores[e:e + 1, :] for e in range(N_EXPERTS)]
    b = [biased[e:e + 1, :] for e in range(N_EXPERTS)]
    gs = [_pair_top2_sum(b[E_PER_G * g:E_PER_G * (g + 1)]) for g in range(N_GROUPS)]
    best, gsel = gs[0], jnp.zeros_like(gs[0], dtype=jnp.int32)
    for g in range(1, N_GROUPS):
        better = gs[g] > best
        gsel = jnp.where(better, g, gsel)
        best = jnp.where(better, gs[g], best)
    vb, vs = [], []
    for k in range(E_PER_G):
        bk, sk = b[k], s[k]
        for g in range(1, N_GROUPS):
            bk = jnp.where(gsel == g, b[E_PER_G * g + k], bk)
            sk = jnp.where(gsel == g, s[E_PER_G * g + k], sk)
        vb.append(bk)
        vs.append(sk)
    m1, i1, w1 = vb[0], jnp.zeros_like(gsel), vs[0]
    for k in range(1, E_PER_G):
        better = vb[k] > m1
        i1 = jnp.where(better, k, i1)
        w1 = jnp.where(better, vs[k], w1)
        m1 = jnp.where(better, vb[k], m1)
    m2, i2, w2, have = vb[0], jnp.zeros_like(gsel), vs[0], i1 != 0
    for k in range(1, E_PER_G):
        valid = i1 != k
        better = valid & (jnp.logical_not(have) | (vb[k] > m2))
        i2 = jnp.where(better, k, i2)
        w2 = jnp.where(better, vs[k], w2)
        m2 = jnp.where(better, vb[k], m2)
        have = have | valid
    tot = w1 + w2
    idx_ref[0:1, :] = gsel * E_PER_G + i1
    idx_ref[1:2, :] = gsel * E_PER_G + i2
    w_ref[0:1, :] = w1 / tot
    w_ref[1:2, :] = w2 / tot


def norm_mod(x, g, sc, sh, rows_per_mod, tm, router=None):
    m, d = x.shape
    assert m % tm == 0 and rows_per_mod % tm == 0
    mod_spec = pl.BlockSpec((None, 1, d), lambda i: ((i * tm) // rows_per_mod, 0, 0))
    in_specs = [pl.BlockSpec((tm, d), lambda i: (i, 0)), pl.BlockSpec((1, d), lambda i: (0, 0)), mod_spec, mod_spec]
    o_spec = pl.BlockSpec((tm, d), lambda i: (i, 0))
    o_shape = jax.ShapeDtypeStruct((m, d), BF16)
    if router is None:
        return pl.pallas_call(
            _norm_mod_kernel, grid=(m // tm,), in_specs=in_specs, out_specs=o_spec, out_shape=o_shape,
            compiler_params=_cparams(("parallel",)), name="norm_mod",
        )(x, g, sc, sh)
    rwt, rb = router
    in_specs += [pl.BlockSpec((N_EXPERTS, d), lambda i: (0, 0)), pl.BlockSpec((N_EXPERTS, 1), lambda i: (0, 0))]
    sel_spec = pl.BlockSpec((TOP_K, tm), lambda i: (0, i))
    return pl.pallas_call(
        _norm_mod_router_kernel, grid=(m // tm,), in_specs=in_specs,
        out_specs=[o_spec, sel_spec, sel_spec],
        out_shape=[o_shape, jax.ShapeDtypeStruct((TOP_K, m), jnp.int32), jax.ShapeDtypeStruct((TOP_K, m), F32)],
        compiler_params=_cparams(("parallel",)), name="norm_mod_router",
    )(x, g, sc, sh, rwt, rb)


def _final_norm_kernel(x_ref, g_ref, o_ref):
    x = x_ref[...]
    ms = jnp.mean(x * x, axis=-1, keepdims=True)
    o_ref[...] = x * lax.rsqrt(ms + EPS) * g_ref[...]


def final_norm(x, g, tm):
    m, d = x.shape
    return pl.pallas_call(
        _final_norm_kernel, grid=(m // tm,),
        in_specs=[pl.BlockSpec((tm, d), lambda i: (i, 0)), pl.BlockSpec((1, d), lambda i: (0, 0))],
        out_specs=pl.BlockSpec((tm, d), lambda i: (i, 0)),
        out_shape=jax.ShapeDtypeStruct((m, d), F32),
        compiler_params=_cparams(("parallel",)), name="final_norm",
    )(x, g)


MOE_TM = 512
MOE_FC = 256


def _moe_kernel(te_ref, nu_ref, x_ref, w1_ref, w3_ref, w2_ref, g_ref, o_ref):
    t = pl.program_id(0)
    f = pl.program_id(1)
    used = t < nu_ref[0]

    @pl.when(used)
    def _():
        x = x_ref[...]
        hg = jnp.dot(x, w1_ref[...], preferred_element_type=F32)
        hu = jnp.dot(x, w3_ref[...], preferred_element_type=F32)
        act = (hg * jax.nn.sigmoid(hg)) * hu * g_ref[...]
        y = jnp.dot(act.astype(BF16), w2_ref[...], preferred_element_type=F32)

        @pl.when(f == 0)
        def _():
            o_ref[...] = y

        @pl.when(f > 0)
        def _():
            o_ref[...] += y

    @pl.when(jnp.logical_not(used) & (f == 0))
    def _():
        o_ref[...] = jnp.zeros_like(o_ref)


def moe_grouped(tile_expert, n_used, xs, w1, w3, w2, gs):
    r, d = xs.shape
    nf = D_FF // MOE_FC

    def fidx(t, f, nu):
        return jnp.where(t < nu[0], f, nf - 1)

    grid_spec = pltpu.PrefetchScalarGridSpec(
        num_scalar_prefetch=2,
        grid=(r // MOE_TM, nf),
        in_specs=[pl.BlockSpec((MOE_TM, d), lambda t, f, te, nu: (t, 0)),
                  pl.BlockSpec((None, d, MOE_FC), lambda t, f, te, nu: (te[t], 0, fidx(t, f, nu))),
                  pl.BlockSpec((None, d, MOE_FC), lambda t, f, te, nu: (te[t], 0, fidx(t, f, nu))),
                  pl.BlockSpec((None, MOE_FC, d), lambda t, f, te, nu: (te[t], fidx(t, f, nu), 0)),
                  pl.BlockSpec((MOE_TM, 1), lambda t, f, te, nu: (t, 0))],
        out_specs=pl.BlockSpec((MOE_TM, d), lambda t, f, te, nu: (t, 0)),
    )
    return pl.pallas_call(
        _moe_kernel, grid_spec=grid_spec,
        out_shape=jax.ShapeDtypeStruct((r, d), F32),
        compiler_params=_cparams(("arbitrary", "arbitrary")), name="moe",
    )(tile_expert, n_used, xs, w1, w3, w2, gs)


def moe_dispatch(idx, wsel):
    n = idx.shape[1]
    e_flat = idx.reshape(-1)
    tok = jnp.tile(jnp.arange(n, dtype=jnp.int32), TOP_K)
    onehot = (e_flat[:, None] == jnp.arange(N_EXPERTS, dtype=jnp.int32)[None, :]).astype(jnp.int32)
    csum = jnp.cumsum(onehot, axis=0)
    counts = csum[-1]
    rank = jnp.take_along_axis(csum, e_flat[:, None], axis=1)[:, 0] - 1
    padded = ((counts + MOE_TM - 1) // MOE_TM) * MOE_TM
    seg_end = jnp.cumsum(padded)
    seg_start = seg_end - padded
    dest = seg_start[e_flat] + rank
    rows = TOP_K * n + N_EXPERTS * MOE_TM
    src_tok = jnp.zeros((rows,), jnp.int32).at[dest].set(tok)
    gate = jnp.zeros((rows,), F32).at[dest].set(wsel.reshape(-1))
    n_tiles = rows // MOE_TM
    n_used = (seg_end[-1] // MOE_TM).astype(jnp.int32)
    tile_start = jnp.arange(n_tiles, dtype=jnp.int32) * MOE_TM
    te = jnp.minimum(jnp.searchsorted(seg_end, tile_start, side='right'), N_EXPERTS - 1).astype(jnp.int32)
    te = jnp.where(jnp.arange(n_tiles) < n_used, te, te[jnp.maximum(n_used - 1, 0)])
    return dest.reshape(TOP_K, n), src_tok, gate[:, None], te, n_used.reshape(1)


def moe_ffn(hb, idx, wsel, w1, w3, w2):
    dest, src_tok, gate, te, n_used = moe_dispatch(idx, wsel)
    xs = jnp.take(hb, src_tok, axis=0)
    ys = moe_grouped(te, n_used, xs, w1, w3, w2, gate)
    return jnp.take(ys, dest[0], axis=0) + jnp.take(ys, dest[1], axis=0)


SEQ_ALL = CTX_LEN + SEQ
S5_T = 16
S5_NC = SEQ_ALL // S5_T
S5_NCC = CTX_LEN // S5_T
S5_W = S5_T * S5_CH


def _gelu_tanh(x):
    return x * (0.5 * (1.0 + jnp.tanh(math.sqrt(2.0 / math.pi) * (x + 0.044715 * (x * x * x)))))


def s5_chunk_operators(lam_re, lam_im, log_step, b_re, b_im, c_re, c_im):
    hp = lax.Precision.HIGHEST
    jj = jnp.arange(S5_T)[:, None]
    ii = jnp.arange(S5_T)[None, :]
    ms, es, fs, a16 = [], [], [], []
    for d in range(2):
        dt = jnp.exp(log_step[d])[:, None]
        lr, li = lam_re[d], lam_im[d]
        er = jnp.exp(lr * dt)
        abr, abi = er * jnp.cos(li * dt), er * jnp.sin(li * dt)
        nr, ni = abr - 1.0, abi
        den = lr * lr + li * li
        fr = (nr * lr + ni * li) / den
        fi = (ni * lr - nr * li) / den
        bbr = fr[..., None] * b_re[d] - fi[..., None] * b_im[d]
        bbi = fr[..., None] * b_im[d] + fi[..., None] * b_re[d]
        k = jnp.arange(S5_T + 1, dtype=F32)[:, None, None]
        mag, ang = jnp.exp(k * (lr * dt)), k * (li * dt)
        pr, pi = mag * jnp.cos(ang), mag * jnp.sin(ang)
        pbr = pr[..., None] * bbr - pi[..., None] * bbi
        pbi = pr[..., None] * bbi + pi[..., None] * bbr
        cr, ci = c_re[d], c_im[d]
        kk = (jnp.einsum('gop,lgpc->lgoc', cr, pbr[:S5_T], precision=hp)
              - jnp.einsum('gop,lgpc->lgoc', ci, pbi[:S5_T], precision=hp))
        cpr = cr[None] * pr[:, :, None, :] - ci[None] * pi[:, :, None, :]
        cpi = -(cr[None] * pi[:, :, None, :] + ci[None] * pr[:, :, None, :])
        if d == 0:
            lag, mask = ii - jj, ii >= jj
            e_pow = S5_T - 1 - jnp.arange(S5_T)
            f_pow = jnp.arange(S5_T) + 1
        else:
            lag, mask = jj - ii, jj >= ii
            e_pow = jnp.arange(S5_T)
            f_pow = S5_T - jnp.arange(S5_T)
        mm = jnp.where(mask[:, :, None, None, None], kk[jnp.clip(lag, 0, S5_T - 1)], 0.0)
        ms.append(mm.transpose(2, 0, 4, 1, 3).reshape(S5_G, S5_W, S5_W))
        for pb in (pbr, pbi):
            es.append(pb[e_pow].transpose(1, 0, 3, 2).reshape(S5_G, S5_W, S5_P))
        for cp in (cpr, cpi):
            fs.append(cp[f_pow].transpose(1, 3, 0, 2).reshape(S5_G, S5_P, S5_W))
        a16 += [pr[S5_T][:, None, :], pi[S5_T][:, None, :]]
    return (ms[0] + ms[1]).astype(BF16), jnp.stack(es).astype(BF16), jnp.stack(fs).astype(BF16), jnp.stack(a16)


def _s5_kernel(u_ref, m_ref, e_ref, f_ref, a_ref, o_ref, e_scr, s_scr):
    u = u_ref[...]
    for k in range(4):
        e_scr[k] = jnp.dot(u, e_ref[k], preferred_element_type=F32)
    arf, aif, arb, aib = (a_ref[k] for k in range(4))

    def segment(start, n, carry):
        def body(k, carry):
            out = []
            for b in range(BATCH):
                sfr, sfi, sbr, sbi = carry[4 * b:4 * b + 4]
                rf = b * S5_NC + start + k
                rb = b * S5_NC + start + n - 1 - k
                s_scr[0, pl.ds(rf, 1), :] = sfr
                s_scr[1, pl.ds(rf, 1), :] = sfi
                s_scr[2, pl.ds(rb, 1), :] = sbr
                s_scr[3, pl.ds(rb, 1), :] = sbi
                nfr = arf * sfr - aif * sfi + e_scr[0, pl.ds(rf, 1), :]
                nfi = arf * sfi + aif * sfr + e_scr[1, pl.ds(rf, 1), :]
                nbr = arb * sbr - aib * sbi + e_scr[2, pl.ds(rb, 1), :]
                nbi = arb * sbi + aib * sbr + e_scr[3, pl.ds(rb, 1), :]
                out += [nfr, nfi, nbr, nbi]
            return tuple(out)
        return lax.fori_loop(0, n, body, carry, unroll=4)

    z = jnp.zeros((1, S5_P), F32)
    carry = segment(0, S5_NCC, (z,) * (4 * BATCH))
    segment(S5_NCC, S5_NC - S5_NCC, carry)
    y = jnp.dot(u, m_ref[...], preferred_element_type=F32)
    for k in range(4):
        y += jnp.dot(s_scr[k].astype(BF16), f_ref[k], preferred_element_type=F32)
    o_ref[...] = y


def s5_scan(u, m, e, f, a16):
    g, r, w = u.shape
    return pl.pallas_call(
        _s5_kernel, grid=(g,),
        in_specs=[pl.BlockSpec((None, r, w), lambda i: (i, 0, 0)),
                  pl.BlockSpec((None, w, w), lambda i: (i, 0, 0)),
                  pl.BlockSpec((4, None, w, S5_P), lambda i: (0, i, 0, 0)),
                  pl.BlockSpec((4, None, S5_P, w), lambda i: (0, i, 0, 0)),
                  pl.BlockSpec((4, None, 1, S5_P), lambda i: (0, i, 0, 0))],
        out_specs=pl.BlockSpec((None, r, w), lambda i: (i, 0, 0)),
        out_shape=jax.ShapeDtypeStruct((g, r, w), F32),
        scratch_shapes=[pltpu.VMEM((4, r, S5_P), F32), pltpu.VMEM((4, r, S5_P), F32)],
        compiler_params=_cparams(("parallel",)), name="s5_scan",
    )(u, m, e, f, a16)


def _s5_out_kernel(y_ref, u_ref, d_ref, w_ref, b_ref, o_ref):
    g = _gelu_tanh(y_ref[...] + d_ref[...] * u_ref[...])
    z = jnp.dot(g.astype(BF16), w_ref[...], preferred_element_type=F32) + b_ref[...]
    o_ref[...] = (g * jax.nn.sigmoid(z)).astype(o_ref.dtype)


def s5_out(y, u, d_skip, w_glu, b_glu, tm):
    m, c = y.shape
    row = pl.BlockSpec((tm, c), lambda i: (i, 0))
    vec = pl.BlockSpec((1, c), lambda i: (0, 0))
    return pl.pallas_call(
        _s5_out_kernel, grid=(m // tm,),
        in_specs=[row, row, vec, pl.BlockSpec((c, c), lambda i: (0, 0)), vec],
        out_specs=row, out_shape=jax.ShapeDtypeStruct((m, c), BF16),
        compiler_params=_cparams(("parallel",)), name="s5_out",
    )(y, u, d_skip, w_glu, b_glu)


def s5_mixer(u_c, u_l, lam_re, lam_im, log_step, b_re, b_im, c_re, c_im, d_skip, w_glu, b_glu):
    bs = u_c.shape[0]
    u_all = jnp.concatenate([u_c, u_l], axis=1)
    uc = u_all.astype(BF16).reshape(bs, S5_NC, S5_T, S5_G, S5_CH).transpose(3, 0, 1, 2, 4)
    uc = uc.reshape(S5_G, bs * S5_NC, S5_W)
    y = s5_scan(uc, *s5_chunk_operators(lam_re, lam_im, log_step, b_re, b_im, c_re, c_im))
    y = y.reshape(S5_G, bs, S5_NC, S5_T, S5_CH).transpose(1, 2, 3, 0, 4).reshape(bs * SEQ_ALL, MIX_W)
    out = s5_out(y, u_all.reshape(bs * SEQ_ALL, MIX_W), d_skip[None], w_glu.astype(BF16), b_glu[None], 544)
    out = out.reshape(bs, SEQ_ALL, MIX_W)
    return out[:, :CTX_LEN], out[:, CTX_LEN:]


LRU_CB = 256
LRU_TB = 256
LRU_HALO = 8
LRU_ROWS = SEQ_ALL + 3 * LRU_HALO


def _lru_kernel(xc_ref, xl_ref, yc_ref, yl_ref, cw_ref, cb_ref, wa_ref, ba_ref, wi_ref, bi_ref, sp_ref,
                oc_ref, ol_ref, xpad, a_f, b_f, a_b, b_b):
    halo = jnp.zeros((LRU_HALO, LRU_CB), F32)
    lat0 = CTX_LEN + 2 * LRU_HALO
    xpad[0:LRU_HALO] = halo
    xpad[LRU_HALO:LRU_HALO + CTX_LEN] = xc_ref[...]
    xpad[LRU_HALO + CTX_LEN:lat0] = halo
    xpad[lat0:lat0 + SEQ] = xl_ref[...]
    xpad[lat0 + SEQ:LRU_ROWS] = halo
    cw = [cw_ref[j:j + 1, :] for j in range(CONV_W)]
    cb = cb_ref[...]

    def gates(p0, r0):
        ext = xpad[pl.ds(p0 - LRU_HALO, LRU_TB + 2 * LRU_HALO), :]
        xb = cb
        for j in range(CONV_W):
            s0 = LRU_HALO - CONV_PAD_L + j
            xb = xb + cw[j] * ext[s0:s0 + LRU_TB]
        for k in range(LRU_CB // LRU_BD):
            lanes = slice(k * LRU_BD, (k + 1) * LRU_BD)
            xk = xb[:, lanes]
            xkb = xk.astype(BF16)
            for d, (a_s, b_s) in enumerate(((a_f, b_f), (a_b, b_b))):
                r = jax.nn.sigmoid(jnp.dot(xkb, wa_ref[d, k], preferred_element_type=F32) + ba_ref[d][:, lanes])
                i = jax.nn.sigmoid(jnp.dot(xkb, wi_ref[d, k], preferred_element_type=F32) + bi_ref[d][:, lanes])
                log_a = -LRU_C * r * sp_ref[d][:, lanes]
                t = jnp.tanh(log_a)
                a_s[pl.ds(r0, LRU_TB), lanes] = jnp.exp(log_a)
                b_s[pl.ds(r0, LRU_TB), lanes] = jnp.sqrt(-2.0 * t / (1.0 - t)) * (i * xk)

    gates(LRU_HALO, 0)

    def lat_gates(k, _):
        gates(pl.multiple_of(lat0 + k * LRU_TB, 8), pl.multiple_of(CTX_LEN + k * LRU_TB, 8))
        return 0
    lax.fori_loop(0, SEQ // LRU_TB, lat_gates, 0)

    def segment(start, n, carry):
        def body(t, carry):
            hf, hb = carry
            rf = start + t
            rb = start + n - 1 - t
            hf = a_f[pl.ds(rf, 1), :] * hf + b_f[pl.ds(rf, 1), :]
            b_f[pl.ds(rf, 1), :] = hf
            hb = a_b[pl.ds(rb, 1), :] * hb + b_b[pl.ds(rb, 1), :]
            b_b[pl.ds(rb, 1), :] = hb
            return hf, hb
        return lax.fori_loop(0, n, body, carry, unroll=8)

    z = jnp.zeros((1, LRU_CB), F32)
    carry = segment(0, CTX_LEN, (z, z))
    segment(CTX_LEN, SEQ, carry)

    oc_ref[...] = (_gelu_tanh(yc_ref[...]) * (b_f[0:CTX_LEN] + b_b[0:CTX_LEN])).astype(oc_ref.dtype)

    def lat_out(k, _):
        r0 = pl.multiple_of(k * LRU_TB, 8)
        rows = pl.ds(pl.multiple_of(CTX_LEN + k * LRU_TB, 8), LRU_TB)
        h = b_f[rows, :] + b_b[rows, :]
        ol_ref[pl.ds(r0, LRU_TB), :] = (_gelu_tanh(yl_ref[pl.ds(r0, LRU_TB), :]) * h).astype(ol_ref.dtype)
        return 0
    lax.fori_loop(0, SEQ // LRU_TB, lat_out, 0)


def lru_mixer(y_c, x_c, y_l, x_l, conv_w, conv_b, w_a, b_a, w_i, b_i, lam):
    bs = x_c.shape[0]
    nb = LRU_CB // LRU_BD
    sp = jax.nn.softplus(-lam)[:, None, :]
    ctx_spec = pl.BlockSpec((None, CTX_LEN, LRU_CB), lambda b, c: (b, 0, c))
    lat_spec = pl.BlockSpec((None, SEQ, LRU_CB), lambda b, c: (b, 0, c))
    vec2 = pl.BlockSpec((2, 1, LRU_CB), lambda b, c: (0, 0, c))
    wspec = pl.BlockSpec((2, nb, LRU_BD, LRU_BD), lambda b, c: (0, c, 0, 0))
    scan_buf = pltpu.VMEM((SEQ_ALL, LRU_CB), F32)
    return pl.pallas_call(
        _lru_kernel, grid=(bs, MIX_W // LRU_CB),
        in_specs=[ctx_spec, lat_spec, ctx_spec, lat_spec,
                  pl.BlockSpec((CONV_W, LRU_CB), lambda b, c: (0, c)),
                  pl.BlockSpec((1, LRU_CB), lambda b, c: (0, c)),
                  wspec, vec2, wspec, vec2, vec2],
        out_specs=[ctx_spec, lat_spec],
        out_shape=[jax.ShapeDtypeStruct((bs, CTX_LEN, MIX_W), BF16), jax.ShapeDtypeStruct((bs, SEQ, MIX_W), BF16)],
        scratch_shapes=[pltpu.VMEM((LRU_ROWS, LRU_CB), F32), scan_buf, scan_buf, scan_buf, scan_buf],
        compiler_params=_cparams(("parallel", "parallel")), name="lru",
    )(x_c, x_l, y_c, y_l, conv_w, conv_b[None], w_a.astype(BF16), b_a[:, None, :], w_i.astype(BF16),
      b_i[:, None, :], sp)


def to_col_major(t):
    b, l = t.shape[:2]
    rows = l // GRID_W
    return jnp.swapaxes(t.reshape(b, rows, GRID_W, -1), 1, 2).reshape(t.shape)


def to_row_major(t):
    b, l = t.shape[:2]
    rows = l // GRID_W
    return jnp.swapaxes(t.reshape(b, GRID_W, rows, -1), 1, 2).reshape(t.shape)


N_CHUNKS = SEQ_ALL // CHUNK
N_CCH = CTX_LEN // CHUNK
GATE_COLBLK = 80
MIX_TB = 256


def _dot(a, b):
    return jnp.dot(a, b, preferred_element_type=F32)


def _dot_nt(a, b):
    return lax.dot_general(a, b, (((1,), (1,)), ((), ())), preferred_element_type=F32)


def _dot_tn(a, b):
    return lax.dot_general(a, b, (((0,), (0,)), ((), ())), preferred_element_type=F32)


def _split2(a):
    hi = a.astype(BF16)
    return hi, (a - hi.astype(F32)).astype(BF16)


def _dot3(a, b):
    ah, al = _split2(a)
    bh, bl = _split2(b)
    return _dot(ah, bh) + (_dot(ah, bl) + _dot(al, bh))


def _dot_mask(m, x):
    hi = x.astype(BF16)
    r1 = x - hi.astype(F32)
    mid = r1.astype(BF16)
    lo = (r1 - mid.astype(F32)).astype(BF16)
    return _dot(m, hi) + (_dot(m, mid) + _dot(m, lo))


def _chunk_masks(backward):
    i = lax.broadcasted_iota(jnp.int32, (CHUNK, CHUNK), 0)
    j = lax.broadcasted_iota(jnp.int32, (CHUNK, CHUNK), 1)
    if backward:
        i, j = j, i
    tri, strict = j <= i, j < i
    return tri, strict, tri.astype(BF16), jnp.where(strict, 1.0, 0.0).astype(F32)


def _lane_pick(blk, lane):
    lanes = lax.broadcasted_iota(jnp.int32, blk.shape, 1)
    return jnp.sum(jnp.where(lanes == lane, blk, 0.0), axis=-1, keepdims=True)


def _softplus(x):
    return jnp.maximum(x, 0.0) + jnp.log1p(jnp.exp(-jnp.abs(x)))


def _silu(x):
    return x * jax.nn.sigmoid(x)


def _bidir_chunks(step, carry):
    def segment(start, n, carry):
        def body(t, carry):
            return step(start + t, start + n - 1 - t, carry)
        return lax.fori_loop(0, n, body, carry)
    carry = segment(0, N_CCH, carry)
    return segment(N_CCH, N_CHUNKS - N_CCH, carry)


DN_HALO = 8
DN_ROWS = SEQ_ALL + 3 * DN_HALO


def _dn_kernel(qc_ref, ql_ref, kc_ref, kl_ref, vc_ref, vl_ref, zc_ref, zl_ref, gc_ref, gl_ref, hp_ref,
               cwq_ref, cwk_ref, cwv_ref, nw_ref, ol_ref, oc_ref,
               xpad, qn, kn, vn, u_s, w_s, qg_s, kd_s, at_s, dec_s):
    h = pl.program_id(1)
    halo = jnp.zeros((DN_HALO, DN_HD), F32)
    lat0 = CTX_LEN + 2 * DN_HALO

    def conv_into(xc_ref, xl_ref, cw_ref, dst, post):
        xpad[0:DN_HALO] = halo
        xpad[DN_HALO:DN_HALO + CTX_LEN] = xc_ref[...]
        xpad[DN_HALO + CTX_LEN:lat0] = halo
        xpad[lat0:lat0 + SEQ] = xl_ref[...]
        xpad[lat0 + SEQ:DN_ROWS] = halo
        cw = [cw_ref[j:j + 1, :] for j in range(CONV_W)]

        def block(p0, r0):
            ext = xpad[pl.ds(p0 - DN_HALO, MIX_TB + 2 * DN_HALO), :]
            s0 = DN_HALO - CONV_PAD_L
            acc = cw[0] * ext[s0:s0 + MIX_TB]
            for j in range(1, CONV_W):
                acc = acc + cw[j] * ext[s0 + j:s0 + j + MIX_TB]
            dst[pl.ds(r0, MIX_TB), :] = post(_silu(acc))

        block(DN_HALO, 0)

        def lat_block(k, _):
            block(pl.multiple_of(lat0 + k * MIX_TB, 8), pl.multiple_of(CTX_LEN + k * MIX_TB, 8))
            return 0
        lax.fori_loop(0, SEQ // MIX_TB, lat_block, 0)

    def l2n(scale):
        return lambda t: t * (lax.rsqrt(jnp.sum(t * t, axis=-1, keepdims=True) + EPS) * scale)

    conv_into(qc_ref, ql_ref, cwq_ref, qn, l2n(DN_HD ** -0.5))
    conv_into(kc_ref, kl_ref, cwk_ref, kn, l2n(1.0))
    conv_into(vc_ref, vl_ref, cwv_ref, vn, lambda t: t)

    masks = [_chunk_masks(False), _chunk_masks(True)]
    eye = jnp.where(lax.broadcasted_iota(jnp.int32, (CHUNK, CHUNK), 0)
                    == lax.broadcasted_iota(jnp.int32, (CHUNK, CHUNK), 1), 1.0, 0.0).astype(F32)

    def prep(c, g_ref, row0):
        rows = pl.ds(pl.multiple_of(c * CHUNK, CHUNK), CHUNK)
        gblk = g_ref[pl.ds(pl.multiple_of(c * CHUNK - row0, CHUNK), CHUNK), :]
        q, k, v = qn[rows, :], kn[rows, :], vn[rows, :]
        kb16 = k.astype(BF16)
        q16 = q.astype(BF16)
        for d in range(2):
            tri, strict, mm, nn = masks[d]
            a_col = _lane_pick(gblk, d * DN_H + h)
            b_col = _lane_pick(gblk, 2 * DN_H + d * DN_H + h)
            g_col = hp_ref[:, d:d + 1] * _softplus(a_col + hp_ref[:, 2 + d:3 + d])
            beta = jax.nn.sigmoid(b_col)
            gb = jnp.broadcast_to(g_col, (CHUNK, DN_HD))
            gcum = _dot_mask(mm, gb)
            dlt = _dot_mask(mm, g_col * nn)
            decay = jnp.where(tri, jnp.exp(jnp.where(tri, dlt, 0.0)), 0.0)
            tot = jnp.sum(gb, axis=0, keepdims=True)
            kb = k * beta
            lower = jnp.where(strict, _dot_nt(kb.astype(BF16), kb16) * decay, 0.0)
            x = -lower
            tinv = eye + x
            for _ in range(5):
                x = _dot3(x, x)
                tinv = tinv + _dot3(tinv, x)
            eg = jnp.exp(gcum)
            u_s[d, rows, :] = _dot3(tinv, v * beta)
            w_s[d, rows, :] = _dot3(tinv, kb * eg).astype(BF16)
            at_s[d, rows, :] = (_dot_nt(q16, kb16) * decay).astype(BF16)
            qg_s[d, rows, :] = (q * eg).astype(BF16)
            kd_s[d, rows, :] = (k * jnp.exp(tot - gcum)).astype(BF16)
            dec_s[d, pl.ds(pl.multiple_of(c * 8, 8), 8), :] = jnp.broadcast_to(jnp.exp(tot), (8, DN_HD))
        return 0
    lax.fori_loop(0, N_CCH, lambda c, _: prep(c, gc_ref, 0), 0, unroll=2)
    lax.fori_loop(N_CCH, N_CHUNKS, lambda c, _: prep(c, gl_ref, CTX_LEN), 0, unroll=2)
    o_f, o_b = qn, kn

    def advance(d, c, s, o_dst):
        rows = pl.ds(pl.multiple_of(c * CHUNK, CHUNK), CHUNK)
        s16 = s.astype(BF16)
        v_new = (u_s[d, rows, :] - _dot(w_s[d, rows, :], s16)).astype(BF16)
        o_dst[rows, :] = _dot(qg_s[d, rows, :], s16) + _dot(at_s[d, rows, :], v_new)
        g_last = dec_s[d, pl.ds(pl.multiple_of(c * 8, 8), 8), :][0:1]
        return s * g_last + _dot_tn(kd_s[d, rows, :], v_new)

    def step(cf, cb, carry):
        return advance(0, cf, carry[0], o_f), advance(1, cb, carry[1], o_b)

    z0 = jnp.zeros((DN_HD, DN_HD), F32)
    _bidir_chunks(step, (z0, z0))

    nw = nw_ref[...]

    def finish(rows, z):
        o = o_f[rows, :] + o_b[rows, :]
        o = o * lax.rsqrt(jnp.mean(o * o, axis=-1, keepdims=True) + EPS) * nw
        return (o * _silu(z)).astype(BF16)

    oc_ref[...] = finish(pl.ds(0, CTX_LEN), zc_ref[...])

    def lat_finish(k, _):
        r0 = pl.multiple_of(k * MIX_TB, 8)
        ol_ref[pl.ds(r0, MIX_TB), :] = finish(pl.ds(pl.multiple_of(CTX_LEN + k * MIX_TB, 8), MIX_TB),
                                              zl_ref[pl.ds(r0, MIX_TB), :])
        return 0
    lax.fori_loop(0, SEQ // MIX_TB, lat_finish, 0)


def deltanet_mixer(proj, conv_w, a_log, dt_bias, norm_w):
    hp = jnp.stack([-jnp.exp(a_log[0]), -jnp.exp(a_log[1]), dt_bias[0], dt_bias[1]], axis=-1)[:, None, :]
    ctx_blk0 = N_LAT // CTX_LEN

    def seg_specs(col0):
        cb = col0 // DN_HD
        return [pl.BlockSpec((CTX_LEN, DN_HD), lambda b, h: (ctx_blk0 + b, cb + h)),
                pl.BlockSpec((SEQ, DN_HD), lambda b, h: (b, cb + h))]

    def conv_spec(k):
        return pl.BlockSpec((CONV_W, DN_HD), lambda b, h: (0, k * DN_H + h))

    in_specs = (seg_specs(COL_DN_QKVZ) + seg_specs(COL_DN_QKVZ + MIX_W) + seg_specs(COL_DN_QKVZ + 2 * MIX_W)
                + seg_specs(COL_DN_QKVZ + 3 * MIX_W)
                + [pl.BlockSpec((CTX_LEN, 128), lambda b, h: (ctx_blk0 + b, GATE_COLBLK)),
                   pl.BlockSpec((SEQ, 128), lambda b, h: (b, GATE_COLBLK)),
                   pl.BlockSpec((None, 1, 4), lambda b, h: (h, 0, 0)),
                   conv_spec(0), conv_spec(1), conv_spec(2),
                   pl.BlockSpec((1, DN_HD), lambda b, h: (0, 0))])
    seq_f32 = pltpu.VMEM((SEQ_ALL, DN_HD), F32)
    dir_bf16 = pltpu.VMEM((2, SEQ_ALL, DN_HD), BF16)
    ol, oc = pl.pallas_call(
        _dn_kernel, grid=(BATCH, DN_H), in_specs=in_specs,
        out_specs=[pl.BlockSpec((SEQ, DN_HD), lambda b, h: (b, h)),
                   pl.BlockSpec((CTX_LEN, DN_HD), lambda b, h: (b, h))],
        out_shape=[jax.ShapeDtypeStruct((N_LAT, MIX_W), BF16), jax.ShapeDtypeStruct((N_CTX, MIX_W), BF16)],
        scratch_shapes=[pltpu.VMEM((DN_ROWS, DN_HD), F32), seq_f32, seq_f32, seq_f32,
                        pltpu.VMEM((2, SEQ_ALL, DN_HD), F32), dir_bf16, dir_bf16, dir_bf16,
                        pltpu.VMEM((2, SEQ_ALL, CHUNK), BF16), pltpu.VMEM((2, N_CHUNKS * 8, DN_HD), F32)],
        compiler_params=_cparams(("parallel", "arbitrary")), name="deltanet",
    )(*([proj] * 10), hp, conv_w, conv_w, conv_w, norm_w[None])
    return ol, oc


def _ml_kernel(qc_ref, ql_ref, kc_ref, kl_ref, vc_ref, vl_ref, oc_ref, ol_ref, gc_ref, gl_ref, hp_ref, nw_ref,
               yl_ref, yc_ref, h_f, h_b):
    hh = pl.program_id(1)
    masks = [_chunk_masks(False), _chunk_masks(True)]
    ones = jnp.ones((CHUNK, CHUNK), BF16)
    eye = jnp.where(lax.broadcasted_iota(jnp.int32, (CHUNK, CHUNK), 0)
                    == lax.broadcasted_iota(jnp.int32, (CHUNK, CHUNK), 1), 1.0, 0.0).astype(F32)

    def advance(d, c, state, refs, row0, h_dst):
        q_ref, k_ref, v_ref, g_ref = refs
        c_s, n_s, m = state
        rows = pl.ds(pl.multiple_of(c * CHUNK - row0, CHUNK), CHUNK)
        tri, _, mm, nn = masks[d]
        gblk = g_ref[rows, :]
        ig = _lane_pick(gblk, 4 * DN_H + d * ML_H + hh) + hp_ref[:, d:d + 1]
        lf = -_softplus(-(_lane_pick(gblk, 4 * DN_H + 2 * ML_H + d * ML_H + hh) + hp_ref[:, 2 + d:3 + d]))
        lb = jnp.broadcast_to(lf, (CHUNK, ML_DQK))
        bcum = _dot_mask(mm, lb)[:, 0:1]
        dmat = jnp.where(tri, _dot_mask(mm, lf * nn) + _dot_mask(ones, ig * eye), -jnp.inf)
        dmax = jnp.max(dmat, axis=-1, keepdims=True)
        tot = jnp.sum(lf, axis=0, keepdims=True)
        logw = tot - bcum + ig
        logw_max = jnp.max(logw, axis=0, keepdims=True)
        q = q_ref[rows, :] * (ML_DQK ** -0.5)
        k = k_ref[rows, :]
        q16, v16 = q.astype(BF16), v_ref[rows, :].astype(BF16)
        sqk = _dot_nt(q16, k.astype(BF16))
        inter = bcum + m
        m_t = jnp.maximum(inter, dmax)
        w_inter = jnp.exp(inter - m_t)
        wmat = jnp.exp(dmat - m_t) * sqk
        num = w_inter * _dot(q16, c_s.astype(BF16)) + _dot(wmat.astype(BF16), v16)
        den = w_inter * jnp.sum(q * n_s, axis=-1, keepdims=True) + jnp.sum(wmat, axis=-1, keepdims=True)
        h_dst[pl.ds(pl.multiple_of(c * CHUNK, CHUNK), CHUNK), :] = num / jnp.maximum(jnp.abs(den), jnp.exp(-m_t))
        m_new = jnp.maximum(tot + m, logw_max)
        keep = jnp.exp(tot + m - m_new)
        kw = k * jnp.exp(logw - m_new)
        return (keep * c_s + _dot_tn(kw.astype(BF16), v16), keep * n_s + jnp.sum(kw, axis=0, keepdims=True), m_new)

    ctx_refs = (qc_ref, kc_ref, vc_ref, gc_ref)
    lat_refs = (ql_ref, kl_ref, vl_ref, gl_ref)

    def segment(refs, start, n, carry):
        def body(t, carry):
            return (advance(0, start + t, carry[0], refs, start * CHUNK, h_f),
                    advance(1, start + n - 1 - t, carry[1], refs, start * CHUNK, h_b))
        return lax.fori_loop(0, n, body, carry)

    s0 = (jnp.zeros((ML_DQK, ML_DV), F32), jnp.zeros((1, ML_DQK), F32), jnp.zeros((1, 1), F32))
    carry = segment(ctx_refs, 0, N_CCH, (s0, s0))
    segment(lat_refs, N_CCH, N_CHUNKS - N_CCH, carry)

    nw = nw_ref[...]

    def finish(rows, o):
        hsum = h_f[rows, :] + h_b[rows, :]
        hsum = hsum * lax.rsqrt(jnp.mean(hsum * hsum, axis=-1, keepdims=True) + EPS) * nw
        return (hsum * jax.nn.sigmoid(o)).astype(BF16)

    yc_ref[...] = finish(pl.ds(0, CTX_LEN), oc_ref[...])

    def lat_finish(k, _):
        r0 = pl.multiple_of(k * MIX_TB, 8)
        yl_ref[pl.ds(r0, MIX_TB), :] = finish(pl.ds(pl.multiple_of(CTX_LEN + k * MIX_TB, 8), MIX_TB),
                                              ol_ref[pl.ds(r0, MIX_TB), :])
        return 0
    lax.fori_loop(0, SEQ // MIX_TB, lat_finish, 0)


def mlstm_mixer(proj, i_bias, f_bias, norm_w):
    hp = jnp.stack([i_bias[0], i_bias[1], f_bias[0], f_bias[1]], axis=-1)[:, None, :]
    ctx_blk0 = N_LAT // CTX_LEN

    def seg_specs(col0, w):
        cb = col0 // w
        return [pl.BlockSpec((CTX_LEN, w), lambda b, h: (ctx_blk0 + b, cb + h)),
                pl.BlockSpec((SEQ, w), lambda b, h: (b, cb + h))]

    in_specs = (seg_specs(COL_ML_Q, ML_DQK) + seg_specs(COL_ML_K, ML_DQK) + seg_specs(COL_ML_V, ML_DV)
                + seg_specs(COL_ML_O, ML_DV)
                + [pl.BlockSpec((CTX_LEN, 128), lambda b, h: (ctx_blk0 + b, GATE_COLBLK)),
                   pl.BlockSpec((SEQ, 128), lambda b, h: (b, GATE_COLBLK)),
                   pl.BlockSpec((None, 1, 4), lambda b, h: (h, 0, 0)),
                   pl.BlockSpec((1, ML_DV), lambda b, h: (0, h))])
    seq_f32 = pltpu.VMEM((SEQ_ALL, ML_DV), F32)
    yl, yc = pl.pallas_call(
        _ml_kernel, grid=(BATCH, ML_H), in_specs=in_specs,
        out_specs=[pl.BlockSpec((SEQ, ML_DV), lambda b, h: (b, h)),
                   pl.BlockSpec((CTX_LEN, ML_DV), lambda b, h: (b, h))],
        out_shape=[jax.ShapeDtypeStruct((N_LAT, MIX_W), BF16), jax.ShapeDtypeStruct((N_CTX, MIX_W), BF16)],
        scratch_shapes=[seq_f32, seq_f32],
        compiler_params=_cparams(("parallel", "arbitrary")), name="mlstm",
    )(*([proj] * 10), hp, norm_w[None])
    return yl, yc


PROJ_TN = 512
PROJ_PAD = -(-PROJ_W // PROJ_TN) * PROJ_TN
COL_S5, COL_DN_QKVZ, COL_ML_Q, COL_ML_K, COL_ML_V, COL_ML_O, COL_LRU_Y, COL_LRU_X = (
    0, 1024, 5120, 5632, 6144, 7168, 8192, 9216)
COL_DN_A, COL_DN_B, COL_ML_I, COL_ML_F = 10240, 10256, 10272, 10280


def permute_proj_weight(w):
    o = [0] + [int(v) for v in np.cumsum(PROJ_SIZES)]
    parts = [w[:, o[0]:o[5]], w[:, o[7]:o[11]], w[:, o[13]:o[15]], w[:, o[5]:o[7]], w[:, o[11]:o[13]]]
    return jnp.pad(jnp.concatenate(parts, axis=1).astype(BF16), ((0, 0), (0, PROJ_PAD - PROJ_W)))


def hybrid_mixer(proj, s5_p, dn_p, ml_p, lru_p):
    def seg(c0, w):
        blk = proj[:, c0:c0 + w]
        return blk[N_LAT:].reshape(BATCH, CTX_LEN, w), blk[:N_LAT].reshape(BATCH, SEQ, w)

    u_c, u_l = seg(COL_S5, MIX_W)
    a_c, a_l = s5_mixer(u_c, to_col_major(u_l), *s5_p)
    y_c, y_l = seg(COL_LRU_Y, MIX_W)
    x_c, x_l = seg(COL_LRU_X, MIX_W)
    d_c, d_l = lru_mixer(y_c, x_c, to_col_major(y_l), to_col_major(x_l), *lru_p)
    flat = lambda t_l, t_c: (to_row_major(t_l).reshape(N_LAT, MIX_W), t_c.reshape(N_CTX, MIX_W))
    return [flat(a_l, a_c), deltanet_mixer(proj, *dn_p), mlstm_mixer(proj, *ml_p), flat(d_l, d_c)]


def kernel(x, c, ctx, c_ctx, w_ada, b_ada, norm1, norm2, norm_f, w_in, w_out, s5_lam_re, s5_lam_im, s5_log_step,
           s5_b_re, s5_b_im, s5_c_re, s5_c_im, s5_d, s5_w_glu, s5_b_glu, dn_conv, dn_a_log, dn_dt_bias, dn_norm,
           ml_i_bias, ml_f_bias, ml_norm, lru_conv_w, lru_conv_b, lru_w_a, lru_b_a, lru_w_i, lru_b_i, lru_lam,
           router_w, router_bias, moe_w1, moe_w3, moe_w2):
    d = D_MODEL
    xa = jnp.concatenate([x.reshape(N_LAT, d), ctx.reshape(N_CTX, d)], axis=0)
    cond = jnp.concatenate([jax.nn.silu(c), jax.nn.silu(c_ctx)[None], jnp.zeros((16 - BATCH - 1, d), F32)], axis=0)
    cond = cond.astype(BF16)
    router = (router_w.T, router_bias.reshape(N_EXPERTS, 1))

    for l in range(DEPTH):
        last = l == DEPTH - 1
        mod = ada_matmul(cond, w_ada[l], b_ada[l][None], 512)[:BATCH + 1].reshape(BATCH + 1, 6, 1, d)
        shift1, scale1, gate1, shift2, scale2, gate2 = (mod[:, k] for k in range(6))

        h1 = norm_mod(xa, norm1[l][None], scale1, shift1, SEQ, 256)
        proj = matmul(h1, permute_proj_weight(w_in[l]), 512, PROJ_TN)
        slabs = hybrid_mixer(
            proj,
            (s5_lam_re[l], s5_lam_im[l], s5_log_step[l], s5_b_re[l], s5_b_im[l], s5_c_re[l], s5_c_im[l], s5_d[l],
             s5_w_glu[l], s5_b_glu[l]),
            (dn_conv[l], dn_a_log[l], dn_dt_bias[l], dn_norm[l]),
            (ml_i_bias[l], ml_f_bias[l], ml_norm[l]),
            (lru_conv_w[l], lru_conv_b[l], lru_w_a[l], lru_b_a[l], lru_w_i[l], lru_b_i[l], lru_lam[l]))
        if last:
            xa, mix = xa[:N_LAT], [s[0] for s in slabs]
        else:
            mix = [jnp.concatenate(s, axis=0) for s in slabs]
        xa = matmul_gated_residual(mix, w_out[l].astype(BF16), xa, gate1, SEQ, 512, 512)

        h2, idx, wsel = norm_mod(xa, norm2[l][None], scale2, shift2, SEQ, 256, router=router)
        y = moe_ffn(h2, idx, wsel, moe_w1[l].astype(BF16), moe_w3[l].astype(BF16), moe_w2[l].astype(BF16))
        upd = (xa[:N_LAT].reshape(BATCH, SEQ, d) + gate2[:BATCH] * y[:N_LAT].reshape(BATCH, SEQ, d)).reshape(N_LAT, d)
        if last:
            xa = upd
        else:
            xa = jnp.concatenate([upd, xa[N_LAT:] + gate2[BATCH] * y[N_LAT:]], axis=0)

    return final_norm(xa[:N_LAT], norm_f[None], 256).reshape(BATCH, SEQ, d)
```

```python
import functools
import math

import numpy as np
import jax
import jax.numpy as jnp
from jax import lax
from jax.experimental import pallas as pl
from jax.experimental.pallas import tpu as pltpu

D_MODEL = 4096
BATCH = 2
SEQ = 4096
DEPTH = 2
GRID_W = 64
CTX_LEN = 256
MIX_W = D_MODEL // 4
MIX_TOTAL = 4 * MIX_W
CHUNK = 64
CONV_W = 4
CONV_PAD_L = 2
CONV_PAD_R = 1
EPS = 1e-6
S5_CH = 16
S5_G = MIX_W // S5_CH
S5_P = 64
DN_HD = 128
DN_H = MIX_W // DN_HD
ML_H = 4
ML_DV = MIX_W // ML_H
ML_DQK = ML_DV // 2
LRU_BLOCKS = 8
LRU_BD = MIX_W // LRU_BLOCKS
LRU_C = 8.0
N_EXPERTS = 16
N_GROUPS = 4
E_PER_G = N_EXPERTS // N_GROUPS
TOP_K = 2
D_FF = D_MODEL // 4
PROJ_SIZES = (MIX_W, MIX_W, MIX_W, MIX_W, MIX_W, 2 * DN_H, 2 * DN_H, ML_H * ML_DQK, ML_H * ML_DQK, MIX_W, MIX_W,
              2 * ML_H, 2 * ML_H, MIX_W, MIX_W)
PROJ_W = sum(PROJ_SIZES)

N_LAT = BATCH * SEQ
N_CTX = BATCH * CTX_LEN
VMEM_LIMIT = 56 * 1024 * 1024

F32 = jnp.float32
BF16 = jnp.bfloat16


def _cparams(sem):
    return pltpu.CompilerParams(dimension_semantics=sem, vmem_limit_bytes=VMEM_LIMIT)


def _mm_kernel(a_ref, b_ref, o_ref):
    o_ref[...] = jnp.dot(a_ref[...], b_ref[...], preferred_element_type=F32).astype(o_ref.dtype)


def matmul(a, b, tm, tn, out_dtype=F32):
    m, k = a.shape
    n = b.shape[1]
    assert m % tm == 0 and n % tn == 0
    return pl.pallas_call(
        _mm_kernel,
        grid=(m // tm, n // tn),
        in_specs=[pl.BlockSpec((tm, k), lambda i, j: (i, 0)),
                  pl.BlockSpec((k, tn), lambda i, j: (0, j))],
        out_specs=pl.BlockSpec((tm, tn), lambda i, j: (i, j)),
        out_shape=jax.ShapeDtypeStruct((m, n), out_dtype),
        compiler_params=_cparams(("parallel", "arbitrary")),
        name="mm",
    )(a, b)


def _mm_resid_kernel(a0_ref, a1_ref, a2_ref, a3_ref, b_ref, r_ref, g_ref, o_ref):
    acc = None
    for s, a_ref in enumerate((a0_ref, a1_ref, a2_ref, a3_ref)):
        part = jnp.dot(a_ref[...], b_ref[s * MIX_W:(s + 1) * MIX_W, :], preferred_element_type=F32)
        acc = part if acc is None else acc + part
    o_ref[...] = r_ref[...] + g_ref[...] * acc


def matmul_gated_residual(a_slabs, b, resid, gate, rows_per_gate, tm, tn):
    m = a_slabs[0].shape[0]
    k, n = b.shape
    assert m % tm == 0 and n % tn == 0 and rows_per_gate % tm == 0 and k == 4 * MIX_W
    return pl.pallas_call(
        _mm_resid_kernel,
        grid=(m // tm, n // tn),
        in_specs=[pl.BlockSpec((tm, MIX_W), lambda i, j: (i, 0))] * 4 + [
                  pl.BlockSpec((k, tn), lambda i, j: (0, j)),
                  pl.BlockSpec((tm, tn), lambda i, j: (i, j)),
                  pl.BlockSpec((None, 1, tn), lambda i, j: ((i * tm) // rows_per_gate, 0, j))],
        out_specs=pl.BlockSpec((tm, tn), lambda i, j: (i, j)),
        out_shape=jax.ShapeDtypeStruct((m, n), F32),
        compiler_params=_cparams(("parallel", "arbitrary")),
        name="mm_resid",
    )(*a_slabs, b, resid, gate)


def _ada_kernel(a_ref, w_ref, b_ref, o_ref):
    o_ref[...] = jnp.dot(a_ref[...], w_ref[...].astype(BF16), preferred_element_type=F32) + b_ref[...]


def ada_matmul(a, w, bias, tn):
    m, k = a.shape
    n = w.shape[1]
    return pl.pallas_call(
        _ada_kernel,
        grid=(n // tn,),
        in_specs=[pl.BlockSpec((m, k), lambda j: (0, 0)),
                  pl.BlockSpec((k, tn), lambda j: (0, j)),
                  pl.BlockSpec((1, tn), lambda j: (0, j))],
        out_specs=pl.BlockSpec((m, tn), lambda j: (0, j)),
        out_shape=jax.ShapeDtypeStruct((m, n), F32),
        compiler_params=_cparams(("arbitrary",)),
        name="ada",
    )(a, w, bias)


def _modulated_norm(x, g, sc, sh):
    ms = jnp.mean(x * x, axis=-1, keepdims=True)
    y = x * lax.rsqrt(ms + EPS) * g
    return y * (1.0 + sc) + sh


def _norm_mod_kernel(x_ref, g_ref, sc_ref, sh_ref, o_ref):
    o_ref[...] = _modulated_norm(x_ref[...], g_ref[...], sc_ref[...], sh_ref[...]).astype(o_ref.dtype)


def _pair_top2_sum(v):
    a, b, c, d = v
    return jnp.maximum(jnp.maximum(jnp.maximum(a + b, a + c), jnp.maximum(a + d, b + c)),
                       jnp.maximum(b + d, c + d))


def _norm_mod_router_kernel(x_ref, g_ref, sc_ref, sh_ref, rwt_ref, rb_ref, o_ref, idx_ref, w_ref):
    h = _modulated_norm(x_ref[...], g_ref[...], sc_ref[...], sh_ref[...])
    o_ref[...] = h.astype(o_ref.dtype)
    logits = lax.dot_general(rwt_ref[...], h, (((1,), (1,)), ((), ())),
                             precision=lax.Precision.HIGHEST, preferred_element_type=F32)
    scores = jax.nn.sigmoid(logits)
    biased = scores + rb_ref[...]
    s = [scores[e:e + 1, :] for e in range(N_EXPERTS)]
    b = [biased[e:e + 1, :] for e in range(N_EXPERTS)]
    gs = [_pair_top2_sum(b[E_PER_G * g:E_PER_G * (g + 1)]) for g in range(N_GROUPS)]
    best, gsel = gs[0], jnp.zeros_like(gs[0], dtype=jnp.int32)
    for g in range(1, N_GROUPS):
        better = gs[g] > best
        gsel = jnp.where(better, g, gsel)
        best = jnp.where(better, gs[g], best)
    vb, vs = [], []
    for k in range(E_PER_G):
        bk, sk = b[k], s[k]
        for g in range(1, N_GROUPS):
            bk = jnp.where(gsel == g, b[E_PER_G * g + k], bk)
            sk = jnp.where(gsel == g, s[E_PER_G * g + k], sk)
        vb.append(bk)
        vs.append(sk)
    m1, i1, w1 = vb[0], jnp.zeros_like(gsel), vs[0]
    for k in range(1, E_PER_G):
        better = vb[k] > m1
        i1 = jnp.where(better, k, i1)
        w1 = jnp.where(better, vs[k], w1)
        m1 = jnp.where(better, vb[k], m1)
    m2, i2, w2, have = vb[0], jnp.zeros_like(gsel), vs[0], i1 != 0
    for k in range(1, E_PER_G):
        valid = i1 != k
        better = valid & (jnp.logical_not(have) | (vb[k] > m2))
        i2 = jnp.where(better, k, i2)
        w2 = jnp.where(better, vs[k], w2)
        m2 = jnp.where(better, vb[k], m2)
        have = have | valid
    tot = w1 + w2
    idx_ref[0:1, :] = gsel * E_PER_G + i1
    idx_ref[1:2, :] = gsel * E_PER_G + i2
    w_ref[0:1, :] = w1 / tot
    w_ref[1:2, :] = w2 / tot


def norm_mod(x, g, sc, sh, rows_per_mod, tm, router=None):
    m, d = x.shape
    assert m % tm == 0 and rows_per_mod % tm == 0
    mod_spec = pl.BlockSpec((None, 1, d), lambda i: ((i * tm) // rows_per_mod, 0, 0))
    in_specs = [pl.BlockSpec((tm, d), lambda i: (i, 0)), pl.BlockSpec((1, d), lambda i: (0, 0)), mod_spec, mod_spec]
    o_spec = pl.BlockSpec((tm, d), lambda i: (i, 0))
    o_shape = jax.ShapeDtypeStruct((m, d), BF16)
    if router is None:
        return pl.pallas_call(
            _norm_mod_kernel, grid=(m // tm,), in_specs=in_specs, out_specs=o_spec, out_shape=o_shape,
            compiler_params=_cparams(("parallel",)), name="norm_mod",
        )(x, g, sc, sh)
    rwt, rb = router
    in_specs += [pl.BlockSpec((N_EXPERTS, d), lambda i: (0, 0)), pl.BlockSpec((N_EXPERTS, 1), lambda i: (0, 0))]
    sel_spec = pl.BlockSpec((TOP_K, tm), lambda i: (0, i))
    return pl.pallas_call(
        _norm_mod_router_kernel, grid=(m // tm,), in_specs=in_specs,
        out_specs=[o_spec, sel_spec, sel_spec],
        out_shape=[o_shape, jax.ShapeDtypeStruct((TOP_K, m), jnp.int32), jax.ShapeDtypeStruct((TOP_K, m), F32)],
        compiler_params=_cparams(("parallel",)), name="norm_mod_router",
    )(x, g, sc, sh, rwt, rb)


def _final_norm_kernel(x_ref, g_ref, o_ref):
    x = x_ref[...]
    ms = jnp.mean(x * x, axis=-1, keepdims=True)
    o_ref[...] = x * lax.rsqrt(ms + EPS) * g_ref[...]


def final_norm(x, g, tm):
    m, d = x.shape
    return pl.pallas_call(
        _final_norm_kernel, grid=(m // tm,),
        in_specs=[pl.BlockSpec((tm, d), lambda i: (i, 0)), pl.BlockSpec((1, d), lambda i: (0, 0))],
        out_specs=pl.BlockSpec((tm, d), lambda i: (i, 0)),
        out_shape=jax.ShapeDtypeStruct((m, d), F32),
        compiler_params=_cparams(("parallel",)), name="final_norm",
    )(x, g)


MOE_TM = 512
MOE_FC = 128


def _moe_kernel(te_ref, nu_ref, x_ref, w1_ref, w3_ref, w2_ref, g_ref, o_ref):
    t = pl.program_id(0)
    f = pl.program_id(1)
    used = t < nu_ref[0]

    @pl.when(used)
    def _():
        x = x_ref[...]
        hg = jnp.dot(x, w1_ref[...].astype(BF16), preferred_element_type=F32)
        hu = jnp.dot(x, w3_ref[...].astype(BF16), preferred_element_type=F32)
        act = (hg * jax.nn.sigmoid(hg)) * hu * g_ref[...]
        y = jnp.dot(act.astype(BF16), w2_ref[...].astype(BF16), preferred_element_type=F32)

        @pl.when(f == 0)
        def _():
            o_ref[...] = y

        @pl.when(f > 0)
        def _():
            o_ref[...] += y

    @pl.when(jnp.logical_not(used) & (f == 0))
    def _():
        o_ref[...] = jnp.zeros_like(o_ref)


def moe_grouped(tile_expert, n_used, xs, w1, w3, w2, gs):
    r, d = xs.shape
    nf = D_FF // MOE_FC

    def fidx(t, f, nu):
        return jnp.where(t < nu[0], f, nf - 1)

    grid_spec = pltpu.PrefetchScalarGridSpec(
        num_scalar_prefetch=2,
        grid=(r // MOE_TM, nf),
        in_specs=[pl.BlockSpec((MOE_TM, d), lambda t, f, te, nu: (t, 0)),
                  pl.BlockSpec((None, d, MOE_FC), lambda t, f, te, nu: (te[t], 0, fidx(t, f, nu))),
                  pl.BlockSpec((None, d, MOE_FC), lambda t, f, te, nu: (te[t], 0, fidx(t, f, nu))),
                  pl.BlockSpec((None, MOE_FC, d), lambda t, f, te, nu: (te[t], fidx(t, f, nu), 0)),
                  pl.BlockSpec((MOE_TM, 1), lambda t, f, te, nu: (t, 0))],
        out_specs=pl.BlockSpec((MOE_TM, d), lambda t, f, te, nu: (t, 0)),
    )
    return pl.pallas_call(
        _moe_kernel, grid_spec=grid_spec,
        out_shape=jax.ShapeDtypeStruct((r, d), F32),
        compiler_params=_cparams(("arbitrary", "arbitrary")), name="moe",
    )(tile_expert, n_used, xs, w1, w3, w2, gs)


def moe_dispatch(idx, wsel):
    n = idx.shape[1]
    e_flat = idx.reshape(-1)
    tok = jnp.tile(jnp.arange(n, dtype=jnp.int32), TOP_K)
    onehot = (e_flat[:, None] == jnp.arange(N_EXPERTS, dtype=jnp.int32)[None, :]).astype(jnp.int32)
    csum = jnp.cumsum(onehot, axis=0)
    counts = csum[-1]
    rank = jnp.take_along_axis(csum, e_flat[:, None], axis=1)[:, 0] - 1
    padded = ((counts + MOE_TM - 1) // MOE_TM) * MOE_TM
    seg_end = jnp.cumsum(padded)
    seg_start = seg_end - padded
    dest = seg_start[e_flat] + rank
    rows = TOP_K * n + N_EXPERTS * MOE_TM
    src_tok = jnp.zeros((rows,), jnp.int32).at[dest].set(tok)
    gate = jnp.zeros((rows,), F32).at[dest].set(wsel.reshape(-1))
    n_tiles = rows // MOE_TM
    n_used = (seg_end[-1] // MOE_TM).astype(jnp.int32)
    tile_start = jnp.arange(n_tiles, dtype=jnp.int32) * MOE_TM
    te = jnp.minimum(jnp.searchsorted(seg_end, tile_start, side='right'), N_EXPERTS - 1).astype(jnp.int32)
    te = jnp.where(jnp.arange(n_tiles) < n_used, te, te[jnp.maximum(n_used - 1, 0)])
    return dest.reshape(TOP_K, n), src_tok, gate[:, None], te, n_used.reshape(1)


def moe_ffn(hb, idx, wsel, w1, w3, w2):
    dest, src_tok, gate, te, n_used = moe_dispatch(idx, wsel)
    xs = jnp.take(hb, src_tok, axis=0)
    ys = moe_grouped(te, n_used, xs, w1, w3, w2, gate)
    return jnp.take(ys, dest[0], axis=0) + jnp.take(ys, dest[1], axis=0)


SEQ_ALL = CTX_LEN + SEQ
S5_T = 16
S5_NC = SEQ_ALL // S5_T
S5_NCC = CTX_LEN // S5_T
S5_W = S5_T * S5_CH


def _gelu_tanh(x):
    return x * (0.5 * (1.0 + jnp.tanh(math.sqrt(2.0 / math.pi) * (x + 0.044715 * (x * x * x)))))


def s5_chunk_operators(lam_re, lam_im, log_step, b_re, b_im, c_re, c_im):
    hp = lax.Precision.HIGHEST
    jj = jnp.arange(S5_T)[:, None]
    ii = jnp.arange(S5_T)[None, :]
    ms, es, fs, a16 = [], [], [], []
    for d in range(2):
        dt = jnp.exp(log_step[d])[:, None]
        lr, li = lam_re[d], lam_im[d]
        er = jnp.exp(lr * dt)
        abr, abi = er * jnp.cos(li * dt), er * jnp.sin(li * dt)
        nr, ni = abr - 1.0, abi
        den = lr * lr + li * li
        fr = (nr * lr + ni * li) / den
        fi = (ni * lr - nr * li) / den
        bbr = fr[..., None] * b_re[d] - fi[..., None] * b_im[d]
        bbi = fr[..., None] * b_im[d] + fi[..., None] * b_re[d]
        k = jnp.arange(S5_T + 1, dtype=F32)[:, None, None]
        mag, ang = jnp.exp(k * (lr * dt)), k * (li * dt)
        pr, pi = mag * jnp.cos(ang), mag * jnp.sin(ang)
        pbr = pr[..., None] * bbr - pi[..., None] * bbi
        pbi = pr[..., None] * bbi + pi[..., None] * bbr
        cr, ci = c_re[d], c_im[d]
        kk = (jnp.einsum('gop,lgpc->lgoc', cr, pbr[:S5_T], precision=hp)
              - jnp.einsum('gop,lgpc->lgoc', ci, pbi[:S5_T], precision=hp))
        cpr = cr[None] * pr[:, :, None, :] - ci[None] * pi[:, :, None, :]
        cpi = -(cr[None] * pi[:, :, None, :] + ci[None] * pr[:, :, None, :])
        if d == 0:
            lag, mask = ii - jj, ii >= jj
            e_pow = S5_T - 1 - jnp.arange(S5_T)
            f_pow = jnp.arange(S5_T) + 1
        else:
            lag, mask = jj - ii, jj >= ii
            e_pow = jnp.arange(S5_T)
            f_pow = S5_T - jnp.arange(S5_T)
        mm = jnp.where(mask[:, :, None, None, None], kk[jnp.clip(lag, 0, S5_T - 1)], 0.0)
        ms.append(mm.transpose(2, 0, 4, 1, 3).reshape(S5_G, S5_W, S5_W))
        for pb in (pbr, pbi):
            es.append(pb[e_pow].transpose(1, 0, 3, 2).reshape(S5_G, S5_W, S5_P))
        for cp in (cpr, cpi):
            fs.append(cp[f_pow].transpose(1, 3, 0, 2).reshape(S5_G, S5_P, S5_W))
        a16 += [pr[S5_T][:, None, :], pi[S5_T][:, None, :]]
    return (ms[0] + ms[1]).astype(BF16), jnp.stack(es).astype(BF16), jnp.stack(fs).astype(BF16), jnp.stack(a16)


def _s5_kernel(u_ref, m_ref, e_ref, f_ref, a_ref, o_ref, e_scr, s_scr):
    u = u_ref[...]
    for k in range(4):
        e_scr[k] = jnp.dot(u, e_ref[k], preferred_element_type=F32)
    arf, aif, arb, aib = (a_ref[k] for k in range(4))

    def segment(start, n, carry):
        def body(k, carry):
            out = []
            for b in range(BATCH):
                sfr, sfi, sbr, sbi = carry[4 * b:4 * b + 4]
                rf = b * S5_NC + start + k
                rb = b * S5_NC + start + n - 1 - k
                s_scr[0, pl.ds(rf, 1), :] = sfr
                s_scr[1, pl.ds(rf, 1), :] = sfi
                s_scr[2, pl.ds(rb, 1), :] = sbr
                s_scr[3, pl.ds(rb, 1), :] = sbi
                nfr = arf * sfr - aif * sfi + e_scr[0, pl.ds(rf, 1), :]
                nfi = arf * sfi + aif * sfr + e_scr[1, pl.ds(rf, 1), :]
                nbr = arb * sbr - aib * sbi + e_scr[2, pl.ds(rb, 1), :]
                nbi = arb * sbi + aib * sbr + e_scr[3, pl.ds(rb, 1), :]
                out += [nfr, nfi, nbr, nbi]
            return tuple(out)
        return lax.fori_loop(0, n, body, carry, unroll=4)

    z = jnp.zeros((1, S5_P), F32)
    carry = segment(0, S5_NCC, (z,) * (4 * BATCH))
    segment(S5_NCC, S5_NC - S5_NCC, carry)
    y = jnp.dot(u, m_ref[...], preferred_element_type=F32)
    for k in range(4):
        y += jnp.dot(s_scr[k].astype(BF16), f_ref[k], preferred_element_type=F32)
    o_ref[...] = y


def s5_scan(u, m, e, f, a16):
    g, r, w = u.shape
    return pl.pallas_call(
        _s5_kernel, grid=(g,),
        in_specs=[pl.BlockSpec((None, r, w), lambda i: (i, 0, 0)),
                  pl.BlockSpec((None, w, w), lambda i: (i, 0, 0)),
                  pl.BlockSpec((4, None, w, S5_P), lambda i: (0, i, 0, 0)),
                  pl.BlockSpec((4, None, S5_P, w), lambda i: (0, i, 0, 0)),
                  pl.BlockSpec((4, None, 1, S5_P), lambda i: (0, i, 0, 0))],
        out_specs=pl.BlockSpec((None, r, w), lambda i: (i, 0, 0)),
        out_shape=jax.ShapeDtypeStruct((g, r, w), F32),
        scratch_shapes=[pltpu.VMEM((4, r, S5_P), F32), pltpu.VMEM((4, r, S5_P), F32)],
        compiler_params=_cparams(("parallel",)), name="s5_scan",
    )(u, m, e, f, a16)


def _s5_out_kernel(y_ref, u_ref, d_ref, w_ref, b_ref, o_ref):
    g = _gelu_tanh(y_ref[...] + d_ref[...] * u_ref[...])
    z = jnp.dot(g.astype(BF16), w_ref[...], preferred_element_type=F32) + b_ref[...]
    o_ref[...] = (g * jax.nn.sigmoid(z)).astype(o_ref.dtype)


def s5_out(y, u, d_skip, w_glu, b_glu, tm):
    m, c = y.shape
    row = pl.BlockSpec((tm, c), lambda i: (i, 0))
    vec = pl.BlockSpec((1, c), lambda i: (0, 0))
    return pl.pallas_call(
        _s5_out_kernel, grid=(m // tm,),
        in_specs=[row, row, vec, pl.BlockSpec((c, c), lambda i: (0, 0)), vec],
        out_specs=row, out_shape=jax.ShapeDtypeStruct((m, c), BF16),
        compiler_params=_cparams(("parallel",)), name="s5_out",
    )(y, u, d_skip, w_glu, b_glu)


def s5_mixer(u_c, u_l, lam_re, lam_im, log_step, b_re, b_im, c_re, c_im, d_skip, w_glu, b_glu):
    bs = u_c.shape[0]
    u_all = jnp.concatenate([u_c, u_l], axis=1)
    uc = u_all.astype(BF16).reshape(bs, S5_NC, S5_T, S5_G, S5_CH).transpose(3, 0, 1, 2, 4)
    uc = uc.reshape(S5_G, bs * S5_NC, S5_W)
    y = s5_scan(uc, *s5_chunk_operators(lam_re, lam_im, log_step, b_re, b_im, c_re, c_im))
    y = y.reshape(S5_G, bs, S5_NC, S5_T, S5_CH).transpose(1, 2, 3, 0, 4).reshape(bs * SEQ_ALL, MIX_W)
    out = s5_out(y, u_all.reshape(bs * SEQ_ALL, MIX_W), d_skip[None], w_glu.astype(BF16), b_glu[None], 544)
    out = out.reshape(bs, SEQ_ALL, MIX_W)
    return out[:, :CTX_LEN], out[:, CTX_LEN:]


LRU_CB = 256
LRU_TB = 256
LRU_HALO = 8
LRU_ROWS = SEQ_ALL + 3 * LRU_HALO


def _lru_kernel(xc_ref, xl_ref, yc_ref, yl_ref, cw_ref, cb_ref, wa_ref, ba_ref, wi_ref, bi_ref, sp_ref,
                oc_ref, ol_ref, xpad, a_f, b_f, a_b, b_b):
    halo = jnp.zeros((LRU_HALO, LRU_CB), F32)
    lat0 = CTX_LEN + 2 * LRU_HALO
    xpad[0:LRU_HALO] = halo
    xpad[LRU_HALO:LRU_HALO + CTX_LEN] = xc_ref[...]
    xpad[LRU_HALO + CTX_LEN:lat0] = halo
    xpad[lat0:lat0 + SEQ] = xl_ref[...]
    xpad[lat0 + SEQ:LRU_ROWS] = halo
    cw = [cw_ref[j:j + 1, :] for j in range(CONV_W)]
    cb = cb_ref[...]

    def gates(p0, r0):
        ext = xpad[pl.ds(p0 - LRU_HALO, LRU_TB + 2 * LRU_HALO), :]
        xb = cb
        for j in range(CONV_W):
            s0 = LRU_HALO - CONV_PAD_L + j
            xb = xb + cw[j] * ext[s0:s0 + LRU_TB]
        for k in range(LRU_CB // LRU_BD):
            lanes = slice(k * LRU_BD, (k + 1) * LRU_BD)
            xk = xb[:, lanes]
            xkb = xk.astype(BF16)
            for d, (a_s, b_s) in enumerate(((a_f, b_f), (a_b, b_b))):
                r = jax.nn.sigmoid(jnp.dot(xkb, wa_ref[d, k], preferred_element_type=F32) + ba_ref[d][:, lanes])
                i = jax.nn.sigmoid(jnp.dot(xkb, wi_ref[d, k], preferred_element_type=F32) + bi_ref[d][:, lanes])
                log_a = -LRU_C * r * sp_ref[d][:, lanes]
                t = jnp.tanh(log_a)
                a_s[pl.ds(r0, LRU_TB), lanes] = jnp.exp(log_a)
                b_s[pl.ds(r0, LRU_TB), lanes] = jnp.sqrt(-2.0 * t / (1.0 - t)) * (i * xk)

    gates(LRU_HALO, 0)

    def lat_gates(k, _):
        gates(pl.multiple_of(lat0 + k * LRU_TB, 8), pl.multiple_of(CTX_LEN + k * LRU_TB, 8))
        return 0
    lax.fori_loop(0, SEQ // LRU_TB, lat_gates, 0)

    def segment(start, n, carry):
        def body(t, carry):
            hf, hb = carry
            rf = start + t
            rb = start + n - 1 - t
            hf = a_f[pl.ds(rf, 1), :] * hf + b_f[pl.ds(rf, 1), :]
            b_f[pl.ds(rf, 1), :] = hf
            hb = a_b[pl.ds(rb, 1), :] * hb + b_b[pl.ds(rb, 1), :]
            b_b[pl.ds(rb, 1), :] = hb
            return hf, hb
        return lax.fori_loop(0, n, body, carry, unroll=8)

    z = jnp.zeros((1, LRU_CB), F32)
    carry = segment(0, CTX_LEN, (z, z))
    segment(CTX_LEN, SEQ, carry)

    oc_ref[...] = (_gelu_tanh(yc_ref[...]) * (b_f[0:CTX_LEN] + b_b[0:CTX_LEN])).astype(oc_ref.dtype)

    def lat_out(k, _):
        r0 = pl.multiple_of(k * LRU_TB, 8)
        rows = pl.ds(pl.multiple_of(CTX_LEN + k * LRU_TB, 8), LRU_TB)
        h = b_f[rows, :] + b_b[rows, :]
        ol_ref[pl.ds(r0, LRU_TB), :] = (_gelu_tanh(yl_ref[pl.ds(r0, LRU_TB), :]) * h).astype(ol_ref.dtype)
        return 0
    lax.fori_loop(0, SEQ // LRU_TB, lat_out, 0)


def lru_mixer(y_c, x_c, y_l, x_l, conv_w, conv_b, w_a, b_a, w_i, b_i, lam):
    bs = x_c.shape[0]
    nb = LRU_CB // LRU_BD
    sp = jax.nn.softplus(-lam)[:, None, :]
    ctx_spec = pl.BlockSpec((None, CTX_LEN, LRU_CB), lambda b, c: (b, 0, c))
    lat_spec = pl.BlockSpec((None, SEQ, LRU_CB), lambda b, c: (b, 0, c))
    vec2 = pl.BlockSpec((2, 1, LRU_CB), lambda b, c: (0, 0, c))
    wspec = pl.BlockSpec((2, nb, LRU_BD, LRU_BD), lambda b, c: (0, c, 0, 0))
    scan_buf = pltpu.VMEM((SEQ_ALL, LRU_CB), F32)
    return pl.pallas_call(
        _lru_kernel, grid=(bs, MIX_W // LRU_CB),
        in_specs=[ctx_spec, lat_spec, ctx_spec, lat_spec,
                  pl.BlockSpec((CONV_W, LRU_CB), lambda b, c: (0, c)),
                  pl.BlockSpec((1, LRU_CB), lambda b, c: (0, c)),
                  wspec, vec2, wspec, vec2, vec2],
        out_specs=[ctx_spec, lat_spec],
        out_shape=[jax.ShapeDtypeStruct((bs, CTX_LEN, MIX_W), BF16), jax.ShapeDtypeStruct((bs, SEQ, MIX_W), BF16)],
        scratch_shapes=[pltpu.VMEM((LRU_ROWS, LRU_CB), F32), scan_buf, scan_buf, scan_buf, scan_buf],
        compiler_params=_cparams(("parallel", "parallel")), name="lru",
    )(x_c, x_l, y_c, y_l, conv_w, conv_b[None], w_a.astype(BF16), b_a[:, None, :], w_i.astype(BF16),
      b_i[:, None, :], sp)


def to_col_major(t):
    b, l = t.shape[:2]
    rows = l // GRID_W
    return jnp.swapaxes(t.reshape(b, rows, GRID_W, -1), 1, 2).reshape(t.shape)


def to_row_major(t):
    b, l = t.shape[:2]
    rows = l // GRID_W
    return jnp.swapaxes(t.reshape(b, GRID_W, rows, -1), 1, 2).reshape(t.shape)


N_CHUNKS = SEQ_ALL // CHUNK
N_CCH = CTX_LEN // CHUNK
GATE_COLBLK = 80
MIX_TB = 256


def _dot(a, b):
    return jnp.dot(a, b, preferred_element_type=F32)


def _dot_nt(a, b):
    return lax.dot_general(a, b, (((1,), (1,)), ((), ())), preferred_element_type=F32)


def _dot_tn(a, b):
    return lax.dot_general(a, b, (((0,), (0,)), ((), ())), preferred_element_type=F32)


def _split2(a):
    hi = a.astype(BF16)
    return hi, (a - hi.astype(F32)).astype(BF16)


def _dot3(a, b):
    ah, al = _split2(a)
    bh, bl = _split2(b)
    return _dot(ah, bh) + (_dot(ah, bl) + _dot(al, bh))


def _dot_mask(m, x):
    hi = x.astype(BF16)
    r1 = x - hi.astype(F32)
    mid = r1.astype(BF16)
    lo = (r1 - mid.astype(F32)).astype(BF16)
    return _dot(m, hi) + (_dot(m, mid) + _dot(m, lo))


def _chunk_masks(backward):
    i = lax.broadcasted_iota(jnp.int32, (CHUNK, CHUNK), 0)
    j = lax.broadcasted_iota(jnp.int32, (CHUNK, CHUNK), 1)
    if backward:
        i, j = j, i
    tri, strict = j <= i, j < i
    return tri, strict, tri.astype(BF16), jnp.where(strict, 1.0, 0.0).astype(F32)


def _lane_pick(blk, lane):
    lanes = lax.broadcasted_iota(jnp.int32, blk.shape, 1)
    return jnp.sum(jnp.where(lanes == lane, blk, 0.0), axis=-1, keepdims=True)


def _softplus(x):
    return jnp.maximum(x, 0.0) + jnp.log1p(jnp.exp(-jnp.abs(x)))


def _silu(x):
    return x * jax.nn.sigmoid(x)


def _bidir_chunks(step, carry):
    def segment(start, n, carry):
        def body(t, carry):
            return step(start + t, start + n - 1 - t, carry)
        return lax.fori_loop(0, n, body, carry)
    carry = segment(0, N_CCH, carry)
    return segment(N_CCH, N_CHUNKS - N_CCH, carry)


DN_HALO = 8
DN_ROWS = SEQ_ALL + 3 * DN_HALO
DN_GROUP = 4 * CHUNK


def _dn_kernel(qc_ref, ql_ref, kc_ref, kl_ref, vc_ref, vl_ref, zc_ref, zl_ref, gc_ref, gl_ref, hp_ref,
               cwq_ref, cwk_ref, cwv_ref, nw_ref, ol_ref, oc_ref,
               xpad, qn, kn, vn, u_s, w_s, qg_s, kd_s, at_s, dec_s):
    h = pl.program_id(1)
    halo = jnp.zeros((DN_HALO, DN_HD), F32)
    lat0 = CTX_LEN + 2 * DN_HALO

    def conv_into(xc_ref, xl_ref, cw_ref, dst, post):
        xpad[0:DN_HALO] = halo
        xpad[DN_HALO:DN_HALO + CTX_LEN] = xc_ref[...]
        xpad[DN_HALO + CTX_LEN:lat0] = halo
        xpad[lat0:lat0 + SEQ] = xl_ref[...]
        xpad[lat0 + SEQ:DN_ROWS] = halo
        cw = [cw_ref[j:j + 1, :] for j in range(CONV_W)]

        def block(p0, r0):
            ext = xpad[pl.ds(p0 - DN_HALO, MIX_TB + 2 * DN_HALO), :]
            s0 = DN_HALO - CONV_PAD_L
            acc = cw[0] * ext[s0:s0 + MIX_TB]
            for j in range(1, CONV_W):
                acc = acc + cw[j] * ext[s0 + j:s0 + j + MIX_TB]
            dst[pl.ds(r0, MIX_TB), :] = post(_silu(acc))

        block(DN_HALO, 0)

        def lat_block(k, _):
            block(pl.multiple_of(lat0 + k * MIX_TB, 8), pl.multiple_of(CTX_LEN + k * MIX_TB, 8))
            return 0
        lax.fori_loop(0, SEQ // MIX_TB, lat_block, 0)

    def l2n(scale):
        return lambda t: t * (lax.rsqrt(jnp.sum(t * t, axis=-1, keepdims=True) + EPS) * scale)

    conv_into(qc_ref, ql_ref, cwq_ref, qn, l2n(DN_HD ** -0.5))
    conv_into(kc_ref, kl_ref, cwk_ref, kn, l2n(1.0))
    conv_into(vc_ref, vl_ref, cwv_ref, vn, lambda t: t)

    gi = lax.broadcasted_iota(jnp.int32, (DN_GROUP, DN_GROUP), 0)
    gj = lax.broadcasted_iota(jnp.int32, (DN_GROUP, DN_GROUP), 1)
    same = (gi // CHUNK) == (gj // CHUNK)
    blk16 = (gi // 16) == (gj // 16)
    blk32 = (gi // 32) == (gj // 32)
    bwd = ((gi // CHUNK) % 2) == 1
    fwd = jnp.logical_not(bwd)
    li, lj = gi % CHUNK, gj % CHUNK
    tri = same & ((fwd & (lj <= li)) | (bwd & (lj >= li)))
    strict = same & ((fwd & (lj < li)) | (bwd & (lj > li)))
    eye = jnp.where(gi == gj, 1.0, 0.0).astype(F32)
    same16 = jnp.where(same, 1.0, 0.0).astype(BF16)
    m16 = jnp.where(tri, 1.0, 0.0).astype(BF16)
    nn = jnp.where(strict, 1.0, 0.0).astype(F32)

    def both_dirs(xf, xb):
        return jnp.concatenate([xf[:CHUNK], xb[:CHUNK], xf[CHUNK:], xb[CHUNK:]], axis=0)

    def prep(t, g_ref, row0):
        r0 = pl.multiple_of(t * (2 * CHUNK), 2 * CHUNK)
        rows = pl.ds(r0, 2 * CHUNK)
        gblk = g_ref[pl.ds(pl.multiple_of(t * (2 * CHUNK) - row0, 2 * CHUNK), 2 * CHUNK), :]
        q2, k2, v2 = qn[rows, :], kn[rows, :], vn[rows, :]
        q, k, v = both_dirs(q2, q2), both_dirs(k2, k2), both_dirs(v2, v2)
        g_dir = [hp_ref[:, d:d + 1] * _softplus(_lane_pick(gblk, d * DN_H + h) + hp_ref[:, 2 + d:3 + d])
                 for d in range(2)]
        b_dir = [jax.nn.sigmoid(_lane_pick(gblk, 2 * DN_H + d * DN_H + h)) for d in range(2)]
        g_col, beta = both_dirs(*g_dir), both_dirs(*b_dir)
        k16, q16 = k.astype(BF16), q.astype(BF16)
        gb = jnp.broadcast_to(g_col, (DN_GROUP, DN_HD))
        gcum = _dot_mask(m16, gb)
        tot = _dot_mask(same16, gb)
        dlt = _dot_mask(m16, g_col * nn)
        decay = jnp.where(tri, jnp.exp(jnp.where(tri, dlt, 0.0)), 0.0)
        kb = k * beta
        lower = jnp.where(strict, _dot_nt(kb.astype(BF16), k16) * decay, 0.0)
        x = -jnp.where(blk16, lower, 0.0)
        tinv = eye + x
        for _ in range(3):
            x = _dot3(x, x)
            tinv = tinv + _dot3(tinv, x)
        for inner, outer in ((blk16, blk32), (blk32, same)):
            off = jnp.where(outer & jnp.logical_not(inner), lower, 0.0)
            tinv = tinv - _dot3(tinv, _dot3(off, tinv))
        eg = jnp.exp(gcum)
        sol = _dot3(tinv, jnp.concatenate([v * beta, kb * eg], axis=1))
        attn = (_dot_nt(q16, k16) * decay).astype(BF16)
        w16 = sol[:, DN_HD:].astype(BF16)
        qg16 = (q * eg).astype(BF16)
        kd16 = (k * jnp.exp(tot - gcum)).astype(BF16)
        dec = jnp.exp(tot)
        for p in range(DN_GROUP // CHUNK):
            d, blk = p % 2, slice(p * CHUNK, (p + 1) * CHUNK)
            dst = pl.ds(r0 + (p // 2) * CHUNK, CHUNK)
            u_s[d, dst, :] = sol[blk, :DN_HD]
            w_s[d, dst, :] = w16[blk]
            at_s[d, dst, :] = attn[blk, blk]
            qg_s[d, dst, :] = qg16[blk]
            kd_s[d, dst, :] = kd16[blk]
            dec_s[d, pl.ds(pl.multiple_of((2 * t + p // 2) * 8, 8), 8), :] = dec[p * CHUNK:p * CHUNK + 8]
        return 0
    lax.fori_loop(0, N_CCH // 2, lambda t, _: prep(t, gc_ref, 0), 0)
    lax.fori_loop(N_CCH // 2, N_CHUNKS // 2, lambda t, _: prep(t, gl_ref, CTX_LEN), 0, unroll=2)
    o_f, o_b = qn, kn

    def advance(d, c, s, o_dst):
        rows = pl.ds(pl.multiple_of(c * CHUNK, CHUNK), CHUNK)
        s16 = s.astype(BF16)
        v_new = (u_s[d, rows, :] - _dot(w_s[d, rows, :], s16)).astype(BF16)
        o_dst[rows, :] = _dot(qg_s[d, rows, :], s16) + _dot(at_s[d, rows, :], v_new)
        g_last = dec_s[d, pl.ds(pl.multiple_of(c * 8, 8), 8), :][0:1]
        return s * g_last + _dot_tn(kd_s[d, rows, :], v_new)

    def step(cf, cb, carry):
        return advance(0, cf, carry[0], o_f), advance(1, cb, carry[1], o_b)

    z0 = jnp.zeros((DN_HD, DN_HD), F32)
    _bidir_chunks(step, (z0, z0))

    nw = nw_ref[...]

    def finish(rows, z):
        o = o_f[rows, :] + o_b[rows, :]
        o = o * lax.rsqrt(jnp.mean(o * o, axis=-1, keepdims=True) + EPS) * nw
        return (o * _silu(z)).astype(BF16)

    oc_ref[...] = finish(pl.ds(0, CTX_LEN), zc_ref[...])

    def lat_finish(k, _):
        r0 = pl.multiple_of(k * MIX_TB, 8)
        ol_ref[pl.ds(r0, MIX_TB), :] = finish(pl.ds(pl.multiple_of(CTX_LEN + k * MIX_TB, 8), MIX_TB),
                                              zl_ref[pl.ds(r0, MIX_TB), :])
        return 0
    lax.fori_loop(0, SEQ // MIX_TB, lat_finish, 0)


def deltanet_mixer(proj, conv_w, a_log, dt_bias, norm_w):
    hp = jnp.stack([-jnp.exp(a_log[0]), -jnp.exp(a_log[1]), dt_bias[0], dt_bias[1]], axis=-1)[:, None, :]
    ctx_blk0 = N_LAT // CTX_LEN

    def seg_specs(col0):
        cb = col0 // DN_HD
        return [pl.BlockSpec((CTX_LEN, DN_HD), lambda b, h: (ctx_blk0 + b, cb + h)),
                pl.BlockSpec((SEQ, DN_HD), lambda b, h: (b, cb + h))]

    def conv_spec(k):
        return pl.BlockSpec((CONV_W, DN_HD), lambda b, h: (0, k * DN_H + h))

    in_specs = (seg_specs(COL_DN_QKVZ) + seg_specs(COL_DN_QKVZ + MIX_W) + seg_specs(COL_DN_QKVZ + 2 * MIX_W)
                + seg_specs(COL_DN_QKVZ + 3 * MIX_W)
                + [pl.BlockSpec((CTX_LEN, 128), lambda b, h: (ctx_blk0 + b, GATE_COLBLK)),
                   pl.BlockSpec((SEQ, 128), lambda b, h: (b, GATE_COLBLK)),
                   pl.BlockSpec((None, 1, 4), lambda b, h: (h, 0, 0)),
                   conv_spec(0), conv_spec(1), conv_spec(2),
                   pl.BlockSpec((1, DN_HD), lambda b, h: (0, 0))])
    seq_f32 = pltpu.VMEM((SEQ_ALL, DN_HD), F32)
    dir_bf16 = pltpu.VMEM((2, SEQ_ALL, DN_HD), BF16)
    ol, oc = pl.pallas_call(
        _dn_kernel, grid=(BATCH, DN_H), in_specs=in_specs,
        out_specs=[pl.BlockSpec((SEQ, DN_HD), lambda b, h: (b, h)),
                   pl.BlockSpec((CTX_LEN, DN_HD), lambda b, h: (b, h))],
        out_shape=[jax.ShapeDtypeStruct((N_LAT, MIX_W), BF16), jax.ShapeDtypeStruct((N_CTX, MIX_W), BF16)],
        scratch_shapes=[pltpu.VMEM((DN_ROWS, DN_HD), F32), seq_f32, seq_f32, seq_f32,
                        pltpu.VMEM((2, SEQ_ALL, DN_HD), F32), dir_bf16, dir_bf16, dir_bf16,
                        pltpu.VMEM((2, SEQ_ALL, CHUNK), BF16), pltpu.VMEM((2, N_CHUNKS * 8, DN_HD), F32)],
        compiler_params=_cparams(("parallel", "arbitrary")), name="deltanet",
    )(*([proj] * 10), hp, conv_w, conv_w, conv_w, norm_w[None])
    return ol, oc


def _ml_kernel(qc_ref, ql_ref, kc_ref, kl_ref, vc_ref, vl_ref, oc_ref, ol_ref, gc_ref, gl_ref, hp_ref, nw_ref,
               yl_ref, yc_ref, h_f, h_b):
    hh = pl.program_id(1)
    masks = [_chunk_masks(False), _chunk_masks(True)]
    ones = jnp.ones((CHUNK, CHUNK), BF16)
    eye = jnp.where(lax.broadcasted_iota(jnp.int32, (CHUNK, CHUNK), 0)
                    == lax.broadcasted_iota(jnp.int32, (CHUNK, CHUNK), 1), 1.0, 0.0).astype(F32)

    def advance(d, c, state, refs, row0, h_dst):
        q_ref, k_ref, v_ref, g_ref = refs
        c_s, n_s, m = state
        rows = pl.ds(pl.multiple_of(c * CHUNK - row0, CHUNK), CHUNK)
        tri, _, mm, nn = masks[d]
        gblk = g_ref[rows, :]
        ig = _lane_pick(gblk, 4 * DN_H + d * ML_H + hh) + hp_ref[:, d:d + 1]
        lf = -_softplus(-(_lane_pick(gblk, 4 * DN_H + 2 * ML_H + d * ML_H + hh) + hp_ref[:, 2 + d:3 + d]))
        lb = jnp.broadcast_to(lf, (CHUNK, ML_DQK))
        bcum = _dot_mask(mm, lb)[:, 0:1]
        dmat = jnp.where(tri, _dot_mask(mm, lf * nn) + _dot_mask(ones, ig * eye), -jnp.inf)
        dmax = jnp.max(dmat, axis=-1, keepdims=True)
        tot = jnp.sum(lf, axis=0, keepdims=True)
        logw = tot - bcum + ig
        logw_max = jnp.max(logw, axis=0, keepdims=True)
        q = q_ref[rows, :] * (ML_DQK ** -0.5)
        k = k_ref[rows, :]
        q16, v16 = q.astype(BF16), v_ref[rows, :].astype(BF16)
        sqk = _dot_nt(q16, k.astype(BF16))
        inter = bcum + m
        m_t = jnp.maximum(inter, dmax)
        w_inter = jnp.exp(inter - m_t)
        wmat = jnp.exp(dmat - m_t) * sqk
        num = w_inter * _dot(q16, c_s.astype(BF16)) + _dot(wmat.astype(BF16), v16)
        den = w_inter * jnp.sum(q * n_s, axis=-1, keepdims=True) + jnp.sum(wmat, axis=-1, keepdims=True)
        h_dst[pl.ds(pl.multiple_of(c * CHUNK, CHUNK), CHUNK), :] = num / jnp.maximum(jnp.abs(den), jnp.exp(-m_t))
        m_new = jnp.maximum(tot + m, logw_max)
        keep = jnp.exp(tot + m - m_new)
        kw = k * jnp.exp(logw - m_new)
        return (keep * c_s + _dot_tn(kw.astype(BF16), v16), keep * n_s + jnp.sum(kw, axis=0, keepdims=True), m_new)

    ctx_refs = (qc_ref, kc_ref, vc_ref, gc_ref)
    lat_refs = (ql_ref, kl_ref, vl_ref, gl_ref)

    def segment(refs, start, n, carry):
        def body(t, carry):
            return (advance(0, start + t, carry[0], refs, start * CHUNK, h_f),
                    advance(1, start + n - 1 - t, carry[1], refs, start * CHUNK, h_b))
        return lax.fori_loop(0, n, body, carry)

    s0 = (jnp.zeros((ML_DQK, ML_DV), F32), jnp.zeros((1, ML_DQK), F32), jnp.zeros((1, 1), F32))
    carry = segment(ctx_refs, 0, N_CCH, (s0, s0))
    segment(lat_refs, N_CCH, N_CHUNKS - N_CCH, carry)

    nw = nw_ref[...]

    def finish(rows, o):
        hsum = h_f[rows, :] + h_b[rows, :]
        hsum = hsum * lax.rsqrt(jnp.mean(hsum * hsum, axis=-1, keepdims=True) + EPS) * nw
        return (hsum * jax.nn.sigmoid(o)).astype(BF16)

    yc_ref[...] = finish(pl.ds(0, CTX_LEN), oc_ref[...])

    def lat_finish(k, _):
        r0 = pl.multiple_of(k * MIX_TB, 8)
        yl_ref[pl.ds(r0, MIX_TB), :] = finish(pl.ds(pl.multiple_of(CTX_LEN + k * MIX_TB, 8), MIX_TB),
                                              ol_ref[pl.ds(r0, MIX_TB), :])
        return 0
    lax.fori_loop(0, SEQ // MIX_TB, lat_finish, 0)


def mlstm_mixer(proj, i_bias, f_bias, norm_w):
    hp = jnp.stack([i_bias[0], i_bias[1], f_bias[0], f_bias[1]], axis=-1)[:, None, :]
    ctx_blk0 = N_LAT // CTX_LEN

    def seg_specs(col0, w):
        cb = col0 // w
        return [pl.BlockSpec((CTX_LEN, w), lambda b, h: (ctx_blk0 + b, cb + h)),
                pl.BlockSpec((SEQ, w), lambda b, h: (b, cb + h))]

    in_specs = (seg_specs(COL_ML_Q, ML_DQK) + seg_specs(COL_ML_K, ML_DQK) + seg_specs(COL_ML_V, ML_DV)
                + seg_specs(COL_ML_O, ML_DV)
                + [pl.BlockSpec((CTX_LEN, 128), lambda b, h: (ctx_blk0 + b, GATE_COLBLK)),
                   pl.BlockSpec((SEQ, 128), lambda b, h: (b, GATE_COLBLK)),
                   pl.BlockSpec((None, 1, 4), lambda b, h: (h, 0, 0)),
                   pl.BlockSpec((1, ML_DV), lambda b, h: (0, h))])
    seq_f32 = pltpu.VMEM((SEQ_ALL, ML_DV), F32)
    yl, yc = pl.pallas_call(
        _ml_kernel, grid=(BATCH, ML_H), in_specs=in_specs,
        out_specs=[pl.BlockSpec((SEQ, ML_DV), lambda b, h: (b, h)),
                   pl.BlockSpec((CTX_LEN, ML_DV), lambda b, h: (b, h))],
        out_shape=[jax.ShapeDtypeStruct((N_LAT, MIX_W), BF16), jax.ShapeDtypeStruct((N_CTX, MIX_W), BF16)],
        scratch_shapes=[seq_f32, seq_f32],
        compiler_params=_cparams(("parallel", "arbitrary")), name="mlstm",
    )(*([proj] * 10), hp, norm_w[None])
    return yl, yc


PROJ_TN = 512
PROJ_PAD = -(-PROJ_W // PROJ_TN) * PROJ_TN
COL_S5, COL_DN_QKVZ, COL_ML_Q, COL_ML_K, COL_ML_V, COL_ML_O, COL_LRU_Y, COL_LRU_X = (
    0, 1024, 5120, 5632, 6144, 7168, 8192, 9216)
COL_DN_A, COL_DN_B, COL_ML_I, COL_ML_F = 10240, 10256, 10272, 10280


def permute_proj_weight(w):
    o = [0] + [int(v) for v in np.cumsum(PROJ_SIZES)]
    parts = [w[:, o[0]:o[5]], w[:, o[7]:o[11]], w[:, o[13]:o[15]], w[:, o[5]:o[7]], w[:, o[11]:o[13]]]
    return jnp.pad(jnp.concatenate(parts, axis=1).astype(BF16), ((0, 0), (0, PROJ_PAD - PROJ_W)))


def hybrid_mixer(proj, s5_p, dn_p, ml_p, lru_p):
    def seg(c0, w):
        blk = proj[:, c0:c0 + w]
        return blk[N_LAT:].reshape(BATCH, CTX_LEN, w), blk[:N_LAT].reshape(BATCH, SEQ, w)

    u_c, u_l = seg(COL_S5, MIX_W)
    a_c, a_l = s5_mixer(u_c, to_col_major(u_l), *s5_p)
    y_c, y_l = seg(COL_LRU_Y, MIX_W)
    x_c, x_l = seg(COL_LRU_X, MIX_W)
    d_c, d_l = lru_mixer(y_c, x_c, to_col_major(y_l), to_col_major(x_l), *lru_p)
    flat = lambda t_l, t_c: (to_row_major(t_l).reshape(N_LAT, MIX_W), t_c.reshape(N_CTX, MIX_W))
    return [flat(a_l, a_c), deltanet_mixer(proj, *dn_p), mlstm_mixer(proj, *ml_p), flat(d_l, d_c)]


def kernel(x, c, ctx, c_ctx, w_ada, b_ada, norm1, norm2, norm_f, w_in, w_out, s5_lam_re, s5_lam_im, s5_log_step,
           s5_b_re, s5_b_im, s5_c_re, s5_c_im, s5_d, s5_w_glu, s5_b_glu, dn_conv, dn_a_log, dn_dt_bias, dn_norm,
           ml_i_bias, ml_f_bias, ml_norm, lru_conv_w, lru_conv_b, lru_w_a, lru_b_a, lru_w_i, lru_b_i, lru_lam,
           router_w, router_bias, moe_w1, moe_w3, moe_w2):
    d = D_MODEL
    xa = jnp.concatenate([x.reshape(N_LAT, d), ctx.reshape(N_CTX, d)], axis=0)
    cond = jnp.concatenate([jax.nn.silu(c), jax.nn.silu(c_ctx)[None], jnp.zeros((16 - BATCH - 1, d), F32)], axis=0)
    cond = cond.astype(BF16)
    router = (router_w.T, router_bias.reshape(N_EXPERTS, 1))

    for l in range(DEPTH):
        last = l == DEPTH - 1
        mod = ada_matmul(cond, w_ada[l], b_ada[l][None], 512)[:BATCH + 1].reshape(BATCH + 1, 6, 1, d)
        shift1, scale1, gate1, shift2, scale2, gate2 = (mod[:, k] for k in range(6))

        h1 = norm_mod(xa, norm1[l][None], scale1, shift1, SEQ, 256)
        proj = matmul(h1, permute_proj_weight(w_in[l]), 512, PROJ_TN)
        slabs = hybrid_mixer(
            proj,
            (s5_lam_re[l], s5_lam_im[l], s5_log_step[l], s5_b_re[l], s5_b_im[l], s5_c_re[l], s5_c_im[l], s5_d[l],
             s5_w_glu[l], s5_b_glu[l]),
            (dn_conv[l], dn_a_log[l], dn_dt_bias[l], dn_norm[l]),
            (ml_i_bias[l], ml_f_bias[l], ml_norm[l]),
            (lru_conv_w[l], lru_conv_b[l], lru_w_a[l], lru_b_a[l], lru_w_i[l], lru_b_i[l], lru_lam[l]))
        if last:
            xa, mix = xa[:N_LAT], [s[0] for s in slabs]
        else:
            mix = [jnp.concatenate(s, axis=0) for s in slabs]
        xa = matmul_gated_residual(mix, w_out[l].astype(BF16), xa, gate1, SEQ, 512, 512)

        h2, idx, wsel = norm_mod(xa, norm2[l][None], scale2, shift2, SEQ, 256, router=router)
        y = moe_ffn(h2, idx, wsel, moe_w1[l], moe_w3[l], moe_w2[l])
        upd = (xa[:N_LAT].reshape(BATCH, SEQ, d) + gate2[:BATCH] * y[:N_LAT].reshape(BATCH, SEQ, d)).reshape(N_LAT, d)
        if last:
            xa = upd
        else:
            xa = jnp.concatenate([upd, xa[N_LAT:] + gate2[BATCH] * y[N_LAT:]], axis=0)

    return final_norm(xa[:N_LAT], norm_f[None], 256).reshape(BATCH, SEQ, d)
```

```python
import functools
import math

import numpy as np
import jax
import jax.numpy as jnp
from jax import lax
from jax.experimental import pallas as pl
from jax.experimental.pallas import tpu as pltpu

D_MODEL = 4096
BATCH = 2
SEQ = 4096
DEPTH = 2
GRID_W = 64
CTX_LEN = 256
MIX_W = D_MODEL // 4
MIX_TOTAL = 4 * MIX_W
CHUNK = 64
CONV_W = 4
CONV_PAD_L = 2
CONV_PAD_R = 1
EPS = 1e-6
S5_CH = 16
S5_G = MIX_W // S5_CH
S5_P = 64
DN_HD = 128
DN_H = MIX_W // DN_HD
ML_H = 4
ML_DV = MIX_W // ML_H
ML_DQK = ML_DV // 2
LRU_BLOCKS = 8
LRU_BD = MIX_W // LRU_BLOCKS
LRU_C = 8.0
N_EXPERTS = 16
N_GROUPS = 4
E_PER_G = N_EXPERTS // N_GROUPS
TOP_K = 2
D_FF = D_MODEL // 4
PROJ_SIZES = (MIX_W, MIX_W, MIX_W, MIX_W, MIX_W, 2 * DN_H, 2 * DN_H, ML_H * ML_DQK, ML_H * ML_DQK, MIX_W, MIX_W,
              2 * ML_H, 2 * ML_H, MIX_W, MIX_W)
PROJ_W = sum(PROJ_SIZES)

N_LAT = BATCH * SEQ
N_CTX = BATCH * CTX_LEN
VMEM_LIMIT = 56 * 1024 * 1024

F32 = jnp.float32
BF16 = jnp.bfloat16


def _cparams(sem):
    return pltpu.CompilerParams(dimension_semantics=sem, vmem_limit_bytes=VMEM_LIMIT)


def _mm_kernel(a_ref, b_ref, o_ref):
    o_ref[...] = jnp.dot(a_ref[...], b_ref[...], preferred_element_type=F32).astype(o_ref.dtype)


def matmul(a, b, tm, tn, out_dtype=F32):
    m, k = a.shape
    n = b.shape[1]
    assert m % tm == 0 and n % tn == 0
    return pl.pallas_call(
        _mm_kernel,
        grid=(m // tm, n // tn),
        in_specs=[pl.BlockSpec((tm, k), lambda i, j: (i, 0)),
                  pl.BlockSpec((k, tn), lambda i, j: (0, j))],
        out_specs=pl.BlockSpec((tm, tn), lambda i, j: (i, j)),
        out_shape=jax.ShapeDtypeStruct((m, n), out_dtype),
        compiler_params=_cparams(("parallel", "arbitrary")),
        name="mm",
    )(a, b)


def _mm_resid_kernel(a0_ref, a1_ref, a2_ref, a3_ref, b_ref, r_ref, g_ref, o_ref):
    acc = None
    for s, a_ref in enumerate((a0_ref, a1_ref, a2_ref, a3_ref)):
        part = jnp.dot(a_ref[...], b_ref[s * MIX_W:(s + 1) * MIX_W, :], preferred_element_type=F32)
        acc = part if acc is None else acc + part
    o_ref[...] = r_ref[...] + g_ref[...] * acc


def matmul_gated_residual(a_slabs, b, resid, gate, rows_per_gate, tm, tn):
    m = a_slabs[0].shape[0]
    k, n = b.shape
    assert m % tm == 0 and n % tn == 0 and rows_per_gate % tm == 0 and k == 4 * MIX_W
    return pl.pallas_call(
        _mm_resid_kernel,
        grid=(m // tm, n // tn),
        in_specs=[pl.BlockSpec((tm, MIX_W), lambda i, j: (i, 0))] * 4 + [
                  pl.BlockSpec((k, tn), lambda i, j: (0, j)),
                  pl.BlockSpec((tm, tn), lambda i, j: (i, j)),
                  pl.BlockSpec((None, 1, tn), lambda i, j: ((i * tm) // rows_per_gate, 0, j))],
        out_specs=pl.BlockSpec((tm, tn), lambda i, j: (i, j)),
        out_shape=jax.ShapeDtypeStruct((m, n), F32),
        compiler_params=_cparams(("parallel", "arbitrary")),
        name="mm_resid",
    )(*a_slabs, b, resid, gate)


def _ada_kernel(a_ref, w_ref, b_ref, o_ref):
    o_ref[...] = jnp.dot(a_ref[...], w_ref[...].astype(BF16), preferred_element_type=F32) + b_ref[...]


def ada_matmul(layer, a, w, bias, tn):
    m, k = a.shape
    n = w.shape[2]
    return pl.pallas_call(
        _ada_kernel,
        grid=(n // tn,),
        in_specs=[pl.BlockSpec((m, k), lambda j: (0, 0)),
                  pl.BlockSpec((None, k, tn), lambda j: (layer, 0, j)),
                  pl.BlockSpec((None, 1, tn), lambda j: (layer, 0, j))],
        out_specs=pl.BlockSpec((m, tn), lambda j: (0, j)),
        out_shape=jax.ShapeDtypeStruct((m, n), F32),
        compiler_params=_cparams(("arbitrary",)),
        name="ada",
    )(a, w, bias)


def _modulated_norm(x, g, sc, sh):
    ms = jnp.mean(x * x, axis=-1, keepdims=True)
    y = x * lax.rsqrt(ms + EPS) * g
    return y * (1.0 + sc) + sh


def _norm_mod_kernel(x_ref, g_ref, sc_ref, sh_ref, o_ref):
    o_ref[...] = _modulated_norm(x_ref[...], g_ref[...], sc_ref[...], sh_ref[...]).astype(o_ref.dtype)


def _pair_top2_sum(v):
    a, b, c, d = v
    return jnp.maximum(jnp.maximum(jnp.maximum(a + b, a + c), jnp.maximum(a + d, b + c)),
                       jnp.maximum(b + d, c + d))


def _norm_mod_router_kernel(x_ref, g_ref, sc_ref, sh_ref, rwt_ref, rb_ref, o_ref, idx_ref, w_ref):
    h = _modulated_norm(x_ref[...], g_ref[...], sc_ref[...], sh_ref[...])
    o_ref[...] = h.astype(o_ref.dtype)
    logits = lax.dot_general(rwt_ref[...], h, (((1,), (1,)), ((), ())),
                             precision=lax.Precision.HIGHEST, preferred_element_type=F32)
    scores = jax.nn.sigmoid(logits)
    biased = scores + rb_ref[...]
    s = [scores[e:e + 1, :] for e in range(N_EXPERTS)]
    b = [biased[e:e + 1, :] for e in range(N_EXPERTS)]
    gs = [_pair_top2_sum(b[E_PER_G * g:E_PER_G * (g + 1)]) for g in range(N_GROUPS)]
    best, gsel = gs[0], jnp.zeros_like(gs[0], dtype=jnp.int32)
    for g in range(1, N_GROUPS):
        better = gs[g] > best
        gsel = jnp.where(better, g, gsel)
        best = jnp.where(better, gs[g], best)
    vb, vs = [], []
    for k in range(E_PER_G):
        bk, sk = b[k], s[k]
        for g in range(1, N_GROUPS):
            bk = jnp.where(gsel == g, b[E_PER_G * g + k], bk)
            sk = jnp.where(gsel == g, s[E_PER_G * g + k], sk)
        vb.append(bk)
        vs.append(sk)
    m1, i1, w1 = vb[0], jnp.zeros_like(gsel), vs[0]
    for k in range(1, E_PER_G):
        better = vb[k] > m1
        i1 = jnp.where(better, k, i1)
        w1 = jnp.where(better, vs[k], w1)
        m1 = jnp.where(better, vb[k], m1)
    m2, i2, w2, have = vb[0], jnp.zeros_like(gsel), vs[0], i1 != 0
    for k in range(1, E_PER_G):
        valid = i1 != k
        better = valid & (jnp.logical_not(have) | (vb[k] > m2))
        i2 = jnp.where(better, k, i2)
        w2 = jnp.where(better, vs[k], w2)
        m2 = jnp.where(better, vb[k], m2)
        have = have | valid
    tot = w1 + w2
    idx_ref[0:1, :] = gsel * E_PER_G + i1
    idx_ref[1:2, :] = gsel * E_PER_G + i2
    w_ref[0:1, :] = w1 / tot
    w_ref[1:2, :] = w2 / tot


def norm_mod(x, g, sc, sh, rows_per_mod, tm, router=None):
    m, d = x.shape
    assert m % tm == 0 and rows_per_mod % tm == 0
    mod_spec = pl.BlockSpec((None, 1, d), lambda i: ((i * tm) // rows_per_mod, 0, 0))
    in_specs = [pl.BlockSpec((tm, d), lambda i: (i, 0)), pl.BlockSpec((1, d), lambda i: (0, 0)), mod_spec, mod_spec]
    o_spec = pl.BlockSpec((tm, d), lambda i: (i, 0))
    o_shape = jax.ShapeDtypeStruct((m, d), BF16)
    if router is None:
        return pl.pallas_call(
            _norm_mod_kernel, grid=(m // tm,), in_specs=in_specs, out_specs=o_spec, out_shape=o_shape,
            compiler_params=_cparams(("parallel",)), name="norm_mod",
        )(x, g, sc, sh)
    rwt, rb = router
    in_specs += [pl.BlockSpec((N_EXPERTS, d), lambda i: (0, 0)), pl.BlockSpec((N_EXPERTS, 1), lambda i: (0, 0))]
    sel_spec = pl.BlockSpec((TOP_K, tm), lambda i: (0, i))
    return pl.pallas_call(
        _norm_mod_router_kernel, grid=(m // tm,), in_specs=in_specs,
        out_specs=[o_spec, sel_spec, sel_spec],
        out_shape=[o_shape, jax.ShapeDtypeStruct((TOP_K, m), jnp.int32), jax.ShapeDtypeStruct((TOP_K, m), F32)],
        compiler_params=_cparams(("parallel",)), name="norm_mod_router",
    )(x, g, sc, sh, rwt, rb)


def _final_norm_kernel(x_ref, g_ref, o_ref):
    x = x_ref[...]
    ms = jnp.mean(x * x, axis=-1, keepdims=True)
    o_ref[...] = x * lax.rsqrt(ms + EPS) * g_ref[...]


def final_norm(x, g, tm):
    m, d = x.shape
    return pl.pallas_call(
        _final_norm_kernel, grid=(m // tm,),
        in_specs=[pl.BlockSpec((tm, d), lambda i: (i, 0)), pl.BlockSpec((1, d), lambda i: (0, 0))],
        out_specs=pl.BlockSpec((tm, d), lambda i: (i, 0)),
        out_shape=jax.ShapeDtypeStruct((m, d), F32),
        compiler_params=_cparams(("parallel",)), name="final_norm",
    )(x, g)


MOE_TM = 512
MOE_KC = 512
MOE_FC = 256
MOE_NK = D_MODEL // MOE_KC
MOE_NF = D_FF // MOE_FC


def _moe_kernel(te_ref, nu_ref, x_ref, w1_ref, w3_ref, w2_ref, g_ref, o_ref, hg_s, hu_s, act_s):
    t = pl.program_id(0)
    s = pl.program_id(1)
    used = t < nu_ref[0]

    @pl.when(used & (s < MOE_NK))
    def _():
        x = x_ref[...]
        pg = jnp.dot(x, w1_ref[...].astype(BF16), preferred_element_type=F32)
        pu = jnp.dot(x, w3_ref[...].astype(BF16), preferred_element_type=F32)

        @pl.when(s == 0)
        def _():
            hg_s[...] = pg
            hu_s[...] = pu

        @pl.when(s > 0)
        def _():
            hg_s[...] += pg
            hu_s[...] += pu

        @pl.when(s == MOE_NK - 1)
        def _():
            hg = hg_s[...]
            act = ((hg * jax.nn.sigmoid(hg)) * hu_s[...] * g_ref[...]).astype(BF16)
            for f in range(MOE_NF):
                act_s[f] = act[:, f * MOE_FC:(f + 1) * MOE_FC]

    @pl.when(used & (s >= MOE_NK))
    def _():
        y = jnp.dot(act_s[s - MOE_NK], w2_ref[...].astype(BF16), preferred_element_type=F32)

        @pl.when(s == MOE_NK)
        def _():
            o_ref[...] = y

        @pl.when(s > MOE_NK)
        def _():
            o_ref[...] += y

    @pl.when(jnp.logical_not(used) & (s == 0))
    def _():
        o_ref[...] = jnp.zeros_like(o_ref)


def moe_grouped(layer, tile_expert, n_used, xs, w1, w3, w2, gs):
    r, d = xs.shape

    def kidx(t, s, nu):
        return jnp.where(t < nu[0], jnp.minimum(s, MOE_NK - 1), MOE_NK - 1)

    def fidx(t, s, nu):
        return jnp.where(t < nu[0], jnp.maximum(s - MOE_NK, 0), MOE_NF - 1)

    grid_spec = pltpu.PrefetchScalarGridSpec(
        num_scalar_prefetch=2,
        grid=(r // MOE_TM, MOE_NK + MOE_NF),
        in_specs=[pl.BlockSpec((MOE_TM, MOE_KC), lambda t, s, te, nu: (t, kidx(t, s, nu))),
                  pl.BlockSpec((None, None, MOE_KC, D_FF), lambda t, s, te, nu: (layer, te[t], kidx(t, s, nu), 0)),
                  pl.BlockSpec((None, None, MOE_KC, D_FF), lambda t, s, te, nu: (layer, te[t], kidx(t, s, nu), 0)),
                  pl.BlockSpec((None, None, MOE_FC, d), lambda t, s, te, nu: (layer, te[t], fidx(t, s, nu), 0)),
                  pl.BlockSpec((MOE_TM, 1), lambda t, s, te, nu: (t, 0))],
        out_specs=pl.BlockSpec((MOE_TM, d), lambda t, s, te, nu: (t, 0)),
        scratch_shapes=[pltpu.VMEM((MOE_TM, D_FF), F32), pltpu.VMEM((MOE_TM, D_FF), F32),
                        pltpu.VMEM((MOE_NF, MOE_TM, MOE_FC), BF16)],
    )
    return pl.pallas_call(
        _moe_kernel, grid_spec=grid_spec,
        out_shape=jax.ShapeDtypeStruct((r, d), F32),
        compiler_params=_cparams(("arbitrary", "arbitrary")), name="moe",
    )(tile_expert, n_used, xs, w1, w3, w2, gs)


def moe_dispatch(idx, wsel):
    n = idx.shape[1]
    e_flat = idx.reshape(-1)
    tok = jnp.tile(jnp.arange(n, dtype=jnp.int32), TOP_K)
    onehot = (e_flat[:, None] == jnp.arange(N_EXPERTS, dtype=jnp.int32)[None, :]).astype(jnp.int32)
    csum = jnp.cumsum(onehot, axis=0)
    counts = csum[-1]
    rank = jnp.take_along_axis(csum, e_flat[:, None], axis=1)[:, 0] - 1
    padded = ((counts + MOE_TM - 1) // MOE_TM) * MOE_TM
    seg_end = jnp.cumsum(padded)
    seg_start = seg_end - padded
    dest = seg_start[e_flat] + rank
    rows = TOP_K * n + N_EXPERTS * MOE_TM
    src_tok = jnp.zeros((rows,), jnp.int32).at[dest].set(tok)
    gate = jnp.zeros((rows,), F32).at[dest].set(wsel.reshape(-1))
    n_tiles = rows // MOE_TM
    n_used = (seg_end[-1] // MOE_TM).astype(jnp.int32)
    tile_start = jnp.arange(n_tiles, dtype=jnp.int32) * MOE_TM
    te = jnp.minimum(jnp.searchsorted(seg_end, tile_start, side='right'), N_EXPERTS - 1).astype(jnp.int32)
    te = jnp.where(jnp.arange(n_tiles) < n_used, te, te[jnp.maximum(n_used - 1, 0)])
    return dest.reshape(TOP_K, n), src_tok, gate[:, None], te, n_used.reshape(1)


def moe_ffn(layer, hb, idx, wsel, w1, w3, w2):
    dest, src_tok, gate, te, n_used = moe_dispatch(idx, wsel)
    xs = jnp.take(hb, src_tok, axis=0)
    ys = moe_grouped(layer, te, n_used, xs, w1, w3, w2, gate)
    return jnp.take(ys, dest[0], axis=0) + jnp.take(ys, dest[1], axis=0)


SEQ_ALL = CTX_LEN + SEQ
S5_T = 16
S5_NC = SEQ_ALL // S5_T
S5_NCC = CTX_LEN // S5_T
S5_W = S5_T * S5_CH


def _gelu_tanh(x):
    return x * (0.5 * (1.0 + jnp.tanh(math.sqrt(2.0 / math.pi) * (x + 0.044715 * (x * x * x)))))


def s5_chunk_operators(lam_re, lam_im, log_step, b_re, b_im, c_re, c_im):
    hp = lax.Precision.HIGHEST
    jj = jnp.arange(S5_T)[:, None]
    ii = jnp.arange(S5_T)[None, :]
    ms, es, fs, a16 = [], [], [], []
    for d in range(2):
        dt = jnp.exp(log_step[d])[:, None]
        lr, li = lam_re[d], lam_im[d]
        er = jnp.exp(lr * dt)
        abr, abi = er * jnp.cos(li * dt), er * jnp.sin(li * dt)
        nr, ni = abr - 1.0, abi
        den = lr * lr + li * li
        fr = (nr * lr + ni * li) / den
        fi = (ni * lr - nr * li) / den
        bbr = fr[..., None] * b_re[d] - fi[..., None] * b_im[d]
        bbi = fr[..., None] * b_im[d] + fi[..., None] * b_re[d]
        k = jnp.arange(S5_T + 1, dtype=F32)[:, None, None]
        mag, ang = jnp.exp(k * (lr * dt)), k * (li * dt)
        pr, pi = mag * jnp.cos(ang), mag * jnp.sin(ang)
        pbr = pr[..., None] * bbr - pi[..., None] * bbi
        pbi = pr[..., None] * bbi + pi[..., None] * bbr
        cr, ci = c_re[d], c_im[d]
        kk = (jnp.einsum('gop,lgpc->lgoc', cr, pbr[:S5_T], precision=hp)
              - jnp.einsum('gop,lgpc->lgoc', ci, pbi[:S5_T], precision=hp))
        cpr = cr[None] * pr[:, :, None, :] - ci[None] * pi[:, :, None, :]
        cpi = -(cr[None] * pi[:, :, None, :] + ci[None] * pr[:, :, None, :])
        if d == 0:
            lag, mask = ii - jj, ii >= jj
            e_pow = S5_T - 1 - jnp.arange(S5_T)
            f_pow = jnp.arange(S5_T) + 1
        else:
            lag, mask = jj - ii, jj >= ii
            e_pow = jnp.arange(S5_T)
            f_pow = S5_T - jnp.arange(S5_T)
        mm = jnp.where(mask[:, :, None, None, None], kk[jnp.clip(lag, 0, S5_T - 1)], 0.0)
        ms.append(mm.transpose(2, 0, 4, 1, 3).reshape(S5_G, S5_W, S5_W))
        for pb in (pbr, pbi):
            es.append(pb[e_pow].transpose(1, 0, 3, 2).reshape(S5_G, S5_W, S5_P))
        for cp in (cpr, cpi):
            fs.append(cp[f_pow].transpose(1, 3, 0, 2).reshape(S5_G, S5_P, S5_W))
        a16 += [pr[S5_T][:, None, :], pi[S5_T][:, None, :]]
    return (ms[0] + ms[1]).astype(BF16), jnp.stack(es).astype(BF16), jnp.stack(fs).astype(BF16), jnp.stack(a16)


def _s5_kernel(u_ref, m_ref, e_ref, f_ref, a_ref, o_ref, e_scr, s_scr):
    u = u_ref[...]
    for k in range(4):
        e_scr[k] = jnp.dot(u, e_ref[k], preferred_element_type=F32)
    arf, aif, arb, aib = (a_ref[k] for k in range(4))

    def segment(start, n, carry):
        def body(k, carry):
            out = []
            for b in range(BATCH):
                sfr, sfi, sbr, sbi = carry[4 * b:4 * b + 4]
                rf = b * S5_NC + start + k
                rb = b * S5_NC + start + n - 1 - k
                s_scr[0, pl.ds(rf, 1), :] = sfr
                s_scr[1, pl.ds(rf, 1), :] = sfi
                s_scr[2, pl.ds(rb, 1), :] = sbr
                s_scr[3, pl.ds(rb, 1), :] = sbi
                nfr = arf * sfr - aif * sfi + e_scr[0, pl.ds(rf, 1), :]
                nfi = arf * sfi + aif * sfr + e_scr[1, pl.ds(rf, 1), :]
                nbr = arb * sbr - aib * sbi + e_scr[2, pl.ds(rb, 1), :]
                nbi = arb * sbi + aib * sbr + e_scr[3, pl.ds(rb, 1), :]
                out += [nfr, nfi, nbr, nbi]
            return tuple(out)
        return lax.fori_loop(0, n, body, carry, unroll=4)

    z = jnp.zeros((1, S5_P), F32)
    carry = segment(0, S5_NCC, (z,) * (4 * BATCH))
    segment(S5_NCC, S5_NC - S5_NCC, carry)
    y = jnp.dot(u, m_ref[...], preferred_element_type=F32)
    for k in range(4):
        y += jnp.dot(s_scr[k].astype(BF16), f_ref[k], preferred_element_type=F32)
    o_ref[...] = y


def s5_scan(u, m, e, f, a16):
    g, r, w = u.shape
    return pl.pallas_call(
        _s5_kernel, grid=(g,),
        in_specs=[pl.BlockSpec((None, r, w), lambda i: (i, 0, 0)),
                  pl.BlockSpec((None, w, w), lambda i: (i, 0, 0)),
                  pl.BlockSpec((4, None, w, S5_P), lambda i: (0, i, 0, 0)),
                  pl.BlockSpec((4, None, S5_P, w), lambda i: (0, i, 0, 0)),
                  pl.BlockSpec((4, None, 1, S5_P), lambda i: (0, i, 0, 0))],
        out_specs=pl.BlockSpec((None, r, w), lambda i: (i, 0, 0)),
        out_shape=jax.ShapeDtypeStruct((g, r, w), F32),
        scratch_shapes=[pltpu.VMEM((4, r, S5_P), F32), pltpu.VMEM((4, r, S5_P), F32)],
        compiler_params=_cparams(("parallel",)), name="s5_scan",
    )(u, m, e, f, a16)


def _s5_out_kernel(y_ref, u_ref, d_ref, w_ref, b_ref, o_ref):
    g = _gelu_tanh(y_ref[...] + d_ref[...] * u_ref[...])
    z = jnp.dot(g.astype(BF16), w_ref[...], preferred_element_type=F32) + b_ref[...]
    o_ref[...] = (g * jax.nn.sigmoid(z)).astype(o_ref.dtype)


def s5_out(y, u, d_skip, w_glu, b_glu, tm):
    m, c = y.shape
    row = pl.BlockSpec((tm, c), lambda i: (i, 0))
    vec = pl.BlockSpec((1, c), lambda i: (0, 0))
    return pl.pallas_call(
        _s5_out_kernel, grid=(m // tm,),
        in_specs=[row, row, vec, pl.BlockSpec((c, c), lambda i: (0, 0)), vec],
        out_specs=row, out_shape=jax.ShapeDtypeStruct((m, c), BF16),
        compiler_params=_cparams(("parallel",)), name="s5_out",
    )(y, u, d_skip, w_glu, b_glu)


def s5_mixer(u_c, u_l, lam_re, lam_im, log_step, b_re, b_im, c_re, c_im, d_skip, w_glu, b_glu):
    bs = u_c.shape[0]
    u_all = jnp.concatenate([u_c, u_l], axis=1)
    uc = u_all.astype(BF16).reshape(bs, S5_NC, S5_T, S5_G, S5_CH).transpose(3, 0, 1, 2, 4)
    uc = uc.reshape(S5_G, bs * S5_NC, S5_W)
    y = s5_scan(uc, *s5_chunk_operators(lam_re, lam_im, log_step, b_re, b_im, c_re, c_im))
    y = y.reshape(S5_G, bs, S5_NC, S5_T, S5_CH).transpose(1, 2, 3, 0, 4).reshape(bs * SEQ_ALL, MIX_W)
    out = s5_out(y, u_all.reshape(bs * SEQ_ALL, MIX_W), d_skip[None], w_glu.astype(BF16), b_glu[None], 544)
    out = out.reshape(bs, SEQ_ALL, MIX_W)
    return out[:, :CTX_LEN], out[:, CTX_LEN:]


LRU_CB = 256
LRU_TB = 256
LRU_HALO = 8
LRU_ROWS = SEQ_ALL + 3 * LRU_HALO


def _lru_kernel(xc_ref, xl_ref, yc_ref, yl_ref, cw_ref, cb_ref, wa_ref, ba_ref, wi_ref, bi_ref, sp_ref,
                oc_ref, ol_ref, xpad, a_f, b_f, a_b, b_b):
    halo = jnp.zeros((LRU_HALO, LRU_CB), F32)
    lat0 = CTX_LEN + 2 * LRU_HALO
    xpad[0:LRU_HALO] = halo
    xpad[LRU_HALO:LRU_HALO + CTX_LEN] = xc_ref[...]
    xpad[LRU_HALO + CTX_LEN:lat0] = halo
    xpad[lat0:lat0 + SEQ] = xl_ref[...]
    xpad[lat0 + SEQ:LRU_ROWS] = halo
    cw = [cw_ref[j:j + 1, :] for j in range(CONV_W)]
    cb = cb_ref[...]

    def gates(p0, r0):
        ext = xpad[pl.ds(p0 - LRU_HALO, LRU_TB + 2 * LRU_HALO), :]
        xb = cb
        for j in range(CONV_W):
            s0 = LRU_HALO - CONV_PAD_L + j
            xb = xb + cw[j] * ext[s0:s0 + LRU_TB]
        for k in range(LRU_CB // LRU_BD):
            lanes = slice(k * LRU_BD, (k + 1) * LRU_BD)
            xk = xb[:, lanes]
            xkb = xk.astype(BF16)
            for d, (a_s, b_s) in enumerate(((a_f, b_f), (a_b, b_b))):
                r = jax.nn.sigmoid(jnp.dot(xkb, wa_ref[d, k], preferred_element_type=F32) + ba_ref[d][:, lanes])
                i = jax.nn.sigmoid(jnp.dot(xkb, wi_ref[d, k], preferred_element_type=F32) + bi_ref[d][:, lanes])
                log_a = -LRU_C * r * sp_ref[d][:, lanes]
                t = jnp.tanh(log_a)
                a_s[pl.ds(r0, LRU_TB), lanes] = jnp.exp(log_a)
                b_s[pl.ds(r0, LRU_TB), lanes] = jnp.sqrt(-2.0 * t / (1.0 - t)) * (i * xk)

    gates(LRU_HALO, 0)

    def lat_gates(k, _):
        gates(pl.multiple_of(lat0 + k * LRU_TB, 8), pl.multiple_of(CTX_LEN + k * LRU_TB, 8))
        return 0
    lax.fori_loop(0, SEQ // LRU_TB, lat_gates, 0)

    def segment(start, n, carry):
        def body(t, carry):
            hf, hb = carry
            rf = start + t
            rb = start + n - 1 - t
            hf = a_f[pl.ds(rf, 1), :] * hf + b_f[pl.ds(rf, 1), :]
            b_f[pl.ds(rf, 1), :] = hf
            hb = a_b[pl.ds(rb, 1), :] * hb + b_b[pl.ds(rb, 1), :]
            b_b[pl.ds(rb, 1), :] = hb
            return hf, hb
        return lax.fori_loop(0, n, body, carry, unroll=8)

    z = jnp.zeros((1, LRU_CB), F32)
    carry = segment(0, CTX_LEN, (z, z))
    segment(CTX_LEN, SEQ, carry)

    oc_ref[...] = (_gelu_tanh(yc_ref[...]) * (b_f[0:CTX_LEN] + b_b[0:CTX_LEN])).astype(oc_ref.dtype)

    def lat_out(k, _):
        r0 = pl.multiple_of(k * LRU_TB, 8)
        rows = pl.ds(pl.multiple_of(CTX_LEN + k * LRU_TB, 8), LRU_TB)
        h = b_f[rows, :] + b_b[rows, :]
        ol_ref[pl.ds(r0, LRU_TB), :] = (_gelu_tanh(yl_ref[pl.ds(r0, LRU_TB), :]) * h).astype(ol_ref.dtype)
        return 0
    lax.fori_loop(0, SEQ // LRU_TB, lat_out, 0)


def lru_mixer(y_c, x_c, y_l, x_l, conv_w, conv_b, w_a, b_a, w_i, b_i, lam):
    bs = x_c.shape[0]
    nb = LRU_CB // LRU_BD
    sp = jax.nn.softplus(-lam)[:, None, :]
    ctx_spec = pl.BlockSpec((None, CTX_LEN, LRU_CB), lambda b, c: (b, 0, c))
    lat_spec = pl.BlockSpec((None, SEQ, LRU_CB), lambda b, c: (b, 0, c))
    vec2 = pl.BlockSpec((2, 1, LRU_CB), lambda b, c: (0, 0, c))
    wspec = pl.BlockSpec((2, nb, LRU_BD, LRU_BD), lambda b, c: (0, c, 0, 0))
    scan_buf = pltpu.VMEM((SEQ_ALL, LRU_CB), F32)
    return pl.pallas_call(
        _lru_kernel, grid=(bs, MIX_W // LRU_CB),
        in_specs=[ctx_spec, lat_spec, ctx_spec, lat_spec,
                  pl.BlockSpec((CONV_W, LRU_CB), lambda b, c: (0, c)),
                  pl.BlockSpec((1, LRU_CB), lambda b, c: (0, c)),
                  wspec, vec2, wspec, vec2, vec2],
        out_specs=[ctx_spec, lat_spec],
        out_shape=[jax.ShapeDtypeStruct((bs, CTX_LEN, MIX_W), BF16), jax.ShapeDtypeStruct((bs, SEQ, MIX_W), BF16)],
        scratch_shapes=[pltpu.VMEM((LRU_ROWS, LRU_CB), F32), scan_buf, scan_buf, scan_buf, scan_buf],
        compiler_params=_cparams(("parallel", "parallel")), name="lru",
    )(x_c, x_l, y_c, y_l, conv_w, conv_b[None], w_a.astype(BF16), b_a[:, None, :], w_i.astype(BF16),
      b_i[:, None, :], sp)


def to_col_major(t):
    b, l = t.shape[:2]
    rows = l // GRID_W
    return jnp.swapaxes(t.reshape(b, rows, GRID_W, -1), 1, 2).reshape(t.shape)


def to_row_major(t):
    b, l = t.shape[:2]
    rows = l // GRID_W
    return jnp.swapaxes(t.reshape(b, GRID_W, rows, -1), 1, 2).reshape(t.shape)


N_CHUNKS = SEQ_ALL // CHUNK
N_CCH = CTX_LEN // CHUNK
GATE_COLBLK = 80
MIX_TB = 256


def _dot(a, b):
    return jnp.dot(a, b, preferred_element_type=F32)


def _dot_nt(a, b):
    return lax.dot_general(a, b, (((1,), (1,)), ((), ())), preferred_element_type=F32)


def _dot_tn(a, b):
    return lax.dot_general(a, b, (((0,), (0,)), ((), ())), preferred_element_type=F32)


def _split2(a):
    hi = a.astype(BF16)
    return hi, (a - hi.astype(F32)).astype(BF16)


def _dot3(a, b):
    ah, al = _split2(a)
    bh, bl = _split2(b)
    return _dot(ah, bh) + (_dot(ah, bl) + _dot(al, bh))


def _dot_mask(m, x):
    hi = x.astype(BF16)
    r1 = x - hi.astype(F32)
    mid = r1.astype(BF16)
    lo = (r1 - mid.astype(F32)).astype(BF16)
    return _dot(m, hi) + (_dot(m, mid) + _dot(m, lo))


def _chunk_masks(backward):
    i = lax.broadcasted_iota(jnp.int32, (CHUNK, CHUNK), 0)
    j = lax.broadcasted_iota(jnp.int32, (CHUNK, CHUNK), 1)
    if backward:
        i, j = j, i
    tri, strict = j <= i, j < i
    return tri, strict, tri.astype(BF16), jnp.where(strict, 1.0, 0.0).astype(F32)


def _lane_pick(blk, lane):
    lanes = lax.broadcasted_iota(jnp.int32, blk.shape, 1)
    return jnp.sum(jnp.where(lanes == lane, blk, 0.0), axis=-1, keepdims=True)


def _softplus(x):
    return jnp.maximum(x, 0.0) + jnp.log1p(jnp.exp(-jnp.abs(x)))


def _silu(x):
    return x * jax.nn.sigmoid(x)


def _bidir_chunks(step, carry):
    def segment(start, n, carry):
        def body(t, carry):
            return step(start + t, start + n - 1 - t, carry)
        return lax.fori_loop(0, n, body, carry)
    carry = segment(0, N_CCH, carry)
    return segment(N_CCH, N_CHUNKS - N_CCH, carry)


DN_HALO = 8
DN_ROWS = SEQ_ALL + 3 * DN_HALO
DN_GROUP = 4 * CHUNK


def _dn_kernel(qc_ref, ql_ref, kc_ref, kl_ref, vc_ref, vl_ref, zc_ref, zl_ref, gc_ref, gl_ref, hp_ref,
               cwq_ref, cwk_ref, cwv_ref, nw_ref, ol_ref, oc_ref,
               xpad, qn, kn, vn, u_s, w_s, qg_s, kd_s, at_s, dec_s):
    h = pl.program_id(1)
    halo = jnp.zeros((DN_HALO, DN_HD), F32)
    lat0 = CTX_LEN + 2 * DN_HALO

    def conv_into(xc_ref, xl_ref, cw_ref, dst, post):
        xpad[0:DN_HALO] = halo
        xpad[DN_HALO:DN_HALO + CTX_LEN] = xc_ref[...]
        xpad[DN_HALO + CTX_LEN:lat0] = halo
        xpad[lat0:lat0 + SEQ] = xl_ref[...]
        xpad[lat0 + SEQ:DN_ROWS] = halo
        cw = [cw_ref[j:j + 1, :] for j in range(CONV_W)]

        def block(p0, r0):
            ext = xpad[pl.ds(p0 - DN_HALO, MIX_TB + 2 * DN_HALO), :]
            s0 = DN_HALO - CONV_PAD_L
            acc = cw[0] * ext[s0:s0 + MIX_TB]
            for j in range(1, CONV_W):
                acc = acc + cw[j] * ext[s0 + j:s0 + j + MIX_TB]
            dst[pl.ds(r0, MIX_TB), :] = post(_silu(acc))

        block(DN_HALO, 0)

        def lat_block(k, _):
            block(pl.multiple_of(lat0 + k * MIX_TB, 8), pl.multiple_of(CTX_LEN + k * MIX_TB, 8))
            return 0
        lax.fori_loop(0, SEQ // MIX_TB, lat_block, 0)

    def l2n(scale):
        return lambda t: t * (lax.rsqrt(jnp.sum(t * t, axis=-1, keepdims=True) + EPS) * scale)

    conv_into(qc_ref, ql_ref, cwq_ref, qn, l2n(DN_HD ** -0.5))
    conv_into(kc_ref, kl_ref, cwk_ref, kn, l2n(1.0))
    conv_into(vc_ref, vl_ref, cwv_ref, vn, lambda t: t)

    gi = lax.broadcasted_iota(jnp.int32, (DN_GROUP, DN_GROUP), 0)
    gj = lax.broadcasted_iota(jnp.int32, (DN_GROUP, DN_GROUP), 1)
    same = (gi // CHUNK) == (gj // CHUNK)
    blk16 = (gi // 16) == (gj // 16)
    blk32 = (gi // 32) == (gj // 32)
    bwd = ((gi // CHUNK) % 2) == 1
    fwd = jnp.logical_not(bwd)
    li, lj = gi % CHUNK, gj % CHUNK
    tri = same & ((fwd & (lj <= li)) | (bwd & (lj >= li)))
    strict = same & ((fwd & (lj < li)) | (bwd & (lj > li)))
    eye = jnp.where(gi == gj, 1.0, 0.0).astype(F32)
    same16 = jnp.where(same, 1.0, 0.0).astype(BF16)
    m16 = jnp.where(tri, 1.0, 0.0).astype(BF16)
    nn = jnp.where(strict, 1.0, 0.0).astype(F32)

    def both_dirs(xf, xb):
        return jnp.concatenate([xf[:CHUNK], xb[:CHUNK], xf[CHUNK:], xb[CHUNK:]], axis=0)

    def prep(t, g_ref, row0):
        r0 = pl.multiple_of(t * (2 * CHUNK), 2 * CHUNK)
        rows = pl.ds(r0, 2 * CHUNK)
        gblk = g_ref[pl.ds(pl.multiple_of(t * (2 * CHUNK) - row0, 2 * CHUNK), 2 * CHUNK), :]
        q2, k2, v2 = qn[rows, :], kn[rows, :], vn[rows, :]
        q, k, v = both_dirs(q2, q2), both_dirs(k2, k2), both_dirs(v2, v2)
        g_dir = [hp_ref[:, d:d + 1] * _softplus(_lane_pick(gblk, d * DN_H + h) + hp_ref[:, 2 + d:3 + d])
                 for d in range(2)]
        b_dir = [jax.nn.sigmoid(_lane_pick(gblk, 2 * DN_H + d * DN_H + h)) for d in range(2)]
        g_col, beta = both_dirs(*g_dir), both_dirs(*b_dir)
        k16, q16 = k.astype(BF16), q.astype(BF16)
        gb = jnp.broadcast_to(g_col, (DN_GROUP, DN_HD))
        gcum = _dot_mask(m16, gb)
        tot = _dot_mask(same16, gb)
        dlt = _dot_mask(m16, g_col * nn)
        decay = jnp.where(tri, jnp.exp(jnp.where(tri, dlt, 0.0)), 0.0)
        kb = k * beta
        lower = jnp.where(strict, _dot_nt(kb.astype(BF16), k16) * decay, 0.0)
        x = -jnp.where(blk16, lower, 0.0)
        tinv = eye + x
        for _ in range(3):
            x = _dot3(x, x)
            tinv = tinv + _dot3(tinv, x)
        for inner, outer in ((blk16, blk32), (blk32, same)):
            off = jnp.where(outer & jnp.logical_not(inner), lower, 0.0)
            tinv = tinv - _dot3(tinv, _dot3(off, tinv))
        eg = jnp.exp(gcum)
        sol = _dot3(tinv, jnp.concatenate([v * beta, kb * eg], axis=1))
        attn = (_dot_nt(q16, k16) * decay).astype(BF16)
        w16 = sol[:, DN_HD:].astype(BF16)
        qg16 = (q * eg).astype(BF16)
        kd16 = (k * jnp.exp(tot - gcum)).astype(BF16)
        dec = jnp.exp(tot)
        for p in range(DN_GROUP // CHUNK):
            d, blk = p % 2, slice(p * CHUNK, (p + 1) * CHUNK)
            dst = pl.ds(r0 + (p // 2) * CHUNK, CHUNK)
            u_s[d, dst, :] = sol[blk, :DN_HD]
            w_s[d, dst, :] = w16[blk]
            at_s[d, dst, :] = attn[blk, blk]
            qg_s[d, dst, :] = qg16[blk]
            kd_s[d, dst, :] = kd16[blk]
            dec_s[d, pl.ds(pl.multiple_of((2 * t + p // 2) * 8, 8), 8), :] = dec[p * CHUNK:p * CHUNK + 8]
        return 0
    lax.fori_loop(0, N_CCH // 2, lambda t, _: prep(t, gc_ref, 0), 0)
    lax.fori_loop(N_CCH // 2, N_CHUNKS // 2, lambda t, _: prep(t, gl_ref, CTX_LEN), 0, unroll=2)
    o_f, o_b = qn, kn

    def advance(d, c, s, o_dst):
        rows = pl.ds(pl.multiple_of(c * CHUNK, CHUNK), CHUNK)
        s16 = s.astype(BF16)
        v_new = (u_s[d, rows, :] - _dot(w_s[d, rows, :], s16)).astype(BF16)
        o_dst[rows, :] = _dot(qg_s[d, rows, :], s16) + _dot(at_s[d, rows, :], v_new)
        g_last = dec_s[d, pl.ds(pl.multiple_of(c * 8, 8), 8), :][0:1]
        return s * g_last + _dot_tn(kd_s[d, rows, :], v_new)

    def step(cf, cb, carry):
        return advance(0, cf, carry[0], o_f), advance(1, cb, carry[1], o_b)

    z0 = jnp.zeros((DN_HD, DN_HD), F32)
    _bidir_chunks(step, (z0, z0))

    nw = nw_ref[...]

    def finish(rows, z):
        o = o_f[rows, :] + o_b[rows, :]
        o = o * lax.rsqrt(jnp.mean(o * o, axis=-1, keepdims=True) + EPS) * nw
        return (o * _silu(z)).astype(BF16)

    oc_ref[...] = finish(pl.ds(0, CTX_LEN), zc_ref[...])

    def lat_finish(k, _):
        r0 = pl.multiple_of(k * MIX_TB, 8)
        ol_ref[pl.ds(r0, MIX_TB), :] = finish(pl.ds(pl.multiple_of(CTX_LEN + k * MIX_TB, 8), MIX_TB),
                                              zl_ref[pl.ds(r0, MIX_TB), :])
        return 0
    lax.fori_loop(0, SEQ // MIX_TB, lat_finish, 0)


def deltanet_mixer(proj, conv_w, a_log, dt_bias, norm_w):
    hp = jnp.stack([-jnp.exp(a_log[0]), -jnp.exp(a_log[1]), dt_bias[0], dt_bias[1]], axis=-1)[:, None, :]
    ctx_blk0 = N_LAT // CTX_LEN

    def seg_specs(col0):
        cb = col0 // DN_HD
        return [pl.BlockSpec((CTX_LEN, DN_HD), lambda b, h: (ctx_blk0 + b, cb + h)),
                pl.BlockSpec((SEQ, DN_HD), lambda b, h: (b, cb + h))]

    def conv_spec(k):
        return pl.BlockSpec((CONV_W, DN_HD), lambda b, h: (0, k * DN_H + h))

    in_specs = (seg_specs(COL_DN_QKVZ) + seg_specs(COL_DN_QKVZ + MIX_W) + seg_specs(COL_DN_QKVZ + 2 * MIX_W)
                + seg_specs(COL_DN_QKVZ + 3 * MIX_W)
                + [pl.BlockSpec((CTX_LEN, 128), lambda b, h: (ctx_blk0 + b, GATE_COLBLK)),
                   pl.BlockSpec((SEQ, 128), lambda b, h: (b, GATE_COLBLK)),
                   pl.BlockSpec((None, 1, 4), lambda b, h: (h, 0, 0)),
                   conv_spec(0), conv_spec(1), conv_spec(2),
                   pl.BlockSpec((1, DN_HD), lambda b, h: (0, 0))])
    seq_f32 = pltpu.VMEM((SEQ_ALL, DN_HD), F32)
    dir_bf16 = pltpu.VMEM((2, SEQ_ALL, DN_HD), BF16)
    ol, oc = pl.pallas_call(
        _dn_kernel, grid=(BATCH, DN_H), in_specs=in_specs,
        out_specs=[pl.BlockSpec((SEQ, DN_HD), lambda b, h: (b, h)),
                   pl.BlockSpec((CTX_LEN, DN_HD), lambda b, h: (b, h))],
        out_shape=[jax.ShapeDtypeStruct((N_LAT, MIX_W), BF16), jax.ShapeDtypeStruct((N_CTX, MIX_W), BF16)],
        scratch_shapes=[pltpu.VMEM((DN_ROWS, DN_HD), F32), seq_f32, seq_f32, seq_f32,
                        pltpu.VMEM((2, SEQ_ALL, DN_HD), F32), dir_bf16, dir_bf16, dir_bf16,
                        pltpu.VMEM((2, SEQ_ALL, CHUNK), BF16), pltpu.VMEM((2, N_CHUNKS * 8, DN_HD), F32)],
        compiler_params=_cparams(("parallel", "arbitrary")), name="deltanet",
    )(*([proj] * 10), hp, conv_w, conv_w, conv_w, norm_w[None])
    return ol, oc


def _ml_kernel(qc_ref, ql_ref, kc_ref, kl_ref, vc_ref, vl_ref, oc_ref, ol_ref, gc_ref, gl_ref, hp_ref, nw_ref,
               yl_ref, yc_ref, h_f, h_b):
    hh = pl.program_id(1)
    masks = [_chunk_masks(False), _chunk_masks(True)]
    ones = jnp.ones((CHUNK, CHUNK), BF16)
    eye = jnp.where(lax.broadcasted_iota(jnp.int32, (CHUNK, CHUNK), 0)
                    == lax.broadcasted_iota(jnp.int32, (CHUNK, CHUNK), 1), 1.0, 0.0).astype(F32)

    def advance(d, c, state, refs, row0, h_dst):
        q_ref, k_ref, v_ref, g_ref = refs
        c_s, n_s, m = state
        rows = pl.ds(pl.multiple_of(c * CHUNK - row0, CHUNK), CHUNK)
        tri, _, mm, nn = masks[d]
        gblk = g_ref[rows, :]
        ig = _lane_pick(gblk, 4 * DN_H + d * ML_H + hh) + hp_ref[:, d:d + 1]
        lf = -_softplus(-(_lane_pick(gblk, 4 * DN_H + 2 * ML_H + d * ML_H + hh) + hp_ref[:, 2 + d:3 + d]))
        lb = jnp.broadcast_to(lf, (CHUNK, ML_DQK))
        bcum = _dot_mask(mm, lb)[:, 0:1]
        dmat = jnp.where(tri, _dot_mask(mm, lf * nn) + _dot_mask(ones, ig * eye), -jnp.inf)
        dmax = jnp.max(dmat, axis=-1, keepdims=True)
        tot = jnp.sum(lf, axis=0, keepdims=True)
        logw = tot - bcum + ig
        logw_max = jnp.max(logw, axis=0, keepdims=True)
        q = q_ref[rows, :] * (ML_DQK ** -0.5)
        k = k_ref[rows, :]
        q16, v16 = q.astype(BF16), v_ref[rows, :].astype(BF16)
        sqk = _dot_nt(q16, k.astype(BF16))
        inter = bcum + m
        m_t = jnp.maximum(inter, dmax)
        w_inter = jnp.exp(inter - m_t)
        wmat = jnp.exp(dmat - m_t) * sqk
        num = w_inter * _dot(q16, c_s.astype(BF16)) + _dot(wmat.astype(BF16), v16)
        den = w_inter * jnp.sum(q * n_s, axis=-1, keepdims=True) + jnp.sum(wmat, axis=-1, keepdims=True)
        h_dst[pl.ds(pl.multiple_of(c * CHUNK, CHUNK), CHUNK), :] = num / jnp.maximum(jnp.abs(den), jnp.exp(-m_t))
        m_new = jnp.maximum(tot + m, logw_max)
        keep = jnp.exp(tot + m - m_new)
        kw = k * jnp.exp(logw - m_new)
        return (keep * c_s + _dot_tn(kw.astype(BF16), v16), keep * n_s + jnp.sum(kw, axis=0, keepdims=True), m_new)

    ctx_refs = (qc_ref, kc_ref, vc_ref, gc_ref)
    lat_refs = (ql_ref, kl_ref, vl_ref, gl_ref)

    def segment(refs, start, n, carry):
        def body(t, carry):
            return (advance(0, start + t, carry[0], refs, start * CHUNK, h_f),
                    advance(1, start + n - 1 - t, carry[1], refs, start * CHUNK, h_b))
        return lax.fori_loop(0, n, body, carry)

    s0 = (jnp.zeros((ML_DQK, ML_DV), F32), jnp.zeros((1, ML_DQK), F32), jnp.zeros((1, 1), F32))
    carry = segment(ctx_refs, 0, N_CCH, (s0, s0))
    segment(lat_refs, N_CCH, N_CHUNKS - N_CCH, carry)

    nw = nw_ref[...]

    def finish(rows, o):
        hsum = h_f[rows, :] + h_b[rows, :]
        hsum = hsum * lax.rsqrt(jnp.mean(hsum * hsum, axis=-1, keepdims=True) + EPS) * nw
        return (hsum * jax.nn.sigmoid(o)).astype(BF16)

    yc_ref[...] = finish(pl.ds(0, CTX_LEN), oc_ref[...])

    def lat_finish(k, _):
        r0 = pl.multiple_of(k * MIX_TB, 8)
        yl_ref[pl.ds(r0, MIX_TB), :] = finish(pl.ds(pl.multiple_of(CTX_LEN + k * MIX_TB, 8), MIX_TB),
                                              ol_ref[pl.ds(r0, MIX_TB), :])
        return 0
    lax.fori_loop(0, SEQ // MIX_TB, lat_finish, 0)


def mlstm_mixer(proj, i_bias, f_bias, norm_w):
    hp = jnp.stack([i_bias[0], i_bias[1], f_bias[0], f_bias[1]], axis=-1)[:, None, :]
    ctx_blk0 = N_LAT // CTX_LEN

    def seg_specs(col0, w):
        cb = col0 // w
        return [pl.BlockSpec((CTX_LEN, w), lambda b, h: (ctx_blk0 + b, cb + h)),
                pl.BlockSpec((SEQ, w), lambda b, h: (b, cb + h))]

    in_specs = (seg_specs(COL_ML_Q, ML_DQK) + seg_specs(COL_ML_K, ML_DQK) + seg_specs(COL_ML_V, ML_DV)
                + seg_specs(COL_ML_O, ML_DV)
                + [pl.BlockSpec((CTX_LEN, 128), lambda b, h: (ctx_blk0 + b, GATE_COLBLK)),
                   pl.BlockSpec((SEQ, 128), lambda b, h: (b, GATE_COLBLK)),
                   pl.BlockSpec((None, 1, 4), lambda b, h: (h, 0, 0)),
                   pl.BlockSpec((1, ML_DV), lambda b, h: (0, h))])
    seq_f32 = pltpu.VMEM((SEQ_ALL, ML_DV), F32)
    yl, yc = pl.pallas_call(
        _ml_kernel, grid=(BATCH, ML_H), in_specs=in_specs,
        out_specs=[pl.BlockSpec((SEQ, ML_DV), lambda b, h: (b, h)),
                   pl.BlockSpec((CTX_LEN, ML_DV), lambda b, h: (b, h))],
        out_shape=[jax.ShapeDtypeStruct((N_LAT, MIX_W), BF16), jax.ShapeDtypeStruct((N_CTX, MIX_W), BF16)],
        scratch_shapes=[seq_f32, seq_f32],
        compiler_params=_cparams(("parallel", "arbitrary")), name="mlstm",
    )(*([proj] * 10), hp, norm_w[None])
    return yl, yc


PROJ_TN = 512
PROJ_PAD = -(-PROJ_W // PROJ_TN) * PROJ_TN
COL_S5, COL_DN_QKVZ, COL_ML_Q, COL_ML_K, COL_ML_V, COL_ML_O, COL_LRU_Y, COL_LRU_X = (
    0, 1024, 5120, 5632, 6144, 7168, 8192, 9216)
COL_DN_A, COL_DN_B, COL_ML_I, COL_ML_F = 10240, 10256, 10272, 10280


def permute_proj_weight(w):
    o = [0] + [int(v) for v in np.cumsum(PROJ_SIZES)]
    parts = [w[:, o[0]:o[5]], w[:, o[7]:o[11]], w[:, o[13]:o[15]], w[:, o[5]:o[7]], w[:, o[11]:o[13]]]
    return jnp.pad(jnp.concatenate(parts, axis=1).astype(BF16), ((0, 0), (0, PROJ_PAD - PROJ_W)))


def hybrid_mixer(proj, s5_p, dn_p, ml_p, lru_p):
    def seg(c0, w):
        blk = proj[:, c0:c0 + w]
        return blk[N_LAT:].reshape(BATCH, CTX_LEN, w), blk[:N_LAT].reshape(BATCH, SEQ, w)

    u_c, u_l = seg(COL_S5, MIX_W)
    a_c, a_l = s5_mixer(u_c, to_col_major(u_l), *s5_p)
    y_c, y_l = seg(COL_LRU_Y, MIX_W)
    x_c, x_l = seg(COL_LRU_X, MIX_W)
    d_c, d_l = lru_mixer(y_c, x_c, to_col_major(y_l), to_col_major(x_l), *lru_p)
    flat = lambda t_l, t_c: (to_row_major(t_l).reshape(N_LAT, MIX_W), t_c.reshape(N_CTX, MIX_W))
    return [flat(a_l, a_c), deltanet_mixer(proj, *dn_p), mlstm_mixer(proj, *ml_p), flat(d_l, d_c)]


def kernel(x, c, ctx, c_ctx, w_ada, b_ada, norm1, norm2, norm_f, w_in, w_out, s5_lam_re, s5_lam_im, s5_log_step,
           s5_b_re, s5_b_im, s5_c_re, s5_c_im, s5_d, s5_w_glu, s5_b_glu, dn_conv, dn_a_log, dn_dt_bias, dn_norm,
           ml_i_bias, ml_f_bias, ml_norm, lru_conv_w, lru_conv_b, lru_w_a, lru_b_a, lru_w_i, lru_b_i, lru_lam,
           router_w, router_bias, moe_w1, moe_w3, moe_w2):
    d = D_MODEL
    xa = jnp.concatenate([x.reshape(N_LAT, d), ctx.reshape(N_CTX, d)], axis=0)
    cond = jnp.concatenate([jax.nn.silu(c), jax.nn.silu(c_ctx)[None], jnp.zeros((16 - BATCH - 1, d), F32)], axis=0)
    cond = cond.astype(BF16)
    router = (router_w.T, router_bias.reshape(N_EXPERTS, 1))

    for l in range(DEPTH):
        last = l == DEPTH - 1
        mod = ada_matmul(l, cond, w_ada, b_ada[:, None, :], 512)[:BATCH + 1].reshape(BATCH + 1, 6, 1, d)
        shift1, scale1, gate1, shift2, scale2, gate2 = (mod[:, k] for k in range(6))

        h1 = norm_mod(xa, norm1[l][None], scale1, shift1, SEQ, 256)
        proj = matmul(h1, permute_proj_weight(w_in[l]), 512, PROJ_TN)
        slabs = hybrid_mixer(
            proj,
            (s5_lam_re[l], s5_lam_im[l], s5_log_step[l], s5_b_re[l], s5_b_im[l], s5_c_re[l], s5_c_im[l], s5_d[l],
             s5_w_glu[l], s5_b_glu[l]),
            (dn_conv[l], dn_a_log[l], dn_dt_bias[l], dn_norm[l]),
            (ml_i_bias[l], ml_f_bias[l], ml_norm[l]),
            (lru_conv_w[l], lru_conv_b[l], lru_w_a[l], lru_b_a[l], lru_w_i[l], lru_b_i[l], lru_lam[l]))
        if last:
            xa, mix = xa[:N_LAT], [s[0] for s in slabs]
        else:
            mix = [jnp.concatenate(s, axis=0) for s in slabs]
        xa = matmul_gated_residual(mix, w_out[l].astype(BF16), xa, gate1, SEQ, 512, 512)

        h2, idx, wsel = norm_mod(xa, norm2[l][None], scale2, shift2, SEQ, 256, router=router)
        y = moe_ffn(l, h2, idx, wsel, moe_w1, moe_w3, moe_w2)
        upd = (xa[:N_LAT].reshape(BATCH, SEQ, d) + gate2[:BATCH] * y[:N_LAT].reshape(BATCH, SEQ, d)).reshape(N_LAT, d)
        if last:
            xa = upd
        else:
            xa = jnp.concatenate([upd, xa[N_LAT:] + gate2[BATCH] * y[N_LAT:]], axis=0)

    return final_norm(xa[:N_LAT], norm_f[None], 256).reshape(BATCH, SEQ, d)
```

```python
import functools
import math

import numpy as np
import jax
import jax.numpy as jnp
from jax import lax
from jax.experimental import pallas as pl
from jax.experimental.pallas import tpu as pltpu

D_MODEL = 4096
BATCH = 2
SEQ = 4096
DEPTH = 2
GRID_W = 64
CTX_LEN = 256
MIX_W = D_MODEL // 4
MIX_TOTAL = 4 * MIX_W
CHUNK = 64
CONV_W = 4
CONV_PAD_L = 2
CONV_PAD_R = 1
EPS = 1e-6
S5_CH = 16
S5_G = MIX_W // S5_CH
S5_P = 64
DN_HD = 128
DN_H = MIX_W // DN_HD
ML_H = 4
ML_DV = MIX_W // ML_H
ML_DQK = ML_DV // 2
LRU_BLOCKS = 8
LRU_BD = MIX_W // LRU_BLOCKS
LRU_C = 8.0
N_EXPERTS = 16
N_GROUPS = 4
E_PER_G = N_EXPERTS // N_GROUPS
TOP_K = 2
D_FF = D_MODEL // 4
PROJ_SIZES = (MIX_W, MIX_W, MIX_W, MIX_W, MIX_W, 2 * DN_H, 2 * DN_H, ML_H * ML_DQK, ML_H * ML_DQK, MIX_W, MIX_W,
              2 * ML_H, 2 * ML_H, MIX_W, MIX_W)
PROJ_W = sum(PROJ_SIZES)

N_LAT = BATCH * SEQ
N_CTX = BATCH * CTX_LEN
VMEM_LIMIT = 56 * 1024 * 1024

F32 = jnp.float32
BF16 = jnp.bfloat16


def _cparams(sem):
    return pltpu.CompilerParams(dimension_semantics=sem, vmem_limit_bytes=VMEM_LIMIT)


def _mm_kernel(a_ref, b_ref, o_ref):
    o_ref[...] = jnp.dot(a_ref[...], b_ref[...], preferred_element_type=F32).astype(o_ref.dtype)


def matmul(a, b, tm, tn, out_dtype=F32):
    m, k = a.shape
    n = b.shape[1]
    assert m % tm == 0 and n % tn == 0
    return pl.pallas_call(
        _mm_kernel,
        grid=(m // tm, n // tn),
        in_specs=[pl.BlockSpec((tm, k), lambda i, j: (i, 0)),
                  pl.BlockSpec((k, tn), lambda i, j: (0, j))],
        out_specs=pl.BlockSpec((tm, tn), lambda i, j: (i, j)),
        out_shape=jax.ShapeDtypeStruct((m, n), out_dtype),
        compiler_params=_cparams(("parallel", "arbitrary")),
        name="mm",
    )(a, b)


def _mm_resid_kernel(a0_ref, a1_ref, a2_ref, a3_ref, b_ref, r_ref, g_ref, o_ref):
    acc = None
    for s, a_ref in enumerate((a0_ref, a1_ref, a2_ref, a3_ref)):
        part = jnp.dot(a_ref[...], b_ref[s * MIX_W:(s + 1) * MIX_W, :], preferred_element_type=F32)
        acc = part if acc is None else acc + part
    o_ref[...] = r_ref[...] + g_ref[...] * acc


def matmul_gated_residual(a_slabs, b, resid, gate, rows_per_gate, tm, tn):
    m = a_slabs[0].shape[0]
    k, n = b.shape
    assert m % tm == 0 and n % tn == 0 and rows_per_gate % tm == 0 and k == 4 * MIX_W
    return pl.pallas_call(
        _mm_resid_kernel,
        grid=(m // tm, n // tn),
        in_specs=[pl.BlockSpec((tm, MIX_W), lambda i, j: (i, 0))] * 4 + [
                  pl.BlockSpec((k, tn), lambda i, j: (0, j)),
                  pl.BlockSpec((tm, tn), lambda i, j: (i, j)),
                  pl.BlockSpec((None, 1, tn), lambda i, j: ((i * tm) // rows_per_gate, 0, j))],
        out_specs=pl.BlockSpec((tm, tn), lambda i, j: (i, j)),
        out_shape=jax.ShapeDtypeStruct((m, n), F32),
        compiler_params=_cparams(("parallel", "arbitrary")),
        name="mm_resid",
    )(*a_slabs, b, resid, gate)


def _ada_kernel(a_ref, w_ref, b_ref, o_ref):
    o_ref[...] = jnp.dot(a_ref[...], w_ref[...].astype(BF16), preferred_element_type=F32) + b_ref[...]


def ada_matmul(layer, a, w, bias, tn):
    m, k = a.shape
    n = w.shape[2]
    return pl.pallas_call(
        _ada_kernel,
        grid=(n // tn,),
        in_specs=[pl.BlockSpec((m, k), lambda j: (0, 0)),
                  pl.BlockSpec((None, k, tn), lambda j: (layer, 0, j)),
                  pl.BlockSpec((None, 1, tn), lambda j: (layer, 0, j))],
        out_specs=pl.BlockSpec((m, tn), lambda j: (0, j)),
        out_shape=jax.ShapeDtypeStruct((m, n), F32),
        compiler_params=_cparams(("arbitrary",)),
        name="ada",
    )(a, w, bias)


def _modulated_norm(x, g, sc, sh):
    ms = jnp.mean(x * x, axis=-1, keepdims=True)
    y = x * lax.rsqrt(ms + EPS) * g
    return y * (1.0 + sc) + sh


def _norm_mod_kernel(x_ref, g_ref, sc_ref, sh_ref, o_ref):
    o_ref[...] = _modulated_norm(x_ref[...], g_ref[...], sc_ref[...], sh_ref[...]).astype(o_ref.dtype)


def _pair_top2_sum(v):
    a, b, c, d = v
    return jnp.maximum(jnp.maximum(jnp.maximum(a + b, a + c), jnp.maximum(a + d, b + c)),
                       jnp.maximum(b + d, c + d))


def _norm_mod_router_kernel(x_ref, g_ref, sc_ref, sh_ref, rwt_ref, rb_ref, o_ref, idx_ref, w_ref):
    h = _modulated_norm(x_ref[...], g_ref[...], sc_ref[...], sh_ref[...])
    o_ref[...] = h.astype(o_ref.dtype)
    logits = lax.dot_general(rwt_ref[...], h, (((1,), (1,)), ((), ())),
                             precision=lax.Precision.HIGHEST, preferred_element_type=F32)
    scores = jax.nn.sigmoid(logits)
    biased = scores + rb_ref[...]
    s = [scores[e:e + 1, :] for e in range(N_EXPERTS)]
    b = [biased[e:e + 1, :] for e in range(N_EXPERTS)]
    gs = [_pair_top2_sum(b[E_PER_G * g:E_PER_G * (g + 1)]) for g in range(N_GROUPS)]
    best, gsel = gs[0], jnp.zeros_like(gs[0], dtype=jnp.int32)
    for g in range(1, N_GROUPS):
        better = gs[g] > best
        gsel = jnp.where(better, g, gsel)
        best = jnp.where(better, gs[g], best)
    vb, vs = [], []
    for k in range(E_PER_G):
        bk, sk = b[k], s[k]
        for g in range(1, N_GROUPS):
            bk = jnp.where(gsel == g, b[E_PER_G * g + k], bk)
            sk = jnp.where(gsel == g, s[E_PER_G * g + k], sk)
        vb.append(bk)
        vs.append(sk)
    m1, i1, w1 = vb[0], jnp.zeros_like(gsel), vs[0]
    for k in range(1, E_PER_G):
        better = vb[k] > m1
        i1 = jnp.where(better, k, i1)
        w1 = jnp.where(better, vs[k], w1)
        m1 = jnp.where(better, vb[k], m1)
    m2, i2, w2, have = vb[0], jnp.zeros_like(gsel), vs[0], i1 != 0
    for k in range(1, E_PER_G):
        valid = i1 != k
        better = valid & (jnp.logical_not(have) | (vb[k] > m2))
        i2 = jnp.where(better, k, i2)
        w2 = jnp.where(better, vs[k], w2)
        m2 = jnp.where(better, vb[k], m2)
        have = have | valid
    tot = w1 + w2
    idx_ref[0:1, :] = gsel * E_PER_G + i1
    idx_ref[1:2, :] = gsel * E_PER_G + i2
    w_ref[0:1, :] = w1 / tot
    w_ref[1:2, :] = w2 / tot


def norm_mod(x, g, sc, sh, rows_per_mod, tm, router=None):
    m, d = x.shape
    assert m % tm == 0 and rows_per_mod % tm == 0
    mod_spec = pl.BlockSpec((None, 1, d), lambda i: ((i * tm) // rows_per_mod, 0, 0))
    in_specs = [pl.BlockSpec((tm, d), lambda i: (i, 0)), pl.BlockSpec((1, d), lambda i: (0, 0)), mod_spec, mod_spec]
    o_spec = pl.BlockSpec((tm, d), lambda i: (i, 0))
    o_shape = jax.ShapeDtypeStruct((m, d), BF16)
    if router is None:
        return pl.pallas_call(
            _norm_mod_kernel, grid=(m // tm,), in_specs=in_specs, out_specs=o_spec, out_shape=o_shape,
            compiler_params=_cparams(("parallel",)), name="norm_mod",
        )(x, g, sc, sh)
    rwt, rb = router
    in_specs += [pl.BlockSpec((N_EXPERTS, d), lambda i: (0, 0)), pl.BlockSpec((N_EXPERTS, 1), lambda i: (0, 0))]
    sel_spec = pl.BlockSpec((TOP_K, tm), lambda i: (0, i))
    return pl.pallas_call(
        _norm_mod_router_kernel, grid=(m // tm,), in_specs=in_specs,
        out_specs=[o_spec, sel_spec, sel_spec],
        out_shape=[o_shape, jax.ShapeDtypeStruct((TOP_K, m), jnp.int32), jax.ShapeDtypeStruct((TOP_K, m), F32)],
        compiler_params=_cparams(("parallel",)), name="norm_mod_router",
    )(x, g, sc, sh, rwt, rb)


def _final_norm_kernel(x_ref, g_ref, o_ref):
    x = x_ref[...]
    ms = jnp.mean(x * x, axis=-1, keepdims=True)
    o_ref[...] = x * lax.rsqrt(ms + EPS) * g_ref[...]


def final_norm(x, g, tm):
    m, d = x.shape
    return pl.pallas_call(
        _final_norm_kernel, grid=(m // tm,),
        in_specs=[pl.BlockSpec((tm, d), lambda i: (i, 0)), pl.BlockSpec((1, d), lambda i: (0, 0))],
        out_specs=pl.BlockSpec((tm, d), lambda i: (i, 0)),
        out_shape=jax.ShapeDtypeStruct((m, d), F32),
        compiler_params=_cparams(("parallel",)), name="final_norm",
    )(x, g)


MOE_TM = 512
MOE_KC = 1024
MOE_FC = 256
MOE_NK = D_MODEL // MOE_KC
MOE_NF = D_FF // MOE_FC


def _moe_kernel(te_ref, nu_ref, x_ref, w1_ref, w3_ref, w2_ref, g_ref, o_ref, hg_s, hu_s, act_s):
    t = pl.program_id(0)
    s = pl.program_id(1)
    used = t < nu_ref[0]

    @pl.when(used & (s < MOE_NK))
    def _():
        x = x_ref[...]
        pg = jnp.dot(x, w1_ref[...].astype(BF16), preferred_element_type=F32)
        pu = jnp.dot(x, w3_ref[...].astype(BF16), preferred_element_type=F32)

        @pl.when(s == 0)
        def _():
            hg_s[...] = pg
            hu_s[...] = pu

        @pl.when(s > 0)
        def _():
            hg_s[...] += pg
            hu_s[...] += pu

        @pl.when(s == MOE_NK - 1)
        def _():
            hg = hg_s[...]
            act = ((hg * jax.nn.sigmoid(hg)) * hu_s[...] * g_ref[...]).astype(BF16)
            for f in range(MOE_NF):
                act_s[f] = act[:, f * MOE_FC:(f + 1) * MOE_FC]

    @pl.when(used & (s >= MOE_NK))
    def _():
        y = jnp.dot(act_s[s - MOE_NK], w2_ref[...].astype(BF16), preferred_element_type=F32)

        @pl.when(s == MOE_NK)
        def _():
            o_ref[...] = y

        @pl.when(s > MOE_NK)
        def _():
            o_ref[...] += y

    @pl.when(jnp.logical_not(used) & (s == 0))
    def _():
        o_ref[...] = jnp.zeros_like(o_ref)


def moe_grouped(layer, tile_expert, n_used, xs, w1, w3, w2, gs):
    r, d = xs.shape

    def kidx(t, s, nu):
        return jnp.where(t < nu[0], jnp.minimum(s, MOE_NK - 1), MOE_NK - 1)

    def fidx(t, s, nu):
        return jnp.where(t < nu[0], jnp.maximum(s - MOE_NK, 0), MOE_NF - 1)

    grid_spec = pltpu.PrefetchScalarGridSpec(
        num_scalar_prefetch=2,
        grid=(r // MOE_TM, MOE_NK + MOE_NF),
        in_specs=[pl.BlockSpec((MOE_TM, MOE_KC), lambda t, s, te, nu: (t, kidx(t, s, nu))),
                  pl.BlockSpec((None, None, MOE_KC, D_FF), lambda t, s, te, nu: (layer, te[t], kidx(t, s, nu), 0)),
                  pl.BlockSpec((None, None, MOE_KC, D_FF), lambda t, s, te, nu: (layer, te[t], kidx(t, s, nu), 0)),
                  pl.BlockSpec((None, None, MOE_FC, d), lambda t, s, te, nu: (layer, te[t], fidx(t, s, nu), 0)),
                  pl.BlockSpec((MOE_TM, 1), lambda t, s, te, nu: (t, 0))],
        out_specs=pl.BlockSpec((MOE_TM, d), lambda t, s, te, nu: (t, 0)),
        scratch_shapes=[pltpu.VMEM((MOE_TM, D_FF), F32), pltpu.VMEM((MOE_TM, D_FF), F32),
                        pltpu.VMEM((MOE_NF, MOE_TM, MOE_FC), BF16)],
    )
    return pl.pallas_call(
        _moe_kernel, grid_spec=grid_spec,
        out_shape=jax.ShapeDtypeStruct((r, d), F32),
        compiler_params=_cparams(("arbitrary", "arbitrary")), name="moe",
    )(tile_expert, n_used, xs, w1, w3, w2, gs)


def moe_dispatch(idx, wsel):
    n = idx.shape[1]
    e_flat = idx.reshape(-1)
    tok = jnp.tile(jnp.arange(n, dtype=jnp.int32), TOP_K)
    onehot = (e_flat[:, None] == jnp.arange(N_EXPERTS, dtype=jnp.int32)[None, :]).astype(jnp.int32)
    csum = jnp.cumsum(onehot, axis=0)
    counts = csum[-1]
    rank = jnp.take_along_axis(csum, e_flat[:, None], axis=1)[:, 0] - 1
    padded = ((counts + MOE_TM - 1) // MOE_TM) * MOE_TM
    seg_end = jnp.cumsum(padded)
    seg_start = seg_end - padded
    dest = seg_start[e_flat] + rank
    rows = TOP_K * n + N_EXPERTS * MOE_TM
    src_tok = jnp.zeros((rows,), jnp.int32).at[dest].set(tok)
    gate = jnp.zeros((rows,), F32).at[dest].set(wsel.reshape(-1))
    n_tiles = rows // MOE_TM
    n_used = (seg_end[-1] // MOE_TM).astype(jnp.int32)
    tile_start = jnp.arange(n_tiles, dtype=jnp.int32) * MOE_TM
    te = jnp.minimum(jnp.searchsorted(seg_end, tile_start, side='right'), N_EXPERTS - 1).astype(jnp.int32)
    te = jnp.where(jnp.arange(n_tiles) < n_used, te, te[jnp.maximum(n_used - 1, 0)])
    return dest.reshape(TOP_K, n), src_tok, gate[:, None], te, n_used.reshape(1)


def moe_ffn(layer, hb, idx, wsel, w1, w3, w2):
    dest, src_tok, gate, te, n_used = moe_dispatch(idx, wsel)
    xs = jnp.take(hb, src_tok, axis=0)
    ys = moe_grouped(layer, te, n_used, xs, w1, w3, w2, gate)
    return jnp.take(ys, dest[0], axis=0) + jnp.take(ys, dest[1], axis=0)


SEQ_ALL = CTX_LEN + SEQ
S5_T = 16
S5_NC = SEQ_ALL // S5_T
S5_NCC = CTX_LEN // S5_T
S5_W = S5_T * S5_CH


def _gelu_tanh(x):
    return x * (0.5 * (1.0 + jnp.tanh(math.sqrt(2.0 / math.pi) * (x + 0.044715 * (x * x * x)))))


def s5_chunk_operators(lam_re, lam_im, log_step, b_re, b_im, c_re, c_im):
    hp = lax.Precision.HIGHEST
    jj = jnp.arange(S5_T)[:, None]
    ii = jnp.arange(S5_T)[None, :]
    ms, es, fs, a16 = [], [], [], []
    for d in range(2):
        dt = jnp.exp(log_step[d])[:, None]
        lr, li = lam_re[d], lam_im[d]
        er = jnp.exp(lr * dt)
        abr, abi = er * jnp.cos(li * dt), er * jnp.sin(li * dt)
        nr, ni = abr - 1.0, abi
        den = lr * lr + li * li
        fr = (nr * lr + ni * li) / den
        fi = (ni * lr - nr * li) / den
        bbr = fr[..., None] * b_re[d] - fi[..., None] * b_im[d]
        bbi = fr[..., None] * b_im[d] + fi[..., None] * b_re[d]
        k = jnp.arange(S5_T + 1, dtype=F32)[:, None, None]
        mag, ang = jnp.exp(k * (lr * dt)), k * (li * dt)
        pr, pi = mag * jnp.cos(ang), mag * jnp.sin(ang)
        pbr = pr[..., None] * bbr - pi[..., None] * bbi
        pbi = pr[..., None] * bbi + pi[..., None] * bbr
        cr, ci = c_re[d], c_im[d]
        kk = (jnp.einsum('gop,lgpc->lgoc', cr, pbr[:S5_T], precision=hp)
              - jnp.einsum('gop,lgpc->lgoc', ci, pbi[:S5_T], precision=hp))
        cpr = cr[None] * pr[:, :, None, :] - ci[None] * pi[:, :, None, :]
        cpi = -(cr[None] * pi[:, :, None, :] + ci[None] * pr[:, :, None, :])
        if d == 0:
            lag, mask = ii - jj, ii >= jj
            e_pow = S5_T - 1 - jnp.arange(S5_T)
            f_pow = jnp.arange(S5_T) + 1
        else:
            lag, mask = jj - ii, jj >= ii
            e_pow = jnp.arange(S5_T)
            f_pow = S5_T - jnp.arange(S5_T)
        mm = jnp.where(mask[:, :, None, None, None], kk[jnp.clip(lag, 0, S5_T - 1)], 0.0)
        ms.append(mm.transpose(2, 0, 4, 1, 3).reshape(S5_G, S5_W, S5_W))
        for pb in (pbr, pbi):
            es.append(pb[e_pow].transpose(1, 0, 3, 2).reshape(S5_G, S5_W, S5_P))
        for cp in (cpr, cpi):
            fs.append(cp[f_pow].transpose(1, 3, 0, 2).reshape(S5_G, S5_P, S5_W))
        a16 += [pr[S5_T][:, None, :], pi[S5_T][:, None, :]]
    return (ms[0] + ms[1]).astype(BF16), jnp.stack(es).astype(BF16), jnp.stack(fs).astype(BF16), jnp.stack(a16)


def _s5_kernel(u_ref, m_ref, e_ref, f_ref, a_ref, o_ref, e_scr, s_scr):
    u = u_ref[...]
    for k in range(4):
        e_scr[k] = jnp.dot(u, e_ref[k], preferred_element_type=F32)
    arf, aif, arb, aib = (a_ref[k] for k in range(4))

    def segment(start, n, carry):
        def body(k, carry):
            out = []
            for b in range(BATCH):
                sfr, sfi, sbr, sbi = carry[4 * b:4 * b + 4]
                rf = b * S5_NC + start + k
                rb = b * S5_NC + start + n - 1 - k
                s_scr[0, pl.ds(rf, 1), :] = sfr
                s_scr[1, pl.ds(rf, 1), :] = sfi
                s_scr[2, pl.ds(rb, 1), :] = sbr
                s_scr[3, pl.ds(rb, 1), :] = sbi
                nfr = arf * sfr - aif * sfi + e_scr[0, pl.ds(rf, 1), :]
                nfi = arf * sfi + aif * sfr + e_scr[1, pl.ds(rf, 1), :]
                nbr = arb * sbr - aib * sbi + e_scr[2, pl.ds(rb, 1), :]
                nbi = arb * sbi + aib * sbr + e_scr[3, pl.ds(rb, 1), :]
                out += [nfr, nfi, nbr, nbi]
            return tuple(out)
        return lax.fori_loop(0, n, body, carry, unroll=4)

    z = jnp.zeros((1, S5_P), F32)
    carry = segment(0, S5_NCC, (z,) * (4 * BATCH))
    segment(S5_NCC, S5_NC - S5_NCC, carry)
    y = jnp.dot(u, m_ref[...], preferred_element_type=F32)
    for k in range(4):
        y += jnp.dot(s_scr[k].astype(BF16), f_ref[k], preferred_element_type=F32)
    o_ref[...] = y


def s5_scan(u, m, e, f, a16):
    g, r, w = u.shape
    return pl.pallas_call(
        _s5_kernel, grid=(g,),
        in_specs=[pl.BlockSpec((None, r, w), lambda i: (i, 0, 0)),
                  pl.BlockSpec((None, w, w), lambda i: (i, 0, 0)),
                  pl.BlockSpec((4, None, w, S5_P), lambda i: (0, i, 0, 0)),
                  pl.BlockSpec((4, None, S5_P, w), lambda i: (0, i, 0, 0)),
                  pl.BlockSpec((4, None, 1, S5_P), lambda i: (0, i, 0, 0))],
        out_specs=pl.BlockSpec((None, r, w), lambda i: (i, 0, 0)),
        out_shape=jax.ShapeDtypeStruct((g, r, w), F32),
        scratch_shapes=[pltpu.VMEM((4, r, S5_P), F32), pltpu.VMEM((4, r, S5_P), F32)],
        compiler_params=_cparams(("parallel",)), name="s5_scan",
    )(u, m, e, f, a16)


def _s5_out_kernel(y_ref, u_ref, d_ref, w_ref, b_ref, o_ref):
    g = _gelu_tanh(y_ref[...] + d_ref[...] * u_ref[...])
    z = jnp.dot(g.astype(BF16), w_ref[...], preferred_element_type=F32) + b_ref[...]
    o_ref[...] = (g * jax.nn.sigmoid(z)).astype(o_ref.dtype)


def s5_out(y, u, d_skip, w_glu, b_glu, tm):
    m, c = y.shape
    row = pl.BlockSpec((tm, c), lambda i: (i, 0))
    vec = pl.BlockSpec((1, c), lambda i: (0, 0))
    return pl.pallas_call(
        _s5_out_kernel, grid=(m // tm,),
        in_specs=[row, row, vec, pl.BlockSpec((c, c), lambda i: (0, 0)), vec],
        out_specs=row, out_shape=jax.ShapeDtypeStruct((m, c), BF16),
        compiler_params=_cparams(("parallel",)), name="s5_out",
    )(y, u, d_skip, w_glu, b_glu)


def s5_mixer(u_c, u_l, lam_re, lam_im, log_step, b_re, b_im, c_re, c_im, d_skip, w_glu, b_glu):
    bs = u_c.shape[0]
    u_all = jnp.concatenate([u_c, u_l], axis=1)
    uc = u_all.astype(BF16).reshape(bs, S5_NC, S5_T, S5_G, S5_CH).transpose(3, 0, 1, 2, 4)
    uc = uc.reshape(S5_G, bs * S5_NC, S5_W)
    y = s5_scan(uc, *s5_chunk_operators(lam_re, lam_im, log_step, b_re, b_im, c_re, c_im))
    y = y.reshape(S5_G, bs, S5_NC, S5_T, S5_CH).transpose(1, 2, 3, 0, 4).reshape(bs * SEQ_ALL, MIX_W)
    out = s5_out(y, u_all.reshape(bs * SEQ_ALL, MIX_W), d_skip[None], w_glu.astype(BF16), b_glu[None], 544)
    out = out.reshape(bs, SEQ_ALL, MIX_W)
    return out[:, :CTX_LEN], out[:, CTX_LEN:]


LRU_CB = 256
LRU_TB = 256
LRU_HALO = 8
LRU_ROWS = SEQ_ALL + 3 * LRU_HALO


def _lru_kernel(xc_ref, xl_ref, yc_ref, yl_ref, cw_ref, cb_ref, wa_ref, ba_ref, wi_ref, bi_ref, sp_ref,
                oc_ref, ol_ref, xpad, a_f, b_f, a_b, b_b):
    halo = jnp.zeros((LRU_HALO, LRU_CB), F32)
    lat0 = CTX_LEN + 2 * LRU_HALO
    xpad[0:LRU_HALO] = halo
    xpad[LRU_HALO:LRU_HALO + CTX_LEN] = xc_ref[...]
    xpad[LRU_HALO + CTX_LEN:lat0] = halo
    xpad[lat0:lat0 + SEQ] = xl_ref[...]
    xpad[lat0 + SEQ:LRU_ROWS] = halo
    cw = [cw_ref[j:j + 1, :] for j in range(CONV_W)]
    cb = cb_ref[...]

    def gates(p0, r0):
        ext = xpad[pl.ds(p0 - LRU_HALO, LRU_TB + 2 * LRU_HALO), :]
        xb = cb
        for j in range(CONV_W):
            s0 = LRU_HALO - CONV_PAD_L + j
            xb = xb + cw[j] * ext[s0:s0 + LRU_TB]
        for k in range(LRU_CB // LRU_BD):
            lanes = slice(k * LRU_BD, (k + 1) * LRU_BD)
            xk = xb[:, lanes]
            xkb = xk.astype(BF16)
            for d, (a_s, b_s) in enumerate(((a_f, b_f), (a_b, b_b))):
                r = jax.nn.sigmoid(jnp.dot(xkb, wa_ref[d, k], preferred_element_type=F32) + ba_ref[d][:, lanes])
                i = jax.nn.sigmoid(jnp.dot(xkb, wi_ref[d, k], preferred_element_type=F32) + bi_ref[d][:, lanes])
                log_a = -LRU_C * r * sp_ref[d][:, lanes]
                t = jnp.tanh(log_a)
                a_s[pl.ds(r0, LRU_TB), lanes] = jnp.exp(log_a)
                b_s[pl.ds(r0, LRU_TB), lanes] = jnp.sqrt(-2.0 * t / (1.0 - t)) * (i * xk)

    gates(LRU_HALO, 0)

    def lat_gates(k, _):
        gates(pl.multiple_of(lat0 + k * LRU_TB, 8), pl.multiple_of(CTX_LEN + k * LRU_TB, 8))
        return 0
    lax.fori_loop(0, SEQ // LRU_TB, lat_gates, 0)

    def segment(start, n, carry):
        def body(t, carry):
            hf, hb = carry
            rf = start + t
            rb = start + n - 1 - t
            hf = a_f[pl.ds(rf, 1), :] * hf + b_f[pl.ds(rf, 1), :]
            b_f[pl.ds(rf, 1), :] = hf
            hb = a_b[pl.ds(rb, 1), :] * hb + b_b[pl.ds(rb, 1), :]
            b_b[pl.ds(rb, 1), :] = hb
            return hf, hb
        return lax.fori_loop(0, n, body, carry, unroll=8)

    z = jnp.zeros((1, LRU_CB), F32)
    carry = segment(0, CTX_LEN, (z, z))
    segment(CTX_LEN, SEQ, carry)

    oc_ref[...] = (_gelu_tanh(yc_ref[...]) * (b_f[0:CTX_LEN] + b_b[0:CTX_LEN])).astype(oc_ref.dtype)

    def lat_out(k, _):
        r0 = pl.multiple_of(k * LRU_TB, 8)
        rows = pl.ds(pl.multiple_of(CTX_LEN + k * LRU_TB, 8), LRU_TB)
        h = b_f[rows, :] + b_b[rows, :]
        ol_ref[pl.ds(r0, LRU_TB), :] = (_gelu_tanh(yl_ref[pl.ds(r0, LRU_TB), :]) * h).astype(ol_ref.dtype)
        return 0
    lax.fori_loop(0, SEQ // LRU_TB, lat_out, 0)


def lru_mixer(y_c, x_c, y_l, x_l, conv_w, conv_b, w_a, b_a, w_i, b_i, lam):
    bs = x_c.shape[0]
    nb = LRU_CB // LRU_BD
    sp = jax.nn.softplus(-lam)[:, None, :]
    ctx_spec = pl.BlockSpec((None, CTX_LEN, LRU_CB), lambda b, c: (b, 0, c))
    lat_spec = pl.BlockSpec((None, SEQ, LRU_CB), lambda b, c: (b, 0, c))
    vec2 = pl.BlockSpec((2, 1, LRU_CB), lambda b, c: (0, 0, c))
    wspec = pl.BlockSpec((2, nb, LRU_BD, LRU_BD), lambda b, c: (0, c, 0, 0))
    scan_buf = pltpu.VMEM((SEQ_ALL, LRU_CB), F32)
    return pl.pallas_call(
        _lru_kernel, grid=(bs, MIX_W // LRU_CB),
        in_specs=[ctx_spec, lat_spec, ctx_spec, lat_spec,
                  pl.BlockSpec((CONV_W, LRU_CB), lambda b, c: (0, c)),
                  pl.BlockSpec((1, LRU_CB), lambda b, c: (0, c)),
                  wspec, vec2, wspec, vec2, vec2],
        out_specs=[ctx_spec, lat_spec],
        out_shape=[jax.ShapeDtypeStruct((bs, CTX_LEN, MIX_W), BF16), jax.ShapeDtypeStruct((bs, SEQ, MIX_W), BF16)],
        scratch_shapes=[pltpu.VMEM((LRU_ROWS, LRU_CB), F32), scan_buf, scan_buf, scan_buf, scan_buf],
        compiler_params=_cparams(("parallel", "parallel")), name="lru",
    )(x_c, x_l, y_c, y_l, conv_w, conv_b[None], w_a.astype(BF16), b_a[:, None, :], w_i.astype(BF16),
      b_i[:, None, :], sp)


def to_col_major(t):
    b, l = t.shape[:2]
    rows = l // GRID_W
    return jnp.swapaxes(t.reshape(b, rows, GRID_W, -1), 1, 2).reshape(t.shape)


def to_row_major(t):
    b, l = t.shape[:2]
    rows = l // GRID_W
    return jnp.swapaxes(t.reshape(b, GRID_W, rows, -1), 1, 2).reshape(t.shape)


N_CHUNKS = SEQ_ALL // CHUNK
N_CCH = CTX_LEN // CHUNK
GATE_COLBLK = 80
MIX_TB = 256


def _dot(a, b):
    return jnp.dot(a, b, preferred_element_type=F32)


def _dot_nt(a, b):
    return lax.dot_general(a, b, (((1,), (1,)), ((), ())), preferred_element_type=F32)


def _dot_tn(a, b):
    return lax.dot_general(a, b, (((0,), (0,)), ((), ())), preferred_element_type=F32)


def _split2(a):
    hi = a.astype(BF16)
    return hi, (a - hi.astype(F32)).astype(BF16)


def _dot3(a, b):
    ah, al = _split2(a)
    bh, bl = _split2(b)
    return _dot(ah, bh) + (_dot(ah, bl) + _dot(al, bh))


def _dot_mask(m, x):
    hi = x.astype(BF16)
    r1 = x - hi.astype(F32)
    mid = r1.astype(BF16)
    lo = (r1 - mid.astype(F32)).astype(BF16)
    return _dot(m, hi) + (_dot(m, mid) + _dot(m, lo))


def _chunk_masks(backward):
    i = lax.broadcasted_iota(jnp.int32, (CHUNK, CHUNK), 0)
    j = lax.broadcasted_iota(jnp.int32, (CHUNK, CHUNK), 1)
    if backward:
        i, j = j, i
    tri, strict = j <= i, j < i
    return tri, strict, tri.astype(BF16), jnp.where(strict, 1.0, 0.0).astype(F32)


def _lane_pick(blk, lane):
    lanes = lax.broadcasted_iota(jnp.int32, blk.shape, 1)
    return jnp.sum(jnp.where(lanes == lane, blk, 0.0), axis=-1, keepdims=True)


def _softplus(x):
    return jnp.maximum(x, 0.0) + jnp.log1p(jnp.exp(-jnp.abs(x)))


def _silu(x):
    return x * jax.nn.sigmoid(x)


def _bidir_chunks(step, carry, n_ctx, n_all):
    def segment(start, n, carry):
        def body(t, carry):
            return step(start + t, start + n - 1 - t, carry)
        return lax.fori_loop(0, n, body, carry)
    carry = segment(0, n_ctx, carry)
    return segment(n_ctx, n_all - n_ctx, carry)


DN_HALO = 8
DN_ROWS = SEQ_ALL + 3 * DN_HALO
DN_C = 64
DN_GROUP = 256
DN_GC = DN_GROUP // (2 * DN_C)
DN_NCH = SEQ_ALL // DN_C
DN_NCC = CTX_LEN // DN_C
DN_LOCKSTEP = 4


def _dn_kernel(qc_ref, ql_ref, kc_ref, kl_ref, vc_ref, vl_ref, zc_ref, zl_ref, gc_ref, gl_ref, hp_ref,
               cwq_ref, cwk_ref, cwv_ref, nw_ref, ol_ref, oc_ref,
               xpad, qn, kn, vn, u_s, w_s, qg_s, kd_s, at_s, dec_s):
    h = pl.program_id(1)
    halo = jnp.zeros((DN_HALO, DN_HD), F32)
    lat0 = CTX_LEN + 2 * DN_HALO

    def conv_into(xc_ref, xl_ref, cw_ref, dst, post):
        xpad[0:DN_HALO] = halo
        xpad[DN_HALO:DN_HALO + CTX_LEN] = xc_ref[...]
        xpad[DN_HALO + CTX_LEN:lat0] = halo
        xpad[lat0:lat0 + SEQ] = xl_ref[...]
        xpad[lat0 + SEQ:DN_ROWS] = halo
        cw = [cw_ref[j:j + 1, :] for j in range(CONV_W)]

        def block(p0, r0):
            ext = xpad[pl.ds(p0 - DN_HALO, MIX_TB + 2 * DN_HALO), :]
            s0 = DN_HALO - CONV_PAD_L
            acc = cw[0] * ext[s0:s0 + MIX_TB]
            for j in range(1, CONV_W):
                acc = acc + cw[j] * ext[s0 + j:s0 + j + MIX_TB]
            dst[pl.ds(r0, MIX_TB), :] = post(_silu(acc))

        block(DN_HALO, 0)

        def lat_block(k, _):
            block(pl.multiple_of(lat0 + k * MIX_TB, 8), pl.multiple_of(CTX_LEN + k * MIX_TB, 8))
            return 0
        lax.fori_loop(0, SEQ // MIX_TB, lat_block, 0)

    def l2n(scale):
        return lambda t: t * (lax.rsqrt(jnp.sum(t * t, axis=-1, keepdims=True) + EPS) * scale)

    conv_into(qc_ref, ql_ref, cwq_ref, qn, l2n(DN_HD ** -0.5))
    conv_into(kc_ref, kl_ref, cwk_ref, kn, l2n(1.0))
    conv_into(vc_ref, vl_ref, cwv_ref, vn, lambda t: t)

    gi = lax.broadcasted_iota(jnp.int32, (DN_GROUP, DN_GROUP), 0)
    gj = lax.broadcasted_iota(jnp.int32, (DN_GROUP, DN_GROUP), 1)
    same = (gi // DN_C) == (gj // DN_C)
    blk16 = (gi // 16) == (gj // 16)
    bwd = ((gi // DN_C) % 2) == 1
    fwd = jnp.logical_not(bwd)
    li, lj = gi % DN_C, gj % DN_C
    tri = same & ((fwd & (lj <= li)) | (bwd & (lj >= li)))
    strict = same & ((fwd & (lj < li)) | (bwd & (lj > li)))
    eye = jnp.where(gi == gj, 1.0, 0.0).astype(F32)
    same16 = jnp.where(same, 1.0, 0.0).astype(BF16)
    m16 = jnp.where(tri, 1.0, 0.0).astype(BF16)
    nn = jnp.where(strict, 1.0, 0.0).astype(F32)

    def both_dirs(xf, xb):
        parts = []
        for c in range(DN_GC):
            parts += [xf[c * DN_C:(c + 1) * DN_C], xb[c * DN_C:(c + 1) * DN_C]]
        return jnp.concatenate(parts, axis=0)

    def prep(ts, g_ref, row0):
        grows = DN_GC * DN_C
        n = range(len(ts))
        r0 = [pl.multiple_of(t * grows, grows) for t in ts]
        gblk = [g_ref[pl.ds(pl.multiple_of(t * grows - row0, grows), grows), :] for t in ts]
        q = [both_dirs(*[qn[pl.ds(r0[i], grows), :]] * 2) for i in n]
        k = [both_dirs(*[kn[pl.ds(r0[i], grows), :]] * 2) for i in n]
        v = [both_dirs(*[vn[pl.ds(r0[i], grows), :]] * 2) for i in n]
        g_col = [both_dirs(*[hp_ref[:, d:d + 1] * _softplus(_lane_pick(gblk[i], d * DN_H + h)
                                                            + hp_ref[:, 2 + d:3 + d]) for d in range(2)]) for i in n]
        beta = [both_dirs(*[jax.nn.sigmoid(_lane_pick(gblk[i], 2 * DN_H + d * DN_H + h)) for d in range(2)])
                for i in n]
        k16 = [k[i].astype(BF16) for i in n]
        kb = [k[i] * beta[i] for i in n]
        gb = [jnp.broadcast_to(g_col[i], (DN_GROUP, DN_HD)) for i in n]
        dlt = [_dot_mask(m16, g_col[i] * nn) for i in n]
        kk = [_dot_nt(kb[i].astype(BF16), k16[i]) for i in n]
        gcum = [_dot_mask(m16, gb[i]) for i in n]
        tot = [_dot_mask(same16, gb[i]) for i in n]
        qk = [_dot_nt(q[i].astype(BF16), k16[i]) for i in n]
        decay = [jnp.where(tri, jnp.exp(jnp.where(tri, dlt[i], 0.0)), 0.0) for i in n]
        lower = [jnp.where(strict, kk[i] * decay[i], 0.0) for i in n]
        x = [-jnp.where(blk16, lower[i], 0.0) for i in n]
        tinv = [eye + x[i] for i in n]
        for _ in range(3):
            x = [_dot3(x[i], x[i]) for i in n]
            tinv = [tinv[i] + _dot3(tinv[i], x[i]) for i in n]
        size = 16
        while size < DN_C:
            inner, outer = (gi // size) == (gj // size), (gi // (2 * size)) == (gj // (2 * size))
            off = [_dot3(jnp.where(outer & jnp.logical_not(inner), lower[i], 0.0), tinv[i]) for i in n]
            tinv = [tinv[i] - _dot3(tinv[i], off[i]) for i in n]
            size *= 2
        eg = [jnp.exp(gcum[i]) for i in n]
        sol = [_dot3(tinv[i], jnp.concatenate([v[i] * beta[i], kb[i] * eg[i]], axis=1)) for i in n]
        for i in n:
            attn = (qk[i] * decay[i]).astype(BF16)
            w16 = sol[i][:, DN_HD:].astype(BF16)
            qg16 = (q[i] * eg[i]).astype(BF16)
            kd16 = (k[i] * jnp.exp(tot[i] - gcum[i])).astype(BF16)
            dec = jnp.exp(tot[i])
            for p in range(DN_GROUP // DN_C):
                d, blk = p % 2, slice(p * DN_C, (p + 1) * DN_C)
                dst = pl.ds(r0[i] + (p // 2) * DN_C, DN_C)
                u_s[d, dst, :] = sol[i][blk, :DN_HD]
                w_s[d, dst, :] = w16[blk]
                at_s[d, dst, :] = attn[blk, blk]
                qg_s[d, dst, :] = qg16[blk]
                kd_s[d, dst, :] = kd16[blk]
                dec_s[d, pl.ds(pl.multiple_of((DN_GC * ts[i] + p // 2) * 8, 8), 8), :] = dec[p * DN_C:p * DN_C + 8]
        return 0

    n_ctx_groups, n_groups = DN_NCC // DN_GC, DN_NCH // DN_GC
    assert (n_groups - n_ctx_groups) % DN_LOCKSTEP == 0
    prep(list(range(n_ctx_groups)), gc_ref, 0)
    lax.fori_loop(0, (n_groups - n_ctx_groups) // DN_LOCKSTEP,
                  lambda t, _: prep([n_ctx_groups + DN_LOCKSTEP * t + i for i in range(DN_LOCKSTEP)],
                                    gl_ref, CTX_LEN), 0)
    o_dst = (qn, kn)
    o_f, o_b = o_dst

    def step(cf, cb, carry):
        dirs = range(2)
        rows = [pl.ds(pl.multiple_of(c * DN_C, DN_C), DN_C) for c in (cf, cb)]
        s16 = [carry[d].astype(BF16) for d in dirs]
        ws = [_dot(w_s[d, rows[d], :], s16[d]) for d in dirs]
        qs = [_dot(qg_s[d, rows[d], :], s16[d]) for d in dirs]
        v_new = [(u_s[d, rows[d], :] - ws[d]).astype(BF16) for d in dirs]
        kv = [_dot_tn(kd_s[d, rows[d], :], v_new[d]) for d in dirs]
        av = [_dot(at_s[d, rows[d], :], v_new[d]) for d in dirs]
        out = []
        for d, c in zip(dirs, (cf, cb)):
            o_dst[d][rows[d], :] = qs[d] + av[d]
            g_last = dec_s[d, pl.ds(pl.multiple_of(c * 8, 8), 8), :][0:1]
            out.append(carry[d] * g_last + kv[d])
        return tuple(out)

    z0 = jnp.zeros((DN_HD, DN_HD), F32)
    _bidir_chunks(step, (z0, z0), DN_NCC, DN_NCH)

    nw = nw_ref[...]

    def finish(rows, z):
        o = o_f[rows, :] + o_b[rows, :]
        o = o * lax.rsqrt(jnp.mean(o * o, axis=-1, keepdims=True) + EPS) * nw
        return (o * _silu(z)).astype(BF16)

    oc_ref[...] = finish(pl.ds(0, CTX_LEN), zc_ref[...])

    def lat_finish(k, _):
        r0 = pl.multiple_of(k * MIX_TB, 8)
        ol_ref[pl.ds(r0, MIX_TB), :] = finish(pl.ds(pl.multiple_of(CTX_LEN + k * MIX_TB, 8), MIX_TB),
                                              zl_ref[pl.ds(r0, MIX_TB), :])
        return 0
    lax.fori_loop(0, SEQ // MIX_TB, lat_finish, 0)


def deltanet_mixer(proj, conv_w, a_log, dt_bias, norm_w):
    hp = jnp.stack([-jnp.exp(a_log[0]), -jnp.exp(a_log[1]), dt_bias[0], dt_bias[1]], axis=-1)[:, None, :]
    ctx_blk0 = N_LAT // CTX_LEN

    def seg_specs(col0):
        cb = col0 // DN_HD
        return [pl.BlockSpec((CTX_LEN, DN_HD), lambda b, h: (ctx_blk0 + b, cb + h)),
                pl.BlockSpec((SEQ, DN_HD), lambda b, h: (b, cb + h))]

    def conv_spec(k):
        return pl.BlockSpec((CONV_W, DN_HD), lambda b, h: (0, k * DN_H + h))

    in_specs = (seg_specs(COL_DN_QKVZ) + seg_specs(COL_DN_QKVZ + MIX_W) + seg_specs(COL_DN_QKVZ + 2 * MIX_W)
                + seg_specs(COL_DN_QKVZ + 3 * MIX_W)
                + [pl.BlockSpec((CTX_LEN, 128), lambda b, h: (ctx_blk0 + b, GATE_COLBLK)),
                   pl.BlockSpec((SEQ, 128), lambda b, h: (b, GATE_COLBLK)),
                   pl.BlockSpec((None, 1, 4), lambda b, h: (h, 0, 0)),
                   conv_spec(0), conv_spec(1), conv_spec(2),
                   pl.BlockSpec((1, DN_HD), lambda b, h: (0, 0))])
    seq_f32 = pltpu.VMEM((SEQ_ALL, DN_HD), F32)
    dir_bf16 = pltpu.VMEM((2, SEQ_ALL, DN_HD), BF16)
    ol, oc = pl.pallas_call(
        _dn_kernel, grid=(BATCH, DN_H), in_specs=in_specs,
        out_specs=[pl.BlockSpec((SEQ, DN_HD), lambda b, h: (b, h)),
                   pl.BlockSpec((CTX_LEN, DN_HD), lambda b, h: (b, h))],
        out_shape=[jax.ShapeDtypeStruct((N_LAT, MIX_W), BF16), jax.ShapeDtypeStruct((N_CTX, MIX_W), BF16)],
        scratch_shapes=[pltpu.VMEM((DN_ROWS, DN_HD), F32), seq_f32, seq_f32, seq_f32,
                        pltpu.VMEM((2, SEQ_ALL, DN_HD), F32), dir_bf16, dir_bf16, dir_bf16,
                        pltpu.VMEM((2, SEQ_ALL, DN_C), BF16), pltpu.VMEM((2, DN_NCH * 8, DN_HD), F32)],
        compiler_params=_cparams(("parallel", "arbitrary")), name="deltanet",
    )(*([proj] * 10), hp, conv_w, conv_w, conv_w, norm_w[None])
    return ol, oc


def _ml_kernel(qc_ref, ql_ref, kc_ref, kl_ref, vc_ref, vl_ref, oc_ref, ol_ref, gc_ref, gl_ref, hp_ref, nw_ref,
               yl_ref, yc_ref, h_f, h_b):
    hh = pl.program_id(1)
    masks = [_chunk_masks(False), _chunk_masks(True)]
    ones = jnp.ones((CHUNK, CHUNK), BF16)
    eye = jnp.where(lax.broadcasted_iota(jnp.int32, (CHUNK, CHUNK), 0)
                    == lax.broadcasted_iota(jnp.int32, (CHUNK, CHUNK), 1), 1.0, 0.0).astype(F32)

    h_dst = (h_f, h_b)

    def advance(cs, states, refs, row0):
        q_ref, k_ref, v_ref, g_ref = refs
        dirs = range(2)
        rows = [pl.ds(pl.multiple_of(c * CHUNK - row0, CHUNK), CHUNK) for c in cs]
        tri = [masks[d][0] for d in dirs]
        gblk = [g_ref[rows[d], :] for d in dirs]
        ig = [_lane_pick(gblk[d], 4 * DN_H + d * ML_H + hh) + hp_ref[:, d:d + 1] for d in dirs]
        lf = [-_softplus(-(_lane_pick(gblk[d], 4 * DN_H + 2 * ML_H + d * ML_H + hh) + hp_ref[:, 2 + d:3 + d]))
              for d in dirs]
        q = [q_ref[rows[d], :] * (ML_DQK ** -0.5) for d in dirs]
        k = [k_ref[rows[d], :] for d in dirs]
        q16 = [q[d].astype(BF16) for d in dirs]
        v16 = [v_ref[rows[d], :].astype(BF16) for d in dirs]
        qc = [_dot(q16[d], states[d][0].astype(BF16)) for d in dirs]
        bcum = [_dot_mask(masks[d][2], jnp.broadcast_to(lf[d], (CHUNK, ML_DQK)))[:, 0:1] for d in dirs]
        dsum = [_dot_mask(masks[d][2], lf[d] * masks[d][3]) + _dot_mask(ones, ig[d] * eye) for d in dirs]
        sqk = [_dot_nt(q16[d], k[d].astype(BF16)) for d in dirs]
        dmat = [jnp.where(tri[d], dsum[d], -jnp.inf) for d in dirs]
        dmax = [jnp.max(dmat[d], axis=-1, keepdims=True) for d in dirs]
        tot = [jnp.sum(lf[d], axis=0, keepdims=True) for d in dirs]
        logw = [tot[d] - bcum[d] + ig[d] for d in dirs]
        logw_max = [jnp.max(logw[d], axis=0, keepdims=True) for d in dirs]
        inter = [bcum[d] + states[d][2] for d in dirs]
        m_t = [jnp.maximum(inter[d], dmax[d]) for d in dirs]
        w_inter = [jnp.exp(inter[d] - m_t[d]) for d in dirs]
        wmat = [jnp.exp(dmat[d] - m_t[d]) * sqk[d] for d in dirs]
        m_new = [jnp.maximum(tot[d] + states[d][2], logw_max[d]) for d in dirs]
        keep = [jnp.exp(tot[d] + states[d][2] - m_new[d]) for d in dirs]
        kw = [k[d] * jnp.exp(logw[d] - m_new[d]) for d in dirs]
        wv = [_dot(wmat[d].astype(BF16), v16[d]) for d in dirs]
        kv = [_dot_tn(kw[d].astype(BF16), v16[d]) for d in dirs]
        out = []
        for d in dirs:
            c_s, n_s, _ = states[d]
            num = w_inter[d] * qc[d] + wv[d]
            den = w_inter[d] * jnp.sum(q[d] * n_s, axis=-1, keepdims=True) + jnp.sum(wmat[d], axis=-1, keepdims=True)
            h_dst[d][pl.ds(pl.multiple_of(cs[d] * CHUNK, CHUNK), CHUNK), :] = (
                num / jnp.maximum(jnp.abs(den), jnp.exp(-m_t[d])))
            out.append((keep[d] * c_s + kv[d], keep[d] * n_s + jnp.sum(kw[d], axis=0, keepdims=True), m_new[d]))
        return tuple(out)

    ctx_refs = (qc_ref, kc_ref, vc_ref, gc_ref)
    lat_refs = (ql_ref, kl_ref, vl_ref, gl_ref)

    def segment(refs, start, n, carry):
        def body(t, carry):
            return advance((start + t, start + n - 1 - t), carry, refs, start * CHUNK)
        return lax.fori_loop(0, n, body, carry)

    s0 = (jnp.zeros((ML_DQK, ML_DV), F32), jnp.zeros((1, ML_DQK), F32), jnp.zeros((1, 1), F32))
    carry = segment(ctx_refs, 0, N_CCH, (s0, s0))
    segment(lat_refs, N_CCH, N_CHUNKS - N_CCH, carry)

    nw = nw_ref[...]

    def finish(rows, o):
        hsum = h_f[rows, :] + h_b[rows, :]
        hsum = hsum * lax.rsqrt(jnp.mean(hsum * hsum, axis=-1, keepdims=True) + EPS) * nw
        return (hsum * jax.nn.sigmoid(o)).astype(BF16)

    yc_ref[...] = finish(pl.ds(0, CTX_LEN), oc_ref[...])

    def lat_finish(k, _):
        r0 = pl.multiple_of(k * MIX_TB, 8)
        yl_ref[pl.ds(r0, MIX_TB), :] = finish(pl.ds(pl.multiple_of(CTX_LEN + k * MIX_TB, 8), MIX_TB),
                                              ol_ref[pl.ds(r0, MIX_TB), :])
        return 0
    lax.fori_loop(0, SEQ // MIX_TB, lat_finish, 0)


def mlstm_mixer(proj, i_bias, f_bias, norm_w):
    hp = jnp.stack([i_bias[0], i_bias[1], f_bias[0], f_bias[1]], axis=-1)[:, None, :]
    ctx_blk0 = N_LAT // CTX_LEN

    def seg_specs(col0, w):
        cb = col0 // w
        return [pl.BlockSpec((CTX_LEN, w), lambda b, h: (ctx_blk0 + b, cb + h)),
                pl.BlockSpec((SEQ, w), lambda b, h: (b, cb + h))]

    in_specs = (seg_specs(COL_ML_Q, ML_DQK) + seg_specs(COL_ML_K, ML_DQK) + seg_specs(COL_ML_V, ML_DV)
                + seg_specs(COL_ML_O, ML_DV)
                + [pl.BlockSpec((CTX_LEN, 128), lambda b, h: (ctx_blk0 + b, GATE_COLBLK)),
                   pl.BlockSpec((SEQ, 128), lambda b, h: (b, GATE_COLBLK)),
                   pl.BlockSpec((None, 1, 4), lambda b, h: (h, 0, 0)),
                   pl.BlockSpec((1, ML_DV), lambda b, h: (0, h))])
    seq_f32 = pltpu.VMEM((SEQ_ALL, ML_DV), F32)
    yl, yc = pl.pallas_call(
        _ml_kernel, grid=(BATCH, ML_H), in_specs=in_specs,
        out_specs=[pl.BlockSpec((SEQ, ML_DV), lambda b, h: (b, h)),
                   pl.BlockSpec((CTX_LEN, ML_DV), lambda b, h: (b, h))],
        out_shape=[jax.ShapeDtypeStruct((N_LAT, MIX_W), BF16), jax.ShapeDtypeStruct((N_CTX, MIX_W), BF16)],
        scratch_shapes=[seq_f32, seq_f32],
        compiler_params=_cparams(("parallel", "arbitrary")), name="mlstm",
    )(*([proj] * 10), hp, norm_w[None])
    return yl, yc


PROJ_TN = 512
PROJ_PAD = -(-PROJ_W // PROJ_TN) * PROJ_TN
COL_S5, COL_DN_QKVZ, COL_ML_Q, COL_ML_K, COL_ML_V, COL_ML_O, COL_LRU_Y, COL_LRU_X = (
    0, 1024, 5120, 5632, 6144, 7168, 8192, 9216)
COL_DN_A, COL_DN_B, COL_ML_I, COL_ML_F = 10240, 10256, 10272, 10280


def permute_proj_weight(w):
    o = [0] + [int(v) for v in np.cumsum(PROJ_SIZES)]
    parts = [w[:, o[0]:o[5]], w[:, o[7]:o[11]], w[:, o[13]:o[15]], w[:, o[5]:o[7]], w[:, o[11]:o[13]]]
    return jnp.pad(jnp.concatenate(parts, axis=1).astype(BF16), ((0, 0), (0, PROJ_PAD - PROJ_W)))


def hybrid_mixer(proj, s5_p, dn_p, ml_p, lru_p):
    def seg(c0, w):
        blk = proj[:, c0:c0 + w]
        return blk[N_LAT:].reshape(BATCH, CTX_LEN, w), blk[:N_LAT].reshape(BATCH, SEQ, w)

    u_c, u_l = seg(COL_S5, MIX_W)
    a_c, a_l = s5_mixer(u_c, to_col_major(u_l), *s5_p)
    y_c, y_l = seg(COL_LRU_Y, MIX_W)
    x_c, x_l = seg(COL_LRU_X, MIX_W)
    d_c, d_l = lru_mixer(y_c, x_c, to_col_major(y_l), to_col_major(x_l), *lru_p)
    flat = lambda t_l, t_c: (to_row_major(t_l).reshape(N_LAT, MIX_W), t_c.reshape(N_CTX, MIX_W))
    return [flat(a_l, a_c), deltanet_mixer(proj, *dn_p), mlstm_mixer(proj, *ml_p), flat(d_l, d_c)]


def kernel(x, c, ctx, c_ctx, w_ada, b_ada, norm1, norm2, norm_f, w_in, w_out, s5_lam_re, s5_lam_im, s5_log_step,
           s5_b_re, s5_b_im, s5_c_re, s5_c_im, s5_d, s5_w_glu, s5_b_glu, dn_conv, dn_a_log, dn_dt_bias, dn_norm,
           ml_i_bias, ml_f_bias, ml_norm, lru_conv_w, lru_conv_b, lru_w_a, lru_b_a, lru_w_i, lru_b_i, lru_lam,
           router_w, router_bias, moe_w1, moe_w3, moe_w2):
    d = D_MODEL
    xa = jnp.concatenate([x.reshape(N_LAT, d), ctx.reshape(N_CTX, d)], axis=0)
    cond = jnp.concatenate([jax.nn.silu(c), jax.nn.silu(c_ctx)[None], jnp.zeros((16 - BATCH - 1, d), F32)], axis=0)
    cond = cond.astype(BF16)
    router = (router_w.T, router_bias.reshape(N_EXPERTS, 1))

    for l in range(DEPTH):
        last = l == DEPTH - 1
        mod = ada_matmul(l, cond, w_ada, b_ada[:, None, :], 512)[:BATCH + 1].reshape(BATCH + 1, 6, 1, d)
        shift1, scale1, gate1, shift2, scale2, gate2 = (mod[:, k] for k in range(6))

        h1 = norm_mod(xa, norm1[l][None], scale1, shift1, SEQ, 256)
        proj = matmul(h1, permute_proj_weight(w_in[l]), 512, PROJ_TN)
        slabs = hybrid_mixer(
            proj,
            (s5_lam_re[l], s5_lam_im[l], s5_log_step[l], s5_b_re[l], s5_b_im[l], s5_c_re[l], s5_c_im[l], s5_d[l],
             s5_w_glu[l], s5_b_glu[l]),
            (dn_conv[l], dn_a_log[l], dn_dt_bias[l], dn_norm[l]),
            (ml_i_bias[l], ml_f_bias[l], ml_norm[l]),
            (lru_conv_w[l], lru_conv_b[l], lru_w_a[l], lru_b_a[l], lru_w_i[l], lru_b_i[l], lru_lam[l]))
        if last:
            xa, mix = xa[:N_LAT], [s[0] for s in slabs]
        else:
            mix = [jnp.concatenate(s, axis=0) for s in slabs]
        xa = matmul_gated_residual(mix, w_out[l].astype(BF16), xa, gate1, SEQ, 512, 512)

        h2, idx, wsel = norm_mod(xa, norm2[l][None], scale2, shift2, SEQ, 256, router=router)
        y = moe_ffn(l, h2, idx, wsel, moe_w1, moe_w3, moe_w2)
        upd = (xa[:N_LAT].reshape(BATCH, SEQ, d) + gate2[:BATCH] * y[:N_LAT].reshape(BATCH, SEQ, d)).reshape(N_LAT, d)
        if last:
            xa = upd
        else:
            xa = jnp.concatenate([upd, xa[N_LAT:] + gate2[BATCH] * y[N_LAT:]], axis=0)

    return final_norm(xa[:N_LAT], norm_f[None], 256).reshape(BATCH, SEQ, d)
```

```python
import functools
import math

import numpy as np
import jax
import jax.numpy as jnp
from jax import lax
from jax.experimental import pallas as pl
from jax.experimental.pallas import tpu as pltpu

D_MODEL = 4096
BATCH = 2
SEQ = 4096
DEPTH = 2
GRID_W = 64
CTX_LEN = 256
MIX_W = D_MODEL // 4
MIX_TOTAL = 4 * MIX_W
CHUNK = 64
CONV_W = 4
CONV_PAD_L = 2
CONV_PAD_R = 1
EPS = 1e-6
S5_CH = 16
S5_G = MIX_W // S5_CH
S5_P = 64
DN_HD = 128
DN_H = MIX_W // DN_HD
ML_H = 4
ML_DV = MIX_W // ML_H
ML_DQK = ML_DV // 2
LRU_BLOCKS = 8
LRU_BD = MIX_W // LRU_BLOCKS
LRU_C = 8.0
N_EXPERTS = 16
N_GROUPS = 4
E_PER_G = N_EXPERTS // N_GROUPS
TOP_K = 2
D_FF = D_MODEL // 4
PROJ_SIZES = (MIX_W, MIX_W, MIX_W, MIX_W, MIX_W, 2 * DN_H, 2 * DN_H, ML_H * ML_DQK, ML_H * ML_DQK, MIX_W, MIX_W,
              2 * ML_H, 2 * ML_H, MIX_W, MIX_W)
PROJ_W = sum(PROJ_SIZES)

N_LAT = BATCH * SEQ
N_CTX = BATCH * CTX_LEN
VMEM_LIMIT = 56 * 1024 * 1024

F32 = jnp.float32
BF16 = jnp.bfloat16


def _cparams(sem):
    return pltpu.CompilerParams(dimension_semantics=sem, vmem_limit_bytes=VMEM_LIMIT)


def _mm_kernel(a_ref, b_ref, o_ref):
    o_ref[...] = jnp.dot(a_ref[...], b_ref[...], preferred_element_type=F32).astype(o_ref.dtype)


def matmul(a, b, tm, tn, out_dtype=F32):
    m, k = a.shape
    n = b.shape[1]
    assert m % tm == 0 and n % tn == 0
    return pl.pallas_call(
        _mm_kernel,
        grid=(m // tm, n // tn),
        in_specs=[pl.BlockSpec((tm, k), lambda i, j: (i, 0)),
                  pl.BlockSpec((k, tn), lambda i, j: (0, j))],
        out_specs=pl.BlockSpec((tm, tn), lambda i, j: (i, j)),
        out_shape=jax.ShapeDtypeStruct((m, n), out_dtype),
        compiler_params=_cparams(("parallel", "arbitrary")),
        name="mm",
    )(a, b)


def _mm_resid_kernel(a0_ref, a1_ref, a2_ref, a3_ref, b_ref, r_ref, g_ref, o_ref):
    acc = None
    for s, a_ref in enumerate((a0_ref, a1_ref, a2_ref, a3_ref)):
        part = jnp.dot(a_ref[...], b_ref[s * MIX_W:(s + 1) * MIX_W, :], preferred_element_type=F32)
        acc = part if acc is None else acc + part
    o_ref[...] = r_ref[...] + g_ref[...] * acc


def matmul_gated_residual(a_slabs, b, resid, gate, rows_per_gate, tm, tn):
    m = a_slabs[0].shape[0]
    k, n = b.shape
    assert m % tm == 0 and n % tn == 0 and rows_per_gate % tm == 0 and k == 4 * MIX_W
    return pl.pallas_call(
        _mm_resid_kernel,
        grid=(m // tm, n // tn),
        in_specs=[pl.BlockSpec((tm, MIX_W), lambda i, j: (i, 0))] * 4 + [
                  pl.BlockSpec((k, tn), lambda i, j: (0, j)),
                  pl.BlockSpec((tm, tn), lambda i, j: (i, j)),
                  pl.BlockSpec((None, 1, tn), lambda i, j: ((i * tm) // rows_per_gate, 0, j))],
        out_specs=pl.BlockSpec((tm, tn), lambda i, j: (i, j)),
        out_shape=jax.ShapeDtypeStruct((m, n), F32),
        compiler_params=_cparams(("parallel", "arbitrary")),
        name="mm_resid",
    )(*a_slabs, b, resid, gate)


def _ada_kernel(a_ref, w_ref, b_ref, o_ref):
    o_ref[...] = jnp.dot(a_ref[...], w_ref[...].astype(BF16), preferred_element_type=F32) + b_ref[...]


def ada_matmul(layer, a, w, bias, tn):
    m, k = a.shape
    n = w.shape[2]
    return pl.pallas_call(
        _ada_kernel,
        grid=(n // tn,),
        in_specs=[pl.BlockSpec((m, k), lambda j: (0, 0)),
                  pl.BlockSpec((None, k, tn), lambda j: (layer, 0, j)),
                  pl.BlockSpec((None, 1, tn), lambda j: (layer, 0, j))],
        out_specs=pl.BlockSpec((m, tn), lambda j: (0, j)),
        out_shape=jax.ShapeDtypeStruct((m, n), F32),
        compiler_params=_cparams(("arbitrary",)),
        name="ada",
    )(a, w, bias)


def _modulated_norm(x, g, sc, sh):
    ms = jnp.mean(x * x, axis=-1, keepdims=True)
    y = x * lax.rsqrt(ms + EPS) * g
    return y * (1.0 + sc) + sh


def _norm_mod_kernel(x_ref, g_ref, sc_ref, sh_ref, o_ref):
    o_ref[...] = _modulated_norm(x_ref[...], g_ref[...], sc_ref[...], sh_ref[...]).astype(o_ref.dtype)


def _pair_top2_sum(v):
    a, b, c, d = v
    return jnp.maximum(jnp.maximum(jnp.maximum(a + b, a + c), jnp.maximum(a + d, b + c)),
                       jnp.maximum(b + d, c + d))


def _norm_mod_router_kernel(x_ref, g_ref, sc_ref, sh_ref, rwt_ref, rb_ref, o_ref, idx_ref, w_ref):
    h = _modulated_norm(x_ref[...], g_ref[...], sc_ref[...], sh_ref[...])
    o_ref[...] = h.astype(o_ref.dtype)
    logits = lax.dot_general(rwt_ref[...], h, (((1,), (1,)), ((), ())),
                             precision=lax.Precision.HIGHEST, preferred_element_type=F32)
    scores = jax.nn.sigmoid(logits)
    biased = scores + rb_ref[...]
    s = [scores[e:e + 1, :] for e in range(N_EXPERTS)]
    b = [biased[e:e + 1, :] for e in range(N_EXPERTS)]
    gs = [_pair_top2_sum(b[E_PER_G * g:E_PER_G * (g + 1)]) for g in range(N_GROUPS)]
    best, gsel = gs[0], jnp.zeros_like(gs[0], dtype=jnp.int32)
    for g in range(1, N_GROUPS):
        better = gs[g] > best
        gsel = jnp.where(better, g, gsel)
        best = jnp.where(better, gs[g], best)
    vb, vs = [], []
    for k in range(E_PER_G):
        bk, sk = b[k], s[k]
        for g in range(1, N_GROUPS):
            bk = jnp.where(gsel == g, b[E_PER_G * g + k], bk)
            sk = jnp.where(gsel == g, s[E_PER_G * g + k], sk)
        vb.append(bk)
        vs.append(sk)
    m1, i1, w1 = vb[0], jnp.zeros_like(gsel), vs[0]
    for k in range(1, E_PER_G):
        better = vb[k] > m1
        i1 = jnp.where(better, k, i1)
        w1 = jnp.where(better, vs[k], w1)
        m1 = jnp.where(better, vb[k], m1)
    m2, i2, w2, have = vb[0], jnp.zeros_like(gsel), vs[0], i1 != 0
    for k in range(1, E_PER_G):
        valid = i1 != k
        better = valid & (jnp.logical_not(have) | (vb[k] > m2))
        i2 = jnp.where(better, k, i2)
        w2 = jnp.where(better, vs[k], w2)
        m2 = jnp.where(better, vb[k], m2)
        have = have | valid
    tot = w1 + w2
    idx_ref[0:1, :] = gsel * E_PER_G + i1
    idx_ref[1:2, :] = gsel * E_PER_G + i2
    w_ref[0:1, :] = w1 / tot
    w_ref[1:2, :] = w2 / tot


def norm_mod(x, g, sc, sh, rows_per_mod, tm, router=None):
    m, d = x.shape
    assert m % tm == 0 and rows_per_mod % tm == 0
    mod_spec = pl.BlockSpec((None, 1, d), lambda i: ((i * tm) // rows_per_mod, 0, 0))
    in_specs = [pl.BlockSpec((tm, d), lambda i: (i, 0)), pl.BlockSpec((1, d), lambda i: (0, 0)), mod_spec, mod_spec]
    o_spec = pl.BlockSpec((tm, d), lambda i: (i, 0))
    o_shape = jax.ShapeDtypeStruct((m, d), BF16)
    if router is None:
        return pl.pallas_call(
            _norm_mod_kernel, grid=(m // tm,), in_specs=in_specs, out_specs=o_spec, out_shape=o_shape,
            compiler_params=_cparams(("parallel",)), name="norm_mod",
        )(x, g, sc, sh)
    rwt, rb = router
    in_specs += [pl.BlockSpec((N_EXPERTS, d), lambda i: (0, 0)), pl.BlockSpec((N_EXPERTS, 1), lambda i: (0, 0))]
    sel_spec = pl.BlockSpec((TOP_K, tm), lambda i: (0, i))
    return pl.pallas_call(
        _norm_mod_router_kernel, grid=(m // tm,), in_specs=in_specs,
        out_specs=[o_spec, sel_spec, sel_spec],
        out_shape=[o_shape, jax.ShapeDtypeStruct((TOP_K, m), jnp.int32), jax.ShapeDtypeStruct((TOP_K, m), F32)],
        compiler_params=_cparams(("parallel",)), name="norm_mod_router",
    )(x, g, sc, sh, rwt, rb)


def _final_norm_kernel(x_ref, g_ref, o_ref):
    x = x_ref[...]
    ms = jnp.mean(x * x, axis=-1, keepdims=True)
    o_ref[...] = x * lax.rsqrt(ms + EPS) * g_ref[...]


def final_norm(x, g, tm):
    m, d = x.shape
    return pl.pallas_call(
        _final_norm_kernel, grid=(m // tm,),
        in_specs=[pl.BlockSpec((tm, d), lambda i: (i, 0)), pl.BlockSpec((1, d), lambda i: (0, 0))],
        out_specs=pl.BlockSpec((tm, d), lambda i: (i, 0)),
        out_shape=jax.ShapeDtypeStruct((m, d), F32),
        compiler_params=_cparams(("parallel",)), name="final_norm",
    )(x, g)


MOE_TM = 512
MOE_KC = 1024
MOE_FC = 256
MOE_NK = D_MODEL // MOE_KC
MOE_NF = D_FF // MOE_FC


def _moe_kernel(te_ref, nu_ref, x_ref, w1_ref, w3_ref, w2_ref, g_ref, o_ref, hg_s, hu_s, act_s):
    t = pl.program_id(0)
    s = pl.program_id(1)
    used = t < nu_ref[0]

    @pl.when(used & (s < MOE_NK))
    def _():
        x = x_ref[...]
        pg = jnp.dot(x, w1_ref[...].astype(BF16), preferred_element_type=F32)
        pu = jnp.dot(x, w3_ref[...].astype(BF16), preferred_element_type=F32)

        @pl.when(s == 0)
        def _():
            hg_s[...] = pg
            hu_s[...] = pu

        @pl.when(s > 0)
        def _():
            hg_s[...] += pg
            hu_s[...] += pu

        @pl.when(s == MOE_NK - 1)
        def _():
            hg = hg_s[...]
            act = ((hg * jax.nn.sigmoid(hg)) * hu_s[...] * g_ref[...]).astype(BF16)
            for f in range(MOE_NF):
                act_s[f] = act[:, f * MOE_FC:(f + 1) * MOE_FC]

    @pl.when(used & (s >= MOE_NK))
    def _():
        y = jnp.dot(act_s[s - MOE_NK], w2_ref[...].astype(BF16), preferred_element_type=F32)

        @pl.when(s == MOE_NK)
        def _():
            o_ref[...] = y

        @pl.when(s > MOE_NK)
        def _():
            o_ref[...] += y

    @pl.when(jnp.logical_not(used) & (s == 0))
    def _():
        o_ref[...] = jnp.zeros_like(o_ref)


def moe_grouped(layer, tile_expert, n_used, xs, w1, w3, w2, gs):
    r, d = xs.shape

    def kidx(t, s, nu):
        return jnp.where(t < nu[0], jnp.minimum(s, MOE_NK - 1), MOE_NK - 1)

    def fidx(t, s, nu):
        return jnp.where(t < nu[0], jnp.maximum(s - MOE_NK, 0), MOE_NF - 1)

    grid_spec = pltpu.PrefetchScalarGridSpec(
        num_scalar_prefetch=2,
        grid=(r // MOE_TM, MOE_NK + MOE_NF),
        in_specs=[pl.BlockSpec((MOE_TM, MOE_KC), lambda t, s, te, nu: (t, kidx(t, s, nu))),
                  pl.BlockSpec((None, None, MOE_KC, D_FF), lambda t, s, te, nu: (layer, te[t], kidx(t, s, nu), 0)),
                  pl.BlockSpec((None, None, MOE_KC, D_FF), lambda t, s, te, nu: (layer, te[t], kidx(t, s, nu), 0)),
                  pl.BlockSpec((None, None, MOE_FC, d), lambda t, s, te, nu: (layer, te[t], fidx(t, s, nu), 0)),
                  pl.BlockSpec((MOE_TM, 1), lambda t, s, te, nu: (t, 0))],
        out_specs=pl.BlockSpec((MOE_TM, d), lambda t, s, te, nu: (t, 0)),
        scratch_shapes=[pltpu.VMEM((MOE_TM, D_FF), F32), pltpu.VMEM((MOE_TM, D_FF), F32),
                        pltpu.VMEM((MOE_NF, MOE_TM, MOE_FC), BF16)],
    )
    return pl.pallas_call(
        _moe_kernel, grid_spec=grid_spec,
        out_shape=jax.ShapeDtypeStruct((r, d), F32),
        compiler_params=_cparams(("arbitrary", "arbitrary")), name="moe",
    )(tile_expert, n_used, xs, w1, w3, w2, gs)


CUMSUM_BLK = 256


def _onehot_cumsum(onehot):
    m, e = onehot.shape
    x = onehot.reshape(m // CUMSUM_BLK, CUMSUM_BLK, e)
    tri = jnp.tril(jnp.ones((CUMSUM_BLK, CUMSUM_BLK), F32))
    within = jnp.einsum('ij,bje->bie', tri, x, precision=lax.Precision.HIGHEST)
    total = within[:, -1, :]
    return (within + (jnp.cumsum(total, axis=0) - total)[:, None, :]).reshape(m, e)


def moe_dispatch(idx, wsel):
    n = idx.shape[1]
    e_flat = idx.reshape(-1)
    tok = jnp.tile(jnp.arange(n, dtype=jnp.int32), TOP_K)
    onehot = (e_flat[:, None] == jnp.arange(N_EXPERTS, dtype=jnp.int32)[None, :]).astype(F32)
    csum = _onehot_cumsum(onehot)
    counts = csum[-1].astype(jnp.int32)
    rank = jnp.sum(onehot * csum, axis=1).astype(jnp.int32) - 1
    padded = ((counts + MOE_TM - 1) // MOE_TM) * MOE_TM
    seg_end = jnp.cumsum(padded)
    seg_start = seg_end - padded
    dest = jnp.sum(onehot * seg_start.astype(F32)[None, :], axis=1).astype(jnp.int32) + rank
    rows = TOP_K * n + N_EXPERTS * MOE_TM
    src_tok = jnp.zeros((rows,), jnp.int32).at[dest].set(tok)
    gate = jnp.zeros((rows,), F32).at[dest].set(wsel.reshape(-1))
    n_tiles = rows // MOE_TM
    n_used = (seg_end[-1] // MOE_TM).astype(jnp.int32)
    tile_start = jnp.arange(n_tiles, dtype=jnp.int32) * MOE_TM
    te = jnp.minimum(jnp.searchsorted(seg_end, tile_start, side='right'), N_EXPERTS - 1).astype(jnp.int32)
    te = jnp.where(jnp.arange(n_tiles) < n_used, te, te[jnp.maximum(n_used - 1, 0)])
    return dest.reshape(TOP_K, n), src_tok, gate[:, None], te, n_used.reshape(1)


def moe_ffn(layer, hb, idx, wsel, w1, w3, w2):
    dest, src_tok, gate, te, n_used = moe_dispatch(idx, wsel)
    words = lax.bitcast_convert_type(hb.reshape(hb.shape[0], -1, 2), jnp.uint32)
    xs = lax.bitcast_convert_type(jnp.take(words, src_tok, axis=0), BF16).reshape(-1, hb.shape[1])
    ys = moe_grouped(layer, te, n_used, xs, w1, w3, w2, gate)
    return jnp.take(ys, dest[0], axis=0) + jnp.take(ys, dest[1], axis=0)


SEQ_ALL = CTX_LEN + SEQ
S5_T = 16
S5_NC = SEQ_ALL // S5_T
S5_NCC = CTX_LEN // S5_T
S5_W = S5_T * S5_CH


def _gelu_tanh(x):
    return x * (0.5 * (1.0 + jnp.tanh(math.sqrt(2.0 / math.pi) * (x + 0.044715 * (x * x * x)))))


def s5_chunk_operators(lam_re, lam_im, log_step, b_re, b_im, c_re, c_im):
    hp = lax.Precision.HIGHEST
    jj = jnp.arange(S5_T)[:, None]
    ii = jnp.arange(S5_T)[None, :]
    ms, es, fs, a16 = [], [], [], []
    for d in range(2):
        dt = jnp.exp(log_step[d])[:, None]
        lr, li = lam_re[d], lam_im[d]
        er = jnp.exp(lr * dt)
        abr, abi = er * jnp.cos(li * dt), er * jnp.sin(li * dt)
        nr, ni = abr - 1.0, abi
        den = lr * lr + li * li
        fr = (nr * lr + ni * li) / den
        fi = (ni * lr - nr * li) / den
        bbr = fr[..., None] * b_re[d] - fi[..., None] * b_im[d]
        bbi = fr[..., None] * b_im[d] + fi[..., None] * b_re[d]
        k = jnp.arange(S5_T + 1, dtype=F32)[:, None, None]
        mag, ang = jnp.exp(k * (lr * dt)), k * (li * dt)
        pr, pi = mag * jnp.cos(ang), mag * jnp.sin(ang)
        pbr = pr[..., None] * bbr - pi[..., None] * bbi
        pbi = pr[..., None] * bbi + pi[..., None] * bbr
        cr, ci = c_re[d], c_im[d]
        kk = (jnp.einsum('gop,lgpc->lgoc', cr, pbr[:S5_T], precision=hp)
              - jnp.einsum('gop,lgpc->lgoc', ci, pbi[:S5_T], precision=hp))
        cpr = cr[None] * pr[:, :, None, :] - ci[None] * pi[:, :, None, :]
        cpi = -(cr[None] * pi[:, :, None, :] + ci[None] * pr[:, :, None, :])
        if d == 0:
            lag, mask = ii - jj, ii >= jj
            e_pow = S5_T - 1 - jnp.arange(S5_T)
            f_pow = jnp.arange(S5_T) + 1
        else:
            lag, mask = jj - ii, jj >= ii
            e_pow = jnp.arange(S5_T)
            f_pow = S5_T - jnp.arange(S5_T)
        mm = jnp.where(mask[:, :, None, None, None], kk[jnp.clip(lag, 0, S5_T - 1)], 0.0)
        ms.append(mm.transpose(2, 0, 4, 1, 3).reshape(S5_G, S5_W, S5_W))
        for pb in (pbr, pbi):
            es.append(pb[e_pow].transpose(1, 0, 3, 2).reshape(S5_G, S5_W, S5_P))
        for cp in (cpr, cpi):
            fs.append(cp[f_pow].transpose(1, 3, 0, 2).reshape(S5_G, S5_P, S5_W))
        a16 += [pr[S5_T][:, None, :], pi[S5_T][:, None, :]]
    return (ms[0] + ms[1]).astype(BF16), jnp.stack(es).astype(BF16), jnp.stack(fs).astype(BF16), jnp.stack(a16)


def _s5_kernel(u_ref, m_ref, e_ref, f_ref, a_ref, o_ref, e_scr, s_scr):
    u = u_ref[...]
    for k in range(4):
        e_scr[k] = jnp.dot(u, e_ref[k], preferred_element_type=F32)
    arf, aif, arb, aib = (a_ref[k] for k in range(4))

    def segment(start, n, carry):
        def body(k, carry):
            out = []
            for b in range(BATCH):
                sfr, sfi, sbr, sbi = carry[4 * b:4 * b + 4]
                rf = b * S5_NC + start + k
                rb = b * S5_NC + start + n - 1 - k
                s_scr[0, pl.ds(rf, 1), :] = sfr
                s_scr[1, pl.ds(rf, 1), :] = sfi
                s_scr[2, pl.ds(rb, 1), :] = sbr
                s_scr[3, pl.ds(rb, 1), :] = sbi
                nfr = arf * sfr - aif * sfi + e_scr[0, pl.ds(rf, 1), :]
                nfi = arf * sfi + aif * sfr + e_scr[1, pl.ds(rf, 1), :]
                nbr = arb * sbr - aib * sbi + e_scr[2, pl.ds(rb, 1), :]
                nbi = arb * sbi + aib * sbr + e_scr[3, pl.ds(rb, 1), :]
                out += [nfr, nfi, nbr, nbi]
            return tuple(out)
        return lax.fori_loop(0, n, body, carry, unroll=4)

    z = jnp.zeros((1, S5_P), F32)
    carry = segment(0, S5_NCC, (z,) * (4 * BATCH))
    segment(S5_NCC, S5_NC - S5_NCC, carry)
    y = jnp.dot(u, m_ref[...], preferred_element_type=F32)
    for k in range(4):
        y += jnp.dot(s_scr[k].astype(BF16), f_ref[k], preferred_element_type=F32)
    o_ref[...] = y


def s5_scan(u, m, e, f, a16):
    g, r, w = u.shape
    return pl.pallas_call(
        _s5_kernel, grid=(g,),
        in_specs=[pl.BlockSpec((None, r, w), lambda i: (i, 0, 0)),
                  pl.BlockSpec((None, w, w), lambda i: (i, 0, 0)),
                  pl.BlockSpec((4, None, w, S5_P), lambda i: (0, i, 0, 0)),
                  pl.BlockSpec((4, None, S5_P, w), lambda i: (0, i, 0, 0)),
                  pl.BlockSpec((4, None, 1, S5_P), lambda i: (0, i, 0, 0))],
        out_specs=pl.BlockSpec((None, r, w), lambda i: (i, 0, 0)),
        out_shape=jax.ShapeDtypeStruct((g, r, w), F32),
        scratch_shapes=[pltpu.VMEM((4, r, S5_P), F32), pltpu.VMEM((4, r, S5_P), F32)],
        compiler_params=_cparams(("parallel",)), name="s5_scan",
    )(u, m, e, f, a16)


def _s5_out_kernel(y_ref, u_ref, d_ref, w_ref, b_ref, o_ref):
    g = _gelu_tanh(y_ref[...] + d_ref[...] * u_ref[...])
    z = jnp.dot(g.astype(BF16), w_ref[...], preferred_element_type=F32) + b_ref[...]
    o_ref[...] = (g * jax.nn.sigmoid(z)).astype(o_ref.dtype)


def s5_out(y, u, d_skip, w_glu, b_glu, tm):
    m, c = y.shape
    row = pl.BlockSpec((tm, c), lambda i: (i, 0))
    vec = pl.BlockSpec((1, c), lambda i: (0, 0))
    return pl.pallas_call(
        _s5_out_kernel, grid=(m // tm,),
        in_specs=[row, row, vec, pl.BlockSpec((c, c), lambda i: (0, 0)), vec],
        out_specs=row, out_shape=jax.ShapeDtypeStruct((m, c), BF16),
        compiler_params=_cparams(("parallel",)), name="s5_out",
    )(y, u, d_skip, w_glu, b_glu)


def s5_mixer(u_c, u_l, lam_re, lam_im, log_step, b_re, b_im, c_re, c_im, d_skip, w_glu, b_glu):
    bs = u_c.shape[0]
    u_all = jnp.concatenate([u_c, u_l], axis=1)
    uc = u_all.astype(BF16).reshape(bs, S5_NC, S5_T, S5_G, S5_CH).transpose(3, 0, 1, 2, 4)
    uc = uc.reshape(S5_G, bs * S5_NC, S5_W)
    y = s5_scan(uc, *s5_chunk_operators(lam_re, lam_im, log_step, b_re, b_im, c_re, c_im))
    y = y.reshape(S5_G, bs, S5_NC, S5_T, S5_CH).transpose(1, 2, 3, 0, 4).reshape(bs * SEQ_ALL, MIX_W)
    out = s5_out(y, u_all.reshape(bs * SEQ_ALL, MIX_W), d_skip[None], w_glu.astype(BF16), b_glu[None], 544)
    out = out.reshape(bs, SEQ_ALL, MIX_W)
    return out[:, :CTX_LEN], out[:, CTX_LEN:]


LRU_CB = 256
LRU_TB = 256
LRU_HALO = 8
LRU_ROWS = SEQ_ALL + 3 * LRU_HALO


def _lru_kernel(xc_ref, xl_ref, yc_ref, yl_ref, cw_ref, cb_ref, wa_ref, ba_ref, wi_ref, bi_ref, sp_ref,
                oc_ref, ol_ref, xpad, a_f, b_f, a_b, b_b):
    halo = jnp.zeros((LRU_HALO, LRU_CB), F32)
    lat0 = CTX_LEN + 2 * LRU_HALO
    xpad[0:LRU_HALO] = halo
    xpad[LRU_HALO:LRU_HALO + CTX_LEN] = xc_ref[...]
    xpad[LRU_HALO + CTX_LEN:lat0] = halo
    xpad[lat0:lat0 + SEQ] = xl_ref[...]
    xpad[lat0 + SEQ:LRU_ROWS] = halo
    cw = [cw_ref[j:j + 1, :] for j in range(CONV_W)]
    cb = cb_ref[...]

    def gates(p0, r0):
        ext = xpad[pl.ds(p0 - LRU_HALO, LRU_TB + 2 * LRU_HALO), :]
        xb = cb
        for j in range(CONV_W):
            s0 = LRU_HALO - CONV_PAD_L + j
            xb = xb + cw[j] * ext[s0:s0 + LRU_TB]
        for k in range(LRU_CB // LRU_BD):
            lanes = slice(k * LRU_BD, (k + 1) * LRU_BD)
            xk = xb[:, lanes]
            xkb = xk.astype(BF16)
            for d, (a_s, b_s) in enumerate(((a_f, b_f), (a_b, b_b))):
                r = jax.nn.sigmoid(jnp.dot(xkb, wa_ref[d, k], preferred_element_type=F32) + ba_ref[d][:, lanes])
                i = jax.nn.sigmoid(jnp.dot(xkb, wi_ref[d, k], preferred_element_type=F32) + bi_ref[d][:, lanes])
                log_a = -LRU_C * r * sp_ref[d][:, lanes]
                t = jnp.tanh(log_a)
                a_s[pl.ds(r0, LRU_TB), lanes] = jnp.exp(log_a)
                b_s[pl.ds(r0, LRU_TB), lanes] = jnp.sqrt(-2.0 * t / (1.0 - t)) * (i * xk)

    gates(LRU_HALO, 0)

    def lat_gates(k, _):
        gates(pl.multiple_of(lat0 + k * LRU_TB, 8), pl.multiple_of(CTX_LEN + k * LRU_TB, 8))
        return 0
    lax.fori_loop(0, SEQ // LRU_TB, lat_gates, 0)

    def segment(start, n, carry):
        def body(t, carry):
            hf, hb = carry
            rf = start + t
            rb = start + n - 1 - t
            hf = a_f[pl.ds(rf, 1), :] * hf + b_f[pl.ds(rf, 1), :]
            b_f[pl.ds(rf, 1), :] = hf
            hb = a_b[pl.ds(rb, 1), :] * hb + b_b[pl.ds(rb, 1), :]
            b_b[pl.ds(rb, 1), :] = hb
            return hf, hb
        return lax.fori_loop(0, n, body, carry, unroll=8)

    z = jnp.zeros((1, LRU_CB), F32)
    carry = segment(0, CTX_LEN, (z, z))
    segment(CTX_LEN, SEQ, carry)

    oc_ref[...] = (_gelu_tanh(yc_ref[...]) * (b_f[0:CTX_LEN] + b_b[0:CTX_LEN])).astype(oc_ref.dtype)

    def lat_out(k, _):
        r0 = pl.multiple_of(k * LRU_TB, 8)
        rows = pl.ds(pl.multiple_of(CTX_LEN + k * LRU_TB, 8), LRU_TB)
        h = b_f[rows, :] + b_b[rows, :]
        ol_ref[pl.ds(r0, LRU_TB), :] = (_gelu_tanh(yl_ref[pl.ds(r0, LRU_TB), :]) * h).astype(ol_ref.dtype)
        return 0
    lax.fori_loop(0, SEQ // LRU_TB, lat_out, 0)


def lru_mixer(y_c, x_c, y_l, x_l, conv_w, conv_b, w_a, b_a, w_i, b_i, lam):
    bs = x_c.shape[0]
    nb = LRU_CB // LRU_BD
    sp = jax.nn.softplus(-lam)[:, None, :]
    ctx_spec = pl.BlockSpec((None, CTX_LEN, LRU_CB), lambda b, c: (b, 0, c))
    lat_spec = pl.BlockSpec((None, SEQ, LRU_CB), lambda b, c: (b, 0, c))
    vec2 = pl.BlockSpec((2, 1, LRU_CB), lambda b, c: (0, 0, c))
    wspec = pl.BlockSpec((2, nb, LRU_BD, LRU_BD), lambda b, c: (0, c, 0, 0))
    scan_buf = pltpu.VMEM((SEQ_ALL, LRU_CB), F32)
    return pl.pallas_call(
        _lru_kernel, grid=(bs, MIX_W // LRU_CB),
        in_specs=[ctx_spec, lat_spec, ctx_spec, lat_spec,
                  pl.BlockSpec((CONV_W, LRU_CB), lambda b, c: (0, c)),
                  pl.BlockSpec((1, LRU_CB), lambda b, c: (0, c)),
                  wspec, vec2, wspec, vec2, vec2],
        out_specs=[ctx_spec, lat_spec],
        out_shape=[jax.ShapeDtypeStruct((bs, CTX_LEN, MIX_W), BF16), jax.ShapeDtypeStruct((bs, SEQ, MIX_W), BF16)],
        scratch_shapes=[pltpu.VMEM((LRU_ROWS, LRU_CB), F32), scan_buf, scan_buf, scan_buf, scan_buf],
        compiler_params=_cparams(("parallel", "parallel")), name="lru",
    )(x_c, x_l, y_c, y_l, conv_w, conv_b[None], w_a.astype(BF16), b_a[:, None, :], w_i.astype(BF16),
      b_i[:, None, :], sp)


def to_col_major(t):
    b, l = t.shape[:2]
    rows = l // GRID_W
    return jnp.swapaxes(t.reshape(b, rows, GRID_W, -1), 1, 2).reshape(t.shape)


def to_row_major(t):
    b, l = t.shape[:2]
    rows = l // GRID_W
    return jnp.swapaxes(t.reshape(b, GRID_W, rows, -1), 1, 2).reshape(t.shape)


N_CHUNKS = SEQ_ALL // CHUNK
N_CCH = CTX_LEN // CHUNK
GATE_COLBLK = 80
MIX_TB = 256


def _dot(a, b):
    return jnp.dot(a, b, preferred_element_type=F32)


def _dot_nt(a, b):
    return lax.dot_general(a, b, (((1,), (1,)), ((), ())), preferred_element_type=F32)


def _dot_tn(a, b):
    return lax.dot_general(a, b, (((0,), (0,)), ((), ())), preferred_element_type=F32)


def _split2(a):
    hi = a.astype(BF16)
    return hi, (a - hi.astype(F32)).astype(BF16)


def _dot3(a, b):
    ah, al = _split2(a)
    bh, bl = _split2(b)
    return _dot(ah, bh) + (_dot(ah, bl) + _dot(al, bh))


def _dot_mask(m, x):
    hi = x.astype(BF16)
    r1 = x - hi.astype(F32)
    mid = r1.astype(BF16)
    lo = (r1 - mid.astype(F32)).astype(BF16)
    return _dot(m, hi) + (_dot(m, mid) + _dot(m, lo))


def _chunk_masks(backward):
    i = lax.broadcasted_iota(jnp.int32, (CHUNK, CHUNK), 0)
    j = lax.broadcasted_iota(jnp.int32, (CHUNK, CHUNK), 1)
    if backward:
        i, j = j, i
    tri, strict = j <= i, j < i
    return tri, strict, tri.astype(BF16), jnp.where(strict, 1.0, 0.0).astype(F32)


def _lane_pick(blk, lane):
    lanes = lax.broadcasted_iota(jnp.int32, blk.shape, 1)
    return jnp.sum(jnp.where(lanes == lane, blk, 0.0), axis=-1, keepdims=True)


def _softplus(x):
    return jnp.maximum(x, 0.0) + jnp.log1p(jnp.exp(-jnp.abs(x)))


def _silu(x):
    return x * jax.nn.sigmoid(x)


def _bidir_chunks(step, carry, n_ctx, n_all):
    def segment(start, n, carry):
        def body(t, carry):
            return step(start + t, start + n - 1 - t, carry)
        return lax.fori_loop(0, n, body, carry)
    carry = segment(0, n_ctx, carry)
    return segment(n_ctx, n_all - n_ctx, carry)


DN_HALO = 8
DN_ROWS = SEQ_ALL + 3 * DN_HALO
DN_C = 64
DN_GROUP = 256
DN_GC = DN_GROUP // (2 * DN_C)
DN_NCH = SEQ_ALL // DN_C
DN_NCC = CTX_LEN // DN_C
DN_LOCKSTEP = 4


def _dn_kernel(qc_ref, ql_ref, kc_ref, kl_ref, vc_ref, vl_ref, zc_ref, zl_ref, gc_ref, gl_ref, hp_ref,
               cwq_ref, cwk_ref, cwv_ref, nw_ref, ol_ref, oc_ref,
               xpad, qn, kn, vn, u_s, w_s, qg_s, kd_s, at_s, dec_s):
    h = pl.program_id(1)
    halo = jnp.zeros((DN_HALO, DN_HD), F32)
    lat0 = CTX_LEN + 2 * DN_HALO

    def conv_into(xc_ref, xl_ref, cw_ref, dst, post):
        xpad[0:DN_HALO] = halo
        xpad[DN_HALO:DN_HALO + CTX_LEN] = xc_ref[...]
        xpad[DN_HALO + CTX_LEN:lat0] = halo
        xpad[lat0:lat0 + SEQ] = xl_ref[...]
        xpad[lat0 + SEQ:DN_ROWS] = halo
        cw = [cw_ref[j:j + 1, :] for j in range(CONV_W)]

        def block(p0, r0):
            ext = xpad[pl.ds(p0 - DN_HALO, MIX_TB + 2 * DN_HALO), :]
            s0 = DN_HALO - CONV_PAD_L
            acc = cw[0] * ext[s0:s0 + MIX_TB]
            for j in range(1, CONV_W):
                acc = acc + cw[j] * ext[s0 + j:s0 + j + MIX_TB]
            dst[pl.ds(r0, MIX_TB), :] = post(_silu(acc))

        block(DN_HALO, 0)

        def lat_block(k, _):
            block(pl.multiple_of(lat0 + k * MIX_TB, 8), pl.multiple_of(CTX_LEN + k * MIX_TB, 8))
            return 0
        lax.fori_loop(0, SEQ // MIX_TB, lat_block, 0)

    def l2n(scale):
        return lambda t: t * (lax.rsqrt(jnp.sum(t * t, axis=-1, keepdims=True) + EPS) * scale)

    conv_into(qc_ref, ql_ref, cwq_ref, qn, l2n(DN_HD ** -0.5))
    conv_into(kc_ref, kl_ref, cwk_ref, kn, l2n(1.0))
    conv_into(vc_ref, vl_ref, cwv_ref, vn, lambda t: t)

    gi = lax.broadcasted_iota(jnp.int32, (DN_GROUP, DN_GROUP), 0)
    gj = lax.broadcasted_iota(jnp.int32, (DN_GROUP, DN_GROUP), 1)
    same = (gi // DN_C) == (gj // DN_C)
    blk16 = (gi // 16) == (gj // 16)
    bwd = ((gi // DN_C) % 2) == 1
    fwd = jnp.logical_not(bwd)
    li, lj = gi % DN_C, gj % DN_C
    tri = same & ((fwd & (lj <= li)) | (bwd & (lj >= li)))
    strict = same & ((fwd & (lj < li)) | (bwd & (lj > li)))
    eye = jnp.where(gi == gj, 1.0, 0.0).astype(F32)
    same16 = jnp.where(same, 1.0, 0.0).astype(BF16)
    m16 = jnp.where(tri, 1.0, 0.0).astype(BF16)
    nn = jnp.where(strict, 1.0, 0.0).astype(F32)

    def both_dirs(xf, xb):
        parts = []
        for c in range(DN_GC):
            parts += [xf[c * DN_C:(c + 1) * DN_C], xb[c * DN_C:(c + 1) * DN_C]]
        return jnp.concatenate(parts, axis=0)

    def prep(ts, g_ref, row0):
        grows = DN_GC * DN_C
        n = range(len(ts))
        r0 = [pl.multiple_of(t * grows, grows) for t in ts]
        gblk = [g_ref[pl.ds(pl.multiple_of(t * grows - row0, grows), grows), :] for t in ts]
        q = [both_dirs(*[qn[pl.ds(r0[i], grows), :]] * 2) for i in n]
        k = [both_dirs(*[kn[pl.ds(r0[i], grows), :]] * 2) for i in n]
        v = [both_dirs(*[vn[pl.ds(r0[i], grows), :]] * 2) for i in n]
        g_col = [both_dirs(*[hp_ref[:, d:d + 1] * _softplus(_lane_pick(gblk[i], d * DN_H + h)
                                                            + hp_ref[:, 2 + d:3 + d]) for d in range(2)]) for i in n]
        beta = [both_dirs(*[jax.nn.sigmoid(_lane_pick(gblk[i], 2 * DN_H + d * DN_H + h)) for d in range(2)])
                for i in n]
        k16 = [k[i].astype(BF16) for i in n]
        kb = [k[i] * beta[i] for i in n]
        gb = [jnp.broadcast_to(g_col[i], (DN_GROUP, DN_HD)) for i in n]
        dlt = [_dot_mask(m16, g_col[i] * nn) for i in n]
        kk = [_dot_nt(kb[i].astype(BF16), k16[i]) for i in n]
        gcum = [_dot_mask(m16, gb[i]) for i in n]
        tot = [_dot_mask(same16, gb[i]) for i in n]
        qk = [_dot_nt(q[i].astype(BF16), k16[i]) for i in n]
        decay = [jnp.where(tri, jnp.exp(jnp.where(tri, dlt[i], 0.0)), 0.0) for i in n]
        lower = [jnp.where(strict, kk[i] * decay[i], 0.0) for i in n]
        x = [-jnp.where(blk16, lower[i], 0.0) for i in n]
        tinv = [eye + x[i] for i in n]
        for _ in range(3):
            x = [_dot3(x[i], x[i]) for i in n]
            tinv = [tinv[i] + _dot3(tinv[i], x[i]) for i in n]
        size = 16
        while size < DN_C:
            inner, outer = (gi // size) == (gj // size), (gi // (2 * size)) == (gj // (2 * size))
            off = [_dot3(jnp.where(outer & jnp.logical_not(inner), lower[i], 0.0), tinv[i]) for i in n]
            tinv = [tinv[i] - _dot3(tinv[i], off[i]) for i in n]
            size *= 2
        eg = [jnp.exp(gcum[i]) for i in n]
        sol = [_dot3(tinv[i], jnp.concatenate([v[i] * beta[i], kb[i] * eg[i]], axis=1)) for i in n]
        for i in n:
            attn = (qk[i] * decay[i]).astype(BF16)
            w16 = sol[i][:, DN_HD:].astype(BF16)
            qg16 = (q[i] * eg[i]).astype(BF16)
            kd16 = (k[i] * jnp.exp(tot[i] - gcum[i])).astype(BF16)
            dec = jnp.exp(tot[i])
            for p in range(DN_GROUP // DN_C):
                d, blk = p % 2, slice(p * DN_C, (p + 1) * DN_C)
                dst = pl.ds(r0[i] + (p // 2) * DN_C, DN_C)
                u_s[d, dst, :] = sol[i][blk, :DN_HD]
                w_s[d, dst, :] = w16[blk]
                at_s[d, dst, :] = attn[blk, blk]
                qg_s[d, dst, :] = qg16[blk]
                kd_s[d, dst, :] = kd16[blk]
                dec_s[d, pl.ds(pl.multiple_of((DN_GC * ts[i] + p // 2) * 8, 8), 8), :] = dec[p * DN_C:p * DN_C + 8]
        return 0

    n_ctx_groups, n_groups = DN_NCC // DN_GC, DN_NCH // DN_GC
    assert (n_groups - n_ctx_groups) % DN_LOCKSTEP == 0
    prep(list(range(n_ctx_groups)), gc_ref, 0)
    lax.fori_loop(0, (n_groups - n_ctx_groups) // DN_LOCKSTEP,
                  lambda t, _: prep([n_ctx_groups + DN_LOCKSTEP * t + i for i in range(DN_LOCKSTEP)],
                                    gl_ref, CTX_LEN), 0)
    o_dst = (qn, kn)
    o_f, o_b = o_dst

    def step(cf, cb, carry):
        dirs = range(2)
        rows = [pl.ds(pl.multiple_of(c * DN_C, DN_C), DN_C) for c in (cf, cb)]
        s16 = [carry[d].astype(BF16) for d in dirs]
        ws = [_dot(w_s[d, rows[d], :], s16[d]) for d in dirs]
        qs = [_dot(qg_s[d, rows[d], :], s16[d]) for d in dirs]
        v_new = [(u_s[d, rows[d], :] - ws[d]).astype(BF16) for d in dirs]
        kv = [_dot_tn(kd_s[d, rows[d], :], v_new[d]) for d in dirs]
        av = [_dot(at_s[d, rows[d], :], v_new[d]) for d in dirs]
        out = []
        for d, c in zip(dirs, (cf, cb)):
            o_dst[d][rows[d], :] = qs[d] + av[d]
            g_last = dec_s[d, pl.ds(pl.multiple_of(c * 8, 8), 8), :][0:1]
            out.append(carry[d] * g_last + kv[d])
        return tuple(out)

    z0 = jnp.zeros((DN_HD, DN_HD), F32)
    _bidir_chunks(step, (z0, z0), DN_NCC, DN_NCH)

    nw = nw_ref[...]

    def finish(rows, z):
        o = o_f[rows, :] + o_b[rows, :]
        o = o * lax.rsqrt(jnp.mean(o * o, axis=-1, keepdims=True) + EPS) * nw
        return (o * _silu(z)).astype(BF16)

    oc_ref[...] = finish(pl.ds(0, CTX_LEN), zc_ref[...])

    def lat_finish(k, _):
        r0 = pl.multiple_of(k * MIX_TB, 8)
        ol_ref[pl.ds(r0, MIX_TB), :] = finish(pl.ds(pl.multiple_of(CTX_LEN + k * MIX_TB, 8), MIX_TB),
                                              zl_ref[pl.ds(r0, MIX_TB), :])
        return 0
    lax.fori_loop(0, SEQ // MIX_TB, lat_finish, 0)


def deltanet_mixer(proj, conv_w, a_log, dt_bias, norm_w):
    hp = jnp.stack([-jnp.exp(a_log[0]), -jnp.exp(a_log[1]), dt_bias[0], dt_bias[1]], axis=-1)[:, None, :]
    ctx_blk0 = N_LAT // CTX_LEN

    def seg_specs(col0):
        cb = col0 // DN_HD
        return [pl.BlockSpec((CTX_LEN, DN_HD), lambda b, h: (ctx_blk0 + b, cb + h)),
                pl.BlockSpec((SEQ, DN_HD), lambda b, h: (b, cb + h))]

    def conv_spec(k):
        return pl.BlockSpec((CONV_W, DN_HD), lambda b, h: (0, k * DN_H + h))

    in_specs = (seg_specs(COL_DN_QKVZ) + seg_specs(COL_DN_QKVZ + MIX_W) + seg_specs(COL_DN_QKVZ + 2 * MIX_W)
                + seg_specs(COL_DN_QKVZ + 3 * MIX_W)
                + [pl.BlockSpec((CTX_LEN, 128), lambda b, h: (ctx_blk0 + b, GATE_COLBLK)),
                   pl.BlockSpec((SEQ, 128), lambda b, h: (b, GATE_COLBLK)),
                   pl.BlockSpec((None, 1, 4), lambda b, h: (h, 0, 0)),
                   conv_spec(0), conv_spec(1), conv_spec(2),
                   pl.BlockSpec((1, DN_HD), lambda b, h: (0, 0))])
    seq_f32 = pltpu.VMEM((SEQ_ALL, DN_HD), F32)
    dir_bf16 = pltpu.VMEM((2, SEQ_ALL, DN_HD), BF16)
    ol, oc = pl.pallas_call(
        _dn_kernel, grid=(BATCH, DN_H), in_specs=in_specs,
        out_specs=[pl.BlockSpec((SEQ, DN_HD), lambda b, h: (b, h)),
                   pl.BlockSpec((CTX_LEN, DN_HD), lambda b, h: (b, h))],
        out_shape=[jax.ShapeDtypeStruct((N_LAT, MIX_W), BF16), jax.ShapeDtypeStruct((N_CTX, MIX_W), BF16)],
        scratch_shapes=[pltpu.VMEM((DN_ROWS, DN_HD), F32), seq_f32, seq_f32, seq_f32,
                        pltpu.VMEM((2, SEQ_ALL, DN_HD), F32), dir_bf16, dir_bf16, dir_bf16,
                        pltpu.VMEM((2, SEQ_ALL, DN_C), BF16), pltpu.VMEM((2, DN_NCH * 8, DN_HD), F32)],
        compiler_params=_cparams(("parallel", "arbitrary")), name="deltanet",
    )(*([proj] * 10), hp, conv_w, conv_w, conv_w, norm_w[None])
    return ol, oc


def _ml_kernel(qc_ref, ql_ref, kc_ref, kl_ref, vc_ref, vl_ref, oc_ref, ol_ref, gc_ref, gl_ref, hp_ref, nw_ref,
               yl_ref, yc_ref, h_f, h_b):
    hh = pl.program_id(1)
    masks = [_chunk_masks(False), _chunk_masks(True)]
    ones = jnp.ones((CHUNK, CHUNK), BF16)
    eye = jnp.where(lax.broadcasted_iota(jnp.int32, (CHUNK, CHUNK), 0)
                    == lax.broadcasted_iota(jnp.int32, (CHUNK, CHUNK), 1), 1.0, 0.0).astype(F32)

    h_dst = (h_f, h_b)

    def advance(cs, states, refs, row0):
        q_ref, k_ref, v_ref, g_ref = refs
        dirs = range(2)
        rows = [pl.ds(pl.multiple_of(c * CHUNK - row0, CHUNK), CHUNK) for c in cs]
        tri = [masks[d][0] for d in dirs]
        gblk = [g_ref[rows[d], :] for d in dirs]
        ig = [_lane_pick(gblk[d], 4 * DN_H + d * ML_H + hh) + hp_ref[:, d:d + 1] for d in dirs]
        lf = [-_softplus(-(_lane_pick(gblk[d], 4 * DN_H + 2 * ML_H + d * ML_H + hh) + hp_ref[:, 2 + d:3 + d]))
              for d in dirs]
        q = [q_ref[rows[d], :] * (ML_DQK ** -0.5) for d in dirs]
        k = [k_ref[rows[d], :] for d in dirs]
        q16 = [q[d].astype(BF16) for d in dirs]
        v16 = [v_ref[rows[d], :].astype(BF16) for d in dirs]
        qc = [_dot(q16[d], states[d][0].astype(BF16)) for d in dirs]
        bcum = [_dot_mask(masks[d][2], jnp.broadcast_to(lf[d], (CHUNK, ML_DQK)))[:, 0:1] for d in dirs]
        dsum = [_dot_mask(masks[d][2], lf[d] * masks[d][3]) + _dot_mask(ones, ig[d] * eye) for d in dirs]
        sqk = [_dot_nt(q16[d], k[d].astype(BF16)) for d in dirs]
        dmat = [jnp.where(tri[d], dsum[d], -jnp.inf) for d in dirs]
        dmax = [jnp.max(dmat[d], axis=-1, keepdims=True) for d in dirs]
        tot = [jnp.sum(lf[d], axis=0, keepdims=True) for d in dirs]
        logw = [tot[d] - bcum[d] + ig[d] for d in dirs]
        logw_max = [jnp.max(logw[d], axis=0, keepdims=True) for d in dirs]
        inter = [bcum[d] + states[d][2] for d in dirs]
        m_t = [jnp.maximum(inter[d], dmax[d]) for d in dirs]
        w_inter = [jnp.exp(inter[d] - m_t[d]) for d in dirs]
        wmat = [jnp.exp(dmat[d] - m_t[d]) * sqk[d] for d in dirs]
        m_new = [jnp.maximum(tot[d] + states[d][2], logw_max[d]) for d in dirs]
        keep = [jnp.exp(tot[d] + states[d][2] - m_new[d]) for d in dirs]
        kw = [k[d] * jnp.exp(logw[d] - m_new[d]) for d in dirs]
        wv = [_dot(wmat[d].astype(BF16), v16[d]) for d in dirs]
        kv = [_dot_tn(kw[d].astype(BF16), v16[d]) for d in dirs]
        out = []
        for d in dirs:
            c_s, n_s, _ = states[d]
            num = w_inter[d] * qc[d] + wv[d]
            den = w_inter[d] * jnp.sum(q[d] * n_s, axis=-1, keepdims=True) + jnp.sum(wmat[d], axis=-1, keepdims=True)
            h_dst[d][pl.ds(pl.multiple_of(cs[d] * CHUNK, CHUNK), CHUNK), :] = (
                num / jnp.maximum(jnp.abs(den), jnp.exp(-m_t[d])))
            out.append((keep[d] * c_s + kv[d], keep[d] * n_s + jnp.sum(kw[d], axis=0, keepdims=True), m_new[d]))
        return tuple(out)

    ctx_refs = (qc_ref, kc_ref, vc_ref, gc_ref)
    lat_refs = (ql_ref, kl_ref, vl_ref, gl_ref)

    def segment(refs, start, n, carry):
        def body(t, carry):
            return advance((start + t, start + n - 1 - t), carry, refs, start * CHUNK)
        return lax.fori_loop(0, n, body, carry)

    s0 = (jnp.zeros((ML_DQK, ML_DV), F32), jnp.zeros((1, ML_DQK), F32), jnp.zeros((1, 1), F32))
    carry = segment(ctx_refs, 0, N_CCH, (s0, s0))
    segment(lat_refs, N_CCH, N_CHUNKS - N_CCH, carry)

    nw = nw_ref[...]

    def finish(rows, o):
        hsum = h_f[rows, :] + h_b[rows, :]
        hsum = hsum * lax.rsqrt(jnp.mean(hsum * hsum, axis=-1, keepdims=True) + EPS) * nw
        return (hsum * jax.nn.sigmoid(o)).astype(BF16)

    yc_ref[...] = finish(pl.ds(0, CTX_LEN), oc_ref[...])

    def lat_finish(k, _):
        r0 = pl.multiple_of(k * MIX_TB, 8)
        yl_ref[pl.ds(r0, MIX_TB), :] = finish(pl.ds(pl.multiple_of(CTX_LEN + k * MIX_TB, 8), MIX_TB),
                                              ol_ref[pl.ds(r0, MIX_TB), :])
        return 0
    lax.fori_loop(0, SEQ // MIX_TB, lat_finish, 0)


def mlstm_mixer(proj, i_bias, f_bias, norm_w):
    hp = jnp.stack([i_bias[0], i_bias[1], f_bias[0], f_bias[1]], axis=-1)[:, None, :]
    ctx_blk0 = N_LAT // CTX_LEN

    def seg_specs(col0, w):
        cb = col0 // w
        return [pl.BlockSpec((CTX_LEN, w), lambda b, h: (ctx_blk0 + b, cb + h)),
                pl.BlockSpec((SEQ, w), lambda b, h: (b, cb + h))]

    in_specs = (seg_specs(COL_ML_Q, ML_DQK) + seg_specs(COL_ML_K, ML_DQK) + seg_specs(COL_ML_V, ML_DV)
                + seg_specs(COL_ML_O, ML_DV)
                + [pl.BlockSpec((CTX_LEN, 128), lambda b, h: (ctx_blk0 + b, GATE_COLBLK)),
                   pl.BlockSpec((SEQ, 128), lambda b, h: (b, GATE_COLBLK)),
                   pl.BlockSpec((None, 1, 4), lambda b, h: (h, 0, 0)),
                   pl.BlockSpec((1, ML_DV), lambda b, h: (0, h))])
    seq_f32 = pltpu.VMEM((SEQ_ALL, ML_DV), F32)
    yl, yc = pl.pallas_call(
        _ml_kernel, grid=(BATCH, ML_H), in_specs=in_specs,
        out_specs=[pl.BlockSpec((SEQ, ML_DV), lambda b, h: (b, h)),
                   pl.BlockSpec((CTX_LEN, ML_DV), lambda b, h: (b, h))],
        out_shape=[jax.ShapeDtypeStruct((N_LAT, MIX_W), BF16), jax.ShapeDtypeStruct((N_CTX, MIX_W), BF16)],
        scratch_shapes=[seq_f32, seq_f32],
        compiler_params=_cparams(("parallel", "arbitrary")), name="mlstm",
    )(*([proj] * 10), hp, norm_w[None])
    return yl, yc


PROJ_TM = (N_LAT + N_CTX) // 8
PROJ_TN = 512
PROJ_PAD = -(-PROJ_W // PROJ_TN) * PROJ_TN
COL_S5, COL_DN_QKVZ, COL_ML_Q, COL_ML_K, COL_ML_V, COL_ML_O, COL_LRU_Y, COL_LRU_X = (
    0, 1024, 5120, 5632, 6144, 7168, 8192, 9216)
COL_DN_A, COL_DN_B, COL_ML_I, COL_ML_F = 10240, 10256, 10272, 10280


def permute_proj_weight(w):
    o = [0] + [int(v) for v in np.cumsum(PROJ_SIZES)]
    parts = [w[:, o[0]:o[5]], w[:, o[7]:o[11]], w[:, o[13]:o[15]], w[:, o[5]:o[7]], w[:, o[11]:o[13]]]
    return jnp.pad(jnp.concatenate(parts, axis=1).astype(BF16), ((0, 0), (0, PROJ_PAD - PROJ_W)))


def hybrid_mixer(proj, s5_p, dn_p, ml_p, lru_p):
    def seg(c0, w):
        blk = proj[:, c0:c0 + w]
        return blk[N_LAT:].reshape(BATCH, CTX_LEN, w), blk[:N_LAT].reshape(BATCH, SEQ, w)

    u_c, u_l = seg(COL_S5, MIX_W)
    a_c, a_l = s5_mixer(u_c, to_col_major(u_l), *s5_p)
    y_c, y_l = seg(COL_LRU_Y, MIX_W)
    x_c, x_l = seg(COL_LRU_X, MIX_W)
    d_c, d_l = lru_mixer(y_c, x_c, to_col_major(y_l), to_col_major(x_l), *lru_p)
    flat = lambda t_l, t_c: (to_row_major(t_l).reshape(N_LAT, MIX_W), t_c.reshape(N_CTX, MIX_W))
    return [flat(a_l, a_c), deltanet_mixer(proj, *dn_p), mlstm_mixer(proj, *ml_p), flat(d_l, d_c)]


def kernel(x, c, ctx, c_ctx, w_ada, b_ada, norm1, norm2, norm_f, w_in, w_out, s5_lam_re, s5_lam_im, s5_log_step,
           s5_b_re, s5_b_im, s5_c_re, s5_c_im, s5_d, s5_w_glu, s5_b_glu, dn_conv, dn_a_log, dn_dt_bias, dn_norm,
           ml_i_bias, ml_f_bias, ml_norm, lru_conv_w, lru_conv_b, lru_w_a, lru_b_a, lru_w_i, lru_b_i, lru_lam,
           router_w, router_bias, moe_w1, moe_w3, moe_w2):
    d = D_MODEL
    xa = jnp.concatenate([x.reshape(N_LAT, d), ctx.reshape(N_CTX, d)], axis=0)
    cond = jnp.concatenate([jax.nn.silu(c), jax.nn.silu(c_ctx)[None], jnp.zeros((16 - BATCH - 1, d), F32)], axis=0)
    cond = cond.astype(BF16)
    router = (router_w.T, router_bias.reshape(N_EXPERTS, 1))

    for l in range(DEPTH):
        last = l == DEPTH - 1
        mod = ada_matmul(l, cond, w_ada, b_ada[:, None, :], 512)[:BATCH + 1].reshape(BATCH + 1, 6, 1, d)
        shift1, scale1, gate1, shift2, scale2, gate2 = (mod[:, k] for k in range(6))

        h1 = norm_mod(xa, norm1[l][None], scale1, shift1, SEQ, 256)
        proj = matmul(h1, permute_proj_weight(w_in[l]), PROJ_TM, PROJ_TN)
        slabs = hybrid_mixer(
            proj,
            (s5_lam_re[l], s5_lam_im[l], s5_log_step[l], s5_b_re[l], s5_b_im[l], s5_c_re[l], s5_c_im[l], s5_d[l],
             s5_w_glu[l], s5_b_glu[l]),
            (dn_conv[l], dn_a_log[l], dn_dt_bias[l], dn_norm[l]),
            (ml_i_bias[l], ml_f_bias[l], ml_norm[l]),
            (lru_conv_w[l], lru_conv_b[l], lru_w_a[l], lru_b_a[l], lru_w_i[l], lru_b_i[l], lru_lam[l]))
        if last:
            xa, mix = xa[:N_LAT], [s[0] for s in slabs]
        else:
            mix = [jnp.concatenate(s, axis=0) for s in slabs]
        xa = matmul_gated_residual(mix, w_out[l].astype(BF16), xa, gate1, SEQ, 512, 512)

        h2, idx, wsel = norm_mod(xa, norm2[l][None], scale2, shift2, SEQ, 256, router=router)
        y = moe_ffn(l, h2, idx, wsel, moe_w1, moe_w3, moe_w2)
        upd = (xa[:N_LAT].reshape(BATCH, SEQ, d) + gate2[:BATCH] * y[:N_LAT].reshape(BATCH, SEQ, d)).reshape(N_LAT, d)
        if last:
            xa = upd
        else:
            xa = jnp.concatenate([upd, xa[N_LAT:] + gate2[BATCH] * y[N_LAT:]], axis=0)

    return final_norm(xa[:N_LAT], norm_f[None], 256).reshape(BATCH, SEQ, d)
```

```python
import functools
import math

import numpy as np
import jax
import jax.numpy as jnp
from jax import lax
from jax.experimental import pallas as pl
from jax.experimental.pallas import tpu as pltpu

D_MODEL = 4096
BATCH = 2
SEQ = 4096
DEPTH = 2
GRID_W = 64
CTX_LEN = 256
MIX_W = D_MODEL // 4
MIX_TOTAL = 4 * MIX_W
CHUNK = 64
CONV_W = 4
CONV_PAD_L = 2
CONV_PAD_R = 1
EPS = 1e-6
S5_CH = 16
S5_G = MIX_W // S5_CH
S5_P = 64
DN_HD = 128
DN_H = MIX_W // DN_HD
ML_H = 4
ML_DV = MIX_W // ML_H
ML_DQK = ML_DV // 2
LRU_BLOCKS = 8
LRU_BD = MIX_W // LRU_BLOCKS
LRU_C = 8.0
N_EXPERTS = 16
N_GROUPS = 4
E_PER_G = N_EXPERTS // N_GROUPS
TOP_K = 2
D_FF = D_MODEL // 4
PROJ_SIZES = (MIX_W, MIX_W, MIX_W, MIX_W, MIX_W, 2 * DN_H, 2 * DN_H, ML_H * ML_DQK, ML_H * ML_DQK, MIX_W, MIX_W,
              2 * ML_H, 2 * ML_H, MIX_W, MIX_W)
PROJ_W = sum(PROJ_SIZES)

N_LAT = BATCH * SEQ
N_CTX = BATCH * CTX_LEN
VMEM_LIMIT = 56 * 1024 * 1024

F32 = jnp.float32
BF16 = jnp.bfloat16


def _cparams(sem):
    return pltpu.CompilerParams(dimension_semantics=sem, vmem_limit_bytes=VMEM_LIMIT)


def _mm_kernel(a_ref, b_ref, o_ref):
    o_ref[...] = jnp.dot(a_ref[...], b_ref[...], preferred_element_type=F32).astype(o_ref.dtype)


def matmul(a, b, tm, tn, out_dtype=F32):
    m, k = a.shape
    n = b.shape[1]
    assert m % tm == 0 and n % tn == 0
    return pl.pallas_call(
        _mm_kernel,
        grid=(m // tm, n // tn),
        in_specs=[pl.BlockSpec((tm, k), lambda i, j: (i, 0)),
                  pl.BlockSpec((k, tn), lambda i, j: (0, j))],
        out_specs=pl.BlockSpec((tm, tn), lambda i, j: (i, j)),
        out_shape=jax.ShapeDtypeStruct((m, n), out_dtype),
        compiler_params=_cparams(("parallel", "arbitrary")),
        name="mm",
    )(a, b)


def _mm_resid_kernel(a0_ref, a1_ref, a2_ref, a3_ref, b_ref, r_ref, g_ref, o_ref):
    acc = None
    for s, a_ref in enumerate((a0_ref, a1_ref, a2_ref, a3_ref)):
        part = jnp.dot(a_ref[...], b_ref[s * MIX_W:(s + 1) * MIX_W, :], preferred_element_type=F32)
        acc = part if acc is None else acc + part
    o_ref[...] = r_ref[...] + g_ref[...] * acc


def matmul_gated_residual(a_slabs, b, resid, gate, rows_per_gate, tm, tn):
    m = a_slabs[0].shape[0]
    k, n = b.shape
    assert m % tm == 0 and n % tn == 0 and rows_per_gate % tm == 0 and k == 4 * MIX_W
    return pl.pallas_call(
        _mm_resid_kernel,
        grid=(m // tm, n // tn),
        in_specs=[pl.BlockSpec((tm, MIX_W), lambda i, j: (i, 0))] * 4 + [
                  pl.BlockSpec((k, tn), lambda i, j: (0, j)),
                  pl.BlockSpec((tm, tn), lambda i, j: (i, j)),
                  pl.BlockSpec((None, 1, tn), lambda i, j: ((i * tm) // rows_per_gate, 0, j))],
        out_specs=pl.BlockSpec((tm, tn), lambda i, j: (i, j)),
        out_shape=jax.ShapeDtypeStruct((m, n), F32),
        compiler_params=_cparams(("parallel", "arbitrary")),
        name="mm_resid",
    )(*a_slabs, b, resid, gate)


def _ada_kernel(a_ref, w_ref, b_ref, o_ref):
    o_ref[...] = jnp.dot(a_ref[...], w_ref[...].astype(BF16), preferred_element_type=F32) + b_ref[...]


def ada_matmul(layer, a, w, bias, tn):
    m, k = a.shape
    n = w.shape[2]
    return pl.pallas_call(
        _ada_kernel,
        grid=(n // tn,),
        in_specs=[pl.BlockSpec((m, k), lambda j: (0, 0)),
                  pl.BlockSpec((None, k, tn), lambda j: (layer, 0, j)),
                  pl.BlockSpec((None, 1, tn), lambda j: (layer, 0, j))],
        out_specs=pl.BlockSpec((m, tn), lambda j: (0, j)),
        out_shape=jax.ShapeDtypeStruct((m, n), F32),
        compiler_params=_cparams(("arbitrary",)),
        name="ada",
    )(a, w, bias)


def _modulated_norm(x, g, sc, sh):
    ms = jnp.mean(x * x, axis=-1, keepdims=True)
    y = x * lax.rsqrt(ms + EPS) * g
    return y * (1.0 + sc) + sh


def _norm_mod_kernel(x_ref, g_ref, sc_ref, sh_ref, o_ref):
    o_ref[...] = _modulated_norm(x_ref[...], g_ref[...], sc_ref[...], sh_ref[...]).astype(o_ref.dtype)


def _pair_top2_sum(v):
    a, b, c, d = v
    return jnp.maximum(jnp.maximum(jnp.maximum(a + b, a + c), jnp.maximum(a + d, b + c)),
                       jnp.maximum(b + d, c + d))


def _norm_mod_router_kernel(x_ref, g_ref, sc_ref, sh_ref, rwt_ref, rb_ref, o_ref, idx_ref, w_ref):
    h = _modulated_norm(x_ref[...], g_ref[...], sc_ref[...], sh_ref[...])
    o_ref[...] = h.astype(o_ref.dtype)
    logits = lax.dot_general(rwt_ref[...], h, (((1,), (1,)), ((), ())),
                             precision=lax.Precision.HIGHEST, preferred_element_type=F32)
    scores = jax.nn.sigmoid(logits)
    biased = scores + rb_ref[...]
    s = [scores[e:e + 1, :] for e in range(N_EXPERTS)]
    b = [biased[e:e + 1, :] for e in range(N_EXPERTS)]
    gs = [_pair_top2_sum(b[E_PER_G * g:E_PER_G * (g + 1)]) for g in range(N_GROUPS)]
    best, gsel = gs[0], jnp.zeros_like(gs[0], dtype=jnp.int32)
    for g in range(1, N_GROUPS):
        better = gs[g] > best
        gsel = jnp.where(better, g, gsel)
        best = jnp.where(better, gs[g], best)
    vb, vs = [], []
    for k in range(E_PER_G):
        bk, sk = b[k], s[k]
        for g in range(1, N_GROUPS):
            bk = jnp.where(gsel == g, b[E_PER_G * g + k], bk)
            sk = jnp.where(gsel == g, s[E_PER_G * g + k], sk)
        vb.append(bk)
        vs.append(sk)
    m1, i1, w1 = vb[0], jnp.zeros_like(gsel), vs[0]
    for k in range(1, E_PER_G):
        better = vb[k] > m1
        i1 = jnp.where(better, k, i1)
        w1 = jnp.where(better, vs[k], w1)
        m1 = jnp.where(better, vb[k], m1)
    m2, i2, w2, have = vb[0], jnp.zeros_like(gsel), vs[0], i1 != 0
    for k in range(1, E_PER_G):
        valid = i1 != k
        better = valid & (jnp.logical_not(have) | (vb[k] > m2))
        i2 = jnp.where(better, k, i2)
        w2 = jnp.where(better, vs[k], w2)
        m2 = jnp.where(better, vb[k], m2)
        have = have | valid
    tot = w1 + w2
    idx_ref[0:1, :] = gsel * E_PER_G + i1
    idx_ref[1:2, :] = gsel * E_PER_G + i2
    w_ref[0:1, :] = w1 / tot
    w_ref[1:2, :] = w2 / tot


def norm_mod(x, g, sc, sh, rows_per_mod, tm, router=None):
    m, d = x.shape
    assert m % tm == 0 and rows_per_mod % tm == 0
    mod_spec = pl.BlockSpec((None, 1, d), lambda i: ((i * tm) // rows_per_mod, 0, 0))
    in_specs = [pl.BlockSpec((tm, d), lambda i: (i, 0)), pl.BlockSpec((1, d), lambda i: (0, 0)), mod_spec, mod_spec]
    o_spec = pl.BlockSpec((tm, d), lambda i: (i, 0))
    o_shape = jax.ShapeDtypeStruct((m, d), BF16)
    if router is None:
        return pl.pallas_call(
            _norm_mod_kernel, grid=(m // tm,), in_specs=in_specs, out_specs=o_spec, out_shape=o_shape,
            compiler_params=_cparams(("parallel",)), name="norm_mod",
        )(x, g, sc, sh)
    rwt, rb = router
    in_specs += [pl.BlockSpec((N_EXPERTS, d), lambda i: (0, 0)), pl.BlockSpec((N_EXPERTS, 1), lambda i: (0, 0))]
    sel_spec = pl.BlockSpec((TOP_K, tm), lambda i: (0, i))
    return pl.pallas_call(
        _norm_mod_router_kernel, grid=(m // tm,), in_specs=in_specs,
        out_specs=[o_spec, sel_spec, sel_spec],
        out_shape=[o_shape, jax.ShapeDtypeStruct((TOP_K, m), jnp.int32), jax.ShapeDtypeStruct((TOP_K, m), F32)],
        compiler_params=_cparams(("parallel",)), name="norm_mod_router",
    )(x, g, sc, sh, rwt, rb)


def _final_norm_kernel(x_ref, g_ref, o_ref):
    x = x_ref[...]
    ms = jnp.mean(x * x, axis=-1, keepdims=True)
    o_ref[...] = x * lax.rsqrt(ms + EPS) * g_ref[...]


def final_norm(x, g, tm):
    m, d = x.shape
    return pl.pallas_call(
        _final_norm_kernel, grid=(m // tm,),
        in_specs=[pl.BlockSpec((tm, d), lambda i: (i, 0)), pl.BlockSpec((1, d), lambda i: (0, 0))],
        out_specs=pl.BlockSpec((tm, d), lambda i: (i, 0)),
        out_shape=jax.ShapeDtypeStruct((m, d), F32),
        compiler_params=_cparams(("parallel",)), name="final_norm",
    )(x, g)


MOE_TM = 512
MOE_KC = 1024
MOE_FC = 256
MOE_NK = D_MODEL // MOE_KC
MOE_NF = D_FF // MOE_FC


def _moe_kernel(te_ref, nu_ref, x_ref, w1_ref, w3_ref, w2_ref, g_ref, o_ref, hg_s, hu_s, act_s):
    t = pl.program_id(0)
    s = pl.program_id(1)
    used = t < nu_ref[0]

    @pl.when(used & (s < MOE_NK))
    def _():
        x = x_ref[...]
        pg = jnp.dot(x, w1_ref[...].astype(BF16), preferred_element_type=F32)
        pu = jnp.dot(x, w3_ref[...].astype(BF16), preferred_element_type=F32)

        @pl.when(s == 0)
        def _():
            hg_s[...] = pg
            hu_s[...] = pu

        @pl.when(s > 0)
        def _():
            hg_s[...] += pg
            hu_s[...] += pu

        @pl.when(s == MOE_NK - 1)
        def _():
            hg = hg_s[...]
            act = ((hg * jax.nn.sigmoid(hg)) * hu_s[...] * g_ref[...]).astype(BF16)
            for f in range(MOE_NF):
                act_s[f] = act[:, f * MOE_FC:(f + 1) * MOE_FC]

    @pl.when(used & (s >= MOE_NK))
    def _():
        y = jnp.dot(act_s[s - MOE_NK], w2_ref[...].astype(BF16), preferred_element_type=F32)

        @pl.when(s == MOE_NK)
        def _():
            o_ref[...] = y

        @pl.when(s > MOE_NK)
        def _():
            o_ref[...] += y

    @pl.when(jnp.logical_not(used) & (s == 0))
    def _():
        o_ref[...] = jnp.zeros_like(o_ref)


def moe_grouped(layer, tile_expert, n_used, xs, w1, w3, w2, gs):
    r, d = xs.shape

    def kidx(t, s, nu):
        return jnp.where(t < nu[0], jnp.minimum(s, MOE_NK - 1), MOE_NK - 1)

    def fidx(t, s, nu):
        return jnp.where(t < nu[0], jnp.maximum(s - MOE_NK, 0), MOE_NF - 1)

    grid_spec = pltpu.PrefetchScalarGridSpec(
        num_scalar_prefetch=2,
        grid=(r // MOE_TM, MOE_NK + MOE_NF),
        in_specs=[pl.BlockSpec((MOE_TM, MOE_KC), lambda t, s, te, nu: (t, kidx(t, s, nu))),
                  pl.BlockSpec((None, None, MOE_KC, D_FF), lambda t, s, te, nu: (layer, te[t], kidx(t, s, nu), 0)),
                  pl.BlockSpec((None, None, MOE_KC, D_FF), lambda t, s, te, nu: (layer, te[t], kidx(t, s, nu), 0)),
                  pl.BlockSpec((None, None, MOE_FC, d), lambda t, s, te, nu: (layer, te[t], fidx(t, s, nu), 0)),
                  pl.BlockSpec((MOE_TM, 1), lambda t, s, te, nu: (t, 0))],
        out_specs=pl.BlockSpec((MOE_TM, d), lambda t, s, te, nu: (t, 0)),
        scratch_shapes=[pltpu.VMEM((MOE_TM, D_FF), F32), pltpu.VMEM((MOE_TM, D_FF), F32),
                        pltpu.VMEM((MOE_NF, MOE_TM, MOE_FC), BF16)],
    )
    return pl.pallas_call(
        _moe_kernel, grid_spec=grid_spec,
        out_shape=jax.ShapeDtypeStruct((r, d), F32),
        compiler_params=_cparams(("arbitrary", "arbitrary")), name="moe",
    )(tile_expert, n_used, xs, w1, w3, w2, gs)


CUMSUM_BLK = 256


def _onehot_cumsum(onehot):
    m, e = onehot.shape
    x = onehot.reshape(m // CUMSUM_BLK, CUMSUM_BLK, e)
    tri = jnp.tril(jnp.ones((CUMSUM_BLK, CUMSUM_BLK), F32))
    within = jnp.einsum('ij,bje->bie', tri, x, precision=lax.Precision.HIGHEST)
    total = within[:, -1, :]
    return (within + (jnp.cumsum(total, axis=0) - total)[:, None, :]).reshape(m, e)


def moe_dispatch(idx, wsel):
    n = idx.shape[1]
    e_flat = idx.reshape(-1)
    tok = jnp.tile(jnp.arange(n, dtype=jnp.int32), TOP_K)
    onehot = (e_flat[:, None] == jnp.arange(N_EXPERTS, dtype=jnp.int32)[None, :]).astype(F32)
    csum = _onehot_cumsum(onehot)
    counts = csum[-1].astype(jnp.int32)
    rank = jnp.sum(onehot * csum, axis=1).astype(jnp.int32) - 1
    padded = ((counts + MOE_TM - 1) // MOE_TM) * MOE_TM
    seg_end = jnp.cumsum(padded)
    seg_start = seg_end - padded
    dest = jnp.sum(onehot * seg_start.astype(F32)[None, :], axis=1).astype(jnp.int32) + rank
    rows = TOP_K * n + N_EXPERTS * MOE_TM
    src_tok = jnp.zeros((rows,), jnp.int32).at[dest].set(tok)
    gate = jnp.zeros((rows,), F32).at[dest].set(wsel.reshape(-1))
    n_tiles = rows // MOE_TM
    n_used = (seg_end[-1] // MOE_TM).astype(jnp.int32)
    tile_start = jnp.arange(n_tiles, dtype=jnp.int32) * MOE_TM
    te = jnp.minimum(jnp.searchsorted(seg_end, tile_start, side='right'), N_EXPERTS - 1).astype(jnp.int32)
    te = jnp.where(jnp.arange(n_tiles) < n_used, te, te[jnp.maximum(n_used - 1, 0)])
    return dest.reshape(TOP_K, n), src_tok, gate[:, None], te, n_used.reshape(1)


def moe_ffn(layer, hb, idx, wsel, w1, w3, w2):
    dest, src_tok, gate, te, n_used = moe_dispatch(idx, wsel)
    xs = jnp.take(hb, src_tok, axis=0)
    ys = moe_grouped(layer, te, n_used, xs, w1, w3, w2, gate)
    return jnp.take(ys, dest[0], axis=0) + jnp.take(ys, dest[1], axis=0)


SEQ_ALL = CTX_LEN + SEQ
S5_T = 16
S5_NC = SEQ_ALL // S5_T
S5_NCC = CTX_LEN // S5_T
S5_W = S5_T * S5_CH


def _gelu_tanh(x):
    return x * (0.5 * (1.0 + jnp.tanh(math.sqrt(2.0 / math.pi) * (x + 0.044715 * (x * x * x)))))


def s5_chunk_operators(lam_re, lam_im, log_step, b_re, b_im, c_re, c_im):
    hp = lax.Precision.HIGHEST
    jj = jnp.arange(S5_T)[:, None]
    ii = jnp.arange(S5_T)[None, :]
    ms, es, fs, a16 = [], [], [], []
    for d in range(2):
        dt = jnp.exp(log_step[d])[:, None]
        lr, li = lam_re[d], lam_im[d]
        er = jnp.exp(lr * dt)
        abr, abi = er * jnp.cos(li * dt), er * jnp.sin(li * dt)
        nr, ni = abr - 1.0, abi
        den = lr * lr + li * li
        fr = (nr * lr + ni * li) / den
        fi = (ni * lr - nr * li) / den
        bbr = fr[..., None] * b_re[d] - fi[..., None] * b_im[d]
        bbi = fr[..., None] * b_im[d] + fi[..., None] * b_re[d]
        k = jnp.arange(S5_T + 1, dtype=F32)[:, None, None]
        mag, ang = jnp.exp(k * (lr * dt)), k * (li * dt)
        pr, pi = mag * jnp.cos(ang), mag * jnp.sin(ang)
        pbr = pr[..., None] * bbr - pi[..., None] * bbi
        pbi = pr[..., None] * bbi + pi[..., None] * bbr
        cr, ci = c_re[d], c_im[d]
        kk = (jnp.einsum('gop,lgpc->lgoc', cr, pbr[:S5_T], precision=hp)
              - jnp.einsum('gop,lgpc->lgoc', ci, pbi[:S5_T], precision=hp))
        cpr = cr[None] * pr[:, :, None, :] - ci[None] * pi[:, :, None, :]
        cpi = -(cr[None] * pi[:, :, None, :] + ci[None] * pr[:, :, None, :])
        if d == 0:
            lag, mask = ii - jj, ii >= jj
            e_pow = S5_T - 1 - jnp.arange(S5_T)
            f_pow = jnp.arange(S5_T) + 1
        else:
            lag, mask = jj - ii, jj >= ii
            e_pow = jnp.arange(S5_T)
            f_pow = S5_T - jnp.arange(S5_T)
        mm = jnp.where(mask[:, :, None, None, None], kk[jnp.clip(lag, 0, S5_T - 1)], 0.0)
        ms.append(mm.transpose(2, 0, 4, 1, 3).reshape(S5_G, S5_W, S5_W))
        for pb in (pbr, pbi):
            es.append(pb[e_pow].transpose(1, 0, 3, 2).reshape(S5_G, S5_W, S5_P))
        for cp in (cpr, cpi):
            fs.append(cp[f_pow].transpose(1, 3, 0, 2).reshape(S5_G, S5_P, S5_W))
        a16 += [pr[S5_T][:, None, :], pi[S5_T][:, None, :]]
    return (ms[0] + ms[1]).astype(BF16), jnp.stack(es).astype(BF16), jnp.stack(fs).astype(BF16), jnp.stack(a16)


def _s5_kernel(u_ref, m_ref, e_ref, f_ref, a_ref, o_ref, e_scr, s_scr):
    u = u_ref[...]
    for k in range(4):
        e_scr[k] = jnp.dot(u, e_ref[k], preferred_element_type=F32)
    arf, aif, arb, aib = (a_ref[k] for k in range(4))

    def segment(start, n, carry):
        def body(k, carry):
            out = []
            for b in range(BATCH):
                sfr, sfi, sbr, sbi = carry[4 * b:4 * b + 4]
                rf = b * S5_NC + start + k
                rb = b * S5_NC + start + n - 1 - k
                s_scr[0, pl.ds(rf, 1), :] = sfr
                s_scr[1, pl.ds(rf, 1), :] = sfi
                s_scr[2, pl.ds(rb, 1), :] = sbr
                s_scr[3, pl.ds(rb, 1), :] = sbi
                nfr = arf * sfr - aif * sfi + e_scr[0, pl.ds(rf, 1), :]
                nfi = arf * sfi + aif * sfr + e_scr[1, pl.ds(rf, 1), :]
                nbr = arb * sbr - aib * sbi + e_scr[2, pl.ds(rb, 1), :]
                nbi = arb * sbi + aib * sbr + e_scr[3, pl.ds(rb, 1), :]
                out += [nfr, nfi, nbr, nbi]
            return tuple(out)
        return lax.fori_loop(0, n, body, carry, unroll=4)

    z = jnp.zeros((1, S5_P), F32)
    carry = segment(0, S5_NCC, (z,) * (4 * BATCH))
    segment(S5_NCC, S5_NC - S5_NCC, carry)
    y = jnp.dot(u, m_ref[...], preferred_element_type=F32)
    for k in range(4):
        y += jnp.dot(s_scr[k].astype(BF16), f_ref[k], preferred_element_type=F32)
    o_ref[...] = y


def s5_scan(u, m, e, f, a16):
    g, r, w = u.shape
    return pl.pallas_call(
        _s5_kernel, grid=(g,),
        in_specs=[pl.BlockSpec((None, r, w), lambda i: (i, 0, 0)),
                  pl.BlockSpec((None, w, w), lambda i: (i, 0, 0)),
                  pl.BlockSpec((4, None, w, S5_P), lambda i: (0, i, 0, 0)),
                  pl.BlockSpec((4, None, S5_P, w), lambda i: (0, i, 0, 0)),
                  pl.BlockSpec((4, None, 1, S5_P), lambda i: (0, i, 0, 0))],
        out_specs=pl.BlockSpec((None, r, w), lambda i: (i, 0, 0)),
        out_shape=jax.ShapeDtypeStruct((g, r, w), F32),
        scratch_shapes=[pltpu.VMEM((4, r, S5_P), F32), pltpu.VMEM((4, r, S5_P), F32)],
        compiler_params=_cparams(("parallel",)), name="s5_scan",
    )(u, m, e, f, a16)


def _s5_out_kernel(y_ref, u_ref, d_ref, w_ref, b_ref, o_ref):
    g = _gelu_tanh(y_ref[...] + d_ref[...] * u_ref[...])
    z = jnp.dot(g.astype(BF16), w_ref[...], preferred_element_type=F32) + b_ref[...]
    o_ref[...] = (g * jax.nn.sigmoid(z)).astype(o_ref.dtype)


def s5_out(y, u, d_skip, w_glu, b_glu, tm):
    m, c = y.shape
    row = pl.BlockSpec((tm, c), lambda i: (i, 0))
    vec = pl.BlockSpec((1, c), lambda i: (0, 0))
    return pl.pallas_call(
        _s5_out_kernel, grid=(m // tm,),
        in_specs=[row, row, vec, pl.BlockSpec((c, c), lambda i: (0, 0)), vec],
        out_specs=row, out_shape=jax.ShapeDtypeStruct((m, c), F32),
        compiler_params=_cparams(("parallel",)), name="s5_out",
    )(y, u, d_skip, w_glu, b_glu)


def s5_mixer(u_c, u_l, lam_re, lam_im, log_step, b_re, b_im, c_re, c_im, d_skip, w_glu, b_glu):
    bs = u_c.shape[0]
    u_all = jnp.concatenate([u_c, u_l], axis=1)
    uc = u_all.astype(BF16).reshape(bs, S5_NC, S5_T, S5_G, S5_CH).transpose(3, 0, 1, 2, 4)
    uc = uc.reshape(S5_G, bs * S5_NC, S5_W)
    y = s5_scan(uc, *s5_chunk_operators(lam_re, lam_im, log_step, b_re, b_im, c_re, c_im))
    y = y.reshape(S5_G, bs, S5_NC, S5_T, S5_CH).transpose(1, 2, 3, 0, 4).reshape(bs * SEQ_ALL, MIX_W)
    out = s5_out(y, u_all.reshape(bs * SEQ_ALL, MIX_W), d_skip[None], w_glu.astype(BF16), b_glu[None], 544)
    out = out.reshape(bs, SEQ_ALL, MIX_W)
    return out[:, :CTX_LEN], out[:, CTX_LEN:]


LRU_CB = 256
LRU_TB = 256
LRU_HALO = 8
LRU_ROWS = SEQ_ALL + 3 * LRU_HALO


def _lru_kernel(xc_ref, xl_ref, yc_ref, yl_ref, cw_ref, cb_ref, wa_ref, ba_ref, wi_ref, bi_ref, sp_ref,
                oc_ref, ol_ref, xpad, a_f, b_f, a_b, b_b):
    halo = jnp.zeros((LRU_HALO, LRU_CB), F32)
    lat0 = CTX_LEN + 2 * LRU_HALO
    xpad[0:LRU_HALO] = halo
    xpad[LRU_HALO:LRU_HALO + CTX_LEN] = xc_ref[...]
    xpad[LRU_HALO + CTX_LEN:lat0] = halo
    xpad[lat0:lat0 + SEQ] = xl_ref[...]
    xpad[lat0 + SEQ:LRU_ROWS] = halo
    cw = [cw_ref[j:j + 1, :] for j in range(CONV_W)]
    cb = cb_ref[...]

    def gates(p0, r0):
        ext = xpad[pl.ds(p0 - LRU_HALO, LRU_TB + 2 * LRU_HALO), :]
        xb = cb
        for j in range(CONV_W):
            s0 = LRU_HALO - CONV_PAD_L + j
            xb = xb + cw[j] * ext[s0:s0 + LRU_TB]
        for k in range(LRU_CB // LRU_BD):
            lanes = slice(k * LRU_BD, (k + 1) * LRU_BD)
            xk = xb[:, lanes]
            xkb = xk.astype(BF16)
            for d, (a_s, b_s) in enumerate(((a_f, b_f), (a_b, b_b))):
                r = jax.nn.sigmoid(jnp.dot(xkb, wa_ref[d, k], preferred_element_type=F32) + ba_ref[d][:, lanes])
                i = jax.nn.sigmoid(jnp.dot(xkb, wi_ref[d, k], preferred_element_type=F32) + bi_ref[d][:, lanes])
                log_a = -LRU_C * r * sp_ref[d][:, lanes]
                t = jnp.tanh(log_a)
                a_s[pl.ds(r0, LRU_TB), lanes] = jnp.exp(log_a)
                b_s[pl.ds(r0, LRU_TB), lanes] = jnp.sqrt(-2.0 * t / (1.0 - t)) * (i * xk)

    gates(LRU_HALO, 0)

    def lat_gates(k, _):
        gates(pl.multiple_of(lat0 + k * LRU_TB, 8), pl.multiple_of(CTX_LEN + k * LRU_TB, 8))
        return 0
    lax.fori_loop(0, SEQ // LRU_TB, lat_gates, 0)

    def segment(start, n, carry):
        def body(t, carry):
            hf, hb = carry
            rf = start + t
            rb = start + n - 1 - t
            hf = a_f[pl.ds(rf, 1), :] * hf + b_f[pl.ds(rf, 1), :]
            b_f[pl.ds(rf, 1), :] = hf
            hb = a_b[pl.ds(rb, 1), :] * hb + b_b[pl.ds(rb, 1), :]
            b_b[pl.ds(rb, 1), :] = hb
            return hf, hb
        return lax.fori_loop(0, n, body, carry, unroll=8)

    z = jnp.zeros((1, LRU_CB), F32)
    carry = segment(0, CTX_LEN, (z, z))
    segment(CTX_LEN, SEQ, carry)

    oc_ref[...] = (_gelu_tanh(yc_ref[...]) * (b_f[0:CTX_LEN] + b_b[0:CTX_LEN])).astype(oc_ref.dtype)

    def lat_out(k, _):
        r0 = pl.multiple_of(k * LRU_TB, 8)
        rows = pl.ds(pl.multiple_of(CTX_LEN + k * LRU_TB, 8), LRU_TB)
        h = b_f[rows, :] + b_b[rows, :]
        ol_ref[pl.ds(r0, LRU_TB), :] = (_gelu_tanh(yl_ref[pl.ds(r0, LRU_TB), :]) * h).astype(ol_ref.dtype)
        return 0
    lax.fori_loop(0, SEQ // LRU_TB, lat_out, 0)


def lru_mixer(y_c, x_c, yx_l, conv_w, conv_b, w_a, b_a, w_i, b_i, lam):
    bs = x_c.shape[0]
    nb = LRU_CB // LRU_BD
    sp = jax.nn.softplus(-lam)[:, None, :]
    ctx_spec = pl.BlockSpec((None, CTX_LEN, LRU_CB), lambda b, c: (b, 0, c))
    lat_spec = pl.BlockSpec((None, SEQ, LRU_CB), lambda b, c: (b, 0, c))
    lat_x_spec = pl.BlockSpec((None, SEQ, LRU_CB), lambda b, c: (b, 0, MIX_W // LRU_CB + c))
    vec2 = pl.BlockSpec((2, 1, LRU_CB), lambda b, c: (0, 0, c))
    wspec = pl.BlockSpec((2, nb, LRU_BD, LRU_BD), lambda b, c: (0, c, 0, 0))
    scan_buf = pltpu.VMEM((SEQ_ALL, LRU_CB), F32)
    return pl.pallas_call(
        _lru_kernel, grid=(bs, MIX_W // LRU_CB),
        in_specs=[ctx_spec, lat_x_spec, ctx_spec, lat_spec,
                  pl.BlockSpec((CONV_W, LRU_CB), lambda b, c: (0, c)),
                  pl.BlockSpec((1, LRU_CB), lambda b, c: (0, c)),
                  wspec, vec2, wspec, vec2, vec2],
        out_specs=[ctx_spec, lat_spec],
        out_shape=[jax.ShapeDtypeStruct((bs, CTX_LEN, MIX_W), F32), jax.ShapeDtypeStruct((bs, SEQ, MIX_W), F32)],
        scratch_shapes=[pltpu.VMEM((LRU_ROWS, LRU_CB), F32), scan_buf, scan_buf, scan_buf, scan_buf],
        compiler_params=_cparams(("parallel", "parallel")), name="lru",
    )(x_c, yx_l, y_c, yx_l, conv_w, conv_b[None], w_a.astype(BF16), b_a[:, None, :], w_i.astype(BF16),
      b_i[:, None, :], sp)


GT_W = 8
GT_C = 1024


def _grid_transpose_kernel(x_ref, o_ref):
    for j in range(GT_W):
        o_ref[j] = x_ref[:, j, :].astype(o_ref.dtype)


def grid_transpose(src, batch, col0, width, out_dtype):
    assert col0 % GT_C == 0 and width % GT_C == 0
    return pl.pallas_call(
        _grid_transpose_kernel, grid=(batch, GRID_W // GT_W, width // GT_C),
        in_specs=[pl.BlockSpec((GRID_W, GT_W, GT_C), lambda b, j, c: (b, j, col0 // GT_C + c))],
        out_specs=pl.BlockSpec((None, GT_W, GRID_W, GT_C), lambda b, j, c: (b, j, 0, c)),
        out_shape=jax.ShapeDtypeStruct((batch, GRID_W, GRID_W, width), out_dtype),
        compiler_params=_cparams(("parallel", "parallel", "parallel")), name="grid_transpose",
    )(src)


N_CHUNKS = SEQ_ALL // CHUNK
N_CCH = CTX_LEN // CHUNK
GATE_COLBLK = 80
MIX_TB = 256


def _dot(a, b):
    return jnp.dot(a, b, preferred_element_type=F32)


def _dot_nt(a, b):
    return lax.dot_general(a, b, (((1,), (1,)), ((), ())), preferred_element_type=F32)


def _dot_tn(a, b):
    return lax.dot_general(a, b, (((0,), (0,)), ((), ())), preferred_element_type=F32)


def _split2(a):
    hi = a.astype(BF16)
    return hi, (a - hi.astype(F32)).astype(BF16)


def _dot3(a, b):
    ah, al = _split2(a)
    bh, bl = _split2(b)
    return _dot(ah, bh) + (_dot(ah, bl) + _dot(al, bh))


def _dot_mask(m, x):
    hi = x.astype(BF16)
    r1 = x - hi.astype(F32)
    mid = r1.astype(BF16)
    lo = (r1 - mid.astype(F32)).astype(BF16)
    return _dot(m, hi) + (_dot(m, mid) + _dot(m, lo))


def _chunk_masks(backward):
    i = lax.broadcasted_iota(jnp.int32, (CHUNK, CHUNK), 0)
    j = lax.broadcasted_iota(jnp.int32, (CHUNK, CHUNK), 1)
    if backward:
        i, j = j, i
    tri, strict = j <= i, j < i
    return tri, strict, tri.astype(BF16), jnp.where(strict, 1.0, 0.0).astype(F32)


def _lane_pick(blk, lane):
    lanes = lax.broadcasted_iota(jnp.int32, blk.shape, 1)
    return jnp.sum(jnp.where(lanes == lane, blk, 0.0), axis=-1, keepdims=True)


def _softplus(x):
    return jnp.maximum(x, 0.0) + jnp.log1p(jnp.exp(-jnp.abs(x)))


def _silu(x):
    return x * jax.nn.sigmoid(x)


def _bidir_chunks(step, carry, n_ctx, n_all):
    def segment(start, n, carry):
        def body(t, carry):
            return step(start + t, start + n - 1 - t, carry)
        return lax.fori_loop(0, n, body, carry)
    carry = segment(0, n_ctx, carry)
    return segment(n_ctx, n_all - n_ctx, carry)


DN_HALO = 8
DN_ROWS = SEQ_ALL + 3 * DN_HALO
DN_C = 64
DN_GROUP = 256
DN_GC = DN_GROUP // (2 * DN_C)
DN_NCH = SEQ_ALL // DN_C
DN_NCC = CTX_LEN // DN_C
DN_LOCKSTEP = 4


def _dn_kernel(qc_ref, ql_ref, kc_ref, kl_ref, vc_ref, vl_ref, zc_ref, zl_ref, gc_ref, gl_ref, hp_ref,
               cwq_ref, cwk_ref, cwv_ref, nw_ref, ol_ref, oc_ref,
               xpad, qn, kn, vn, u_s, w_s, qg_s, kd_s, at_s, dec_s):
    h = pl.program_id(1)
    halo = jnp.zeros((DN_HALO, DN_HD), F32)
    lat0 = CTX_LEN + 2 * DN_HALO

    def conv_into(xc_ref, xl_ref, cw_ref, dst, post):
        xpad[0:DN_HALO] = halo
        xpad[DN_HALO:DN_HALO + CTX_LEN] = xc_ref[...]
        xpad[DN_HALO + CTX_LEN:lat0] = halo
        xpad[lat0:lat0 + SEQ] = xl_ref[...]
        xpad[lat0 + SEQ:DN_ROWS] = halo
        cw = [cw_ref[j:j + 1, :] for j in range(CONV_W)]

        def block(p0, r0):
            ext = xpad[pl.ds(p0 - DN_HALO, MIX_TB + 2 * DN_HALO), :]
            s0 = DN_HALO - CONV_PAD_L
            acc = cw[0] * ext[s0:s0 + MIX_TB]
            for j in range(1, CONV_W):
                acc = acc + cw[j] * ext[s0 + j:s0 + j + MIX_TB]
            dst[pl.ds(r0, MIX_TB), :] = post(_silu(acc))

        block(DN_HALO, 0)

        def lat_block(k, _):
            block(pl.multiple_of(lat0 + k * MIX_TB, 8), pl.multiple_of(CTX_LEN + k * MIX_TB, 8))
            return 0
        lax.fori_loop(0, SEQ // MIX_TB, lat_block, 0)

    def l2n(scale):
        return lambda t: t * (lax.rsqrt(jnp.sum(t * t, axis=-1, keepdims=True) + EPS) * scale)

    conv_into(qc_ref, ql_ref, cwq_ref, qn, l2n(DN_HD ** -0.5))
    conv_into(kc_ref, kl_ref, cwk_ref, kn, l2n(1.0))
    conv_into(vc_ref, vl_ref, cwv_ref, vn, lambda t: t)

    gi = lax.broadcasted_iota(jnp.int32, (DN_GROUP, DN_GROUP), 0)
    gj = lax.broadcasted_iota(jnp.int32, (DN_GROUP, DN_GROUP), 1)
    same = (gi // DN_C) == (gj // DN_C)
    blk16 = (gi // 16) == (gj // 16)
    bwd = ((gi // DN_C) % 2) == 1
    fwd = jnp.logical_not(bwd)
    li, lj = gi % DN_C, gj % DN_C
    tri = same & ((fwd & (lj <= li)) | (bwd & (lj >= li)))
    strict = same & ((fwd & (lj < li)) | (bwd & (lj > li)))
    eye = jnp.where(gi == gj, 1.0, 0.0).astype(F32)
    same16 = jnp.where(same, 1.0, 0.0).astype(BF16)
    m16 = jnp.where(tri, 1.0, 0.0).astype(BF16)
    nn = jnp.where(strict, 1.0, 0.0).astype(F32)

    def both_dirs(xf, xb):
        parts = []
        for c in range(DN_GC):
            parts += [xf[c * DN_C:(c + 1) * DN_C], xb[c * DN_C:(c + 1) * DN_C]]
        return jnp.concatenate(parts, axis=0)

    def prep(ts, g_ref, row0):
        grows = DN_GC * DN_C
        n = range(len(ts))
        r0 = [pl.multiple_of(t * grows, grows) for t in ts]
        gblk = [g_ref[pl.ds(pl.multiple_of(t * grows - row0, grows), grows), :] for t in ts]
        q = [both_dirs(*[qn[pl.ds(r0[i], grows), :]] * 2) for i in n]
        k = [both_dirs(*[kn[pl.ds(r0[i], grows), :]] * 2) for i in n]
        v = [both_dirs(*[vn[pl.ds(r0[i], grows), :]] * 2) for i in n]
        g_col = [both_dirs(*[hp_ref[:, d:d + 1] * _softplus(_lane_pick(gblk[i], d * DN_H + h)
                                                            + hp_ref[:, 2 + d:3 + d]) for d in range(2)]) for i in n]
        beta = [both_dirs(*[jax.nn.sigmoid(_lane_pick(gblk[i], 2 * DN_H + d * DN_H + h)) for d in range(2)])
                for i in n]
        k16 = [k[i].astype(BF16) for i in n]
        kb = [k[i] * beta[i] for i in n]
        gb = [jnp.broadcast_to(g_col[i], (DN_GROUP, DN_HD)) for i in n]
        dlt = [_dot_mask(m16, g_col[i] * nn) for i in n]
        kk = [_dot_nt(kb[i].astype(BF16), k16[i]) for i in n]
        gcum = [_dot_mask(m16, gb[i]) for i in n]
        tot = [_dot_mask(same16, gb[i]) for i in n]
        qk = [_dot_nt(q[i].astype(BF16), k16[i]) for i in n]
        decay = [jnp.where(tri, jnp.exp(jnp.where(tri, dlt[i], 0.0)), 0.0) for i in n]
        lower = [jnp.where(strict, kk[i] * decay[i], 0.0) for i in n]
        x = [-jnp.where(blk16, lower[i], 0.0) for i in n]
        tinv = [eye + x[i] for i in n]
        for _ in range(3):
            x = [_dot3(x[i], x[i]) for i in n]
            tinv = [tinv[i] + _dot3(tinv[i], x[i]) for i in n]
        size = 16
        while size < DN_C:
            inner, outer = (gi // size) == (gj // size), (gi // (2 * size)) == (gj // (2 * size))
            off = [_dot3(jnp.where(outer & jnp.logical_not(inner), lower[i], 0.0), tinv[i]) for i in n]
            tinv = [tinv[i] - _dot3(tinv[i], off[i]) for i in n]
            size *= 2
        eg = [jnp.exp(gcum[i]) for i in n]
        sol = [_dot3(tinv[i], jnp.concatenate([v[i] * beta[i], kb[i] * eg[i]], axis=1)) for i in n]
        for i in n:
            attn = (qk[i] * decay[i]).astype(BF16)
            w16 = sol[i][:, DN_HD:].astype(BF16)
            qg16 = (q[i] * eg[i]).astype(BF16)
            kd16 = (k[i] * jnp.exp(tot[i] - gcum[i])).astype(BF16)
            dec = jnp.exp(tot[i])
            for p in range(DN_GROUP // DN_C):
                d, blk = p % 2, slice(p * DN_C, (p + 1) * DN_C)
                dst = pl.ds(r0[i] + (p // 2) * DN_C, DN_C)
                u_s[d, dst, :] = sol[i][blk, :DN_HD]
                w_s[d, dst, :] = w16[blk]
                at_s[d, dst, :] = attn[blk, blk]
                qg_s[d, dst, :] = qg16[blk]
                kd_s[d, dst, :] = kd16[blk]
                dec_s[d, pl.ds(pl.multiple_of((DN_GC * ts[i] + p // 2) * 8, 8), 8), :] = dec[p * DN_C:p * DN_C + 8]
        return 0

    n_ctx_groups, n_groups = DN_NCC // DN_GC, DN_NCH // DN_GC
    assert (n_groups - n_ctx_groups) % DN_LOCKSTEP == 0
    prep(list(range(n_ctx_groups)), gc_ref, 0)
    lax.fori_loop(0, (n_groups - n_ctx_groups) // DN_LOCKSTEP,
                  lambda t, _: prep([n_ctx_groups + DN_LOCKSTEP * t + i for i in range(DN_LOCKSTEP)],
                                    gl_ref, CTX_LEN), 0)
    o_dst = (qn, kn)
    o_f, o_b = o_dst

    def step(cf, cb, carry):
        dirs = range(2)
        rows = [pl.ds(pl.multiple_of(c * DN_C, DN_C), DN_C) for c in (cf, cb)]
        s16 = [carry[d].astype(BF16) for d in dirs]
        ws = [_dot(w_s[d, rows[d], :], s16[d]) for d in dirs]
        qs = [_dot(qg_s[d, rows[d], :], s16[d]) for d in dirs]
        v_new = [(u_s[d, rows[d], :] - ws[d]).astype(BF16) for d in dirs]
        kv = [_dot_tn(kd_s[d, rows[d], :], v_new[d]) for d in dirs]
        av = [_dot(at_s[d, rows[d], :], v_new[d]) for d in dirs]
        out = []
        for d, c in zip(dirs, (cf, cb)):
            o_dst[d][rows[d], :] = qs[d] + av[d]
            g_last = dec_s[d, pl.ds(pl.multiple_of(c * 8, 8), 8), :][0:1]
            out.append(carry[d] * g_last + kv[d])
        return tuple(out)

    z0 = jnp.zeros((DN_HD, DN_HD), F32)
    _bidir_chunks(step, (z0, z0), DN_NCC, DN_NCH)

    nw = nw_ref[...]

    def finish(rows, z):
        o = o_f[rows, :] + o_b[rows, :]
        o = o * lax.rsqrt(jnp.mean(o * o, axis=-1, keepdims=True) + EPS) * nw
        return (o * _silu(z)).astype(BF16)

    oc_ref[...] = finish(pl.ds(0, CTX_LEN), zc_ref[...])

    def lat_finish(k, _):
        r0 = pl.multiple_of(k * MIX_TB, 8)
        ol_ref[pl.ds(r0, MIX_TB), :] = finish(pl.ds(pl.multiple_of(CTX_LEN + k * MIX_TB, 8), MIX_TB),
                                              zl_ref[pl.ds(r0, MIX_TB), :])
        return 0
    lax.fori_loop(0, SEQ // MIX_TB, lat_finish, 0)


def deltanet_mixer(proj, conv_w, a_log, dt_bias, norm_w):
    hp = jnp.stack([-jnp.exp(a_log[0]), -jnp.exp(a_log[1]), dt_bias[0], dt_bias[1]], axis=-1)[:, None, :]
    ctx_blk0 = N_LAT // CTX_LEN

    def seg_specs(col0):
        cb = col0 // DN_HD
        return [pl.BlockSpec((CTX_LEN, DN_HD), lambda b, h: (ctx_blk0 + b, cb + h)),
                pl.BlockSpec((SEQ, DN_HD), lambda b, h: (b, cb + h))]

    def conv_spec(k):
        return pl.BlockSpec((CONV_W, DN_HD), lambda b, h: (0, k * DN_H + h))

    in_specs = (seg_specs(COL_DN_QKVZ) + seg_specs(COL_DN_QKVZ + MIX_W) + seg_specs(COL_DN_QKVZ + 2 * MIX_W)
                + seg_specs(COL_DN_QKVZ + 3 * MIX_W)
                + [pl.BlockSpec((CTX_LEN, 128), lambda b, h: (ctx_blk0 + b, GATE_COLBLK)),
                   pl.BlockSpec((SEQ, 128), lambda b, h: (b, GATE_COLBLK)),
                   pl.BlockSpec((None, 1, 4), lambda b, h: (h, 0, 0)),
                   conv_spec(0), conv_spec(1), conv_spec(2),
                   pl.BlockSpec((1, DN_HD), lambda b, h: (0, 0))])
    seq_f32 = pltpu.VMEM((SEQ_ALL, DN_HD), F32)
    dir_bf16 = pltpu.VMEM((2, SEQ_ALL, DN_HD), BF16)
    ol, oc = pl.pallas_call(
        _dn_kernel, grid=(BATCH, DN_H), in_specs=in_specs,
        out_specs=[pl.BlockSpec((SEQ, DN_HD), lambda b, h: (b, h)),
                   pl.BlockSpec((CTX_LEN, DN_HD), lambda b, h: (b, h))],
        out_shape=[jax.ShapeDtypeStruct((N_LAT, MIX_W), BF16), jax.ShapeDtypeStruct((N_CTX, MIX_W), BF16)],
        scratch_shapes=[pltpu.VMEM((DN_ROWS, DN_HD), F32), seq_f32, seq_f32, seq_f32,
                        pltpu.VMEM((2, SEQ_ALL, DN_HD), F32), dir_bf16, dir_bf16, dir_bf16,
                        pltpu.VMEM((2, SEQ_ALL, DN_C), BF16), pltpu.VMEM((2, DN_NCH * 8, DN_HD), F32)],
        compiler_params=_cparams(("parallel", "arbitrary")), name="deltanet",
    )(*([proj] * 10), hp, conv_w, conv_w, conv_w, norm_w[None])
    return ol, oc


def _ml_kernel(qc_ref, ql_ref, kc_ref, kl_ref, vc_ref, vl_ref, oc_ref, ol_ref, gc_ref, gl_ref, hp_ref, nw_ref,
               yl_ref, yc_ref, h_f, h_b):
    hh = pl.program_id(1)
    masks = [_chunk_masks(False), _chunk_masks(True)]
    ones = jnp.ones((CHUNK, CHUNK), BF16)
    eye = jnp.where(lax.broadcasted_iota(jnp.int32, (CHUNK, CHUNK), 0)
                    == lax.broadcasted_iota(jnp.int32, (CHUNK, CHUNK), 1), 1.0, 0.0).astype(F32)

    h_dst = (h_f, h_b)

    def advance(cs, states, refs, row0):
        q_ref, k_ref, v_ref, g_ref = refs
        dirs = range(2)
        rows = [pl.ds(pl.multiple_of(c * CHUNK - row0, CHUNK), CHUNK) for c in cs]
        tri = [masks[d][0] for d in dirs]
        gblk = [g_ref[rows[d], :] for d in dirs]
        ig = [_lane_pick(gblk[d], 4 * DN_H + d * ML_H + hh) + hp_ref[:, d:d + 1] for d in dirs]
        lf = [-_softplus(-(_lane_pick(gblk[d], 4 * DN_H + 2 * ML_H + d * ML_H + hh) + hp_ref[:, 2 + d:3 + d]))
              for d in dirs]
        q = [q_ref[rows[d], :] * (ML_DQK ** -0.5) for d in dirs]
        k = [k_ref[rows[d], :] for d in dirs]
        q16 = [q[d].astype(BF16) for d in dirs]
        v16 = [v_ref[rows[d], :].astype(BF16) for d in dirs]
        qc = [_dot(q16[d], states[d][0].astype(BF16)) for d in dirs]
        bcum = [_dot_mask(masks[d][2], jnp.broadcast_to(lf[d], (CHUNK, ML_DQK)))[:, 0:1] for d in dirs]
        dsum = [_dot_mask(masks[d][2], lf[d] * masks[d][3]) + _dot_mask(ones, ig[d] * eye) for d in dirs]
        sqk = [_dot_nt(q16[d], k[d].astype(BF16)) for d in dirs]
        dmat = [jnp.where(tri[d], dsum[d], -jnp.inf) for d in dirs]
        dmax = [jnp.max(dmat[d], axis=-1, keepdims=True) for d in dirs]
        tot = [jnp.sum(lf[d], axis=0, keepdims=True) for d in dirs]
        logw = [tot[d] - bcum[d] + ig[d] for d in dirs]
        logw_max = [jnp.max(logw[d], axis=0, keepdims=True) for d in dirs]
        inter = [bcum[d] + states[d][2] for d in dirs]
        m_t = [jnp.maximum(inter[d], dmax[d]) for d in dirs]
        w_inter = [jnp.exp(inter[d] - m_t[d]) for d in dirs]
        wmat = [jnp.exp(dmat[d] - m_t[d]) * sqk[d] for d in dirs]
        m_new = [jnp.maximum(tot[d] + states[d][2], logw_max[d]) for d in dirs]
        keep = [jnp.exp(tot[d] + states[d][2] - m_new[d]) for d in dirs]
        kw = [k[d] * jnp.exp(logw[d] - m_new[d]) for d in dirs]
        wv = [_dot(wmat[d].astype(BF16), v16[d]) for d in dirs]
        kv = [_dot_tn(kw[d].astype(BF16), v16[d]) for d in dirs]
        out = []
        for d in dirs:
            c_s, n_s, _ = states[d]
            num = w_inter[d] * qc[d] + wv[d]
            den = w_inter[d] * jnp.sum(q[d] * n_s, axis=-1, keepdims=True) + jnp.sum(wmat[d], axis=-1, keepdims=True)
            h_dst[d][pl.ds(pl.multiple_of(cs[d] * CHUNK, CHUNK), CHUNK), :] = (
                num / jnp.maximum(jnp.abs(den), jnp.exp(-m_t[d])))
            out.append((keep[d] * c_s + kv[d], keep[d] * n_s + jnp.sum(kw[d], axis=0, keepdims=True), m_new[d]))
        return tuple(out)

    ctx_refs = (qc_ref, kc_ref, vc_ref, gc_ref)
    lat_refs = (ql_ref, kl_ref, vl_ref, gl_ref)

    def segment(refs, start, n, carry):
        def body(t, carry):
            return advance((start + t, start + n - 1 - t), carry, refs, start * CHUNK)
        return lax.fori_loop(0, n, body, carry)

    s0 = (jnp.zeros((ML_DQK, ML_DV), F32), jnp.zeros((1, ML_DQK), F32), jnp.zeros((1, 1), F32))
    carry = segment(ctx_refs, 0, N_CCH, (s0, s0))
    segment(lat_refs, N_CCH, N_CHUNKS - N_CCH, carry)

    nw = nw_ref[...]

    def finish(rows, o):
        hsum = h_f[rows, :] + h_b[rows, :]
        hsum = hsum * lax.rsqrt(jnp.mean(hsum * hsum, axis=-1, keepdims=True) + EPS) * nw
        return (hsum * jax.nn.sigmoid(o)).astype(BF16)

    yc_ref[...] = finish(pl.ds(0, CTX_LEN), oc_ref[...])

    def lat_finish(k, _):
        r0 = pl.multiple_of(k * MIX_TB, 8)
        yl_ref[pl.ds(r0, MIX_TB), :] = finish(pl.ds(pl.multiple_of(CTX_LEN + k * MIX_TB, 8), MIX_TB),
                                              ol_ref[pl.ds(r0, MIX_TB), :])
        return 0
    lax.fori_loop(0, SEQ // MIX_TB, lat_finish, 0)


def mlstm_mixer(proj, i_bias, f_bias, norm_w):
    hp = jnp.stack([i_bias[0], i_bias[1], f_bias[0], f_bias[1]], axis=-1)[:, None, :]
    ctx_blk0 = N_LAT // CTX_LEN

    def seg_specs(col0, w):
        cb = col0 // w
        return [pl.BlockSpec((CTX_LEN, w), lambda b, h: (ctx_blk0 + b, cb + h)),
                pl.BlockSpec((SEQ, w), lambda b, h: (b, cb + h))]

    in_specs = (seg_specs(COL_ML_Q, ML_DQK) + seg_specs(COL_ML_K, ML_DQK) + seg_specs(COL_ML_V, ML_DV)
                + seg_specs(COL_ML_O, ML_DV)
                + [pl.BlockSpec((CTX_LEN, 128), lambda b, h: (ctx_blk0 + b, GATE_COLBLK)),
                   pl.BlockSpec((SEQ, 128), lambda b, h: (b, GATE_COLBLK)),
                   pl.BlockSpec((None, 1, 4), lambda b, h: (h, 0, 0)),
                   pl.BlockSpec((1, ML_DV), lambda b, h: (0, h))])
    seq_f32 = pltpu.VMEM((SEQ_ALL, ML_DV), F32)
    yl, yc = pl.pallas_call(
        _ml_kernel, grid=(BATCH, ML_H), in_specs=in_specs,
        out_specs=[pl.BlockSpec((SEQ, ML_DV), lambda b, h: (b, h)),
                   pl.BlockSpec((CTX_LEN, ML_DV), lambda b, h: (b, h))],
        out_shape=[jax.ShapeDtypeStruct((N_LAT, MIX_W), BF16), jax.ShapeDtypeStruct((N_CTX, MIX_W), BF16)],
        scratch_shapes=[seq_f32, seq_f32],
        compiler_params=_cparams(("parallel", "arbitrary")), name="mlstm",
    )(*([proj] * 10), hp, norm_w[None])
    return yl, yc


PROJ_TM = (N_LAT + N_CTX) // 8
PROJ_TN = 512
PROJ_PAD = -(-PROJ_W // PROJ_TN) * PROJ_TN
COL_S5, COL_DN_QKVZ, COL_ML_Q, COL_ML_K, COL_ML_V, COL_ML_O, COL_LRU_Y, COL_LRU_X = (
    0, 1024, 5120, 5632, 6144, 7168, 8192, 9216)
COL_DN_A, COL_DN_B, COL_ML_I, COL_ML_F = 10240, 10256, 10272, 10280


def permute_proj_weight(w):
    o = [0] + [int(v) for v in np.cumsum(PROJ_SIZES)]
    parts = [w[:, o[0]:o[5]], w[:, o[7]:o[11]], w[:, o[13]:o[15]], w[:, o[5]:o[7]], w[:, o[11]:o[13]]]
    return jnp.pad(jnp.concatenate(parts, axis=1).astype(BF16), ((0, 0), (0, PROJ_PAD - PROJ_W)))


def hybrid_mixer(proj, s5_p, dn_p, ml_p, lru_p):
    def ctx_seg(c0):
        return proj[N_LAT:, c0:c0 + MIX_W].reshape(BATCH, CTX_LEN, MIX_W)

    def to_row_major(t):
        return grid_transpose(t.reshape(BATCH * GRID_W, GRID_W, MIX_W), BATCH, 0, MIX_W, BF16).reshape(N_LAT, MIX_W)

    grid = proj.reshape((N_LAT + N_CTX) // GRID_W, GRID_W, PROJ_PAD)
    u_l = grid_transpose(grid, BATCH, COL_S5, MIX_W, F32).reshape(BATCH, SEQ, MIX_W)
    yx_l = grid_transpose(grid, BATCH, COL_LRU_Y, 2 * MIX_W, F32).reshape(BATCH, SEQ, 2 * MIX_W)
    a_c, a_l = s5_mixer(ctx_seg(COL_S5), u_l, *s5_p)
    d_c, d_l = lru_mixer(ctx_seg(COL_LRU_Y), ctx_seg(COL_LRU_X), yx_l, *lru_p)
    flat = lambda t_l, t_c: (to_row_major(t_l), t_c.reshape(N_CTX, MIX_W).astype(BF16))
    return [flat(a_l, a_c), deltanet_mixer(proj, *dn_p), mlstm_mixer(proj, *ml_p), flat(d_l, d_c)]


def kernel(x, c, ctx, c_ctx, w_ada, b_ada, norm1, norm2, norm_f, w_in, w_out, s5_lam_re, s5_lam_im, s5_log_step,
           s5_b_re, s5_b_im, s5_c_re, s5_c_im, s5_d, s5_w_glu, s5_b_glu, dn_conv, dn_a_log, dn_dt_bias, dn_norm,
           ml_i_bias, ml_f_bias, ml_norm, lru_conv_w, lru_conv_b, lru_w_a, lru_b_a, lru_w_i, lru_b_i, lru_lam,
           router_w, router_bias, moe_w1, moe_w3, moe_w2):
    d = D_MODEL
    xa = jnp.concatenate([x.reshape(N_LAT, d), ctx.reshape(N_CTX, d)], axis=0)
    cond = jnp.concatenate([jax.nn.silu(c), jax.nn.silu(c_ctx)[None], jnp.zeros((16 - BATCH - 1, d), F32)], axis=0)
    cond = cond.astype(BF16)
    router = (router_w.T, router_bias.reshape(N_EXPERTS, 1))

    for l in range(DEPTH):
        last = l == DEPTH - 1
        mod = ada_matmul(l, cond, w_ada, b_ada[:, None, :], 512)[:BATCH + 1].reshape(BATCH + 1, 6, 1, d)
        shift1, scale1, gate1, shift2, scale2, gate2 = (mod[:, k] for k in range(6))

        h1 = norm_mod(xa, norm1[l][None], scale1, shift1, SEQ, 256)
        proj = matmul(h1, permute_proj_weight(w_in[l]), PROJ_TM, PROJ_TN)
        slabs = hybrid_mixer(
            proj,
            (s5_lam_re[l], s5_lam_im[l], s5_log_step[l], s5_b_re[l], s5_b_im[l], s5_c_re[l], s5_c_im[l], s5_d[l],
             s5_w_glu[l], s5_b_glu[l]),
            (dn_conv[l], dn_a_log[l], dn_dt_bias[l], dn_norm[l]),
            (ml_i_bias[l], ml_f_bias[l], ml_norm[l]),
            (lru_conv_w[l], lru_conv_b[l], lru_w_a[l], lru_b_a[l], lru_w_i[l], lru_b_i[l], lru_lam[l]))
        if last:
            xa, mix = xa[:N_LAT], [s[0] for s in slabs]
        else:
            mix = [jnp.concatenate(s, axis=0) for s in slabs]
        out_tm = 1024 if xa.shape[0] % 1024 == 0 else 512
        xa = matmul_gated_residual(mix, w_out[l].astype(BF16), xa, gate1, SEQ, out_tm, 512)

        h2, idx, wsel = norm_mod(xa, norm2[l][None], scale2, shift2, SEQ, 256, router=router)
        y = moe_ffn(l, h2, idx, wsel, moe_w1, moe_w3, moe_w2)
        upd = (xa[:N_LAT].reshape(BATCH, SEQ, d) + gate2[:BATCH] * y[:N_LAT].reshape(BATCH, SEQ, d)).reshape(N_LAT, d)
        if last:
            xa = upd
        else:
            xa = jnp.concatenate([upd, xa[N_LAT:] + gate2[BATCH] * y[N_LAT:]], axis=0)

    return final_norm(xa[:N_LAT], norm_f[None], 256).reshape(BATCH, SEQ, d)
```

```python
import functools
import math

import numpy as np
import jax
import jax.numpy as jnp
from jax import lax
from jax.experimental import pallas as pl
from jax.experimental.pallas import tpu as pltpu

D_MODEL = 4096
BATCH = 2
SEQ = 4096
DEPTH = 2
GRID_W = 64
CTX_LEN = 256
MIX_W = D_MODEL // 4
MIX_TOTAL = 4 * MIX_W
CHUNK = 64
CONV_W = 4
CONV_PAD_L = 2
CONV_PAD_R = 1
EPS = 1e-6
S5_CH = 16
S5_G = MIX_W // S5_CH
S5_P = 64
DN_HD = 128
DN_H = MIX_W // DN_HD
ML_H = 4
ML_DV = MIX_W // ML_H
ML_DQK = ML_DV // 2
LRU_BLOCKS = 8
LRU_BD = MIX_W // LRU_BLOCKS
LRU_C = 8.0
N_EXPERTS = 16
N_GROUPS = 4
E_PER_G = N_EXPERTS // N_GROUPS
TOP_K = 2
D_FF = D_MODEL // 4
PROJ_SIZES = (MIX_W, MIX_W, MIX_W, MIX_W, MIX_W, 2 * DN_H, 2 * DN_H, ML_H * ML_DQK, ML_H * ML_DQK, MIX_W, MIX_W,
              2 * ML_H, 2 * ML_H, MIX_W, MIX_W)
PROJ_W = sum(PROJ_SIZES)

N_LAT = BATCH * SEQ
N_CTX = BATCH * CTX_LEN
VMEM_LIMIT = 56 * 1024 * 1024

F32 = jnp.float32
BF16 = jnp.bfloat16


def _cparams(sem):
    return pltpu.CompilerParams(dimension_semantics=sem, vmem_limit_bytes=VMEM_LIMIT)


def _mm_kernel(a_ref, b_ref, o_ref):
    o_ref[...] = jnp.dot(a_ref[...], b_ref[...], preferred_element_type=F32).astype(o_ref.dtype)


def matmul(a, b, tm, tn, out_dtype=F32):
    m, k = a.shape
    n = b.shape[1]
    assert m % tm == 0 and n % tn == 0
    return pl.pallas_call(
        _mm_kernel,
        grid=(m // tm, n // tn),
        in_specs=[pl.BlockSpec((tm, k), lambda i, j: (i, 0)),
                  pl.BlockSpec((k, tn), lambda i, j: (0, j))],
        out_specs=pl.BlockSpec((tm, tn), lambda i, j: (i, j)),
        out_shape=jax.ShapeDtypeStruct((m, n), out_dtype),
        compiler_params=_cparams(("parallel", "arbitrary")),
        name="mm",
    )(a, b)


def _mm_resid_kernel(a0_ref, a1_ref, a2_ref, a3_ref, b_ref, r_ref, g_ref, o_ref):
    acc = None
    for s, a_ref in enumerate((a0_ref, a1_ref, a2_ref, a3_ref)):
        part = jnp.dot(a_ref[...], b_ref[s * MIX_W:(s + 1) * MIX_W, :], preferred_element_type=F32)
        acc = part if acc is None else acc + part
    o_ref[...] = r_ref[...] + g_ref[...] * acc


def matmul_gated_residual(a_slabs, b, resid, gate, rows_per_gate, tm, tn):
    m = a_slabs[0].shape[0]
    k, n = b.shape
    assert m % tm == 0 and n % tn == 0 and rows_per_gate % tm == 0 and k == 4 * MIX_W
    return pl.pallas_call(
        _mm_resid_kernel,
        grid=(m // tm, n // tn),
        in_specs=[pl.BlockSpec((tm, MIX_W), lambda i, j: (i, 0))] * 4 + [
                  pl.BlockSpec((k, tn), lambda i, j: (0, j)),
                  pl.BlockSpec((tm, tn), lambda i, j: (i, j)),
                  pl.BlockSpec((None, 1, tn), lambda i, j: ((i * tm) // rows_per_gate, 0, j))],
        out_specs=pl.BlockSpec((tm, tn), lambda i, j: (i, j)),
        out_shape=jax.ShapeDtypeStruct((m, n), F32),
        compiler_params=_cparams(("parallel", "arbitrary")),
        name="mm_resid",
    )(*a_slabs, b, resid, gate)


def _ada_kernel(a_ref, w_ref, b_ref, o_ref):
    o_ref[...] = jnp.dot(a_ref[...], w_ref[...].astype(BF16), preferred_element_type=F32) + b_ref[...]


def ada_matmul(layer, a, w, bias, tn):
    m, k = a.shape
    n = w.shape[2]
    return pl.pallas_call(
        _ada_kernel,
        grid=(n // tn,),
        in_specs=[pl.BlockSpec((m, k), lambda j: (0, 0)),
                  pl.BlockSpec((None, k, tn), lambda j: (layer, 0, j)),
                  pl.BlockSpec((None, 1, tn), lambda j: (layer, 0, j))],
        out_specs=pl.BlockSpec((m, tn), lambda j: (0, j)),
        out_shape=jax.ShapeDtypeStruct((m, n), F32),
        compiler_params=_cparams(("arbitrary",)),
        name="ada",
    )(a, w, bias)


def _modulated_norm(x, g, sc, sh):
    ms = jnp.mean(x * x, axis=-1, keepdims=True)
    y = x * lax.rsqrt(ms + EPS) * g
    return y * (1.0 + sc) + sh


def _norm_mod_kernel(x_ref, g_ref, sc_ref, sh_ref, o_ref):
    o_ref[...] = _modulated_norm(x_ref[...], g_ref[...], sc_ref[...], sh_ref[...]).astype(o_ref.dtype)


def _pair_top2_sum(v):
    a, b, c, d = v
    return jnp.maximum(jnp.maximum(jnp.maximum(a + b, a + c), jnp.maximum(a + d, b + c)),
                       jnp.maximum(b + d, c + d))


def _norm_mod_router_kernel(x_ref, g_ref, sc_ref, sh_ref, rwt_ref, rb_ref, o_ref, idx_ref, w_ref):
    h = _modulated_norm(x_ref[...], g_ref[...], sc_ref[...], sh_ref[...])
    o_ref[...] = h.astype(o_ref.dtype)
    logits = lax.dot_general(rwt_ref[...], h, (((1,), (1,)), ((), ())),
                             precision=lax.Precision.HIGHEST, preferred_element_type=F32)
    scores = jax.nn.sigmoid(logits)
    biased = scores + rb_ref[...]
    s = [scores[e:e + 1, :] for e in range(N_EXPERTS)]
    b = [biased[e:e + 1, :] for e in range(N_EXPERTS)]
    gs = [_pair_top2_sum(b[E_PER_G * g:E_PER_G * (g + 1)]) for g in range(N_GROUPS)]
    best, gsel = gs[0], jnp.zeros_like(gs[0], dtype=jnp.int32)
    for g in range(1, N_GROUPS):
        better = gs[g] > best
        gsel = jnp.where(better, g, gsel)
        best = jnp.where(better, gs[g], best)
    vb, vs = [], []
    for k in range(E_PER_G):
        bk, sk = b[k], s[k]
        for g in range(1, N_GROUPS):
            bk = jnp.where(gsel == g, b[E_PER_G * g + k], bk)
            sk = jnp.where(gsel == g, s[E_PER_G * g + k], sk)
        vb.append(bk)
        vs.append(sk)
    m1, i1, w1 = vb[0], jnp.zeros_like(gsel), vs[0]
    for k in range(1, E_PER_G):
        better = vb[k] > m1
        i1 = jnp.where(better, k, i1)
        w1 = jnp.where(better, vs[k], w1)
        m1 = jnp.where(better, vb[k], m1)
    m2, i2, w2, have = vb[0], jnp.zeros_like(gsel), vs[0], i1 != 0
    for k in range(1, E_PER_G):
        valid = i1 != k
        better = valid & (jnp.logical_not(have) | (vb[k] > m2))
        i2 = jnp.where(better, k, i2)
        w2 = jnp.where(better, vs[k], w2)
        m2 = jnp.where(better, vb[k], m2)
        have = have | valid
    tot = w1 + w2
    idx_ref[0:1, :] = gsel * E_PER_G + i1
    idx_ref[1:2, :] = gsel * E_PER_G + i2
    w_ref[0:1, :] = w1 / tot
    w_ref[1:2, :] = w2 / tot


def norm_mod(x, g, sc, sh, rows_per_mod, tm, router=None):
    m, d = x.shape
    assert m % tm == 0 and rows_per_mod % tm == 0
    mod_spec = pl.BlockSpec((None, 1, d), lambda i: ((i * tm) // rows_per_mod, 0, 0))
    in_specs = [pl.BlockSpec((tm, d), lambda i: (i, 0)), pl.BlockSpec((1, d), lambda i: (0, 0)), mod_spec, mod_spec]
    o_spec = pl.BlockSpec((tm, d), lambda i: (i, 0))
    o_shape = jax.ShapeDtypeStruct((m, d), BF16)
    if router is None:
        return pl.pallas_call(
            _norm_mod_kernel, grid=(m // tm,), in_specs=in_specs, out_specs=o_spec, out_shape=o_shape,
            compiler_params=_cparams(("parallel",)), name="norm_mod",
        )(x, g, sc, sh)
    rwt, rb = router
    in_specs += [pl.BlockSpec((N_EXPERTS, d), lambda i: (0, 0)), pl.BlockSpec((N_EXPERTS, 1), lambda i: (0, 0))]
    sel_spec = pl.BlockSpec((TOP_K, tm), lambda i: (0, i))
    return pl.pallas_call(
        _norm_mod_router_kernel, grid=(m // tm,), in_specs=in_specs,
        out_specs=[o_spec, sel_spec, sel_spec],
        out_shape=[o_shape, jax.ShapeDtypeStruct((TOP_K, m), jnp.int32), jax.ShapeDtypeStruct((TOP_K, m), F32)],
        compiler_params=_cparams(("parallel",)), name="norm_mod_router",
    )(x, g, sc, sh, rwt, rb)


def _gated_add_kernel(x_ref, y_ref, g_ref, o_ref):
    o_ref[...] = x_ref[...] + g_ref[...] * y_ref[...]


def gated_add(x, y, gate, rows_per_gate, tm):
    m, d = x.shape
    assert m % tm == 0 and rows_per_gate % tm == 0
    row = pl.BlockSpec((tm, d), lambda i: (i, 0))
    return pl.pallas_call(
        _gated_add_kernel, grid=(m // tm,),
        in_specs=[row, row, pl.BlockSpec((None, 1, d), lambda i: ((i * tm) // rows_per_gate, 0, 0))],
        out_specs=row, out_shape=jax.ShapeDtypeStruct((m, d), F32),
        compiler_params=_cparams(("parallel",)), name="gated_add",
    )(x, y, gate)


def _final_norm_kernel(x_ref, g_ref, o_ref):
    x = x_ref[...]
    ms = jnp.mean(x * x, axis=-1, keepdims=True)
    o_ref[...] = x * lax.rsqrt(ms + EPS) * g_ref[...]


def final_norm(x, g, tm):
    m, d = x.shape
    return pl.pallas_call(
        _final_norm_kernel, grid=(m // tm,),
        in_specs=[pl.BlockSpec((tm, d), lambda i: (i, 0)), pl.BlockSpec((1, d), lambda i: (0, 0))],
        out_specs=pl.BlockSpec((tm, d), lambda i: (i, 0)),
        out_shape=jax.ShapeDtypeStruct((m, d), F32),
        compiler_params=_cparams(("parallel",)), name="final_norm",
    )(x, g)


MOE_TM = 512
MOE_KC = 1024
MOE_FC = 256
MOE_NK = D_MODEL // MOE_KC
MOE_NF = D_FF // MOE_FC
MOE_PARTS = 2


def _moe_kernel(te_ref, nu_ref, x_ref, w1_ref, w3_ref, w2_ref, g_ref, *rest, tile0, has_prev):
    o_ref, hg_s, hu_s, act_s = rest[1:] if has_prev else rest
    t = pl.program_id(0)
    s = pl.program_id(1)
    used = t + tile0 < nu_ref[0]

    @pl.when(used & (s < MOE_NK))
    def _():
        x = x_ref[...]
        pg = jnp.dot(x, w1_ref[...].astype(BF16), preferred_element_type=F32)
        pu = jnp.dot(x, w3_ref[...].astype(BF16), preferred_element_type=F32)

        @pl.when(s == 0)
        def _():
            hg_s[...] = pg
            hu_s[...] = pu

        @pl.when(s > 0)
        def _():
            hg_s[...] += pg
            hu_s[...] += pu

        @pl.when(s == MOE_NK - 1)
        def _():
            hg = hg_s[...]
            act = ((hg * jax.nn.sigmoid(hg)) * hu_s[...] * g_ref[...]).astype(BF16)
            for f in range(MOE_NF):
                act_s[f] = act[:, f * MOE_FC:(f + 1) * MOE_FC]

    @pl.when(used & (s >= MOE_NK))
    def _():
        y = jnp.dot(act_s[s - MOE_NK], w2_ref[...].astype(BF16), preferred_element_type=F32)

        @pl.when(s == MOE_NK)
        def _():
            o_ref[...] = y

        @pl.when(s > MOE_NK)
        def _():
            o_ref[...] += y

    @pl.when(jnp.logical_not(used) & (s == 0))
    def _():
        o_ref[...] = jnp.zeros_like(o_ref)


def moe_grouped(layer, tile_expert, n_used, xs, w1, w3, w2, gs, tile0, ys_prev):
    rp, d = xs.shape
    r = gs.shape[0]

    def live(t, nu):
        return t + tile0 < nu[0]

    def kidx(t, s, nu):
        return jnp.where(live(t, nu), jnp.minimum(s, MOE_NK - 1), MOE_NK - 1)

    def fidx(t, s, nu):
        return jnp.where(live(t, nu), jnp.maximum(s - MOE_NK, 0), MOE_NF - 1)

    in_specs = [pl.BlockSpec((MOE_TM, MOE_KC), lambda t, s, te, nu: (t, kidx(t, s, nu))),
                pl.BlockSpec((None, None, MOE_KC, D_FF),
                             lambda t, s, te, nu: (layer, te[t + tile0], kidx(t, s, nu), 0)),
                pl.BlockSpec((None, None, MOE_KC, D_FF),
                             lambda t, s, te, nu: (layer, te[t + tile0], kidx(t, s, nu), 0)),
                pl.BlockSpec((None, None, MOE_FC, d), lambda t, s, te, nu: (layer, te[t + tile0], fidx(t, s, nu), 0)),
                pl.BlockSpec((MOE_TM, 1), lambda t, s, te, nu: (t + tile0, 0))]
    args = [tile_expert, n_used, xs, w1, w3, w2, gs]
    aliases = {}
    if ys_prev is not None:
        in_specs.append(pl.BlockSpec(memory_space=pl.ANY))
        aliases = {len(args): 0}
        args.append(ys_prev)
    grid_spec = pltpu.PrefetchScalarGridSpec(
        num_scalar_prefetch=2,
        grid=(rp // MOE_TM, MOE_NK + MOE_NF),
        in_specs=in_specs,
        out_specs=pl.BlockSpec((MOE_TM, d), lambda t, s, te, nu: (t + tile0, 0)),
        scratch_shapes=[pltpu.VMEM((MOE_TM, D_FF), F32), pltpu.VMEM((MOE_TM, D_FF), F32),
                        pltpu.VMEM((MOE_NF, MOE_TM, MOE_FC), BF16)],
    )
    return pl.pallas_call(
        functools.partial(_moe_kernel, tile0=tile0, has_prev=ys_prev is not None), grid_spec=grid_spec,
        out_shape=jax.ShapeDtypeStruct((r, d), F32), input_output_aliases=aliases,
        compiler_params=_cparams(("arbitrary", "arbitrary")), name="moe",
    )(*args)


CUMSUM_BLK = 256


def _onehot_cumsum(onehot):
    m, e = onehot.shape
    x = onehot.reshape(m // CUMSUM_BLK, CUMSUM_BLK, e)
    tri = jnp.tril(jnp.ones((CUMSUM_BLK, CUMSUM_BLK), F32))
    within = jnp.einsum('ij,bje->bie', tri, x, precision=lax.Precision.HIGHEST)
    total = within[:, -1, :]
    return (within + (jnp.cumsum(total, axis=0) - total)[:, None, :]).reshape(m, e)


def moe_dispatch(idx, wsel):
    n = idx.shape[1]
    e_flat = idx.reshape(-1)
    tok = jnp.tile(jnp.arange(n, dtype=jnp.int32), TOP_K)
    onehot = (e_flat[:, None] == jnp.arange(N_EXPERTS, dtype=jnp.int32)[None, :]).astype(F32)
    csum = _onehot_cumsum(onehot)
    counts = csum[-1].astype(jnp.int32)
    rank = jnp.sum(onehot * csum, axis=1).astype(jnp.int32) - 1
    padded = ((counts + MOE_TM - 1) // MOE_TM) * MOE_TM
    seg_end = jnp.cumsum(padded)
    seg_start = seg_end - padded
    dest = jnp.sum(onehot * seg_start.astype(F32)[None, :], axis=1).astype(jnp.int32) + rank
    rows = TOP_K * n + N_EXPERTS * MOE_TM
    src_tok = jnp.zeros((rows,), jnp.int32).at[dest].set(tok)
    gate = jnp.zeros((rows,), F32).at[dest].set(wsel.reshape(-1))
    n_tiles = rows // MOE_TM
    n_used = (seg_end[-1] // MOE_TM).astype(jnp.int32)
    tile_start = jnp.arange(n_tiles, dtype=jnp.int32) * MOE_TM
    te = jnp.minimum(jnp.searchsorted(seg_end, tile_start, side='right'), N_EXPERTS - 1).astype(jnp.int32)
    te = jnp.where(jnp.arange(n_tiles) < n_used, te, te[jnp.maximum(n_used - 1, 0)])
    return dest.reshape(TOP_K, n), src_tok, gate[:, None], te, n_used.reshape(1)


def moe_ffn(layer, hb, idx, wsel, w1, w3, w2):
    dest, src_tok, gate, te, n_used = moe_dispatch(idx, wsel)
    n_tiles = src_tok.shape[0] // MOE_TM
    assert n_tiles % MOE_PARTS == 0
    part_tiles = n_tiles // MOE_PARTS
    ys = None
    for p in range(MOE_PARTS):
        rows = slice(p * part_tiles * MOE_TM, (p + 1) * part_tiles * MOE_TM)
        ys = moe_grouped(layer, te, n_used, jnp.take(hb, src_tok[rows], axis=0), w1, w3, w2, gate,
                         p * part_tiles, ys)
    return jnp.take(ys, dest[0], axis=0) + jnp.take(ys, dest[1], axis=0)


SEQ_ALL = CTX_LEN + SEQ
S5_T = 16
S5_NC = SEQ_ALL // S5_T
S5_NCC = CTX_LEN // S5_T
S5_W = S5_T * S5_CH


def _gelu_tanh(x):
    return x * (0.5 * (1.0 + jnp.tanh(math.sqrt(2.0 / math.pi) * (x + 0.044715 * (x * x * x)))))


def s5_chunk_operators(lam_re, lam_im, log_step, b_re, b_im, c_re, c_im):
    hp = lax.Precision.HIGHEST
    jj = jnp.arange(S5_T)[:, None]
    ii = jnp.arange(S5_T)[None, :]
    ms, es, fs, a16 = [], [], [], []
    for d in range(2):
        dt = jnp.exp(log_step[d])[:, None]
        lr, li = lam_re[d], lam_im[d]
        er = jnp.exp(lr * dt)
        abr, abi = er * jnp.cos(li * dt), er * jnp.sin(li * dt)
        nr, ni = abr - 1.0, abi
        den = lr * lr + li * li
        fr = (nr * lr + ni * li) / den
        fi = (ni * lr - nr * li) / den
        bbr = fr[..., None] * b_re[d] - fi[..., None] * b_im[d]
        bbi = fr[..., None] * b_im[d] + fi[..., None] * b_re[d]
        k = jnp.arange(S5_T + 1, dtype=F32)[:, None, None]
        mag, ang = jnp.exp(k * (lr * dt)), k * (li * dt)
        pr, pi = mag * jnp.cos(ang), mag * jnp.sin(ang)
        pbr = pr[..., None] * bbr - pi[..., None] * bbi
        pbi = pr[..., None] * bbi + pi[..., None] * bbr
        cr, ci = c_re[d], c_im[d]
        kk = (jnp.einsum('gop,lgpc->lgoc', cr, pbr[:S5_T], precision=hp)
              - jnp.einsum('gop,lgpc->lgoc', ci, pbi[:S5_T], precision=hp))
        cpr = cr[None] * pr[:, :, None, :] - ci[None] * pi[:, :, None, :]
        cpi = -(cr[None] * pi[:, :, None, :] + ci[None] * pr[:, :, None, :])
        if d == 0:
            lag, mask = ii - jj, ii >= jj
            e_pow = S5_T - 1 - jnp.arange(S5_T)
            f_pow = jnp.arange(S5_T) + 1
        else:
            lag, mask = jj - ii, jj >= ii
            e_pow = jnp.arange(S5_T)
            f_pow = S5_T - jnp.arange(S5_T)
        mm = jnp.where(mask[:, :, None, None, None], kk[jnp.clip(lag, 0, S5_T - 1)], 0.0)
        ms.append(mm.transpose(2, 0, 4, 1, 3).reshape(S5_G, S5_W, S5_W))
        for pb in (pbr, pbi):
            es.append(pb[e_pow].transpose(1, 0, 3, 2).reshape(S5_G, S5_W, S5_P))
        for cp in (cpr, cpi):
            fs.append(cp[f_pow].transpose(1, 3, 0, 2).reshape(S5_G, S5_P, S5_W))
        a16 += [pr[S5_T][:, None, :], pi[S5_T][:, None, :]]
    return (ms[0] + ms[1]).astype(BF16), jnp.stack(es).astype(BF16), jnp.stack(fs).astype(BF16), jnp.stack(a16)


def _s5_kernel(u_ref, m_ref, e_ref, f_ref, a_ref, o_ref, e_scr, s_scr):
    u = u_ref[...]
    for k in range(4):
        e_scr[k] = jnp.dot(u, e_ref[k], preferred_element_type=F32)
    arf, aif, arb, aib = (a_ref[k] for k in range(4))

    def segment(start, n, carry):
        def body(k, carry):
            out = []
            for b in range(BATCH):
                sfr, sfi, sbr, sbi = carry[4 * b:4 * b + 4]
                rf = b * S5_NC + start + k
                rb = b * S5_NC + start + n - 1 - k
                s_scr[0, pl.ds(rf, 1), :] = sfr
                s_scr[1, pl.ds(rf, 1), :] = sfi
                s_scr[2, pl.ds(rb, 1), :] = sbr
                s_scr[3, pl.ds(rb, 1), :] = sbi
                nfr = arf * sfr - aif * sfi + e_scr[0, pl.ds(rf, 1), :]
                nfi = arf * sfi + aif * sfr + e_scr[1, pl.ds(rf, 1), :]
                nbr = arb * sbr - aib * sbi + e_scr[2, pl.ds(rb, 1), :]
                nbi = arb * sbi + aib * sbr + e_scr[3, pl.ds(rb, 1), :]
                out += [nfr, nfi, nbr, nbi]
            return tuple(out)
        return lax.fori_loop(0, n, body, carry, unroll=4)

    z = jnp.zeros((1, S5_P), F32)
    carry = segment(0, S5_NCC, (z,) * (4 * BATCH))
    segment(S5_NCC, S5_NC - S5_NCC, carry)
    y = jnp.dot(u, m_ref[...], preferred_element_type=F32)
    for k in range(4):
        y += jnp.dot(s_scr[k].astype(BF16), f_ref[k], preferred_element_type=F32)
    o_ref[...] = y


def s5_scan(u, m, e, f, a16):
    g, r, w = u.shape
    return pl.pallas_call(
        _s5_kernel, grid=(g,),
        in_specs=[pl.BlockSpec((None, r, w), lambda i: (i, 0, 0)),
                  pl.BlockSpec((None, w, w), lambda i: (i, 0, 0)),
                  pl.BlockSpec((4, None, w, S5_P), lambda i: (0, i, 0, 0)),
                  pl.BlockSpec((4, None, S5_P, w), lambda i: (0, i, 0, 0)),
                  pl.BlockSpec((4, None, 1, S5_P), lambda i: (0, i, 0, 0))],
        out_specs=pl.BlockSpec((None, r, w), lambda i: (i, 0, 0)),
        out_shape=jax.ShapeDtypeStruct((g, r, w), F32),
        scratch_shapes=[pltpu.VMEM((4, r, S5_P), F32), pltpu.VMEM((4, r, S5_P), F32)],
        compiler_params=_cparams(("parallel",)), name="s5_scan",
    )(u, m, e, f, a16)


def _s5_out_kernel(y_ref, u_ref, d_ref, w_ref, b_ref, o_ref):
    g = _gelu_tanh(y_ref[...] + d_ref[...] * u_ref[...])
    z = jnp.dot(g.astype(BF16), w_ref[...], preferred_element_type=F32) + b_ref[...]
    o_ref[...] = (g * jax.nn.sigmoid(z)).astype(o_ref.dtype)


def s5_out(y, u, d_skip, w_glu, b_glu, tm):
    m, c = y.shape
    row = pl.BlockSpec((tm, c), lambda i: (i, 0))
    vec = pl.BlockSpec((1, c), lambda i: (0, 0))
    return pl.pallas_call(
        _s5_out_kernel, grid=(m // tm,),
        in_specs=[row, row, vec, pl.BlockSpec((c, c), lambda i: (0, 0)), vec],
        out_specs=row, out_shape=jax.ShapeDtypeStruct((m, c), F32),
        compiler_params=_cparams(("parallel",)), name="s5_out",
    )(y, u, d_skip, w_glu, b_glu)


def s5_mixer(u_c, u_l, lam_re, lam_im, log_step, b_re, b_im, c_re, c_im, d_skip, w_glu, b_glu):
    bs = u_c.shape[0]
    u_all = jnp.concatenate([u_c, u_l], axis=1)
    uc = u_all.astype(BF16).reshape(bs, S5_NC, S5_T, S5_G, S5_CH).transpose(3, 0, 1, 2, 4)
    uc = uc.reshape(S5_G, bs * S5_NC, S5_W)
    y = s5_scan(uc, *s5_chunk_operators(lam_re, lam_im, log_step, b_re, b_im, c_re, c_im))
    y = y.reshape(S5_G, bs, S5_NC, S5_T, S5_CH).transpose(1, 2, 3, 0, 4).reshape(bs * SEQ_ALL, MIX_W)
    out = s5_out(y, u_all.reshape(bs * SEQ_ALL, MIX_W), d_skip[None], w_glu.astype(BF16), b_glu[None], 544)
    out = out.reshape(bs, SEQ_ALL, MIX_W)
    return out[:, :CTX_LEN], out[:, CTX_LEN:]


LRU_CB = 256
LRU_TB = 256
LRU_HALO = 8
LRU_ROWS = SEQ_ALL + 3 * LRU_HALO


def _lru_kernel(xc_ref, xl_ref, yc_ref, yl_ref, cw_ref, cb_ref, wa_ref, ba_ref, wi_ref, bi_ref, sp_ref,
                oc_ref, ol_ref, xpad, a_f, b_f, a_b, b_b):
    halo = jnp.zeros((LRU_HALO, LRU_CB), F32)
    lat0 = CTX_LEN + 2 * LRU_HALO
    xpad[0:LRU_HALO] = halo
    xpad[LRU_HALO:LRU_HALO + CTX_LEN] = xc_ref[...]
    xpad[LRU_HALO + CTX_LEN:lat0] = halo
    xpad[lat0:lat0 + SEQ] = xl_ref[...]
    xpad[lat0 + SEQ:LRU_ROWS] = halo
    cw = [cw_ref[j:j + 1, :] for j in range(CONV_W)]
    cb = cb_ref[...]

    def gates(p0, r0):
        ext = xpad[pl.ds(p0 - LRU_HALO, LRU_TB + 2 * LRU_HALO), :]
        xb = cb
        for j in range(CONV_W):
            s0 = LRU_HALO - CONV_PAD_L + j
            xb = xb + cw[j] * ext[s0:s0 + LRU_TB]
        for k in range(LRU_CB // LRU_BD):
            lanes = slice(k * LRU_BD, (k + 1) * LRU_BD)
            xk = xb[:, lanes]
            xkb = xk.astype(BF16)
            for d, (a_s, b_s) in enumerate(((a_f, b_f), (a_b, b_b))):
                r = jax.nn.sigmoid(jnp.dot(xkb, wa_ref[d, k], preferred_element_type=F32) + ba_ref[d][:, lanes])
                i = jax.nn.sigmoid(jnp.dot(xkb, wi_ref[d, k], preferred_element_type=F32) + bi_ref[d][:, lanes])
                log_a = -LRU_C * r * sp_ref[d][:, lanes]
                t = jnp.tanh(log_a)
                a_s[pl.ds(r0, LRU_TB), lanes] = jnp.exp(log_a)
                b_s[pl.ds(r0, LRU_TB), lanes] = jnp.sqrt(-2.0 * t / (1.0 - t)) * (i * xk)

    gates(LRU_HALO, 0)

    def lat_gates(k, _):
        gates(pl.multiple_of(lat0 + k * LRU_TB, 8), pl.multiple_of(CTX_LEN + k * LRU_TB, 8))
        return 0
    lax.fori_loop(0, SEQ // LRU_TB, lat_gates, 0)

    def segment(start, n, carry):
        def body(t, carry):
            hf, hb = carry
            rf = start + t
            rb = start + n - 1 - t
            hf = a_f[pl.ds(rf, 1), :] * hf + b_f[pl.ds(rf, 1), :]
            b_f[pl.ds(rf, 1), :] = hf
            hb = a_b[pl.ds(rb, 1), :] * hb + b_b[pl.ds(rb, 1), :]
            b_b[pl.ds(rb, 1), :] = hb
            return hf, hb
        return lax.fori_loop(0, n, body, carry, unroll=8)

    z = jnp.zeros((1, LRU_CB), F32)
    carry = segment(0, CTX_LEN, (z, z))
    segment(CTX_LEN, SEQ, carry)

    oc_ref[...] = (_gelu_tanh(yc_ref[...]) * (b_f[0:CTX_LEN] + b_b[0:CTX_LEN])).astype(oc_ref.dtype)

    def lat_out(k, _):
        r0 = pl.multiple_of(k * LRU_TB, 8)
        rows = pl.ds(pl.multiple_of(CTX_LEN + k * LRU_TB, 8), LRU_TB)
        h = b_f[rows, :] + b_b[rows, :]
        ol_ref[pl.ds(r0, LRU_TB), :] = (_gelu_tanh(yl_ref[pl.ds(r0, LRU_TB), :]) * h).astype(ol_ref.dtype)
        return 0
    lax.fori_loop(0, SEQ // LRU_TB, lat_out, 0)


def lru_mixer(y_c, x_c, yx_l, conv_w, conv_b, w_a, b_a, w_i, b_i, lam):
    bs = x_c.shape[0]
    nb = LRU_CB // LRU_BD
    sp = jax.nn.softplus(-lam)[:, None, :]
    ctx_spec = pl.BlockSpec((None, CTX_LEN, LRU_CB), lambda b, c: (b, 0, c))
    lat_spec = pl.BlockSpec((None, SEQ, LRU_CB), lambda b, c: (b, 0, c))
    lat_x_spec = pl.BlockSpec((None, SEQ, LRU_CB), lambda b, c: (b, 0, MIX_W // LRU_CB + c))
    vec2 = pl.BlockSpec((2, 1, LRU_CB), lambda b, c: (0, 0, c))
    wspec = pl.BlockSpec((2, nb, LRU_BD, LRU_BD), lambda b, c: (0, c, 0, 0))
    scan_buf = pltpu.VMEM((SEQ_ALL, LRU_CB), F32)
    return pl.pallas_call(
        _lru_kernel, grid=(bs, MIX_W // LRU_CB),
        in_specs=[ctx_spec, lat_x_spec, ctx_spec, lat_spec,
                  pl.BlockSpec((CONV_W, LRU_CB), lambda b, c: (0, c)),
                  pl.BlockSpec((1, LRU_CB), lambda b, c: (0, c)),
                  wspec, vec2, wspec, vec2, vec2],
        out_specs=[ctx_spec, lat_spec],
        out_shape=[jax.ShapeDtypeStruct((bs, CTX_LEN, MIX_W), F32), jax.ShapeDtypeStruct((bs, SEQ, MIX_W), F32)],
        scratch_shapes=[pltpu.VMEM((LRU_ROWS, LRU_CB), F32), scan_buf, scan_buf, scan_buf, scan_buf],
        compiler_params=_cparams(("parallel", "parallel")), name="lru",
    )(x_c, yx_l, y_c, yx_l, conv_w, conv_b[None], w_a.astype(BF16), b_a[:, None, :], w_i.astype(BF16),
      b_i[:, None, :], sp)


GT_W = 8
GT_C = 1024


def _grid_transpose_kernel(x_ref, o_ref):
    for j in range(GT_W):
        o_ref[j] = x_ref[:, j, :].astype(o_ref.dtype)


def grid_transpose(src, batch, col0, width, out_dtype):
    assert col0 % GT_C == 0 and width % GT_C == 0
    return pl.pallas_call(
        _grid_transpose_kernel, grid=(batch, GRID_W // GT_W, width // GT_C),
        in_specs=[pl.BlockSpec((GRID_W, GT_W, GT_C), lambda b, j, c: (b, j, col0 // GT_C + c))],
        out_specs=pl.BlockSpec((None, GT_W, GRID_W, GT_C), lambda b, j, c: (b, j, 0, c)),
        out_shape=jax.ShapeDtypeStruct((batch, GRID_W, GRID_W, width), out_dtype),
        compiler_params=_cparams(("parallel", "parallel", "parallel")), name="grid_transpose",
    )(src)


N_CHUNKS = SEQ_ALL // CHUNK
N_CCH = CTX_LEN // CHUNK
GATE_COLBLK = 80
MIX_TB = 256


def _dot(a, b):
    return jnp.dot(a, b, preferred_element_type=F32)


def _dot_nt(a, b):
    return lax.dot_general(a, b, (((1,), (1,)), ((), ())), preferred_element_type=F32)


def _dot_tn(a, b):
    return lax.dot_general(a, b, (((0,), (0,)), ((), ())), preferred_element_type=F32)


def _split2(a):
    hi = a.astype(BF16)
    return hi, (a - hi.astype(F32)).astype(BF16)


def _dot3(a, b):
    ah, al = _split2(a)
    bh, bl = _split2(b)
    return _dot(ah, bh) + (_dot(ah, bl) + _dot(al, bh))


def _dot_mask(m, x):
    hi = x.astype(BF16)
    r1 = x - hi.astype(F32)
    mid = r1.astype(BF16)
    lo = (r1 - mid.astype(F32)).astype(BF16)
    return _dot(m, hi) + (_dot(m, mid) + _dot(m, lo))


def _chunk_masks(backward):
    i = lax.broadcasted_iota(jnp.int32, (CHUNK, CHUNK), 0)
    j = lax.broadcasted_iota(jnp.int32, (CHUNK, CHUNK), 1)
    if backward:
        i, j = j, i
    tri, strict = j <= i, j < i
    return tri, strict, tri.astype(BF16), jnp.where(strict, 1.0, 0.0).astype(F32)


def _lane_pick(blk, lane):
    lanes = lax.broadcasted_iota(jnp.int32, blk.shape, 1)
    return jnp.sum(jnp.where(lanes == lane, blk, 0.0), axis=-1, keepdims=True)


def _softplus(x):
    return jnp.maximum(x, 0.0) + jnp.log1p(jnp.exp(-jnp.abs(x)))


def _silu(x):
    return x * jax.nn.sigmoid(x)


def _bidir_chunks(step, carry, n_ctx, n_all):
    def segment(start, n, carry):
        def body(t, carry):
            return step(start + t, start + n - 1 - t, carry)
        return lax.fori_loop(0, n, body, carry)
    carry = segment(0, n_ctx, carry)
    return segment(n_ctx, n_all - n_ctx, carry)


DN_HALO = 8
DN_ROWS = SEQ_ALL + 3 * DN_HALO
DN_C = 64
DN_GROUP = 256
DN_GC = DN_GROUP // (2 * DN_C)
DN_NCH = SEQ_ALL // DN_C
DN_NCC = CTX_LEN // DN_C
DN_LOCKSTEP = 4


def _dn_kernel(qc_ref, ql_ref, kc_ref, kl_ref, vc_ref, vl_ref, zc_ref, zl_ref, gc_ref, gl_ref, hp_ref,
               cwq_ref, cwk_ref, cwv_ref, nw_ref, ol_ref, oc_ref,
               xpad, qn, kn, vn, u_s, w_s, qg_s, kd_s, at_s, dec_s):
    h = pl.program_id(1)
    halo = jnp.zeros((DN_HALO, DN_HD), F32)
    lat0 = CTX_LEN + 2 * DN_HALO

    def conv_into(xc_ref, xl_ref, cw_ref, dst, post):
        xpad[0:DN_HALO] = halo
        xpad[DN_HALO:DN_HALO + CTX_LEN] = xc_ref[...]
        xpad[DN_HALO + CTX_LEN:lat0] = halo
        xpad[lat0:lat0 + SEQ] = xl_ref[...]
        xpad[lat0 + SEQ:DN_ROWS] = halo
        cw = [cw_ref[j:j + 1, :] for j in range(CONV_W)]

        def block(p0, r0):
            ext = xpad[pl.ds(p0 - DN_HALO, MIX_TB + 2 * DN_HALO), :]
            s0 = DN_HALO - CONV_PAD_L
            acc = cw[0] * ext[s0:s0 + MIX_TB]
            for j in range(1, CONV_W):
                acc = acc + cw[j] * ext[s0 + j:s0 + j + MIX_TB]
            dst[pl.ds(r0, MIX_TB), :] = post(_silu(acc))

        block(DN_HALO, 0)

        def lat_block(k, _):
            block(pl.multiple_of(lat0 + k * MIX_TB, 8), pl.multiple_of(CTX_LEN + k * MIX_TB, 8))
            return 0
        lax.fori_loop(0, SEQ // MIX_TB, lat_block, 0)

    def l2n(scale):
        return lambda t: t * (lax.rsqrt(jnp.sum(t * t, axis=-1, keepdims=True) + EPS) * scale)

    conv_into(qc_ref, ql_ref, cwq_ref, qn, l2n(DN_HD ** -0.5))
    conv_into(kc_ref, kl_ref, cwk_ref, kn, l2n(1.0))
    conv_into(vc_ref, vl_ref, cwv_ref, vn, lambda t: t)

    gi = lax.broadcasted_iota(jnp.int32, (DN_GROUP, DN_GROUP), 0)
    gj = lax.broadcasted_iota(jnp.int32, (DN_GROUP, DN_GROUP), 1)
    same = (gi // DN_C) == (gj // DN_C)
    blk16 = (gi // 16) == (gj // 16)
    bwd = ((gi // DN_C) % 2) == 1
    fwd = jnp.logical_not(bwd)
    li, lj = gi % DN_C, gj % DN_C
    tri = same & ((fwd & (lj <= li)) | (bwd & (lj >= li)))
    strict = same & ((fwd & (lj < li)) | (bwd & (lj > li)))
    eye = jnp.where(gi == gj, 1.0, 0.0).astype(F32)
    same16 = jnp.where(same, 1.0, 0.0).astype(BF16)
    m16 = jnp.where(tri, 1.0, 0.0).astype(BF16)
    nn = jnp.where(strict, 1.0, 0.0).astype(F32)

    def both_dirs(xf, xb):
        parts = []
        for c in range(DN_GC):
            parts += [xf[c * DN_C:(c + 1) * DN_C], xb[c * DN_C:(c + 1) * DN_C]]
        return jnp.concatenate(parts, axis=0)

    def prep(ts, g_ref, row0):
        grows = DN_GC * DN_C
        n = range(len(ts))
        r0 = [pl.multiple_of(t * grows, grows) for t in ts]
        gblk = [g_ref[pl.ds(pl.multiple_of(t * grows - row0, grows), grows), :] for t in ts]
        q = [both_dirs(*[qn[pl.ds(r0[i], grows), :]] * 2) for i in n]
        k = [both_dirs(*[kn[pl.ds(r0[i], grows), :]] * 2) for i in n]
        v = [both_dirs(*[vn[pl.ds(r0[i], grows), :]] * 2) for i in n]
        g_col = [both_dirs(*[hp_ref[:, d:d + 1] * _softplus(_lane_pick(gblk[i], d * DN_H + h)
                                                            + hp_ref[:, 2 + d:3 + d]) for d in range(2)]) for i in n]
        beta = [both_dirs(*[jax.nn.sigmoid(_lane_pick(gblk[i], 2 * DN_H + d * DN_H + h)) for d in range(2)])
                for i in n]
        k16 = [k[i].astype(BF16) for i in n]
        kb = [k[i] * beta[i] for i in n]
        gb = [jnp.broadcast_to(g_col[i], (DN_GROUP, DN_HD)) for i in n]
        dlt = [_dot_mask(m16, g_col[i] * nn) for i in n]
        kk = [_dot_nt(kb[i].astype(BF16), k16[i]) for i in n]
        gcum = [_dot_mask(m16, gb[i]) for i in n]
        tot = [_dot_mask(same16, gb[i]) for i in n]
        qk = [_dot_nt(q[i].astype(BF16), k16[i]) for i in n]
        decay = [jnp.where(tri, jnp.exp(jnp.where(tri, dlt[i], 0.0)), 0.0) for i in n]
        lower = [jnp.where(strict, kk[i] * decay[i], 0.0) for i in n]
        x = [-jnp.where(blk16, lower[i], 0.0) for i in n]
        tinv = [eye + x[i] for i in n]
        for _ in range(3):
            x = [_dot3(x[i], x[i]) for i in n]
            tinv = [tinv[i] + _dot3(tinv[i], x[i]) for i in n]
        size = 16
        while size < DN_C:
            inner, outer = (gi // size) == (gj // size), (gi // (2 * size)) == (gj // (2 * size))
            off = [_dot3(jnp.where(outer & jnp.logical_not(inner), lower[i], 0.0), tinv[i]) for i in n]
            tinv = [tinv[i] - _dot3(tinv[i], off[i]) for i in n]
            size *= 2
        eg = [jnp.exp(gcum[i]) for i in n]
        sol = [_dot3(tinv[i], jnp.concatenate([v[i] * beta[i], kb[i] * eg[i]], axis=1)) for i in n]
        for i in n:
            attn = (qk[i] * decay[i]).astype(BF16)
            w16 = sol[i][:, DN_HD:].astype(BF16)
            qg16 = (q[i] * eg[i]).astype(BF16)
            kd16 = (k[i] * jnp.exp(tot[i] - gcum[i])).astype(BF16)
            dec = jnp.exp(tot[i])
            for p in range(DN_GROUP // DN_C):
                d, blk = p % 2, slice(p * DN_C, (p + 1) * DN_C)
                dst = pl.ds(r0[i] + (p // 2) * DN_C, DN_C)
                u_s[d, dst, :] = sol[i][blk, :DN_HD]
                w_s[d, dst, :] = w16[blk]
                at_s[d, dst, :] = attn[blk, blk]
                qg_s[d, dst, :] = qg16[blk]
                kd_s[d, dst, :] = kd16[blk]
                dec_s[d, pl.ds(pl.multiple_of((DN_GC * ts[i] + p // 2) * 8, 8), 8), :] = dec[p * DN_C:p * DN_C + 8]
        return 0

    n_ctx_groups, n_groups = DN_NCC // DN_GC, DN_NCH // DN_GC
    assert (n_groups - n_ctx_groups) % DN_LOCKSTEP == 0
    prep(list(range(n_ctx_groups)), gc_ref, 0)
    lax.fori_loop(0, (n_groups - n_ctx_groups) // DN_LOCKSTEP,
                  lambda t, _: prep([n_ctx_groups + DN_LOCKSTEP * t + i for i in range(DN_LOCKSTEP)],
                                    gl_ref, CTX_LEN), 0)
    o_dst = (qn, kn)
    o_f, o_b = o_dst

    def step(cf, cb, carry):
        dirs = range(2)
        rows = [pl.ds(pl.multiple_of(c * DN_C, DN_C), DN_C) for c in (cf, cb)]
        s16 = [carry[d].astype(BF16) for d in dirs]
        ws = [_dot(w_s[d, rows[d], :], s16[d]) for d in dirs]
        qs = [_dot(qg_s[d, rows[d], :], s16[d]) for d in dirs]
        v_new = [(u_s[d, rows[d], :] - ws[d]).astype(BF16) for d in dirs]
        kv = [_dot_tn(kd_s[d, rows[d], :], v_new[d]) for d in dirs]
        av = [_dot(at_s[d, rows[d], :], v_new[d]) for d in dirs]
        out = []
        for d, c in zip(dirs, (cf, cb)):
            o_dst[d][rows[d], :] = qs[d] + av[d]
            g_last = dec_s[d, pl.ds(pl.multiple_of(c * 8, 8), 8), :][0:1]
            out.append(carry[d] * g_last + kv[d])
        return tuple(out)

    z0 = jnp.zeros((DN_HD, DN_HD), F32)
    _bidir_chunks(step, (z0, z0), DN_NCC, DN_NCH)

    nw = nw_ref[...]

    def finish(rows, z):
        o = o_f[rows, :] + o_b[rows, :]
        o = o * lax.rsqrt(jnp.mean(o * o, axis=-1, keepdims=True) + EPS) * nw
        return (o * _silu(z)).astype(BF16)

    oc_ref[...] = finish(pl.ds(0, CTX_LEN), zc_ref[...])

    def lat_finish(k, _):
        r0 = pl.multiple_of(k * MIX_TB, 8)
        ol_ref[pl.ds(r0, MIX_TB), :] = finish(pl.ds(pl.multiple_of(CTX_LEN + k * MIX_TB, 8), MIX_TB),
                                              zl_ref[pl.ds(r0, MIX_TB), :])
        return 0
    lax.fori_loop(0, SEQ // MIX_TB, lat_finish, 0)


def deltanet_mixer(proj, conv_w, a_log, dt_bias, norm_w):
    hp = jnp.stack([-jnp.exp(a_log[0]), -jnp.exp(a_log[1]), dt_bias[0], dt_bias[1]], axis=-1)[:, None, :]
    ctx_blk0 = N_LAT // CTX_LEN

    def seg_specs(col0):
        cb = col0 // DN_HD
        return [pl.BlockSpec((CTX_LEN, DN_HD), lambda b, h: (ctx_blk0 + b, cb + h)),
                pl.BlockSpec((SEQ, DN_HD), lambda b, h: (b, cb + h))]

    def conv_spec(k):
        return pl.BlockSpec((CONV_W, DN_HD), lambda b, h: (0, k * DN_H + h))

    in_specs = (seg_specs(COL_DN_QKVZ) + seg_specs(COL_DN_QKVZ + MIX_W) + seg_specs(COL_DN_QKVZ + 2 * MIX_W)
                + seg_specs(COL_DN_QKVZ + 3 * MIX_W)
                + [pl.BlockSpec((CTX_LEN, 128), lambda b, h: (ctx_blk0 + b, GATE_COLBLK)),
                   pl.BlockSpec((SEQ, 128), lambda b, h: (b, GATE_COLBLK)),
                   pl.BlockSpec((None, 1, 4), lambda b, h: (h, 0, 0)),
                   conv_spec(0), conv_spec(1), conv_spec(2),
                   pl.BlockSpec((1, DN_HD), lambda b, h: (0, 0))])
    seq_f32 = pltpu.VMEM((SEQ_ALL, DN_HD), F32)
    dir_bf16 = pltpu.VMEM((2, SEQ_ALL, DN_HD), BF16)
    ol, oc = pl.pallas_call(
        _dn_kernel, grid=(BATCH, DN_H), in_specs=in_specs,
        out_specs=[pl.BlockSpec((SEQ, DN_HD), lambda b, h: (b, h)),
                   pl.BlockSpec((CTX_LEN, DN_HD), lambda b, h: (b, h))],
        out_shape=[jax.ShapeDtypeStruct((N_LAT, MIX_W), BF16), jax.ShapeDtypeStruct((N_CTX, MIX_W), BF16)],
        scratch_shapes=[pltpu.VMEM((DN_ROWS, DN_HD), F32), seq_f32, seq_f32, seq_f32,
                        pltpu.VMEM((2, SEQ_ALL, DN_HD), F32), dir_bf16, dir_bf16, dir_bf16,
                        pltpu.VMEM((2, SEQ_ALL, DN_C), BF16), pltpu.VMEM((2, DN_NCH * 8, DN_HD), F32)],
        compiler_params=_cparams(("parallel", "arbitrary")), name="deltanet",
    )(*([proj] * 10), hp, conv_w, conv_w, conv_w, norm_w[None])
    return ol, oc


def _ml_kernel(qc_ref, ql_ref, kc_ref, kl_ref, vc_ref, vl_ref, oc_ref, ol_ref, gc_ref, gl_ref, hp_ref, nw_ref,
               yl_ref, yc_ref, h_f, h_b):
    hh = pl.program_id(1)
    masks = [_chunk_masks(False), _chunk_masks(True)]
    ones = jnp.ones((CHUNK, CHUNK), BF16)
    eye = jnp.where(lax.broadcasted_iota(jnp.int32, (CHUNK, CHUNK), 0)
                    == lax.broadcasted_iota(jnp.int32, (CHUNK, CHUNK), 1), 1.0, 0.0).astype(F32)

    h_dst = (h_f, h_b)

    def advance(cs, states, refs, row0):
        q_ref, k_ref, v_ref, g_ref = refs
        dirs = range(2)
        rows = [pl.ds(pl.multiple_of(c * CHUNK - row0, CHUNK), CHUNK) for c in cs]
        tri = [masks[d][0] for d in dirs]
        gblk = [g_ref[rows[d], :] for d in dirs]
        ig = [_lane_pick(gblk[d], 4 * DN_H + d * ML_H + hh) + hp_ref[:, d:d + 1] for d in dirs]
        lf = [-_softplus(-(_lane_pick(gblk[d], 4 * DN_H + 2 * ML_H + d * ML_H + hh) + hp_ref[:, 2 + d:3 + d]))
              for d in dirs]
        q = [q_ref[rows[d], :] * (ML_DQK ** -0.5) for d in dirs]
        k = [k_ref[rows[d], :] for d in dirs]
        q16 = [q[d].astype(BF16) for d in dirs]
        v16 = [v_ref[rows[d], :].astype(BF16) for d in dirs]
        qc = [_dot(q16[d], states[d][0].astype(BF16)) for d in dirs]
        bcum = [_dot_mask(masks[d][2], jnp.broadcast_to(lf[d], (CHUNK, ML_DQK)))[:, 0:1] for d in dirs]
        dsum = [_dot_mask(masks[d][2], lf[d] * masks[d][3]) + _dot_mask(ones, ig[d] * eye) for d in dirs]
        sqk = [_dot_nt(q16[d], k[d].astype(BF16)) for d in dirs]
        dmat = [jnp.where(tri[d], dsum[d], -jnp.inf) for d in dirs]
        dmax = [jnp.max(dmat[d], axis=-1, keepdims=True) for d in dirs]
        tot = [jnp.sum(lf[d], axis=0, keepdims=True) for d in dirs]
        logw = [tot[d] - bcum[d] + ig[d] for d in dirs]
        logw_max = [jnp.max(logw[d], axis=0, keepdims=True) for d in dirs]
        inter = [bcum[d] + states[d][2] for d in dirs]
        m_t = [jnp.maximum(inter[d], dmax[d]) for d in dirs]
        w_inter = [jnp.exp(inter[d] - m_t[d]) for d in dirs]
        wmat = [jnp.exp(dmat[d] - m_t[d]) * sqk[d] for d in dirs]
        m_new = [jnp.maximum(tot[d] + states[d][2], logw_max[d]) for d in dirs]
        keep = [jnp.exp(tot[d] + states[d][2] - m_new[d]) for d in dirs]
        kw = [k[d] * jnp.exp(logw[d] - m_new[d]) for d in dirs]
        wv = [_dot(wmat[d].astype(BF16), v16[d]) for d in dirs]
        kv = [_dot_tn(kw[d].astype(BF16), v16[d]) for d in dirs]
        out = []
        for d in dirs:
            c_s, n_s, _ = states[d]
            num = w_inter[d] * qc[d] + wv[d]
            den = w_inter[d] * jnp.sum(q[d] * n_s, axis=-1, keepdims=True) + jnp.sum(wmat[d], axis=-1, keepdims=True)
            h_dst[d][pl.ds(pl.multiple_of(cs[d] * CHUNK, CHUNK), CHUNK), :] = (
                num / jnp.maximum(jnp.abs(den), jnp.exp(-m_t[d])))
            out.append((keep[d] * c_s + kv[d], keep[d] * n_s + jnp.sum(kw[d], axis=0, keepdims=True), m_new[d]))
        return tuple(out)

    ctx_refs = (qc_ref, kc_ref, vc_ref, gc_ref)
    lat_refs = (ql_ref, kl_ref, vl_ref, gl_ref)

    def segment(refs, start, n, carry):
        def body(t, carry):
            return advance((start + t, start + n - 1 - t), carry, refs, start * CHUNK)
        return lax.fori_loop(0, n, body, carry)

    s0 = (jnp.zeros((ML_DQK, ML_DV), F32), jnp.zeros((1, ML_DQK), F32), jnp.zeros((1, 1), F32))
    carry = segment(ctx_refs, 0, N_CCH, (s0, s0))
    segment(lat_refs, N_CCH, N_CHUNKS - N_CCH, carry)

    nw = nw_ref[...]

    def finish(rows, o):
        hsum = h_f[rows, :] + h_b[rows, :]
        hsum = hsum * lax.rsqrt(jnp.mean(hsum * hsum, axis=-1, keepdims=True) + EPS) * nw
        return (hsum * jax.nn.sigmoid(o)).astype(BF16)

    yc_ref[...] = finish(pl.ds(0, CTX_LEN), oc_ref[...])

    def lat_finish(k, _):
        r0 = pl.multiple_of(k * MIX_TB, 8)
        yl_ref[pl.ds(r0, MIX_TB), :] = finish(pl.ds(pl.multiple_of(CTX_LEN + k * MIX_TB, 8), MIX_TB),
                                              ol_ref[pl.ds(r0, MIX_TB), :])
        return 0
    lax.fori_loop(0, SEQ // MIX_TB, lat_finish, 0)


def mlstm_mixer(proj, i_bias, f_bias, norm_w):
    hp = jnp.stack([i_bias[0], i_bias[1], f_bias[0], f_bias[1]], axis=-1)[:, None, :]
    ctx_blk0 = N_LAT // CTX_LEN

    def seg_specs(col0, w):
        cb = col0 // w
        return [pl.BlockSpec((CTX_LEN, w), lambda b, h: (ctx_blk0 + b, cb + h)),
                pl.BlockSpec((SEQ, w), lambda b, h: (b, cb + h))]

    in_specs = (seg_specs(COL_ML_Q, ML_DQK) + seg_specs(COL_ML_K, ML_DQK) + seg_specs(COL_ML_V, ML_DV)
                + seg_specs(COL_ML_O, ML_DV)
                + [pl.BlockSpec((CTX_LEN, 128), lambda b, h: (ctx_blk0 + b, GATE_COLBLK)),
                   pl.BlockSpec((SEQ, 128), lambda b, h: (b, GATE_COLBLK)),
                   pl.BlockSpec((None, 1, 4), lambda b, h: (h, 0, 0)),
                   pl.BlockSpec((1, ML_DV), lambda b, h: (0, h))])
    seq_f32 = pltpu.VMEM((SEQ_ALL, ML_DV), F32)
    yl, yc = pl.pallas_call(
        _ml_kernel, grid=(BATCH, ML_H), in_specs=in_specs,
        out_specs=[pl.BlockSpec((SEQ, ML_DV), lambda b, h: (b, h)),
                   pl.BlockSpec((CTX_LEN, ML_DV), lambda b, h: (b, h))],
        out_shape=[jax.ShapeDtypeStruct((N_LAT, MIX_W), BF16), jax.ShapeDtypeStruct((N_CTX, MIX_W), BF16)],
        scratch_shapes=[seq_f32, seq_f32],
        compiler_params=_cparams(("parallel", "arbitrary")), name="mlstm",
    )(*([proj] * 10), hp, norm_w[None])
    return yl, yc


PROJ_TM = (N_LAT + N_CTX) // 8
PROJ_TN = 512
PROJ_PAD = -(-PROJ_W // PROJ_TN) * PROJ_TN
COL_S5, COL_DN_QKVZ, COL_ML_Q, COL_ML_K, COL_ML_V, COL_ML_O, COL_LRU_Y, COL_LRU_X = (
    0, 1024, 5120, 5632, 6144, 7168, 8192, 9216)
COL_DN_A, COL_DN_B, COL_ML_I, COL_ML_F = 10240, 10256, 10272, 10280


def permute_proj_weight(w):
    o = [0] + [int(v) for v in np.cumsum(PROJ_SIZES)]
    parts = [w[:, o[0]:o[5]], w[:, o[7]:o[11]], w[:, o[13]:o[15]], w[:, o[5]:o[7]], w[:, o[11]:o[13]]]
    return jnp.pad(jnp.concatenate(parts, axis=1).astype(BF16), ((0, 0), (0, PROJ_PAD - PROJ_W)))


def hybrid_mixer(proj, s5_p, dn_p, ml_p, lru_p):
    def ctx_seg(c0):
        return proj[N_LAT:, c0:c0 + MIX_W].reshape(BATCH, CTX_LEN, MIX_W)

    def to_row_major(t):
        return grid_transpose(t.reshape(BATCH * GRID_W, GRID_W, MIX_W), BATCH, 0, MIX_W, BF16).reshape(N_LAT, MIX_W)

    grid = proj.reshape((N_LAT + N_CTX) // GRID_W, GRID_W, PROJ_PAD)
    u_l = grid_transpose(grid, BATCH, COL_S5, MIX_W, F32).reshape(BATCH, SEQ, MIX_W)
    yx_l = grid_transpose(grid, BATCH, COL_LRU_Y, 2 * MIX_W, F32).reshape(BATCH, SEQ, 2 * MIX_W)
    a_c, a_l = s5_mixer(ctx_seg(COL_S5), u_l, *s5_p)
    d_c, d_l = lru_mixer(ctx_seg(COL_LRU_Y), ctx_seg(COL_LRU_X), yx_l, *lru_p)
    flat = lambda t_l, t_c: (to_row_major(t_l), t_c.reshape(N_CTX, MIX_W).astype(BF16))
    return [flat(a_l, a_c), deltanet_mixer(proj, *dn_p), mlstm_mixer(proj, *ml_p), flat(d_l, d_c)]


def kernel(x, c, ctx, c_ctx, w_ada, b_ada, norm1, norm2, norm_f, w_in, w_out, s5_lam_re, s5_lam_im, s5_log_step,
           s5_b_re, s5_b_im, s5_c_re, s5_c_im, s5_d, s5_w_glu, s5_b_glu, dn_conv, dn_a_log, dn_dt_bias, dn_norm,
           ml_i_bias, ml_f_bias, ml_norm, lru_conv_w, lru_conv_b, lru_w_a, lru_b_a, lru_w_i, lru_b_i, lru_lam,
           router_w, router_bias, moe_w1, moe_w3, moe_w2):
    d = D_MODEL
    xa = jnp.concatenate([x.reshape(N_LAT, d), ctx.reshape(N_CTX, d)], axis=0)
    cond = jnp.concatenate([jax.nn.silu(c), jax.nn.silu(c_ctx)[None], jnp.zeros((16 - BATCH - 1, d), F32)], axis=0)
    cond = cond.astype(BF16)
    router = (router_w.T, router_bias.reshape(N_EXPERTS, 1))

    for l in range(DEPTH):
        last = l == DEPTH - 1
        mod = ada_matmul(l, cond, w_ada, b_ada[:, None, :], 512)[:BATCH + 1].reshape(BATCH + 1, 6, 1, d)
        shift1, scale1, gate1, shift2, scale2, gate2 = (mod[:, k] for k in range(6))

        h1 = norm_mod(xa, norm1[l][None], scale1, shift1, SEQ, 256)
        proj = matmul(h1, permute_proj_weight(w_in[l]), PROJ_TM, PROJ_TN)
        slabs = hybrid_mixer(
            proj,
            (s5_lam_re[l], s5_lam_im[l], s5_log_step[l], s5_b_re[l], s5_b_im[l], s5_c_re[l], s5_c_im[l], s5_d[l],
             s5_w_glu[l], s5_b_glu[l]),
            (dn_conv[l], dn_a_log[l], dn_dt_bias[l], dn_norm[l]),
            (ml_i_bias[l], ml_f_bias[l], ml_norm[l]),
            (lru_conv_w[l], lru_conv_b[l], lru_w_a[l], lru_b_a[l], lru_w_i[l], lru_b_i[l], lru_lam[l]))
        if last:
            xa, mix = xa[:N_LAT], [s[0] for s in slabs]
        else:
            mix = [jnp.concatenate(s, axis=0) for s in slabs]
        out_tm = 1024 if xa.shape[0] % 1024 == 0 else 512
        xa = matmul_gated_residual(mix, w_out[l].astype(BF16), xa, gate1, SEQ, out_tm, 512)

        h2, idx, wsel = norm_mod(xa, norm2[l][None], scale2, shift2, SEQ, 256, router=router)
        y = moe_ffn(l, h2, idx, wsel, moe_w1, moe_w3, moe_w2)
        xa = gated_add(xa, y, gate2, SEQ, 256)

    return final_norm(xa[:N_LAT], norm_f[None], 256).reshape(BATCH, SEQ, d)
```

```python
import functools
import math

import numpy as np
import jax
import jax.numpy as jnp
from jax import lax
from jax.experimental import pallas as pl
from jax.experimental.pallas import tpu as pltpu

D_MODEL = 4096
BATCH = 2
SEQ = 4096
DEPTH = 2
GRID_W = 64
CTX_LEN = 256
MIX_W = D_MODEL // 4
MIX_TOTAL = 4 * MIX_W
CHUNK = 64
CONV_W = 4
CONV_PAD_L = 2
CONV_PAD_R = 1
EPS = 1e-6
S5_CH = 16
S5_G = MIX_W // S5_CH
S5_P = 64
DN_HD = 128
DN_H = MIX_W // DN_HD
ML_H = 4
ML_DV = MIX_W // ML_H
ML_DQK = ML_DV // 2
LRU_BLOCKS = 8
LRU_BD = MIX_W // LRU_BLOCKS
LRU_C = 8.0
N_EXPERTS = 16
N_GROUPS = 4
E_PER_G = N_EXPERTS // N_GROUPS
TOP_K = 2
D_FF = D_MODEL // 4
PROJ_SIZES = (MIX_W, MIX_W, MIX_W, MIX_W, MIX_W, 2 * DN_H, 2 * DN_H, ML_H * ML_DQK, ML_H * ML_DQK, MIX_W, MIX_W,
              2 * ML_H, 2 * ML_H, MIX_W, MIX_W)
PROJ_W = sum(PROJ_SIZES)

N_LAT = BATCH * SEQ
N_CTX = BATCH * CTX_LEN
VMEM_LIMIT = 56 * 1024 * 1024

F32 = jnp.float32
BF16 = jnp.bfloat16


def _cparams(sem):
    return pltpu.CompilerParams(dimension_semantics=sem, vmem_limit_bytes=VMEM_LIMIT)


def _mm_kernel(a_ref, b_ref, o_ref):
    o_ref[...] = jnp.dot(a_ref[...], b_ref[...], preferred_element_type=F32).astype(o_ref.dtype)


def matmul(a, b, tm, tn, out_dtype=F32):
    m, k = a.shape
    n = b.shape[1]
    assert m % tm == 0 and n % tn == 0
    return pl.pallas_call(
        _mm_kernel,
        grid=(m // tm, n // tn),
        in_specs=[pl.BlockSpec((tm, k), lambda i, j: (i, 0)),
                  pl.BlockSpec((k, tn), lambda i, j: (0, j))],
        out_specs=pl.BlockSpec((tm, tn), lambda i, j: (i, j)),
        out_shape=jax.ShapeDtypeStruct((m, n), out_dtype),
        compiler_params=_cparams(("parallel", "arbitrary")),
        name="mm",
    )(a, b)


def _mm_resid_kernel(a0_ref, a1_ref, a2_ref, a3_ref, b_ref, r_ref, g_ref, o_ref):
    acc = None
    for s, a_ref in enumerate((a0_ref, a1_ref, a2_ref, a3_ref)):
        part = jnp.dot(a_ref[...], b_ref[s * MIX_W:(s + 1) * MIX_W, :], preferred_element_type=F32)
        acc = part if acc is None else acc + part
    o_ref[...] = r_ref[...] + g_ref[...] * acc


def matmul_gated_residual(a_slabs, b, resid, gate, rows_per_gate, tm, tn):
    m = a_slabs[0].shape[0]
    k, n = b.shape
    assert m % tm == 0 and n % tn == 0 and rows_per_gate % tm == 0 and k == 4 * MIX_W
    return pl.pallas_call(
        _mm_resid_kernel,
        grid=(m // tm, n // tn),
        in_specs=[pl.BlockSpec((tm, MIX_W), lambda i, j: (i, 0))] * 4 + [
                  pl.BlockSpec((k, tn), lambda i, j: (0, j)),
                  pl.BlockSpec((tm, tn), lambda i, j: (i, j)),
                  pl.BlockSpec((None, 1, tn), lambda i, j: ((i * tm) // rows_per_gate, 0, j))],
        out_specs=pl.BlockSpec((tm, tn), lambda i, j: (i, j)),
        out_shape=jax.ShapeDtypeStruct((m, n), F32),
        compiler_params=_cparams(("parallel", "arbitrary")),
        name="mm_resid",
    )(*a_slabs, b, resid, gate)


def _ada_kernel(a_ref, w_ref, b_ref, o_ref):
    o_ref[...] = jnp.dot(a_ref[...], w_ref[...].astype(BF16), preferred_element_type=F32) + b_ref[...]


def ada_matmul(layer, a, w, bias, tn):
    m, k = a.shape
    n = w.shape[2]
    return pl.pallas_call(
        _ada_kernel,
        grid=(n // tn,),
        in_specs=[pl.BlockSpec((m, k), lambda j: (0, 0)),
                  pl.BlockSpec((None, k, tn), lambda j: (layer, 0, j)),
                  pl.BlockSpec((None, 1, tn), lambda j: (layer, 0, j))],
        out_specs=pl.BlockSpec((m, tn), lambda j: (0, j)),
        out_shape=jax.ShapeDtypeStruct((m, n), F32),
        compiler_params=_cparams(("arbitrary",)),
        name="ada",
    )(a, w, bias)


def _modulated_norm(x, g, sc, sh):
    ms = jnp.mean(x * x, axis=-1, keepdims=True)
    y = x * lax.rsqrt(ms + EPS) * g
    return y * (1.0 + sc) + sh


def _norm_mod_kernel(x_ref, g_ref, sc_ref, sh_ref, o_ref):
    o_ref[...] = _modulated_norm(x_ref[...], g_ref[...], sc_ref[...], sh_ref[...]).astype(o_ref.dtype)


def _pair_top2_sum(v):
    a, b, c, d = v
    return jnp.maximum(jnp.maximum(jnp.maximum(a + b, a + c), jnp.maximum(a + d, b + c)),
                       jnp.maximum(b + d, c + d))


def _norm_mod_router_kernel(x_ref, g_ref, sc_ref, sh_ref, rwt_ref, rb_ref, o_ref, idx_ref, w_ref):
    h = _modulated_norm(x_ref[...], g_ref[...], sc_ref[...], sh_ref[...])
    o_ref[...] = h.astype(o_ref.dtype)
    logits = lax.dot_general(rwt_ref[...], h, (((1,), (1,)), ((), ())),
                             precision=lax.Precision.HIGHEST, preferred_element_type=F32)
    scores = jax.nn.sigmoid(logits)
    biased = scores + rb_ref[...]
    s = [scores[e:e + 1, :] for e in range(N_EXPERTS)]
    b = [biased[e:e + 1, :] for e in range(N_EXPERTS)]
    gs = [_pair_top2_sum(b[E_PER_G * g:E_PER_G * (g + 1)]) for g in range(N_GROUPS)]
    best, gsel = gs[0], jnp.zeros_like(gs[0], dtype=jnp.int32)
    for g in range(1, N_GROUPS):
        better = gs[g] > best
        gsel = jnp.where(better, g, gsel)
        best = jnp.where(better, gs[g], best)
    vb, vs = [], []
    for k in range(E_PER_G):
        bk, sk = b[k], s[k]
        for g in range(1, N_GROUPS):
            bk = jnp.where(gsel == g, b[E_PER_G * g + k], bk)
            sk = jnp.where(gsel == g, s[E_PER_G * g + k], sk)
        vb.append(bk)
        vs.append(sk)
    m1, i1, w1 = vb[0], jnp.zeros_like(gsel), vs[0]
    for k in range(1, E_PER_G):
        better = vb[k] > m1
        i1 = jnp.where(better, k, i1)
        w1 = jnp.where(better, vs[k], w1)
        m1 = jnp.where(better, vb[k], m1)
    m2, i2, w2, have = vb[0], jnp.zeros_like(gsel), vs[0], i1 != 0
    for k in range(1, E_PER_G):
        valid = i1 != k
        better = valid & (jnp.logical_not(have) | (vb[k] > m2))
        i2 = jnp.where(better, k, i2)
        w2 = jnp.where(better, vs[k], w2)
        m2 = jnp.where(better, vb[k], m2)
        have = have | valid
    tot = w1 + w2
    idx_ref[0:1, :] = gsel * E_PER_G + i1
    idx_ref[1:2, :] = gsel * E_PER_G + i2
    w_ref[0:1, :] = w1 / tot
    w_ref[1:2, :] = w2 / tot


def norm_mod(x, g, sc, sh, rows_per_mod, tm, router=None):
    m, d = x.shape
    assert m % tm == 0 and rows_per_mod % tm == 0
    mod_spec = pl.BlockSpec((None, 1, d), lambda i: ((i * tm) // rows_per_mod, 0, 0))
    in_specs = [pl.BlockSpec((tm, d), lambda i: (i, 0)), pl.BlockSpec((1, d), lambda i: (0, 0)), mod_spec, mod_spec]
    o_spec = pl.BlockSpec((tm, d), lambda i: (i, 0))
    o_shape = jax.ShapeDtypeStruct((m, d), BF16)
    if router is None:
        return pl.pallas_call(
            _norm_mod_kernel, grid=(m // tm,), in_specs=in_specs, out_specs=o_spec, out_shape=o_shape,
            compiler_params=_cparams(("parallel",)), name="norm_mod",
        )(x, g, sc, sh)
    rwt, rb = router
    in_specs += [pl.BlockSpec((N_EXPERTS, d), lambda i: (0, 0)), pl.BlockSpec((N_EXPERTS, 1), lambda i: (0, 0))]
    sel_spec = pl.BlockSpec((TOP_K, tm), lambda i: (0, i))
    return pl.pallas_call(
        _norm_mod_router_kernel, grid=(m // tm,), in_specs=in_specs,
        out_specs=[o_spec, sel_spec, sel_spec],
        out_shape=[o_shape, jax.ShapeDtypeStruct((TOP_K, m), jnp.int32), jax.ShapeDtypeStruct((TOP_K, m), F32)],
        compiler_params=_cparams(("parallel",)), name="norm_mod_router",
    )(x, g, sc, sh, rwt, rb)


def _gated_add_kernel(x_ref, y_ref, g_ref, o_ref):
    o_ref[...] = x_ref[...] + g_ref[...] * y_ref[...]


def gated_add(x, y, gate, rows_per_gate, tm):
    m, d = x.shape
    assert m % tm == 0 and rows_per_gate % tm == 0
    row = pl.BlockSpec((tm, d), lambda i: (i, 0))
    return pl.pallas_call(
        _gated_add_kernel, grid=(m // tm,),
        in_specs=[row, row, pl.BlockSpec((None, 1, d), lambda i: ((i * tm) // rows_per_gate, 0, 0))],
        out_specs=row, out_shape=jax.ShapeDtypeStruct((m, d), F32),
        compiler_params=_cparams(("parallel",)), name="gated_add",
    )(x, y, gate)


def _final_norm_kernel(x_ref, g_ref, o_ref):
    x = x_ref[...]
    ms = jnp.mean(x * x, axis=-1, keepdims=True)
    o_ref[...] = x * lax.rsqrt(ms + EPS) * g_ref[...]


def final_norm(x, g, tm):
    m, d = x.shape
    return pl.pallas_call(
        _final_norm_kernel, grid=(m // tm,),
        in_specs=[pl.BlockSpec((tm, d), lambda i: (i, 0)), pl.BlockSpec((1, d), lambda i: (0, 0))],
        out_specs=pl.BlockSpec((tm, d), lambda i: (i, 0)),
        out_shape=jax.ShapeDtypeStruct((m, d), F32),
        compiler_params=_cparams(("parallel",)), name="final_norm",
    )(x, g)


MOE_TM = 512
MOE_KC = 1024
MOE_FC = 256
MOE_NK = D_MODEL // MOE_KC
MOE_NF = D_FF // MOE_FC
MOE_PARTS = 2


def _moe_kernel(te_ref, nu_ref, x_ref, w1_ref, w3_ref, w2_ref, g_ref, *rest, tile0, has_prev):
    o_ref, hg_s, hu_s, act_s = rest[1:] if has_prev else rest
    t = pl.program_id(0)
    s = pl.program_id(1)
    used = t + tile0 < nu_ref[0]

    @pl.when(used & (s < MOE_NK))
    def _():
        x = x_ref[...]
        pg = jnp.dot(x, w1_ref[...].astype(BF16), preferred_element_type=F32)
        pu = jnp.dot(x, w3_ref[...].astype(BF16), preferred_element_type=F32)

        @pl.when(s == 0)
        def _():
            hg_s[...] = pg
            hu_s[...] = pu

        @pl.when(s > 0)
        def _():
            hg_s[...] += pg
            hu_s[...] += pu

        @pl.when(s == MOE_NK - 1)
        def _():
            hg = hg_s[...]
            act = ((hg * jax.nn.sigmoid(hg)) * hu_s[...] * g_ref[...]).astype(BF16)
            for f in range(MOE_NF):
                act_s[f] = act[:, f * MOE_FC:(f + 1) * MOE_FC]

    @pl.when(used & (s >= MOE_NK))
    def _():
        y = jnp.dot(act_s[s - MOE_NK], w2_ref[...].astype(BF16), preferred_element_type=F32)

        @pl.when(s == MOE_NK)
        def _():
            o_ref[...] = y

        @pl.when(s > MOE_NK)
        def _():
            o_ref[...] += y

    @pl.when(jnp.logical_not(used) & (s == 0))
    def _():
        o_ref[...] = jnp.zeros_like(o_ref)


def moe_grouped(layer, tile_expert, n_used, xs, w1, w3, w2, gs, tile0, ys_prev):
    rp, d = xs.shape
    r = gs.shape[0]

    def live(t, nu):
        return t + tile0 < nu[0]

    def kidx(t, s, nu):
        return jnp.where(live(t, nu), jnp.minimum(s, MOE_NK - 1), MOE_NK - 1)

    def fidx(t, s, nu):
        return jnp.where(live(t, nu), jnp.maximum(s - MOE_NK, 0), MOE_NF - 1)

    in_specs = [pl.BlockSpec((MOE_TM, MOE_KC), lambda t, s, te, nu: (t, kidx(t, s, nu))),
                pl.BlockSpec((None, None, MOE_KC, D_FF),
                             lambda t, s, te, nu: (layer, te[t + tile0], kidx(t, s, nu), 0)),
                pl.BlockSpec((None, None, MOE_KC, D_FF),
                             lambda t, s, te, nu: (layer, te[t + tile0], kidx(t, s, nu), 0)),
                pl.BlockSpec((None, None, MOE_FC, d), lambda t, s, te, nu: (layer, te[t + tile0], fidx(t, s, nu), 0)),
                pl.BlockSpec((MOE_TM, 1), lambda t, s, te, nu: (t + tile0, 0))]
    args = [tile_expert, n_used, xs, w1, w3, w2, gs]
    aliases = {}
    if ys_prev is not None:
        in_specs.append(pl.BlockSpec(memory_space=pl.ANY))
        aliases = {len(args): 0}
        args.append(ys_prev)
    grid_spec = pltpu.PrefetchScalarGridSpec(
        num_scalar_prefetch=2,
        grid=(rp // MOE_TM, MOE_NK + MOE_NF),
        in_specs=in_specs,
        out_specs=pl.BlockSpec((MOE_TM, d), lambda t, s, te, nu: (t + tile0, 0)),
        scratch_shapes=[pltpu.VMEM((MOE_TM, D_FF), F32), pltpu.VMEM((MOE_TM, D_FF), F32),
                        pltpu.VMEM((MOE_NF, MOE_TM, MOE_FC), BF16)],
    )
    return pl.pallas_call(
        functools.partial(_moe_kernel, tile0=tile0, has_prev=ys_prev is not None), grid_spec=grid_spec,
        out_shape=jax.ShapeDtypeStruct((r, d), F32), input_output_aliases=aliases,
        compiler_params=_cparams(("arbitrary", "arbitrary")), name="moe",
    )(*args)


CUMSUM_BLK = 256


def _onehot_cumsum(onehot):
    m, e = onehot.shape
    x = onehot.reshape(m // CUMSUM_BLK, CUMSUM_BLK, e)
    tri = jnp.tril(jnp.ones((CUMSUM_BLK, CUMSUM_BLK), F32))
    within = jnp.einsum('ij,bje->bie', tri, x, precision=lax.Precision.HIGHEST)
    total = within[:, -1, :]
    return (within + (jnp.cumsum(total, axis=0) - total)[:, None, :]).reshape(m, e)


def moe_dispatch(idx, wsel):
    n = idx.shape[1]
    e_flat = idx.reshape(-1)
    tok = jnp.tile(jnp.arange(n, dtype=jnp.int32), TOP_K)
    onehot = (e_flat[:, None] == jnp.arange(N_EXPERTS, dtype=jnp.int32)[None, :]).astype(F32)
    csum = _onehot_cumsum(onehot)
    counts = csum[-1].astype(jnp.int32)
    rank = jnp.sum(onehot * csum, axis=1).astype(jnp.int32) - 1
    padded = ((counts + MOE_TM - 1) // MOE_TM) * MOE_TM
    seg_end = jnp.cumsum(padded)
    seg_start = seg_end - padded
    dest = jnp.sum(onehot * seg_start.astype(F32)[None, :], axis=1).astype(jnp.int32) + rank
    rows = TOP_K * n + N_EXPERTS * MOE_TM
    src_tok = jnp.zeros((rows,), jnp.int32).at[dest].set(tok)
    gate = jnp.zeros((rows,), F32).at[dest].set(wsel.reshape(-1))
    n_tiles = rows // MOE_TM
    n_used = (seg_end[-1] // MOE_TM).astype(jnp.int32)
    tile_start = jnp.arange(n_tiles, dtype=jnp.int32) * MOE_TM
    te = jnp.minimum(jnp.searchsorted(seg_end, tile_start, side='right'), N_EXPERTS - 1).astype(jnp.int32)
    te = jnp.where(jnp.arange(n_tiles) < n_used, te, te[jnp.maximum(n_used - 1, 0)])
    return dest.reshape(TOP_K, n), src_tok, gate[:, None], te, n_used.reshape(1)


def _take_rows(a, rows):
    return a.at[rows].get(mode="promise_in_bounds")


def moe_ffn(layer, hb, idx, wsel, w1, w3, w2):
    dest, src_tok, gate, te, n_used = moe_dispatch(idx, wsel)
    n_tiles = src_tok.shape[0] // MOE_TM
    assert n_tiles % MOE_PARTS == 0
    part_tiles = n_tiles // MOE_PARTS
    ys = None
    for p in range(MOE_PARTS):
        rows = slice(p * part_tiles * MOE_TM, (p + 1) * part_tiles * MOE_TM)
        ys = moe_grouped(layer, te, n_used, _take_rows(hb, src_tok[rows]), w1, w3, w2, gate,
                         p * part_tiles, ys)
    return _take_rows(ys, dest[0]) + _take_rows(ys, dest[1])


SEQ_ALL = CTX_LEN + SEQ
S5_T = 16
S5_NC = SEQ_ALL // S5_T
S5_NCC = CTX_LEN // S5_T
S5_W = S5_T * S5_CH


def _gelu_tanh(x):
    return x * (0.5 * (1.0 + jnp.tanh(math.sqrt(2.0 / math.pi) * (x + 0.044715 * (x * x * x)))))


def s5_chunk_operators(lam_re, lam_im, log_step, b_re, b_im, c_re, c_im):
    hp = lax.Precision.HIGHEST
    jj = jnp.arange(S5_T)[:, None]
    ii = jnp.arange(S5_T)[None, :]
    ms, es, fs, a16 = [], [], [], []
    for d in range(2):
        dt = jnp.exp(log_step[d])[:, None]
        lr, li = lam_re[d], lam_im[d]
        er = jnp.exp(lr * dt)
        abr, abi = er * jnp.cos(li * dt), er * jnp.sin(li * dt)
        nr, ni = abr - 1.0, abi
        den = lr * lr + li * li
        fr = (nr * lr + ni * li) / den
        fi = (ni * lr - nr * li) / den
        bbr = fr[..., None] * b_re[d] - fi[..., None] * b_im[d]
        bbi = fr[..., None] * b_im[d] + fi[..., None] * b_re[d]
        k = jnp.arange(S5_T + 1, dtype=F32)[:, None, None]
        mag, ang = jnp.exp(k * (lr * dt)), k * (li * dt)
        pr, pi = mag * jnp.cos(ang), mag * jnp.sin(ang)
        pbr = pr[..., None] * bbr - pi[..., None] * bbi
        pbi = pr[..., None] * bbi + pi[..., None] * bbr
        cr, ci = c_re[d], c_im[d]
        kk = (jnp.einsum('gop,lgpc->lgoc', cr, pbr[:S5_T], precision=hp)
              - jnp.einsum('gop,lgpc->lgoc', ci, pbi[:S5_T], precision=hp))
        cpr = cr[None] * pr[:, :, None, :] - ci[None] * pi[:, :, None, :]
        cpi = -(cr[None] * pi[:, :, None, :] + ci[None] * pr[:, :, None, :])
        if d == 0:
            lag, mask = ii - jj, ii >= jj
            e_pow = S5_T - 1 - jnp.arange(S5_T)
            f_pow = jnp.arange(S5_T) + 1
        else:
            lag, mask = jj - ii, jj >= ii
            e_pow = jnp.arange(S5_T)
            f_pow = S5_T - jnp.arange(S5_T)
        mm = jnp.where(mask[:, :, None, None, None], kk[jnp.clip(lag, 0, S5_T - 1)], 0.0)
        ms.append(mm.transpose(2, 0, 4, 1, 3).reshape(S5_G, S5_W, S5_W))
        for pb in (pbr, pbi):
            es.append(pb[e_pow].transpose(1, 0, 3, 2).reshape(S5_G, S5_W, S5_P))
        for cp in (cpr, cpi):
            fs.append(cp[f_pow].transpose(1, 3, 0, 2).reshape(S5_G, S5_P, S5_W))
        a16 += [pr[S5_T][:, None, :], pi[S5_T][:, None, :]]
    return (ms[0] + ms[1]).astype(BF16), jnp.stack(es).astype(BF16), jnp.stack(fs).astype(BF16), jnp.stack(a16)


def _s5_kernel(u_ref, m_ref, e_ref, f_ref, a_ref, o_ref, e_scr, s_scr):
    u = u_ref[...]
    for k in range(4):
        e_scr[k] = jnp.dot(u, e_ref[k], preferred_element_type=F32)
    arf, aif, arb, aib = (a_ref[k] for k in range(4))

    def segment(start, n, carry):
        def body(k, carry):
            out = []
            for b in range(BATCH):
                sfr, sfi, sbr, sbi = carry[4 * b:4 * b + 4]
                rf = b * S5_NC + start + k
                rb = b * S5_NC + start + n - 1 - k
                s_scr[0, pl.ds(rf, 1), :] = sfr
                s_scr[1, pl.ds(rf, 1), :] = sfi
                s_scr[2, pl.ds(rb, 1), :] = sbr
                s_scr[3, pl.ds(rb, 1), :] = sbi
                nfr = arf * sfr - aif * sfi + e_scr[0, pl.ds(rf, 1), :]
                nfi = arf * sfi + aif * sfr + e_scr[1, pl.ds(rf, 1), :]
                nbr = arb * sbr - aib * sbi + e_scr[2, pl.ds(rb, 1), :]
                nbi = arb * sbi + aib * sbr + e_scr[3, pl.ds(rb, 1), :]
                out += [nfr, nfi, nbr, nbi]
            return tuple(out)
        return lax.fori_loop(0, n, body, carry, unroll=4)

    z = jnp.zeros((1, S5_P), F32)
    carry = segment(0, S5_NCC, (z,) * (4 * BATCH))
    segment(S5_NCC, S5_NC - S5_NCC, carry)
    y = jnp.dot(u, m_ref[...], preferred_element_type=F32)
    for k in range(4):
        y += jnp.dot(s_scr[k].astype(BF16), f_ref[k], preferred_element_type=F32)
    o_ref[...] = y


def s5_scan(u, m, e, f, a16):
    g, r, w = u.shape
    return pl.pallas_call(
        _s5_kernel, grid=(g,),
        in_specs=[pl.BlockSpec((None, r, w), lambda i: (i, 0, 0)),
                  pl.BlockSpec((None, w, w), lambda i: (i, 0, 0)),
                  pl.BlockSpec((4, None, w, S5_P), lambda i: (0, i, 0, 0)),
                  pl.BlockSpec((4, None, S5_P, w), lambda i: (0, i, 0, 0)),
                  pl.BlockSpec((4, None, 1, S5_P), lambda i: (0, i, 0, 0))],
        out_specs=pl.BlockSpec((None, r, w), lambda i: (i, 0, 0)),
        out_shape=jax.ShapeDtypeStruct((g, r, w), F32),
        scratch_shapes=[pltpu.VMEM((4, r, S5_P), F32), pltpu.VMEM((4, r, S5_P), F32)],
        compiler_params=_cparams(("parallel",)), name="s5_scan",
    )(u, m, e, f, a16)


def _s5_out_kernel(y_ref, u_ref, d_ref, w_ref, b_ref, o_ref):
    g = _gelu_tanh(y_ref[...] + d_ref[...] * u_ref[...])
    z = jnp.dot(g.astype(BF16), w_ref[...], preferred_element_type=F32) + b_ref[...]
    o_ref[...] = (g * jax.nn.sigmoid(z)).astype(o_ref.dtype)


def s5_out(y, u, d_skip, w_glu, b_glu, tm):
    m, c = y.shape
    row = pl.BlockSpec((tm, c), lambda i: (i, 0))
    vec = pl.BlockSpec((1, c), lambda i: (0, 0))
    return pl.pallas_call(
        _s5_out_kernel, grid=(m // tm,),
        in_specs=[row, row, vec, pl.BlockSpec((c, c), lambda i: (0, 0)), vec],
        out_specs=row, out_shape=jax.ShapeDtypeStruct((m, c), F32),
        compiler_params=_cparams(("parallel",)), name="s5_out",
    )(y, u, d_skip, w_glu, b_glu)


S5_LG = 128 // S5_CH


def _lane_window(lane, k):
    return (lane >= k * S5_CH) & (lane < (k + 1) * S5_CH)


def _s5_pack_kernel(x_ref, o_ref):
    lane = lax.broadcasted_iota(jnp.int32, (x_ref.shape[0], 128), 1)
    tok = [x_ref[:, t, :] for t in range(S5_T)]
    for g in range(S5_LG):
        for h in range(S5_W // 128):
            acc = jnp.zeros(lane.shape, F32)
            for k in range(S5_LG):
                shift = ((k - g) * S5_CH) % 128
                src = tok[h * S5_LG + k]
                acc = jnp.where(_lane_window(lane, k), pltpu.roll(src, shift, axis=1) if shift else src, acc)
            o_ref[g, :, h * 128:(h + 1) * 128] = acc.astype(o_ref.dtype)


def _s5_unpack_kernel(y_ref, o_ref):
    lane = lax.broadcasted_iota(jnp.int32, (y_ref.shape[1], 128), 1)
    for t in range(S5_T):
        h, k = divmod(t, S5_LG)
        acc = jnp.zeros(lane.shape, F32)
        for g in range(S5_LG):
            shift = ((g - k) * S5_CH) % 128
            src = y_ref[g, :, h * 128:(h + 1) * 128]
            acc = jnp.where(_lane_window(lane, g), pltpu.roll(src, shift, axis=1) if shift else src, acc)
        o_ref[:, t, :] = acc


def s5_pack(u):
    n = u.shape[0]
    cb = n // 2
    return pl.pallas_call(
        _s5_pack_kernel, grid=(n // cb, MIX_W // 128),
        in_specs=[pl.BlockSpec((cb, S5_T, 128), lambda i, j: (i, 0, j))],
        out_specs=pl.BlockSpec((S5_LG, cb, S5_W), lambda i, j: (j, i, 0)),
        out_shape=jax.ShapeDtypeStruct((S5_G, n, S5_W), BF16),
        compiler_params=_cparams(("parallel", "parallel")), name="s5_pack",
    )(u)


def s5_unpack(y):
    n = y.shape[1]
    cb = n // 2
    return pl.pallas_call(
        _s5_unpack_kernel, grid=(n // cb, MIX_W // 128),
        in_specs=[pl.BlockSpec((S5_LG, cb, S5_W), lambda i, j: (j, i, 0))],
        out_specs=pl.BlockSpec((cb, S5_T, 128), lambda i, j: (i, 0, j)),
        out_shape=jax.ShapeDtypeStruct((n, S5_T, MIX_W), F32),
        compiler_params=_cparams(("parallel", "parallel")), name="s5_unpack",
    )(y)


def s5_mixer(u_c, u_l, operators, d_skip, w_glu, b_glu):
    bs = u_c.shape[0]
    u_all = jnp.concatenate([u_c, u_l], axis=1)
    uc = s5_pack(u_all.reshape(bs * S5_NC, S5_T, MIX_W))
    y = s5_scan(uc, *operators)
    y = s5_unpack(y).reshape(bs * SEQ_ALL, MIX_W)
    out = s5_out(y, u_all.reshape(bs * SEQ_ALL, MIX_W), d_skip[None], w_glu.astype(BF16), b_glu[None], 544)
    out = out.reshape(bs, SEQ_ALL, MIX_W)
    return out[:, :CTX_LEN], out[:, CTX_LEN:]


LRU_CB = 256
LRU_TB = 256
LRU_HALO = 8
LRU_ROWS = SEQ_ALL + 3 * LRU_HALO


def _lru_kernel(xc_ref, xl_ref, yc_ref, yl_ref, cw_ref, cb_ref, wa_ref, ba_ref, wi_ref, bi_ref, sp_ref,
                oc_ref, ol_ref, xpad, a_f, b_f, a_b, b_b):
    halo = jnp.zeros((LRU_HALO, LRU_CB), F32)
    lat0 = CTX_LEN + 2 * LRU_HALO
    xpad[0:LRU_HALO] = halo
    xpad[LRU_HALO:LRU_HALO + CTX_LEN] = xc_ref[...]
    xpad[LRU_HALO + CTX_LEN:lat0] = halo
    xpad[lat0:lat0 + SEQ] = xl_ref[...]
    xpad[lat0 + SEQ:LRU_ROWS] = halo
    cw = [cw_ref[j:j + 1, :] for j in range(CONV_W)]
    cb = cb_ref[...]

    def gates(p0, r0):
        ext = xpad[pl.ds(p0 - LRU_HALO, LRU_TB + 2 * LRU_HALO), :]
        xb = cb
        for j in range(CONV_W):
            s0 = LRU_HALO - CONV_PAD_L + j
            xb = xb + cw[j] * ext[s0:s0 + LRU_TB]
        for k in range(LRU_CB // LRU_BD):
            lanes = slice(k * LRU_BD, (k + 1) * LRU_BD)
            xk = xb[:, lanes]
            xkb = xk.astype(BF16)
            for d, (a_s, b_s) in enumerate(((a_f, b_f), (a_b, b_b))):
                r = jax.nn.sigmoid(jnp.dot(xkb, wa_ref[d, k], preferred_element_type=F32) + ba_ref[d][:, lanes])
                i = jax.nn.sigmoid(jnp.dot(xkb, wi_ref[d, k], preferred_element_type=F32) + bi_ref[d][:, lanes])
                log_a = -LRU_C * r * sp_ref[d][:, lanes]
                t = jnp.tanh(log_a)
                a_s[pl.ds(r0, LRU_TB), lanes] = jnp.exp(log_a)
                b_s[pl.ds(r0, LRU_TB), lanes] = jnp.sqrt(-2.0 * t / (1.0 - t)) * (i * xk)

    gates(LRU_HALO, 0)

    def lat_gates(k, _):
        gates(pl.multiple_of(lat0 + k * LRU_TB, 8), pl.multiple_of(CTX_LEN + k * LRU_TB, 8))
        return 0
    lax.fori_loop(0, SEQ // LRU_TB, lat_gates, 0)

    def segment(start, n, carry):
        def body(t, carry):
            hf, hb = carry
            rf = start + t
            rb = start + n - 1 - t
            hf = a_f[pl.ds(rf, 1), :] * hf + b_f[pl.ds(rf, 1), :]
            b_f[pl.ds(rf, 1), :] = hf
            hb = a_b[pl.ds(rb, 1), :] * hb + b_b[pl.ds(rb, 1), :]
            b_b[pl.ds(rb, 1), :] = hb
            return hf, hb
        return lax.fori_loop(0, n, body, carry, unroll=8)

    z = jnp.zeros((1, LRU_CB), F32)
    carry = segment(0, CTX_LEN, (z, z))
    segment(CTX_LEN, SEQ, carry)

    oc_ref[...] = (_gelu_tanh(yc_ref[...]) * (b_f[0:CTX_LEN] + b_b[0:CTX_LEN])).astype(oc_ref.dtype)

    def lat_out(k, _):
        r0 = pl.multiple_of(k * LRU_TB, 8)
        rows = pl.ds(pl.multiple_of(CTX_LEN + k * LRU_TB, 8), LRU_TB)
        h = b_f[rows, :] + b_b[rows, :]
        ol_ref[pl.ds(r0, LRU_TB), :] = (_gelu_tanh(yl_ref[pl.ds(r0, LRU_TB), :]) * h).astype(ol_ref.dtype)
        return 0
    lax.fori_loop(0, SEQ // LRU_TB, lat_out, 0)


def lru_mixer(y_c, x_c, yx_l, conv_w, conv_b, w_a, b_a, w_i, b_i, lam):
    bs = x_c.shape[0]
    nb = LRU_CB // LRU_BD
    sp = jax.nn.softplus(-lam)[:, None, :]
    ctx_spec = pl.BlockSpec((None, CTX_LEN, LRU_CB), lambda b, c: (b, 0, c))
    lat_spec = pl.BlockSpec((None, SEQ, LRU_CB), lambda b, c: (b, 0, c))
    lat_x_spec = pl.BlockSpec((None, SEQ, LRU_CB), lambda b, c: (b, 0, MIX_W // LRU_CB + c))
    vec2 = pl.BlockSpec((2, 1, LRU_CB), lambda b, c: (0, 0, c))
    wspec = pl.BlockSpec((2, nb, LRU_BD, LRU_BD), lambda b, c: (0, c, 0, 0))
    scan_buf = pltpu.VMEM((SEQ_ALL, LRU_CB), F32)
    return pl.pallas_call(
        _lru_kernel, grid=(bs, MIX_W // LRU_CB),
        in_specs=[ctx_spec, lat_x_spec, ctx_spec, lat_spec,
                  pl.BlockSpec((CONV_W, LRU_CB), lambda b, c: (0, c)),
                  pl.BlockSpec((1, LRU_CB), lambda b, c: (0, c)),
                  wspec, vec2, wspec, vec2, vec2],
        out_specs=[ctx_spec, lat_spec],
        out_shape=[jax.ShapeDtypeStruct((bs, CTX_LEN, MIX_W), F32), jax.ShapeDtypeStruct((bs, SEQ, MIX_W), F32)],
        scratch_shapes=[pltpu.VMEM((LRU_ROWS, LRU_CB), F32), scan_buf, scan_buf, scan_buf, scan_buf],
        compiler_params=_cparams(("parallel", "parallel")), name="lru",
    )(x_c, yx_l, y_c, yx_l, conv_w, conv_b[None], w_a.astype(BF16), b_a[:, None, :], w_i.astype(BF16),
      b_i[:, None, :], sp)


GT_W = 8
GT_C = 1024


def _grid_transpose_kernel(x_ref, o_ref):
    for j in range(GT_W):
        o_ref[j] = x_ref[:, j, :].astype(o_ref.dtype)


def grid_transpose(src, batch, col0, width, out_dtype):
    assert col0 % GT_C == 0 and width % GT_C == 0
    return pl.pallas_call(
        _grid_transpose_kernel, grid=(batch, GRID_W // GT_W, width // GT_C),
        in_specs=[pl.BlockSpec((GRID_W, GT_W, GT_C), lambda b, j, c: (b, j, col0 // GT_C + c))],
        out_specs=pl.BlockSpec((None, GT_W, GRID_W, GT_C), lambda b, j, c: (b, j, 0, c)),
        out_shape=jax.ShapeDtypeStruct((batch, GRID_W, GRID_W, width), out_dtype),
        compiler_params=_cparams(("parallel", "parallel", "parallel")), name="grid_transpose",
    )(src)


N_CHUNKS = SEQ_ALL // CHUNK
N_CCH = CTX_LEN // CHUNK
GATE_COLBLK = 80
MIX_TB = 256


def _dot(a, b):
    return jnp.dot(a, b, preferred_element_type=F32)


def _dot_nt(a, b):
    return lax.dot_general(a, b, (((1,), (1,)), ((), ())), preferred_element_type=F32)


def _dot_tn(a, b):
    return lax.dot_general(a, b, (((0,), (0,)), ((), ())), preferred_element_type=F32)


def _split2(a):
    hi = a.astype(BF16)
    return hi, (a - hi.astype(F32)).astype(BF16)


def _dot3(a, b):
    ah, al = _split2(a)
    bh, bl = _split2(b)
    return _dot(ah, bh) + (_dot(ah, bl) + _dot(al, bh))


def _dot_mask(m, x):
    hi = x.astype(BF16)
    r1 = x - hi.astype(F32)
    mid = r1.astype(BF16)
    lo = (r1 - mid.astype(F32)).astype(BF16)
    return _dot(m, hi) + (_dot(m, mid) + _dot(m, lo))


def _chunk_masks(backward):
    i = lax.broadcasted_iota(jnp.int32, (CHUNK, CHUNK), 0)
    j = lax.broadcasted_iota(jnp.int32, (CHUNK, CHUNK), 1)
    if backward:
        i, j = j, i
    tri, strict = j <= i, j < i
    return tri, strict, tri.astype(BF16), jnp.where(strict, 1.0, 0.0).astype(F32)


def _lane_pick(blk, lane):
    lanes = lax.broadcasted_iota(jnp.int32, blk.shape, 1)
    return jnp.sum(jnp.where(lanes == lane, blk, 0.0), axis=-1, keepdims=True)


def _softplus(x):
    return jnp.maximum(x, 0.0) + jnp.log1p(jnp.exp(-jnp.abs(x)))


def _silu(x):
    return x * jax.nn.sigmoid(x)


def _bidir_chunks(step, carry, n_ctx, n_all):
    def segment(start, n, carry):
        def body(t, carry):
            return step(start + t, start + n - 1 - t, carry)
        return lax.fori_loop(0, n, body, carry)
    carry = segment(0, n_ctx, carry)
    return segment(n_ctx, n_all - n_ctx, carry)


DN_HALO = 8
DN_ROWS = SEQ_ALL + 3 * DN_HALO
DN_C = 64
DN_GROUP = 256
DN_GC = DN_GROUP // (2 * DN_C)
DN_NCH = SEQ_ALL // DN_C
DN_NCC = CTX_LEN // DN_C
DN_LOCKSTEP = 4


def _dn_kernel(qc_ref, ql_ref, kc_ref, kl_ref, vc_ref, vl_ref, zc_ref, zl_ref, gc_ref, gl_ref, hp_ref,
               cwq_ref, cwk_ref, cwv_ref, nw_ref, ol_ref, oc_ref,
               xpad, qn, kn, vn, u_s, w_s, qg_s, kd_s, at_s, dec_s):
    h = pl.program_id(1)
    halo = jnp.zeros((DN_HALO, DN_HD), F32)
    lat0 = CTX_LEN + 2 * DN_HALO

    def conv_into(xc_ref, xl_ref, cw_ref, dst, post):
        xpad[0:DN_HALO] = halo
        xpad[DN_HALO:DN_HALO + CTX_LEN] = xc_ref[...]
        xpad[DN_HALO + CTX_LEN:lat0] = halo
        xpad[lat0:lat0 + SEQ] = xl_ref[...]
        xpad[lat0 + SEQ:DN_ROWS] = halo
        cw = [cw_ref[j:j + 1, :] for j in range(CONV_W)]

        def block(p0, r0):
            ext = xpad[pl.ds(p0 - DN_HALO, MIX_TB + 2 * DN_HALO), :]
            s0 = DN_HALO - CONV_PAD_L
            acc = cw[0] * ext[s0:s0 + MIX_TB]
            for j in range(1, CONV_W):
                acc = acc + cw[j] * ext[s0 + j:s0 + j + MIX_TB]
            dst[pl.ds(r0, MIX_TB), :] = post(_silu(acc))

        block(DN_HALO, 0)

        def lat_block(k, _):
            block(pl.multiple_of(lat0 + k * MIX_TB, 8), pl.multiple_of(CTX_LEN + k * MIX_TB, 8))
            return 0
        lax.fori_loop(0, SEQ // MIX_TB, lat_block, 0)

    def l2n(scale):
        return lambda t: t * (lax.rsqrt(jnp.sum(t * t, axis=-1, keepdims=True) + EPS) * scale)

    conv_into(qc_ref, ql_ref, cwq_ref, qn, l2n(DN_HD ** -0.5))
    conv_into(kc_ref, kl_ref, cwk_ref, kn, l2n(1.0))
    conv_into(vc_ref, vl_ref, cwv_ref, vn, lambda t: t)

    gi = lax.broadcasted_iota(jnp.int32, (DN_GROUP, DN_GROUP), 0)
    gj = lax.broadcasted_iota(jnp.int32, (DN_GROUP, DN_GROUP), 1)
    same = (gi // DN_C) == (gj // DN_C)
    blk16 = (gi // 16) == (gj // 16)
    bwd = ((gi // DN_C) % 2) == 1
    fwd = jnp.logical_not(bwd)
    li, lj = gi % DN_C, gj % DN_C
    tri = same & ((fwd & (lj <= li)) | (bwd & (lj >= li)))
    strict = same & ((fwd & (lj < li)) | (bwd & (lj > li)))
    eye = jnp.where(gi == gj, 1.0, 0.0).astype(F32)
    same16 = jnp.where(same, 1.0, 0.0).astype(BF16)
    m16 = jnp.where(tri, 1.0, 0.0).astype(BF16)
    nn = jnp.where(strict, 1.0, 0.0).astype(F32)

    def both_dirs(xf, xb):
        parts = []
        for c in range(DN_GC):
            parts += [xf[c * DN_C:(c + 1) * DN_C], xb[c * DN_C:(c + 1) * DN_C]]
        return jnp.concatenate(parts, axis=0)

    def prep(ts, g_ref, row0):
        grows = DN_GC * DN_C
        n = range(len(ts))
        r0 = [pl.multiple_of(t * grows, grows) for t in ts]
        gblk = [g_ref[pl.ds(pl.multiple_of(t * grows - row0, grows), grows), :] for t in ts]
        q = [both_dirs(*[qn[pl.ds(r0[i], grows), :]] * 2) for i in n]
        k = [both_dirs(*[kn[pl.ds(r0[i], grows), :]] * 2) for i in n]
        v = [both_dirs(*[vn[pl.ds(r0[i], grows), :]] * 2) for i in n]
        g_col = [both_dirs(*[hp_ref[:, d:d + 1] * _softplus(_lane_pick(gblk[i], d * DN_H + h)
                                                            + hp_ref[:, 2 + d:3 + d]) for d in range(2)]) for i in n]
        beta = [both_dirs(*[jax.nn.sigmoid(_lane_pick(gblk[i], 2 * DN_H + d * DN_H + h)) for d in range(2)])
                for i in n]
        k16 = [k[i].astype(BF16) for i in n]
        kb = [k[i] * beta[i] for i in n]
        gb = [jnp.broadcast_to(g_col[i], (DN_GROUP, DN_HD)) for i in n]
        dlt = [_dot_mask(m16, g_col[i] * nn) for i in n]
        kk = [_dot_nt(kb[i].astype(BF16), k16[i]) for i in n]
        gcum = [_dot_mask(m16, gb[i]) for i in n]
        tot = [_dot_mask(same16, gb[i]) for i in n]
        qk = [_dot_nt(q[i].astype(BF16), k16[i]) for i in n]
        decay = [jnp.where(tri, jnp.exp(jnp.where(tri, dlt[i], 0.0)), 0.0) for i in n]
        lower = [jnp.where(strict, kk[i] * decay[i], 0.0) for i in n]
        x = [-jnp.where(blk16, lower[i], 0.0) for i in n]
        tinv = [eye + x[i] for i in n]
        for _ in range(3):
            x = [_dot3(x[i], x[i]) for i in n]
            tinv = [tinv[i] + _dot3(tinv[i], x[i]) for i in n]
        size = 16
        while size < DN_C:
            inner, outer = (gi // size) == (gj // size), (gi // (2 * size)) == (gj // (2 * size))
            off = [_dot3(jnp.where(outer & jnp.logical_not(inner), lower[i], 0.0), tinv[i]) for i in n]
            tinv = [tinv[i] - _dot3(tinv[i], off[i]) for i in n]
            size *= 2
        eg = [jnp.exp(gcum[i]) for i in n]
        sol = [_dot3(tinv[i], jnp.concatenate([v[i] * beta[i], kb[i] * eg[i]], axis=1)) for i in n]
        for i in n:
            attn = (qk[i] * decay[i]).astype(BF16)
            w16 = sol[i][:, DN_HD:].astype(BF16)
            qg16 = (q[i] * eg[i]).astype(BF16)
            kd16 = (k[i] * jnp.exp(tot[i] - gcum[i])).astype(BF16)
            dec = jnp.exp(tot[i])
            for p in range(DN_GROUP // DN_C):
                d, blk = p % 2, slice(p * DN_C, (p + 1) * DN_C)
                dst = pl.ds(r0[i] + (p // 2) * DN_C, DN_C)
                u_s[d, dst, :] = sol[i][blk, :DN_HD]
                w_s[d, dst, :] = w16[blk]
                at_s[d, dst, :] = attn[blk, blk]
                qg_s[d, dst, :] = qg16[blk]
                kd_s[d, dst, :] = kd16[blk]
                dec_s[d, pl.ds(pl.multiple_of((DN_GC * ts[i] + p // 2) * 8, 8), 8), :] = dec[p * DN_C:p * DN_C + 8]
        return 0

    n_ctx_groups, n_groups = DN_NCC // DN_GC, DN_NCH // DN_GC
    assert (n_groups - n_ctx_groups) % DN_LOCKSTEP == 0
    prep(list(range(n_ctx_groups)), gc_ref, 0)
    lax.fori_loop(0, (n_groups - n_ctx_groups) // DN_LOCKSTEP,
                  lambda t, _: prep([n_ctx_groups + DN_LOCKSTEP * t + i for i in range(DN_LOCKSTEP)],
                                    gl_ref, CTX_LEN), 0)
    o_dst = (qn, kn)
    o_f, o_b = o_dst

    def step(cf, cb, carry):
        dirs = range(2)
        rows = [pl.ds(pl.multiple_of(c * DN_C, DN_C), DN_C) for c in (cf, cb)]
        s16 = [carry[d].astype(BF16) for d in dirs]
        ws = [_dot(w_s[d, rows[d], :], s16[d]) for d in dirs]
        qs = [_dot(qg_s[d, rows[d], :], s16[d]) for d in dirs]
        v_new = [(u_s[d, rows[d], :] - ws[d]).astype(BF16) for d in dirs]
        kv = [_dot_tn(kd_s[d, rows[d], :], v_new[d]) for d in dirs]
        av = [_dot(at_s[d, rows[d], :], v_new[d]) for d in dirs]
        out = []
        for d, c in zip(dirs, (cf, cb)):
            o_dst[d][rows[d], :] = qs[d] + av[d]
            g_last = dec_s[d, pl.ds(pl.multiple_of(c * 8, 8), 8), :][0:1]
            out.append(carry[d] * g_last + kv[d])
        return tuple(out)

    z0 = jnp.zeros((DN_HD, DN_HD), F32)
    _bidir_chunks(step, (z0, z0), DN_NCC, DN_NCH)

    nw = nw_ref[...]

    def finish(rows, z):
        o = o_f[rows, :] + o_b[rows, :]
        o = o * lax.rsqrt(jnp.mean(o * o, axis=-1, keepdims=True) + EPS) * nw
        return (o * _silu(z)).astype(BF16)

    oc_ref[...] = finish(pl.ds(0, CTX_LEN), zc_ref[...])

    def lat_finish(k, _):
        r0 = pl.multiple_of(k * MIX_TB, 8)
        ol_ref[pl.ds(r0, MIX_TB), :] = finish(pl.ds(pl.multiple_of(CTX_LEN + k * MIX_TB, 8), MIX_TB),
                                              zl_ref[pl.ds(r0, MIX_TB), :])
        return 0
    lax.fori_loop(0, SEQ // MIX_TB, lat_finish, 0)


def deltanet_mixer(proj, conv_w, a_log, dt_bias, norm_w):
    hp = jnp.stack([-jnp.exp(a_log[0]), -jnp.exp(a_log[1]), dt_bias[0], dt_bias[1]], axis=-1)[:, None, :]
    ctx_blk0 = N_LAT // CTX_LEN

    def seg_specs(col0):
        cb = col0 // DN_HD
        return [pl.BlockSpec((CTX_LEN, DN_HD), lambda b, h: (ctx_blk0 + b, cb + h)),
                pl.BlockSpec((SEQ, DN_HD), lambda b, h: (b, cb + h))]

    def conv_spec(k):
        return pl.BlockSpec((CONV_W, DN_HD), lambda b, h: (0, k * DN_H + h))

    in_specs = (seg_specs(COL_DN_QKVZ) + seg_specs(COL_DN_QKVZ + MIX_W) + seg_specs(COL_DN_QKVZ + 2 * MIX_W)
                + seg_specs(COL_DN_QKVZ + 3 * MIX_W)
                + [pl.BlockSpec((CTX_LEN, 128), lambda b, h: (ctx_blk0 + b, GATE_COLBLK)),
                   pl.BlockSpec((SEQ, 128), lambda b, h: (b, GATE_COLBLK)),
                   pl.BlockSpec((None, 1, 4), lambda b, h: (h, 0, 0)),
                   conv_spec(0), conv_spec(1), conv_spec(2),
                   pl.BlockSpec((1, DN_HD), lambda b, h: (0, 0))])
    seq_f32 = pltpu.VMEM((SEQ_ALL, DN_HD), F32)
    dir_bf16 = pltpu.VMEM((2, SEQ_ALL, DN_HD), BF16)
    ol, oc = pl.pallas_call(
        _dn_kernel, grid=(BATCH, DN_H), in_specs=in_specs,
        out_specs=[pl.BlockSpec((SEQ, DN_HD), lambda b, h: (b, h)),
                   pl.BlockSpec((CTX_LEN, DN_HD), lambda b, h: (b, h))],
        out_shape=[jax.ShapeDtypeStruct((N_LAT, MIX_W), BF16), jax.ShapeDtypeStruct((N_CTX, MIX_W), BF16)],
        scratch_shapes=[pltpu.VMEM((DN_ROWS, DN_HD), F32), seq_f32, seq_f32, seq_f32,
                        pltpu.VMEM((2, SEQ_ALL, DN_HD), F32), dir_bf16, dir_bf16, dir_bf16,
                        pltpu.VMEM((2, SEQ_ALL, DN_C), BF16), pltpu.VMEM((2, DN_NCH * 8, DN_HD), F32)],
        compiler_params=_cparams(("parallel", "arbitrary")), name="deltanet",
    )(*([proj] * 10), hp, conv_w, conv_w, conv_w, norm_w[None])
    return ol, oc


def _ml_kernel(qc_ref, ql_ref, kc_ref, kl_ref, vc_ref, vl_ref, oc_ref, ol_ref, gc_ref, gl_ref, hp_ref, nw_ref,
               yl_ref, yc_ref, h_f, h_b):
    hh = pl.program_id(1)
    masks = [_chunk_masks(False), _chunk_masks(True)]
    ones = jnp.ones((CHUNK, CHUNK), BF16)
    eye = jnp.where(lax.broadcasted_iota(jnp.int32, (CHUNK, CHUNK), 0)
                    == lax.broadcasted_iota(jnp.int32, (CHUNK, CHUNK), 1), 1.0, 0.0).astype(F32)

    h_dst = (h_f, h_b)

    def advance(cs, states, refs, row0):
        q_ref, k_ref, v_ref, g_ref = refs
        dirs = range(2)
        rows = [pl.ds(pl.multiple_of(c * CHUNK - row0, CHUNK), CHUNK) for c in cs]
        tri = [masks[d][0] for d in dirs]
        gblk = [g_ref[rows[d], :] for d in dirs]
        ig = [_lane_pick(gblk[d], 4 * DN_H + d * ML_H + hh) + hp_ref[:, d:d + 1] for d in dirs]
        lf = [-_softplus(-(_lane_pick(gblk[d], 4 * DN_H + 2 * ML_H + d * ML_H + hh) + hp_ref[:, 2 + d:3 + d]))
              for d in dirs]
        q = [q_ref[rows[d], :] * (ML_DQK ** -0.5) for d in dirs]
        k = [k_ref[rows[d], :] for d in dirs]
        q16 = [q[d].astype(BF16) for d in dirs]
        v16 = [v_ref[rows[d], :].astype(BF16) for d in dirs]
        qc = [_dot(q16[d], states[d][0].astype(BF16)) for d in dirs]
        bcum = [_dot_mask(masks[d][2], jnp.broadcast_to(lf[d], (CHUNK, ML_DQK)))[:, 0:1] for d in dirs]
        dsum = [_dot_mask(masks[d][2], lf[d] * masks[d][3]) + _dot_mask(ones, ig[d] * eye) for d in dirs]
        sqk = [_dot_nt(q16[d], k[d].astype(BF16)) for d in dirs]
        dmat = [jnp.where(tri[d], dsum[d], -jnp.inf) for d in dirs]
        dmax = [jnp.max(dmat[d], axis=-1, keepdims=True) for d in dirs]
        tot = [jnp.sum(lf[d], axis=0, keepdims=True) for d in dirs]
        logw = [tot[d] - bcum[d] + ig[d] for d in dirs]
        logw_max = [jnp.max(logw[d], axis=0, keepdims=True) for d in dirs]
        inter = [bcum[d] + states[d][2] for d in dirs]
        m_t = [jnp.maximum(inter[d], dmax[d]) for d in dirs]
        w_inter = [jnp.exp(inter[d] - m_t[d]) for d in dirs]
        wmat = [jnp.exp(dmat[d] - m_t[d]) * sqk[d] for d in dirs]
        m_new = [jnp.maximum(tot[d] + states[d][2], logw_max[d]) for d in dirs]
        keep = [jnp.exp(tot[d] + states[d][2] - m_new[d]) for d in dirs]
        kw = [k[d] * jnp.exp(logw[d] - m_new[d]) for d in dirs]
        wv = [_dot(wmat[d].astype(BF16), v16[d]) for d in dirs]
        kv = [_dot_tn(kw[d].astype(BF16), v16[d]) for d in dirs]
        out = []
        for d in dirs:
            c_s, n_s, _ = states[d]
            num = w_inter[d] * qc[d] + wv[d]
            den = w_inter[d] * jnp.sum(q[d] * n_s, axis=-1, keepdims=True) + jnp.sum(wmat[d], axis=-1, keepdims=True)
            h_dst[d][pl.ds(pl.multiple_of(cs[d] * CHUNK, CHUNK), CHUNK), :] = (
                num / jnp.maximum(jnp.abs(den), jnp.exp(-m_t[d])))
            out.append((keep[d] * c_s + kv[d], keep[d] * n_s + jnp.sum(kw[d], axis=0, keepdims=True), m_new[d]))
        return tuple(out)

    ctx_refs = (qc_ref, kc_ref, vc_ref, gc_ref)
    lat_refs = (ql_ref, kl_ref, vl_ref, gl_ref)

    def segment(refs, start, n, carry):
        def body(t, carry):
            return advance((start + t, start + n - 1 - t), carry, refs, start * CHUNK)
        return lax.fori_loop(0, n, body, carry)

    s0 = (jnp.zeros((ML_DQK, ML_DV), F32), jnp.zeros((1, ML_DQK), F32), jnp.zeros((1, 1), F32))
    carry = segment(ctx_refs, 0, N_CCH, (s0, s0))
    segment(lat_refs, N_CCH, N_CHUNKS - N_CCH, carry)

    nw = nw_ref[...]

    def finish(rows, o):
        hsum = h_f[rows, :] + h_b[rows, :]
        hsum = hsum * lax.rsqrt(jnp.mean(hsum * hsum, axis=-1, keepdims=True) + EPS) * nw
        return (hsum * jax.nn.sigmoid(o)).astype(BF16)

    yc_ref[...] = finish(pl.ds(0, CTX_LEN), oc_ref[...])

    def lat_finish(k, _):
        r0 = pl.multiple_of(k * MIX_TB, 8)
        yl_ref[pl.ds(r0, MIX_TB), :] = finish(pl.ds(pl.multiple_of(CTX_LEN + k * MIX_TB, 8), MIX_TB),
                                              ol_ref[pl.ds(r0, MIX_TB), :])
        return 0
    lax.fori_loop(0, SEQ // MIX_TB, lat_finish, 0)


def mlstm_mixer(proj, i_bias, f_bias, norm_w):
    hp = jnp.stack([i_bias[0], i_bias[1], f_bias[0], f_bias[1]], axis=-1)[:, None, :]
    ctx_blk0 = N_LAT // CTX_LEN

    def seg_specs(col0, w):
        cb = col0 // w
        return [pl.BlockSpec((CTX_LEN, w), lambda b, h: (ctx_blk0 + b, cb + h)),
                pl.BlockSpec((SEQ, w), lambda b, h: (b, cb + h))]

    in_specs = (seg_specs(COL_ML_Q, ML_DQK) + seg_specs(COL_ML_K, ML_DQK) + seg_specs(COL_ML_V, ML_DV)
                + seg_specs(COL_ML_O, ML_DV)
                + [pl.BlockSpec((CTX_LEN, 128), lambda b, h: (ctx_blk0 + b, GATE_COLBLK)),
                   pl.BlockSpec((SEQ, 128), lambda b, h: (b, GATE_COLBLK)),
                   pl.BlockSpec((None, 1, 4), lambda b, h: (h, 0, 0)),
                   pl.BlockSpec((1, ML_DV), lambda b, h: (0, h))])
    seq_f32 = pltpu.VMEM((SEQ_ALL, ML_DV), F32)
    yl, yc = pl.pallas_call(
        _ml_kernel, grid=(BATCH, ML_H), in_specs=in_specs,
        out_specs=[pl.BlockSpec((SEQ, ML_DV), lambda b, h: (b, h)),
                   pl.BlockSpec((CTX_LEN, ML_DV), lambda b, h: (b, h))],
        out_shape=[jax.ShapeDtypeStruct((N_LAT, MIX_W), BF16), jax.ShapeDtypeStruct((N_CTX, MIX_W), BF16)],
        scratch_shapes=[seq_f32, seq_f32],
        compiler_params=_cparams(("parallel", "arbitrary")), name="mlstm",
    )(*([proj] * 10), hp, norm_w[None])
    return yl, yc


PROJ_TM = (N_LAT + N_CTX) // 8
PROJ_TN = 512
PROJ_PAD = -(-PROJ_W // PROJ_TN) * PROJ_TN
COL_S5, COL_DN_QKVZ, COL_ML_Q, COL_ML_K, COL_ML_V, COL_ML_O, COL_LRU_Y, COL_LRU_X = (
    0, 1024, 5120, 5632, 6144, 7168, 8192, 9216)
COL_DN_A, COL_DN_B, COL_ML_I, COL_ML_F = 10240, 10256, 10272, 10280


def permute_proj_weight(w):
    o = [0] + [int(v) for v in np.cumsum(PROJ_SIZES)]
    parts = [w[:, o[0]:o[5]], w[:, o[7]:o[11]], w[:, o[13]:o[15]], w[:, o[5]:o[7]], w[:, o[11]:o[13]]]
    return jnp.pad(jnp.concatenate(parts, axis=1).astype(BF16), ((0, 0), (0, PROJ_PAD - PROJ_W)))


def hybrid_mixer(proj, s5_p, dn_p, ml_p, lru_p):
    def ctx_seg(c0):
        return proj[N_LAT:, c0:c0 + MIX_W].reshape(BATCH, CTX_LEN, MIX_W)

    def to_row_major(t):
        return grid_transpose(t.reshape(BATCH * GRID_W, GRID_W, MIX_W), BATCH, 0, MIX_W, BF16).reshape(N_LAT, MIX_W)

    grid = proj.reshape((N_LAT + N_CTX) // GRID_W, GRID_W, PROJ_PAD)
    u_l = grid_transpose(grid, BATCH, COL_S5, MIX_W, F32).reshape(BATCH, SEQ, MIX_W)
    yx_l = grid_transpose(grid, BATCH, COL_LRU_Y, 2 * MIX_W, F32).reshape(BATCH, SEQ, 2 * MIX_W)
    a_c, a_l = s5_mixer(ctx_seg(COL_S5), u_l, *s5_p)
    d_c, d_l = lru_mixer(ctx_seg(COL_LRU_Y), ctx_seg(COL_LRU_X), yx_l, *lru_p)
    flat = lambda t_l, t_c: (to_row_major(t_l), t_c.reshape(N_CTX, MIX_W).astype(BF16))
    return [flat(a_l, a_c), deltanet_mixer(proj, *dn_p), mlstm_mixer(proj, *ml_p), flat(d_l, d_c)]


def kernel(x, c, ctx, c_ctx, w_ada, b_ada, norm1, norm2, norm_f, w_in, w_out, s5_lam_re, s5_lam_im, s5_log_step,
           s5_b_re, s5_b_im, s5_c_re, s5_c_im, s5_d, s5_w_glu, s5_b_glu, dn_conv, dn_a_log, dn_dt_bias, dn_norm,
           ml_i_bias, ml_f_bias, ml_norm, lru_conv_w, lru_conv_b, lru_w_a, lru_b_a, lru_w_i, lru_b_i, lru_lam,
           router_w, router_bias, moe_w1, moe_w3, moe_w2):
    d = D_MODEL
    xa = jnp.concatenate([x.reshape(N_LAT, d), ctx.reshape(N_CTX, d)], axis=0)
    cond = jnp.concatenate([jax.nn.silu(c), jax.nn.silu(c_ctx)[None], jnp.zeros((16 - BATCH - 1, d), F32)], axis=0)
    cond = cond.astype(BF16)
    router = (router_w.T, router_bias.reshape(N_EXPERTS, 1))

    s5_ops = jax.vmap(s5_chunk_operators)(s5_lam_re, s5_lam_im, s5_log_step, s5_b_re, s5_b_im, s5_c_re, s5_c_im)

    for l in range(DEPTH):
        last = l == DEPTH - 1
        mod = ada_matmul(l, cond, w_ada, b_ada[:, None, :], 512)[:BATCH + 1].reshape(BATCH + 1, 6, 1, d)
        shift1, scale1, gate1, shift2, scale2, gate2 = (mod[:, k] for k in range(6))

        h1 = norm_mod(xa, norm1[l][None], scale1, shift1, SEQ, 256)
        proj = matmul(h1, permute_proj_weight(w_in[l]), PROJ_TM, PROJ_TN)
        slabs = hybrid_mixer(
            proj,
            (tuple(op[l] for op in s5_ops), s5_d[l], s5_w_glu[l], s5_b_glu[l]),
            (dn_conv[l], dn_a_log[l], dn_dt_bias[l], dn_norm[l]),
            (ml_i_bias[l], ml_f_bias[l], ml_norm[l]),
            (lru_conv_w[l], lru_conv_b[l], lru_w_a[l], lru_b_a[l], lru_w_i[l], lru_b_i[l], lru_lam[l]))
        if last:
            xa, mix = xa[:N_LAT], [s[0] for s in slabs]
        else:
            mix = [jnp.concatenate(s, axis=0) for s in slabs]
        out_tm = 1024 if xa.shape[0] % 1024 == 0 else 512
        xa = matmul_gated_residual(mix, w_out[l].astype(BF16), xa, gate1, SEQ, out_tm, 512)

        h2, idx, wsel = norm_mod(xa, norm2[l][None], scale2, shift2, SEQ, 256, router=router)
        y = moe_ffn(l, h2, idx, wsel, moe_w1, moe_w3, moe_w2)
        xa = gated_add(xa, y, gate2, SEQ, 256)

    return final_norm(xa[:N_LAT], norm_f[None], 256).reshape(BATCH, SEQ, d)
```

```python
import functools
import math

import numpy as np
import jax
import jax.numpy as jnp
from jax import lax
from jax.experimental import pallas as pl
from jax.experimental.pallas import tpu as pltpu

D_MODEL = 4096
BATCH = 2
SEQ = 4096
DEPTH = 2
GRID_W = 64
CTX_LEN = 256
MIX_W = D_MODEL // 4
MIX_TOTAL = 4 * MIX_W
CHUNK = 64
CONV_W = 4
CONV_PAD_L = 2
CONV_PAD_R = 1
EPS = 1e-6
S5_CH = 16
S5_G = MIX_W // S5_CH
S5_P = 64
DN_HD = 128
DN_H = MIX_W // DN_HD
ML_H = 4
ML_DV = MIX_W // ML_H
ML_DQK = ML_DV // 2
LRU_BLOCKS = 8
LRU_BD = MIX_W // LRU_BLOCKS
LRU_C = 8.0
N_EXPERTS = 16
N_GROUPS = 4
E_PER_G = N_EXPERTS // N_GROUPS
TOP_K = 2
D_FF = D_MODEL // 4
PROJ_SIZES = (MIX_W, MIX_W, MIX_W, MIX_W, MIX_W, 2 * DN_H, 2 * DN_H, ML_H * ML_DQK, ML_H * ML_DQK, MIX_W, MIX_W,
              2 * ML_H, 2 * ML_H, MIX_W, MIX_W)
PROJ_W = sum(PROJ_SIZES)

N_LAT = BATCH * SEQ
N_CTX = BATCH * CTX_LEN
VMEM_LIMIT = 56 * 1024 * 1024

F32 = jnp.float32
BF16 = jnp.bfloat16


def _cparams(sem):
    return pltpu.CompilerParams(dimension_semantics=sem, vmem_limit_bytes=VMEM_LIMIT)


def _mm_kernel(a_ref, b_ref, o_ref):
    o_ref[...] = jnp.dot(a_ref[...], b_ref[...], preferred_element_type=F32).astype(o_ref.dtype)


def matmul(a, b, tm, tn, out_dtype=F32):
    m, k = a.shape
    n = b.shape[1]
    assert m % tm == 0 and n % tn == 0
    return pl.pallas_call(
        _mm_kernel,
        grid=(m // tm, n // tn),
        in_specs=[pl.BlockSpec((tm, k), lambda i, j: (i, 0)),
                  pl.BlockSpec((k, tn), lambda i, j: (0, j))],
        out_specs=pl.BlockSpec((tm, tn), lambda i, j: (i, j)),
        out_shape=jax.ShapeDtypeStruct((m, n), out_dtype),
        compiler_params=_cparams(("parallel", "arbitrary")),
        name="mm",
    )(a, b)


def _mm_resid_kernel(a0_ref, a1_ref, a2_ref, a3_ref, b_ref, r_ref, g_ref, o_ref):
    acc = None
    for s, a_ref in enumerate((a0_ref, a1_ref, a2_ref, a3_ref)):
        part = jnp.dot(a_ref[...], b_ref[s * MIX_W:(s + 1) * MIX_W, :], preferred_element_type=F32)
        acc = part if acc is None else acc + part
    o_ref[...] = r_ref[...] + g_ref[...] * acc


def matmul_gated_residual(a_slabs, b, resid, gate, rows_per_gate, tm, tn):
    m = a_slabs[0].shape[0]
    k, n = b.shape
    assert m % tm == 0 and n % tn == 0 and rows_per_gate % tm == 0 and k == 4 * MIX_W
    return pl.pallas_call(
        _mm_resid_kernel,
        grid=(m // tm, n // tn),
        in_specs=[pl.BlockSpec((tm, MIX_W), lambda i, j: (i, 0))] * 4 + [
                  pl.BlockSpec((k, tn), lambda i, j: (0, j)),
                  pl.BlockSpec((tm, tn), lambda i, j: (i, j)),
                  pl.BlockSpec((None, 1, tn), lambda i, j: ((i * tm) // rows_per_gate, 0, j))],
        out_specs=pl.BlockSpec((tm, tn), lambda i, j: (i, j)),
        out_shape=jax.ShapeDtypeStruct((m, n), F32),
        compiler_params=_cparams(("parallel", "arbitrary")),
        name="mm_resid",
    )(*a_slabs, b, resid, gate)


def _ada_kernel(a_ref, w_ref, b_ref, o_ref):
    o_ref[...] = jnp.dot(a_ref[...], w_ref[...].astype(BF16), preferred_element_type=F32) + b_ref[...]


def ada_matmul(layer, a, w, bias, tn):
    m, k = a.shape
    n = w.shape[2]
    return pl.pallas_call(
        _ada_kernel,
        grid=(n // tn,),
        in_specs=[pl.BlockSpec((m, k), lambda j: (0, 0)),
                  pl.BlockSpec((None, k, tn), lambda j: (layer, 0, j)),
                  pl.BlockSpec((None, 1, tn), lambda j: (layer, 0, j))],
        out_specs=pl.BlockSpec((m, tn), lambda j: (0, j)),
        out_shape=jax.ShapeDtypeStruct((m, n), F32),
        compiler_params=_cparams(("arbitrary",)),
        name="ada",
    )(a, w, bias)


def _modulated_norm(x, g, sc, sh):
    ms = jnp.mean(x * x, axis=-1, keepdims=True)
    y = x * lax.rsqrt(ms + EPS) * g
    return y * (1.0 + sc) + sh


def _norm_mod_kernel(x_ref, g_ref, sc_ref, sh_ref, o_ref):
    o_ref[...] = _modulated_norm(x_ref[...], g_ref[...], sc_ref[...], sh_ref[...]).astype(o_ref.dtype)


def _pair_top2_sum(v):
    a, b, c, d = v
    return jnp.maximum(jnp.maximum(jnp.maximum(a + b, a + c), jnp.maximum(a + d, b + c)),
                       jnp.maximum(b + d, c + d))


def _norm_mod_router_kernel(x_ref, g_ref, sc_ref, sh_ref, rwt_ref, rb_ref, o_ref, idx_ref, w_ref):
    h = _modulated_norm(x_ref[...], g_ref[...], sc_ref[...], sh_ref[...])
    o_ref[...] = h.astype(o_ref.dtype)
    logits = lax.dot_general(rwt_ref[...], h, (((1,), (1,)), ((), ())),
                             precision=lax.Precision.HIGHEST, preferred_element_type=F32)
    scores = jax.nn.sigmoid(logits)
    biased = scores + rb_ref[...]
    s = [scores[e:e + 1, :] for e in range(N_EXPERTS)]
    b = [biased[e:e + 1, :] for e in range(N_EXPERTS)]
    gs = [_pair_top2_sum(b[E_PER_G * g:E_PER_G * (g + 1)]) for g in range(N_GROUPS)]
    best, gsel = gs[0], jnp.zeros_like(gs[0], dtype=jnp.int32)
    for g in range(1, N_GROUPS):
        better = gs[g] > best
        gsel = jnp.where(better, g, gsel)
        best = jnp.where(better, gs[g], best)
    vb, vs = [], []
    for k in range(E_PER_G):
        bk, sk = b[k], s[k]
        for g in range(1, N_GROUPS):
            bk = jnp.where(gsel == g, b[E_PER_G * g + k], bk)
            sk = jnp.where(gsel == g, s[E_PER_G * g + k], sk)
        vb.append(bk)
        vs.append(sk)
    m1, i1, w1 = vb[0], jnp.zeros_like(gsel), vs[0]
    for k in range(1, E_PER_G):
        better = vb[k] > m1
        i1 = jnp.where(better, k, i1)
        w1 = jnp.where(better, vs[k], w1)
        m1 = jnp.where(better, vb[k], m1)
    m2, i2, w2, have = vb[0], jnp.zeros_like(gsel), vs[0], i1 != 0
    for k in range(1, E_PER_G):
        valid = i1 != k
        better = valid & (jnp.logical_not(have) | (vb[k] > m2))
        i2 = jnp.where(better, k, i2)
        w2 = jnp.where(better, vs[k], w2)
        m2 = jnp.where(better, vb[k], m2)
        have = have | valid
    tot = w1 + w2
    idx_ref[0:1, :] = gsel * E_PER_G + i1
    idx_ref[1:2, :] = gsel * E_PER_G + i2
    w_ref[0:1, :] = w1 / tot
    w_ref[1:2, :] = w2 / tot


def norm_mod(x, g, sc, sh, rows_per_mod, tm, router=None):
    m, d = x.shape
    assert m % tm == 0 and rows_per_mod % tm == 0
    mod_spec = pl.BlockSpec((None, 1, d), lambda i: ((i * tm) // rows_per_mod, 0, 0))
    in_specs = [pl.BlockSpec((tm, d), lambda i: (i, 0)), pl.BlockSpec((1, d), lambda i: (0, 0)), mod_spec, mod_spec]
    o_spec = pl.BlockSpec((tm, d), lambda i: (i, 0))
    o_shape = jax.ShapeDtypeStruct((m, d), BF16)
    if router is None:
        return pl.pallas_call(
            _norm_mod_kernel, grid=(m // tm,), in_specs=in_specs, out_specs=o_spec, out_shape=o_shape,
            compiler_params=_cparams(("parallel",)), name="norm_mod",
        )(x, g, sc, sh)
    rwt, rb = router
    in_specs += [pl.BlockSpec((N_EXPERTS, d), lambda i: (0, 0)), pl.BlockSpec((N_EXPERTS, 1), lambda i: (0, 0))]
    sel_spec = pl.BlockSpec((TOP_K, tm), lambda i: (0, i))
    return pl.pallas_call(
        _norm_mod_router_kernel, grid=(m // tm,), in_specs=in_specs,
        out_specs=[o_spec, sel_spec, sel_spec],
        out_shape=[o_shape, jax.ShapeDtypeStruct((TOP_K, m), jnp.int32), jax.ShapeDtypeStruct((TOP_K, m), F32)],
        compiler_params=_cparams(("parallel",)), name="norm_mod_router",
    )(x, g, sc, sh, rwt, rb)


def _gated_add_kernel(x_ref, y0_ref, y1_ref, g_ref, o_ref):
    o_ref[...] = x_ref[...] + g_ref[...] * (y0_ref[...] + y1_ref[...])


def gated_add(x, y0, y1, gate, rows_per_gate, tm):
    m, d = x.shape
    assert m % tm == 0 and rows_per_gate % tm == 0
    row = pl.BlockSpec((tm, d), lambda i: (i, 0))
    return pl.pallas_call(
        _gated_add_kernel, grid=(m // tm,),
        in_specs=[row, row, row, pl.BlockSpec((None, 1, d), lambda i: ((i * tm) // rows_per_gate, 0, 0))],
        out_specs=row, out_shape=jax.ShapeDtypeStruct((m, d), F32),
        compiler_params=_cparams(("parallel",)), name="gated_add",
    )(x, y0, y1, gate)


def _final_norm_kernel(x_ref, g_ref, o_ref):
    x = x_ref[...]
    ms = jnp.mean(x * x, axis=-1, keepdims=True)
    o_ref[...] = x * lax.rsqrt(ms + EPS) * g_ref[...]


def final_norm(x, g, tm):
    m, d = x.shape
    return pl.pallas_call(
        _final_norm_kernel, grid=(m // tm,),
        in_specs=[pl.BlockSpec((tm, d), lambda i: (i, 0)), pl.BlockSpec((1, d), lambda i: (0, 0))],
        out_specs=pl.BlockSpec((tm, d), lambda i: (i, 0)),
        out_shape=jax.ShapeDtypeStruct((m, d), F32),
        compiler_params=_cparams(("parallel",)), name="final_norm",
    )(x, g)


MOE_TM = 512
MOE_KC = 1024
MOE_FC = 256
MOE_NK = D_MODEL // MOE_KC
MOE_NF = D_FF // MOE_FC
MOE_PARTS = 2


def _moe_kernel(te_ref, nu_ref, x_ref, w1_ref, w3_ref, w2_ref, g_ref, *rest, tile0, has_prev):
    o_ref, hg_s, hu_s, act_s = rest[1:] if has_prev else rest
    t = pl.program_id(0)
    s = pl.program_id(1)
    used = t + tile0 < nu_ref[0]

    @pl.when(used & (s < MOE_NK))
    def _():
        x = x_ref[...]
        pg = jnp.dot(x, w1_ref[...].astype(BF16), preferred_element_type=F32)
        pu = jnp.dot(x, w3_ref[...].astype(BF16), preferred_element_type=F32)

        @pl.when(s == 0)
        def _():
            hg_s[...] = pg
            hu_s[...] = pu

        @pl.when(s > 0)
        def _():
            hg_s[...] += pg
            hu_s[...] += pu

        @pl.when(s == MOE_NK - 1)
        def _():
            hg = hg_s[...]
            act = ((hg * jax.nn.sigmoid(hg)) * hu_s[...] * g_ref[...]).astype(BF16)
            for f in range(MOE_NF):
                act_s[f] = act[:, f * MOE_FC:(f + 1) * MOE_FC]

    @pl.when(used & (s >= MOE_NK))
    def _():
        y = jnp.dot(act_s[s - MOE_NK], w2_ref[...].astype(BF16), preferred_element_type=F32)

        @pl.when(s == MOE_NK)
        def _():
            o_ref[...] = y

        @pl.when(s > MOE_NK)
        def _():
            o_ref[...] += y

    @pl.when(jnp.logical_not(used) & (s == 0))
    def _():
        o_ref[...] = jnp.zeros_like(o_ref)


def moe_grouped(layer, tile_expert, n_used, xs, w1, w3, w2, gs, tile0, ys_prev):
    rp, d = xs.shape
    r = gs.shape[0]

    def live(t, nu):
        return t + tile0 < nu[0]

    def kidx(t, s, nu):
        return jnp.where(live(t, nu), jnp.minimum(s, MOE_NK - 1), MOE_NK - 1)

    def fidx(t, s, nu):
        return jnp.where(live(t, nu), jnp.maximum(s - MOE_NK, 0), MOE_NF - 1)

    in_specs = [pl.BlockSpec((MOE_TM, MOE_KC), lambda t, s, te, nu: (t, kidx(t, s, nu))),
                pl.BlockSpec((None, None, MOE_KC, D_FF),
                             lambda t, s, te, nu: (layer, te[t + tile0], kidx(t, s, nu), 0)),
                pl.BlockSpec((None, None, MOE_KC, D_FF),
                             lambda t, s, te, nu: (layer, te[t + tile0], kidx(t, s, nu), 0)),
                pl.BlockSpec((None, None, MOE_FC, d), lambda t, s, te, nu: (layer, te[t + tile0], fidx(t, s, nu), 0)),
                pl.BlockSpec((MOE_TM, 1), lambda t, s, te, nu: (t + tile0, 0))]
    args = [tile_expert, n_used, xs, w1, w3, w2, gs]
    aliases = {}
    if ys_prev is not None:
        in_specs.append(pl.BlockSpec(memory_space=pl.ANY))
        aliases = {len(args): 0}
        args.append(ys_prev)
    grid_spec = pltpu.PrefetchScalarGridSpec(
        num_scalar_prefetch=2,
        grid=(rp // MOE_TM, MOE_NK + MOE_NF),
        in_specs=in_specs,
        out_specs=pl.BlockSpec((MOE_TM, d), lambda t, s, te, nu: (t + tile0, 0)),
        scratch_shapes=[pltpu.VMEM((MOE_TM, D_FF), F32), pltpu.VMEM((MOE_TM, D_FF), F32),
                        pltpu.VMEM((MOE_NF, MOE_TM, MOE_FC), BF16)],
    )
    return pl.pallas_call(
        functools.partial(_moe_kernel, tile0=tile0, has_prev=ys_prev is not None), grid_spec=grid_spec,
        out_shape=jax.ShapeDtypeStruct((r, d), F32), input_output_aliases=aliases,
        compiler_params=_cparams(("arbitrary", "arbitrary")), name="moe",
    )(*args)


CUMSUM_BLK = 256


def _onehot_cumsum(onehot):
    m, e = onehot.shape
    x = onehot.reshape(m // CUMSUM_BLK, CUMSUM_BLK, e)
    tri = jnp.tril(jnp.ones((CUMSUM_BLK, CUMSUM_BLK), F32))
    within = jnp.einsum('ij,bje->bie', tri, x, precision=lax.Precision.HIGHEST)
    total = within[:, -1, :]
    return (within + (jnp.cumsum(total, axis=0) - total)[:, None, :]).reshape(m, e)


def moe_dispatch(idx, wsel):
    n = idx.shape[1]
    e_flat = idx.reshape(-1)
    onehot = (e_flat[:, None] == jnp.arange(N_EXPERTS, dtype=jnp.int32)[None, :]).astype(F32)
    csum = _onehot_cumsum(onehot)
    counts = csum[-1].astype(jnp.int32)
    rank = jnp.sum(onehot * csum, axis=1).astype(jnp.int32) - 1
    padded = ((counts + MOE_TM - 1) // MOE_TM) * MOE_TM
    seg_end = jnp.cumsum(padded)
    seg_start = seg_end - padded
    dest = jnp.sum(onehot * seg_start.astype(F32)[None, :], axis=1).astype(jnp.int32) + rank
    rows = TOP_K * n + N_EXPERTS * MOE_TM
    pair = jnp.zeros((rows,), jnp.int32).at[dest].set(jnp.arange(TOP_K * n, dtype=jnp.int32) + 1) - 1
    src_tok = jnp.maximum(pair, 0) % n
    gate = jnp.where(pair >= 0, wsel.reshape(-1).at[jnp.maximum(pair, 0)].get(mode="promise_in_bounds"), 0.0)
    n_tiles = rows // MOE_TM
    n_used = (seg_end[-1] // MOE_TM).astype(jnp.int32)
    tile_start = jnp.arange(n_tiles, dtype=jnp.int32) * MOE_TM
    te = jnp.minimum(jnp.searchsorted(seg_end, tile_start, side='right'), N_EXPERTS - 1).astype(jnp.int32)
    te = jnp.where(jnp.arange(n_tiles) < n_used, te, te[jnp.maximum(n_used - 1, 0)])
    return dest.reshape(TOP_K, n), src_tok, gate[:, None], te, n_used.reshape(1)


def _take_rows(a, rows):
    return a.at[rows].get(mode="promise_in_bounds")


def moe_ffn(layer, hb, idx, wsel, w1, w3, w2):
    dest, src_tok, gate, te, n_used = moe_dispatch(idx, wsel)
    n_tiles = src_tok.shape[0] // MOE_TM
    assert n_tiles % MOE_PARTS == 0
    part_tiles = n_tiles // MOE_PARTS
    ys = None
    for p in range(MOE_PARTS):
        rows = slice(p * part_tiles * MOE_TM, (p + 1) * part_tiles * MOE_TM)
        ys = moe_grouped(layer, te, n_used, _take_rows(hb, src_tok[rows]), w1, w3, w2, gate,
                         p * part_tiles, ys)
    return _take_rows(ys, dest[0]), _take_rows(ys, dest[1])


SEQ_ALL = CTX_LEN + SEQ
S5_T = 16
S5_NC = SEQ_ALL // S5_T
S5_NCC = CTX_LEN // S5_T
S5_W = S5_T * S5_CH


def _gelu_tanh(x):
    return x * (0.5 * (1.0 + jnp.tanh(math.sqrt(2.0 / math.pi) * (x + 0.044715 * (x * x * x)))))


def s5_chunk_operators(lam_re, lam_im, log_step, b_re, b_im, c_re, c_im):
    hp = lax.Precision.HIGHEST
    jj = jnp.arange(S5_T)[:, None]
    ii = jnp.arange(S5_T)[None, :]
    ms, es, fs, a16 = [], [], [], []
    for d in range(2):
        dt = jnp.exp(log_step[d])[:, None]
        lr, li = lam_re[d], lam_im[d]
        er = jnp.exp(lr * dt)
        abr, abi = er * jnp.cos(li * dt), er * jnp.sin(li * dt)
        nr, ni = abr - 1.0, abi
        den = lr * lr + li * li
        fr = (nr * lr + ni * li) / den
        fi = (ni * lr - nr * li) / den
        bbr = fr[..., None] * b_re[d] - fi[..., None] * b_im[d]
        bbi = fr[..., None] * b_im[d] + fi[..., None] * b_re[d]
        k = jnp.arange(S5_T + 1, dtype=F32)[:, None, None]
        mag, ang = jnp.exp(k * (lr * dt)), k * (li * dt)
        pr, pi = mag * jnp.cos(ang), mag * jnp.sin(ang)
        pbr = pr[..., None] * bbr - pi[..., None] * bbi
        pbi = pr[..., None] * bbi + pi[..., None] * bbr
        cr, ci = c_re[d], c_im[d]
        kk = (jnp.einsum('gop,lgpc->lgoc', cr, pbr[:S5_T], precision=hp)
              - jnp.einsum('gop,lgpc->lgoc', ci, pbi[:S5_T], precision=hp))
        cpr = cr[None] * pr[:, :, None, :] - ci[None] * pi[:, :, None, :]
        cpi = -(cr[None] * pi[:, :, None, :] + ci[None] * pr[:, :, None, :])
        if d == 0:
            lag, mask = ii - jj, ii >= jj
            e_pow = S5_T - 1 - jnp.arange(S5_T)
            f_pow = jnp.arange(S5_T) + 1
        else:
            lag, mask = jj - ii, jj >= ii
            e_pow = jnp.arange(S5_T)
            f_pow = S5_T - jnp.arange(S5_T)
        mm = jnp.where(mask[:, :, None, None, None], kk[jnp.clip(lag, 0, S5_T - 1)], 0.0)
        ms.append(mm.transpose(2, 0, 4, 1, 3).reshape(S5_G, S5_W, S5_W))
        for pb in (pbr, pbi):
            es.append(pb[e_pow].transpose(1, 0, 3, 2).reshape(S5_G, S5_W, S5_P))
        for cp in (cpr, cpi):
            fs.append(cp[f_pow].transpose(1, 3, 0, 2).reshape(S5_G, S5_P, S5_W))
        a16 += [pr[S5_T][:, None, :], pi[S5_T][:, None, :]]
    return (ms[0] + ms[1]).astype(BF16), jnp.stack(es).astype(BF16), jnp.stack(fs).astype(BF16), jnp.stack(a16)


def _s5_kernel(u_ref, m_ref, e_ref, f_ref, a_ref, o_ref, e_scr, s_scr):
    u = u_ref[...]
    for k in range(4):
        e_scr[k] = jnp.dot(u, e_ref[k], preferred_element_type=F32)
    arf, aif, arb, aib = (a_ref[k] for k in range(4))

    def segment(start, n, carry):
        def body(k, carry):
            out = []
            for b in range(BATCH):
                sfr, sfi, sbr, sbi = carry[4 * b:4 * b + 4]
                rf = b * S5_NC + start + k
                rb = b * S5_NC + start + n - 1 - k
                s_scr[0, pl.ds(rf, 1), :] = sfr
                s_scr[1, pl.ds(rf, 1), :] = sfi
                s_scr[2, pl.ds(rb, 1), :] = sbr
                s_scr[3, pl.ds(rb, 1), :] = sbi
                nfr = arf * sfr - aif * sfi + e_scr[0, pl.ds(rf, 1), :]
                nfi = arf * sfi + aif * sfr + e_scr[1, pl.ds(rf, 1), :]
                nbr = arb * sbr - aib * sbi + e_scr[2, pl.ds(rb, 1), :]
                nbi = arb * sbi + aib * sbr + e_scr[3, pl.ds(rb, 1), :]
                out += [nfr, nfi, nbr, nbi]
            return tuple(out)
        return lax.fori_loop(0, n, body, carry, unroll=4)

    z = jnp.zeros((1, S5_P), F32)
    carry = segment(0, S5_NCC, (z,) * (4 * BATCH))
    segment(S5_NCC, S5_NC - S5_NCC, carry)
    y = jnp.dot(u, m_ref[...], preferred_element_type=F32)
    for k in range(4):
        y += jnp.dot(s_scr[k].astype(BF16), f_ref[k], preferred_element_type=F32)
    o_ref[...] = y


def s5_scan(u, m, e, f, a16):
    g, r, w = u.shape
    return pl.pallas_call(
        _s5_kernel, grid=(g,),
        in_specs=[pl.BlockSpec((None, r, w), lambda i: (i, 0, 0)),
                  pl.BlockSpec((None, w, w), lambda i: (i, 0, 0)),
                  pl.BlockSpec((4, None, w, S5_P), lambda i: (0, i, 0, 0)),
                  pl.BlockSpec((4, None, S5_P, w), lambda i: (0, i, 0, 0)),
                  pl.BlockSpec((4, None, 1, S5_P), lambda i: (0, i, 0, 0))],
        out_specs=pl.BlockSpec((None, r, w), lambda i: (i, 0, 0)),
        out_shape=jax.ShapeDtypeStruct((g, r, w), F32),
        scratch_shapes=[pltpu.VMEM((4, r, S5_P), F32), pltpu.VMEM((4, r, S5_P), F32)],
        compiler_params=_cparams(("parallel",)), name="s5_scan",
    )(u, m, e, f, a16)


def _s5_out_kernel(y_ref, u_ref, d_ref, w_ref, b_ref, o_ref):
    g = _gelu_tanh(y_ref[...] + d_ref[...] * u_ref[...])
    z = jnp.dot(g.astype(BF16), w_ref[...], preferred_element_type=F32) + b_ref[...]
    o_ref[...] = (g * jax.nn.sigmoid(z)).astype(o_ref.dtype)


def s5_out(y, u, d_skip, w_glu, b_glu, tm):
    m, c = y.shape
    row = pl.BlockSpec((tm, c), lambda i: (i, 0))
    vec = pl.BlockSpec((1, c), lambda i: (0, 0))
    return pl.pallas_call(
        _s5_out_kernel, grid=(m // tm,),
        in_specs=[row, row, vec, pl.BlockSpec((c, c), lambda i: (0, 0)), vec],
        out_specs=row, out_shape=jax.ShapeDtypeStruct((m, c), F32),
        compiler_params=_cparams(("parallel",)), name="s5_out",
    )(y, u, d_skip, w_glu, b_glu)


S5_LG = 128 // S5_CH


def _lane_window(lane, k):
    return (lane >= k * S5_CH) & (lane < (k + 1) * S5_CH)


def _s5_pack_kernel(x_ref, o_ref):
    lane = lax.broadcasted_iota(jnp.int32, (x_ref.shape[0], 128), 1)
    tok = [x_ref[:, t, :] for t in range(S5_T)]
    for g in range(S5_LG):
        for h in range(S5_W // 128):
            acc = jnp.zeros(lane.shape, F32)
            for k in range(S5_LG):
                shift = ((k - g) * S5_CH) % 128
                src = tok[h * S5_LG + k]
                acc = jnp.where(_lane_window(lane, k), pltpu.roll(src, shift, axis=1) if shift else src, acc)
            o_ref[g, :, h * 128:(h + 1) * 128] = acc.astype(o_ref.dtype)


def _s5_unpack_kernel(y_ref, o_ref):
    lane = lax.broadcasted_iota(jnp.int32, (y_ref.shape[1], 128), 1)
    for t in range(S5_T):
        h, k = divmod(t, S5_LG)
        acc = jnp.zeros(lane.shape, F32)
        for g in range(S5_LG):
            shift = ((g - k) * S5_CH) % 128
            src = y_ref[g, :, h * 128:(h + 1) * 128]
            acc = jnp.where(_lane_window(lane, g), pltpu.roll(src, shift, axis=1) if shift else src, acc)
        o_ref[:, t, :] = acc


def s5_pack(u):
    n = u.shape[0]
    cb = n // 2
    return pl.pallas_call(
        _s5_pack_kernel, grid=(n // cb, MIX_W // 128),
        in_specs=[pl.BlockSpec((cb, S5_T, 128), lambda i, j: (i, 0, j))],
        out_specs=pl.BlockSpec((S5_LG, cb, S5_W), lambda i, j: (j, i, 0)),
        out_shape=jax.ShapeDtypeStruct((S5_G, n, S5_W), BF16),
        compiler_params=_cparams(("parallel", "parallel")), name="s5_pack",
    )(u)


def s5_unpack(y):
    n = y.shape[1]
    cb = n // 2
    return pl.pallas_call(
        _s5_unpack_kernel, grid=(n // cb, MIX_W // 128),
        in_specs=[pl.BlockSpec((S5_LG, cb, S5_W), lambda i, j: (j, i, 0))],
        out_specs=pl.BlockSpec((cb, S5_T, 128), lambda i, j: (i, 0, j)),
        out_shape=jax.ShapeDtypeStruct((n, S5_T, MIX_W), F32),
        compiler_params=_cparams(("parallel", "parallel")), name="s5_unpack",
    )(y)


def s5_mixer(u_c, u_l, operators, d_skip, w_glu, b_glu):
    bs = u_c.shape[0]
    u_all = jnp.concatenate([u_c, u_l], axis=1)
    uc = s5_pack(u_all.reshape(bs * S5_NC, S5_T, MIX_W))
    y = s5_scan(uc, *operators)
    y = s5_unpack(y).reshape(bs * SEQ_ALL, MIX_W)
    out = s5_out(y, u_all.reshape(bs * SEQ_ALL, MIX_W), d_skip[None], w_glu.astype(BF16), b_glu[None], 544)
    out = out.reshape(bs, SEQ_ALL, MIX_W)
    return out[:, :CTX_LEN], out[:, CTX_LEN:]


LRU_CB = 256
LRU_TB = 256
LRU_HALO = 8
LRU_ROWS = SEQ_ALL + 3 * LRU_HALO


def _lru_kernel(xc_ref, xl_ref, yc_ref, yl_ref, cw_ref, cb_ref, wa_ref, ba_ref, wi_ref, bi_ref, sp_ref,
                oc_ref, ol_ref, xpad, a_f, b_f, a_b, b_b):
    halo = jnp.zeros((LRU_HALO, LRU_CB), F32)
    lat0 = CTX_LEN + 2 * LRU_HALO
    xpad[0:LRU_HALO] = halo
    xpad[LRU_HALO:LRU_HALO + CTX_LEN] = xc_ref[...]
    xpad[LRU_HALO + CTX_LEN:lat0] = halo
    xpad[lat0:lat0 + SEQ] = xl_ref[...]
    xpad[lat0 + SEQ:LRU_ROWS] = halo
    cw = [cw_ref[j:j + 1, :] for j in range(CONV_W)]
    cb = cb_ref[...]

    def gates(p0, r0):
        ext = xpad[pl.ds(p0 - LRU_HALO, LRU_TB + 2 * LRU_HALO), :]
        xb = cb
        for j in range(CONV_W):
            s0 = LRU_HALO - CONV_PAD_L + j
            xb = xb + cw[j] * ext[s0:s0 + LRU_TB]
        for k in range(LRU_CB // LRU_BD):
            lanes = slice(k * LRU_BD, (k + 1) * LRU_BD)
            xk = xb[:, lanes]
            xkb = xk.astype(BF16)
            for d, (a_s, b_s) in enumerate(((a_f, b_f), (a_b, b_b))):
                r = jax.nn.sigmoid(jnp.dot(xkb, wa_ref[d, k], preferred_element_type=F32) + ba_ref[d][:, lanes])
                i = jax.nn.sigmoid(jnp.dot(xkb, wi_ref[d, k], preferred_element_type=F32) + bi_ref[d][:, lanes])
                log_a = -LRU_C * r * sp_ref[d][:, lanes]
                t = jnp.tanh(log_a)
                a_s[pl.ds(r0, LRU_TB), lanes] = jnp.exp(log_a)
                b_s[pl.ds(r0, LRU_TB), lanes] = jnp.sqrt(-2.0 * t / (1.0 - t)) * (i * xk)

    gates(LRU_HALO, 0)

    def lat_gates(k, _):
        gates(pl.multiple_of(lat0 + k * LRU_TB, 8), pl.multiple_of(CTX_LEN + k * LRU_TB, 8))
        return 0
    lax.fori_loop(0, SEQ // LRU_TB, lat_gates, 0)

    def segment(start, n, carry):
        def body(t, carry):
            hf, hb = carry
            rf = start + t
            rb = start + n - 1 - t
            hf = a_f[pl.ds(rf, 1), :] * hf + b_f[pl.ds(rf, 1), :]
            b_f[pl.ds(rf, 1), :] = hf
            hb = a_b[pl.ds(rb, 1), :] * hb + b_b[pl.ds(rb, 1), :]
            b_b[pl.ds(rb, 1), :] = hb
            return hf, hb
        return lax.fori_loop(0, n, body, carry, unroll=8)

    z = jnp.zeros((1, LRU_CB), F32)
    carry = segment(0, CTX_LEN, (z, z))
    segment(CTX_LEN, SEQ, carry)

    oc_ref[...] = (_gelu_tanh(yc_ref[...]) * (b_f[0:CTX_LEN] + b_b[0:CTX_LEN])).astype(oc_ref.dtype)

    def lat_out(k, _):
        r0 = pl.multiple_of(k * LRU_TB, 8)
        rows = pl.ds(pl.multiple_of(CTX_LEN + k * LRU_TB, 8), LRU_TB)
        h = b_f[rows, :] + b_b[rows, :]
        ol_ref[pl.ds(r0, LRU_TB), :] = (_gelu_tanh(yl_ref[pl.ds(r0, LRU_TB), :]) * h).astype(ol_ref.dtype)
        return 0
    lax.fori_loop(0, SEQ // LRU_TB, lat_out, 0)


def lru_mixer(y_c, x_c, yx_l, conv_w, conv_b, w_a, b_a, w_i, b_i, lam):
    bs = x_c.shape[0]
    nb = LRU_CB // LRU_BD
    sp = jax.nn.softplus(-lam)[:, None, :]
    ctx_spec = pl.BlockSpec((None, CTX_LEN, LRU_CB), lambda b, c: (b, 0, c))
    lat_spec = pl.BlockSpec((None, SEQ, LRU_CB), lambda b, c: (b, 0, c))
    lat_x_spec = pl.BlockSpec((None, SEQ, LRU_CB), lambda b, c: (b, 0, MIX_W // LRU_CB + c))
    vec2 = pl.BlockSpec((2, 1, LRU_CB), lambda b, c: (0, 0, c))
    wspec = pl.BlockSpec((2, nb, LRU_BD, LRU_BD), lambda b, c: (0, c, 0, 0))
    scan_buf = pltpu.VMEM((SEQ_ALL, LRU_CB), F32)
    return pl.pallas_call(
        _lru_kernel, grid=(bs, MIX_W // LRU_CB),
        in_specs=[ctx_spec, lat_x_spec, ctx_spec, lat_spec,
                  pl.BlockSpec((CONV_W, LRU_CB), lambda b, c: (0, c)),
                  pl.BlockSpec((1, LRU_CB), lambda b, c: (0, c)),
                  wspec, vec2, wspec, vec2, vec2],
        out_specs=[ctx_spec, lat_spec],
        out_shape=[jax.ShapeDtypeStruct((bs, CTX_LEN, MIX_W), F32), jax.ShapeDtypeStruct((bs, SEQ, MIX_W), F32)],
        scratch_shapes=[pltpu.VMEM((LRU_ROWS, LRU_CB), F32), scan_buf, scan_buf, scan_buf, scan_buf],
        compiler_params=_cparams(("parallel", "parallel")), name="lru",
    )(x_c, yx_l, y_c, yx_l, conv_w, conv_b[None], w_a.astype(BF16), b_a[:, None, :], w_i.astype(BF16),
      b_i[:, None, :], sp)


GT_W = 8
GT_C = 1024


def _grid_transpose_kernel(x_ref, o_ref):
    for j in range(GT_W):
        o_ref[j] = x_ref[:, j, :].astype(o_ref.dtype)


def grid_transpose(src, batch, col0, width, out_dtype):
    assert col0 % GT_C == 0 and width % GT_C == 0
    return pl.pallas_call(
        _grid_transpose_kernel, grid=(batch, GRID_W // GT_W, width // GT_C),
        in_specs=[pl.BlockSpec((GRID_W, GT_W, GT_C), lambda b, j, c: (b, j, col0 // GT_C + c))],
        out_specs=pl.BlockSpec((None, GT_W, GRID_W, GT_C), lambda b, j, c: (b, j, 0, c)),
        out_shape=jax.ShapeDtypeStruct((batch, GRID_W, GRID_W, width), out_dtype),
        compiler_params=_cparams(("parallel", "parallel", "parallel")), name="grid_transpose",
    )(src)


N_CHUNKS = SEQ_ALL // CHUNK
N_CCH = CTX_LEN // CHUNK
GATE_COLBLK = 80
MIX_TB = 256


def _dot(a, b):
    return jnp.dot(a, b, preferred_element_type=F32)


def _dot_nt(a, b):
    return lax.dot_general(a, b, (((1,), (1,)), ((), ())), preferred_element_type=F32)


def _dot_tn(a, b):
    return lax.dot_general(a, b, (((0,), (0,)), ((), ())), preferred_element_type=F32)


def _split2(a):
    hi = a.astype(BF16)
    return hi, (a - hi.astype(F32)).astype(BF16)


def _dot3(a, b):
    ah, al = _split2(a)
    bh, bl = _split2(b)
    return _dot(ah, bh) + (_dot(ah, bl) + _dot(al, bh))


def _dot_mask(m, x):
    hi = x.astype(BF16)
    r1 = x - hi.astype(F32)
    mid = r1.astype(BF16)
    lo = (r1 - mid.astype(F32)).astype(BF16)
    return _dot(m, hi) + (_dot(m, mid) + _dot(m, lo))


def _chunk_masks(backward):
    i = lax.broadcasted_iota(jnp.int32, (CHUNK, CHUNK), 0)
    j = lax.broadcasted_iota(jnp.int32, (CHUNK, CHUNK), 1)
    if backward:
        i, j = j, i
    tri, strict = j <= i, j < i
    return tri, strict, tri.astype(BF16), jnp.where(strict, 1.0, 0.0).astype(F32)


def _lane_pick(blk, lane):
    lanes = lax.broadcasted_iota(jnp.int32, blk.shape, 1)
    return jnp.sum(jnp.where(lanes == lane, blk, 0.0), axis=-1, keepdims=True)


def _softplus(x):
    return jnp.maximum(x, 0.0) + jnp.log1p(jnp.exp(-jnp.abs(x)))


def _silu(x):
    return x * jax.nn.sigmoid(x)


def _bidir_chunks(step, carry, n_ctx, n_all):
    def segment(start, n, carry):
        def body(t, carry):
            return step(start + t, start + n - 1 - t, carry)
        return lax.fori_loop(0, n, body, carry)
    carry = segment(0, n_ctx, carry)
    return segment(n_ctx, n_all - n_ctx, carry)


DN_HALO = 8
DN_ROWS = SEQ_ALL + 3 * DN_HALO
DN_C = 64
DN_GROUP = 256
DN_GC = DN_GROUP // (2 * DN_C)
DN_NCH = SEQ_ALL // DN_C
DN_NCC = CTX_LEN // DN_C
DN_LOCKSTEP = 4


def _dn_kernel(qc_ref, ql_ref, kc_ref, kl_ref, vc_ref, vl_ref, zc_ref, zl_ref, gc_ref, gl_ref, hp_ref,
               cwq_ref, cwk_ref, cwv_ref, nw_ref, ol_ref, oc_ref,
               xpad, qn, kn, vn, u_s, w_s, qg_s, kd_s, at_s, dec_s):
    h = pl.program_id(1)
    halo = jnp.zeros((DN_HALO, DN_HD), F32)
    lat0 = CTX_LEN + 2 * DN_HALO

    def conv_into(xc_ref, xl_ref, cw_ref, dst, post):
        xpad[0:DN_HALO] = halo
        xpad[DN_HALO:DN_HALO + CTX_LEN] = xc_ref[...]
        xpad[DN_HALO + CTX_LEN:lat0] = halo
        xpad[lat0:lat0 + SEQ] = xl_ref[...]
        xpad[lat0 + SEQ:DN_ROWS] = halo
        cw = [cw_ref[j:j + 1, :] for j in range(CONV_W)]

        def block(p0, r0):
            ext = xpad[pl.ds(p0 - DN_HALO, MIX_TB + 2 * DN_HALO), :]
            s0 = DN_HALO - CONV_PAD_L
            acc = cw[0] * ext[s0:s0 + MIX_TB]
            for j in range(1, CONV_W):
                acc = acc + cw[j] * ext[s0 + j:s0 + j + MIX_TB]
            dst[pl.ds(r0, MIX_TB), :] = post(_silu(acc))

        block(DN_HALO, 0)

        def lat_block(k, _):
            block(pl.multiple_of(lat0 + k * MIX_TB, 8), pl.multiple_of(CTX_LEN + k * MIX_TB, 8))
            return 0
        lax.fori_loop(0, SEQ // MIX_TB, lat_block, 0)

    def l2n(scale):
        return lambda t: t * (lax.rsqrt(jnp.sum(t * t, axis=-1, keepdims=True) + EPS) * scale)

    conv_into(qc_ref, ql_ref, cwq_ref, qn, l2n(DN_HD ** -0.5))
    conv_into(kc_ref, kl_ref, cwk_ref, kn, l2n(1.0))
    conv_into(vc_ref, vl_ref, cwv_ref, vn, lambda t: t)

    gi = lax.broadcasted_iota(jnp.int32, (DN_GROUP, DN_GROUP), 0)
    gj = lax.broadcasted_iota(jnp.int32, (DN_GROUP, DN_GROUP), 1)
    same = (gi // DN_C) == (gj // DN_C)
    blk16 = (gi // 16) == (gj // 16)
    bwd = ((gi // DN_C) % 2) == 1
    fwd = jnp.logical_not(bwd)
    li, lj = gi % DN_C, gj % DN_C
    tri = same & ((fwd & (lj <= li)) | (bwd & (lj >= li)))
    strict = same & ((fwd & (lj < li)) | (bwd & (lj > li)))
    eye = jnp.where(gi == gj, 1.0, 0.0).astype(F32)
    m16 = jnp.where(tri, 1.0, 0.0).astype(BF16)

    def both_dirs(xf, xb):
        parts = []
        for c in range(DN_GC):
            parts += [xf[c * DN_C:(c + 1) * DN_C], xb[c * DN_C:(c + 1) * DN_C]]
        return jnp.concatenate(parts, axis=0)

    def prep(ts, g_ref, row0):
        grows = DN_GC * DN_C
        n = range(len(ts))
        r0 = [pl.multiple_of(t * grows, grows) for t in ts]
        gblk = [g_ref[pl.ds(pl.multiple_of(t * grows - row0, grows), grows), :] for t in ts]
        q = [both_dirs(*[qn[pl.ds(r0[i], grows), :]] * 2) for i in n]
        k = [both_dirs(*[kn[pl.ds(r0[i], grows), :]] * 2) for i in n]
        v = [both_dirs(*[vn[pl.ds(r0[i], grows), :]] * 2) for i in n]
        g_col = [both_dirs(*[hp_ref[:, d:d + 1] * _softplus(_lane_pick(gblk[i], d * DN_H + h)
                                                            + hp_ref[:, 2 + d:3 + d]) for d in range(2)]) for i in n]
        beta = [both_dirs(*[jax.nn.sigmoid(_lane_pick(gblk[i], 2 * DN_H + d * DN_H + h)) for d in range(2)])
                for i in n]
        k16 = [k[i].astype(BF16) for i in n]
        kb = [k[i] * beta[i] for i in n]
        gb = [jnp.broadcast_to(g_col[i], (DN_GROUP, DN_HD)) for i in n]
        kk = [_dot_nt(kb[i].astype(BF16), k16[i]) for i in n]
        gcum = [_dot_mask(m16, gb[i]) for i in n]
        tot = [jnp.broadcast_to(jnp.sum(gb[i].reshape(DN_GROUP // DN_C, DN_C, DN_HD), axis=1, keepdims=True),
                                (DN_GROUP // DN_C, DN_C, DN_HD)).reshape(DN_GROUP, DN_HD) for i in n]
        qk = [_dot_nt(q[i].astype(BF16), k16[i]) for i in n]
        dlt = [gcum[i][:, 0:1] - gcum[i].T[0:1, :] for i in n]
        decay = [jnp.where(tri, jnp.exp(jnp.where(tri, dlt[i], 0.0)), 0.0) for i in n]
        lower = [jnp.where(strict, kk[i] * decay[i], 0.0) for i in n]
        x = [-jnp.where(blk16, lower[i], 0.0) for i in n]
        tinv = [eye + x[i] for i in n]
        for _ in range(3):
            x = [_dot3(x[i], x[i]) for i in n]
            tinv = [tinv[i] + _dot3(tinv[i], x[i]) for i in n]
        size = 16
        while size < DN_C:
            inner, outer = (gi // size) == (gj // size), (gi // (2 * size)) == (gj // (2 * size))
            off = [_dot3(jnp.where(outer & jnp.logical_not(inner), lower[i], 0.0), tinv[i]) for i in n]
            tinv = [tinv[i] - _dot3(tinv[i], off[i]) for i in n]
            size *= 2
        eg = [jnp.exp(gcum[i]) for i in n]
        sol = [_dot3(tinv[i], jnp.concatenate([v[i] * beta[i], kb[i] * eg[i]], axis=1)) for i in n]
        for i in n:
            attn = (qk[i] * decay[i]).astype(BF16)
            w16 = sol[i][:, DN_HD:].astype(BF16)
            qg16 = (q[i] * eg[i]).astype(BF16)
            kd16 = (k[i] * jnp.exp(tot[i] - gcum[i])).astype(BF16)
            dec = jnp.exp(tot[i])
            for p in range(DN_GROUP // DN_C):
                d, blk = p % 2, slice(p * DN_C, (p + 1) * DN_C)
                dst = pl.ds(r0[i] + (p // 2) * DN_C, DN_C)
                u_s[d, dst, :] = sol[i][blk, :DN_HD]
                w_s[d, dst, :] = w16[blk]
                at_s[d, dst, :] = attn[blk, blk]
                qg_s[d, dst, :] = qg16[blk]
                kd_s[d, dst, :] = kd16[blk]
                dec_s[d, pl.ds(pl.multiple_of((DN_GC * ts[i] + p // 2) * 8, 8), 8), :] = dec[p * DN_C:p * DN_C + 8]
        return 0

    n_ctx_groups, n_groups = DN_NCC // DN_GC, DN_NCH // DN_GC
    assert (n_groups - n_ctx_groups) % DN_LOCKSTEP == 0
    prep(list(range(n_ctx_groups)), gc_ref, 0)
    lax.fori_loop(0, (n_groups - n_ctx_groups) // DN_LOCKSTEP,
                  lambda t, _: prep([n_ctx_groups + DN_LOCKSTEP * t + i for i in range(DN_LOCKSTEP)],
                                    gl_ref, CTX_LEN), 0)
    o_dst = (qn, kn)
    o_f, o_b = o_dst

    def step(cf, cb, carry):
        dirs = range(2)
        rows = [pl.ds(pl.multiple_of(c * DN_C, DN_C), DN_C) for c in (cf, cb)]
        s16 = [carry[d].astype(BF16) for d in dirs]
        ws = [_dot(w_s[d, rows[d], :], s16[d]) for d in dirs]
        qs = [_dot(qg_s[d, rows[d], :], s16[d]) for d in dirs]
        v_new = [(u_s[d, rows[d], :] - ws[d]).astype(BF16) for d in dirs]
        kv = [_dot_tn(kd_s[d, rows[d], :], v_new[d]) for d in dirs]
        av = [_dot(at_s[d, rows[d], :], v_new[d]) for d in dirs]
        out = []
        for d, c in zip(dirs, (cf, cb)):
            o_dst[d][rows[d], :] = qs[d] + av[d]
            g_last = dec_s[d, pl.ds(pl.multiple_of(c * 8, 8), 8), :][0:1]
            out.append(carry[d] * g_last + kv[d])
        return tuple(out)

    z0 = jnp.zeros((DN_HD, DN_HD), F32)
    _bidir_chunks(step, (z0, z0), DN_NCC, DN_NCH)

    nw = nw_ref[...]

    def finish(rows, z):
        o = o_f[rows, :] + o_b[rows, :]
        o = o * lax.rsqrt(jnp.mean(o * o, axis=-1, keepdims=True) + EPS) * nw
        return (o * _silu(z)).astype(BF16)

    oc_ref[...] = finish(pl.ds(0, CTX_LEN), zc_ref[...])

    def lat_finish(k, _):
        r0 = pl.multiple_of(k * MIX_TB, 8)
        ol_ref[pl.ds(r0, MIX_TB), :] = finish(pl.ds(pl.multiple_of(CTX_LEN + k * MIX_TB, 8), MIX_TB),
                                              zl_ref[pl.ds(r0, MIX_TB), :])
        return 0
    lax.fori_loop(0, SEQ // MIX_TB, lat_finish, 0)


def deltanet_mixer(proj, conv_w, a_log, dt_bias, norm_w):
    hp = jnp.stack([-jnp.exp(a_log[0]), -jnp.exp(a_log[1]), dt_bias[0], dt_bias[1]], axis=-1)[:, None, :]
    ctx_blk0 = N_LAT // CTX_LEN

    def seg_specs(col0):
        cb = col0 // DN_HD
        return [pl.BlockSpec((CTX_LEN, DN_HD), lambda b, h: (ctx_blk0 + b, cb + h)),
                pl.BlockSpec((SEQ, DN_HD), lambda b, h: (b, cb + h))]

    def conv_spec(k):
        return pl.BlockSpec((CONV_W, DN_HD), lambda b, h: (0, k * DN_H + h))

    in_specs = (seg_specs(COL_DN_QKVZ) + seg_specs(COL_DN_QKVZ + MIX_W) + seg_specs(COL_DN_QKVZ + 2 * MIX_W)
                + seg_specs(COL_DN_QKVZ + 3 * MIX_W)
                + [pl.BlockSpec((CTX_LEN, 128), lambda b, h: (ctx_blk0 + b, GATE_COLBLK)),
                   pl.BlockSpec((SEQ, 128), lambda b, h: (b, GATE_COLBLK)),
                   pl.BlockSpec((None, 1, 4), lambda b, h: (h, 0, 0)),
                   conv_spec(0), conv_spec(1), conv_spec(2),
                   pl.BlockSpec((1, DN_HD), lambda b, h: (0, 0))])
    seq_f32 = pltpu.VMEM((SEQ_ALL, DN_HD), F32)
    dir_bf16 = pltpu.VMEM((2, SEQ_ALL, DN_HD), BF16)
    ol, oc = pl.pallas_call(
        _dn_kernel, grid=(BATCH, DN_H), in_specs=in_specs,
        out_specs=[pl.BlockSpec((SEQ, DN_HD), lambda b, h: (b, h)),
                   pl.BlockSpec((CTX_LEN, DN_HD), lambda b, h: (b, h))],
        out_shape=[jax.ShapeDtypeStruct((N_LAT, MIX_W), BF16), jax.ShapeDtypeStruct((N_CTX, MIX_W), BF16)],
        scratch_shapes=[pltpu.VMEM((DN_ROWS, DN_HD), F32), seq_f32, seq_f32, seq_f32,
                        pltpu.VMEM((2, SEQ_ALL, DN_HD), F32), dir_bf16, dir_bf16, dir_bf16,
                        pltpu.VMEM((2, SEQ_ALL, DN_C), BF16), pltpu.VMEM((2, DN_NCH * 8, DN_HD), F32)],
        compiler_params=_cparams(("parallel", "arbitrary")), name="deltanet",
    )(*([proj] * 10), hp, conv_w, conv_w, conv_w, norm_w[None])
    return ol, oc


def _ml_kernel(qc_ref, ql_ref, kc_ref, kl_ref, vc_ref, vl_ref, oc_ref, ol_ref, gc_ref, gl_ref, hp_ref, nw_ref,
               yl_ref, yc_ref, h_f, h_b):
    hh = pl.program_id(1)
    masks = [_chunk_masks(False), _chunk_masks(True)]
    ones = jnp.ones((CHUNK, CHUNK), BF16)
    eye = jnp.where(lax.broadcasted_iota(jnp.int32, (CHUNK, CHUNK), 0)
                    == lax.broadcasted_iota(jnp.int32, (CHUNK, CHUNK), 1), 1.0, 0.0).astype(F32)

    h_dst = (h_f, h_b)

    def advance(cs, states, refs, row0):
        q_ref, k_ref, v_ref, g_ref = refs
        dirs = range(2)
        rows = [pl.ds(pl.multiple_of(c * CHUNK - row0, CHUNK), CHUNK) for c in cs]
        tri = [masks[d][0] for d in dirs]
        gblk = [g_ref[rows[d], :] for d in dirs]
        ig = [_lane_pick(gblk[d], 4 * DN_H + d * ML_H + hh) + hp_ref[:, d:d + 1] for d in dirs]
        lf = [-_softplus(-(_lane_pick(gblk[d], 4 * DN_H + 2 * ML_H + d * ML_H + hh) + hp_ref[:, 2 + d:3 + d]))
              for d in dirs]
        q = [q_ref[rows[d], :] * (ML_DQK ** -0.5) for d in dirs]
        k = [k_ref[rows[d], :] for d in dirs]
        q16 = [q[d].astype(BF16) for d in dirs]
        v16 = [v_ref[rows[d], :].astype(BF16) for d in dirs]
        qc = [_dot(q16[d], states[d][0].astype(BF16)) for d in dirs]
        bcum = [_dot_mask(masks[d][2], jnp.broadcast_to(lf[d], (CHUNK, ML_DQK)))[:, 0:1] for d in dirs]
        dsum = [_dot_mask(masks[d][2], lf[d] * masks[d][3]) + _dot_mask(ones, ig[d] * eye) for d in dirs]
        sqk = [_dot_nt(q16[d], k[d].astype(BF16)) for d in dirs]
        dmat = [jnp.where(tri[d], dsum[d], -jnp.inf) for d in dirs]
        dmax = [jnp.max(dmat[d], axis=-1, keepdims=True) for d in dirs]
        tot = [jnp.sum(lf[d], axis=0, keepdims=True) for d in dirs]
        logw = [tot[d] - bcum[d] + ig[d] for d in dirs]
        logw_max = [jnp.max(logw[d], axis=0, keepdims=True) for d in dirs]
        inter = [bcum[d] + states[d][2] for d in dirs]
        m_t = [jnp.maximum(inter[d], dmax[d]) for d in dirs]
        w_inter = [jnp.exp(inter[d] - m_t[d]) for d in dirs]
        wmat = [jnp.exp(dmat[d] - m_t[d]) * sqk[d] for d in dirs]
        m_new = [jnp.maximum(tot[d] + states[d][2], logw_max[d]) for d in dirs]
        keep = [jnp.exp(tot[d] + states[d][2] - m_new[d]) for d in dirs]
        kw = [k[d] * jnp.exp(logw[d] - m_new[d]) for d in dirs]
        wv = [_dot(wmat[d].astype(BF16), v16[d]) for d in dirs]
        kv = [_dot_tn(kw[d].astype(BF16), v16[d]) for d in dirs]
        out = []
        for d in dirs:
            c_s, n_s, _ = states[d]
            num = w_inter[d] * qc[d] + wv[d]
            den = w_inter[d] * jnp.sum(q[d] * n_s, axis=-1, keepdims=True) + jnp.sum(wmat[d], axis=-1, keepdims=True)
            h_dst[d][pl.ds(pl.multiple_of(cs[d] * CHUNK, CHUNK), CHUNK), :] = (
                num / jnp.maximum(jnp.abs(den), jnp.exp(-m_t[d])))
            out.append((keep[d] * c_s + kv[d], keep[d] * n_s + jnp.sum(kw[d], axis=0, keepdims=True), m_new[d]))
        return tuple(out)

    ctx_refs = (qc_ref, kc_ref, vc_ref, gc_ref)
    lat_refs = (ql_ref, kl_ref, vl_ref, gl_ref)

    def segment(refs, start, n, carry):
        def body(t, carry):
            return advance((start + t, start + n - 1 - t), carry, refs, start * CHUNK)
        return lax.fori_loop(0, n, body, carry)

    s0 = (jnp.zeros((ML_DQK, ML_DV), F32), jnp.zeros((1, ML_DQK), F32), jnp.zeros((1, 1), F32))
    carry = segment(ctx_refs, 0, N_CCH, (s0, s0))
    segment(lat_refs, N_CCH, N_CHUNKS - N_CCH, carry)

    nw = nw_ref[...]

    def finish(rows, o):
        hsum = h_f[rows, :] + h_b[rows, :]
        hsum = hsum * lax.rsqrt(jnp.mean(hsum * hsum, axis=-1, keepdims=True) + EPS) * nw
        return (hsum * jax.nn.sigmoid(o)).astype(BF16)

    yc_ref[...] = finish(pl.ds(0, CTX_LEN), oc_ref[...])

    def lat_finish(k, _):
        r0 = pl.multiple_of(k * MIX_TB, 8)
        yl_ref[pl.ds(r0, MIX_TB), :] = finish(pl.ds(pl.multiple_of(CTX_LEN + k * MIX_TB, 8), MIX_TB),
                                              ol_ref[pl.ds(r0, MIX_TB), :])
        return 0
    lax.fori_loop(0, SEQ // MIX_TB, lat_finish, 0)


def mlstm_mixer(proj, i_bias, f_bias, norm_w):
    hp = jnp.stack([i_bias[0], i_bias[1], f_bias[0], f_bias[1]], axis=-1)[:, None, :]
    ctx_blk0 = N_LAT // CTX_LEN

    def seg_specs(col0, w):
        cb = col0 // w
        return [pl.BlockSpec((CTX_LEN, w), lambda b, h: (ctx_blk0 + b, cb + h)),
                pl.BlockSpec((SEQ, w), lambda b, h: (b, cb + h))]

    in_specs = (seg_specs(COL_ML_Q, ML_DQK) + seg_specs(COL_ML_K, ML_DQK) + seg_specs(COL_ML_V, ML_DV)
                + seg_specs(COL_ML_O, ML_DV)
                + [pl.BlockSpec((CTX_LEN, 128), lambda b, h: (ctx_blk0 + b, GATE_COLBLK)),
                   pl.BlockSpec((SEQ, 128), lambda b, h: (b, GATE_COLBLK)),
                   pl.BlockSpec((None, 1, 4), lambda b, h: (h, 0, 0)),
                   pl.BlockSpec((1, ML_DV), lambda b, h: (0, h))])
    seq_f32 = pltpu.VMEM((SEQ_ALL, ML_DV), F32)
    yl, yc = pl.pallas_call(
        _ml_kernel, grid=(BATCH, ML_H), in_specs=in_specs,
        out_specs=[pl.BlockSpec((SEQ, ML_DV), lambda b, h: (b, h)),
                   pl.BlockSpec((CTX_LEN, ML_DV), lambda b, h: (b, h))],
        out_shape=[jax.ShapeDtypeStruct((N_LAT, MIX_W), BF16), jax.ShapeDtypeStruct((N_CTX, MIX_W), BF16)],
        scratch_shapes=[seq_f32, seq_f32],
        compiler_params=_cparams(("parallel", "arbitrary")), name="mlstm",
    )(*([proj] * 10), hp, norm_w[None])
    return yl, yc


PROJ_TM = (N_LAT + N_CTX) // 8
PROJ_TN = 512
PROJ_PAD = -(-PROJ_W // PROJ_TN) * PROJ_TN
COL_S5, COL_DN_QKVZ, COL_ML_Q, COL_ML_K, COL_ML_V, COL_ML_O, COL_LRU_Y, COL_LRU_X = (
    0, 1024, 5120, 5632, 6144, 7168, 8192, 9216)
COL_DN_A, COL_DN_B, COL_ML_I, COL_ML_F = 10240, 10256, 10272, 10280


def permute_proj_weight(w):
    o = [0] + [int(v) for v in np.cumsum(PROJ_SIZES)]
    parts = [w[:, o[0]:o[5]], w[:, o[7]:o[11]], w[:, o[13]:o[15]], w[:, o[5]:o[7]], w[:, o[11]:o[13]]]
    return jnp.pad(jnp.concatenate(parts, axis=1).astype(BF16), ((0, 0), (0, PROJ_PAD - PROJ_W)))


def hybrid_mixer(proj, s5_p, dn_p, ml_p, lru_p):
    def ctx_seg(c0):
        return proj[N_LAT:, c0:c0 + MIX_W].reshape(BATCH, CTX_LEN, MIX_W)

    def to_row_major(t):
        return grid_transpose(t.reshape(BATCH * GRID_W, GRID_W, MIX_W), BATCH, 0, MIX_W, BF16).reshape(N_LAT, MIX_W)

    grid = proj.reshape((N_LAT + N_CTX) // GRID_W, GRID_W, PROJ_PAD)
    u_l = grid_transpose(grid, BATCH, COL_S5, MIX_W, F32).reshape(BATCH, SEQ, MIX_W)
    yx_l = grid_transpose(grid, BATCH, COL_LRU_Y, 2 * MIX_W, F32).reshape(BATCH, SEQ, 2 * MIX_W)
    a_c, a_l = s5_mixer(ctx_seg(COL_S5), u_l, *s5_p)
    d_c, d_l = lru_mixer(ctx_seg(COL_LRU_Y), ctx_seg(COL_LRU_X), yx_l, *lru_p)
    flat = lambda t_l, t_c: (to_row_major(t_l), t_c.reshape(N_CTX, MIX_W).astype(BF16))
    return [flat(a_l, a_c), deltanet_mixer(proj, *dn_p), mlstm_mixer(proj, *ml_p), flat(d_l, d_c)]


def kernel(x, c, ctx, c_ctx, w_ada, b_ada, norm1, norm2, norm_f, w_in, w_out, s5_lam_re, s5_lam_im, s5_log_step,
           s5_b_re, s5_b_im, s5_c_re, s5_c_im, s5_d, s5_w_glu, s5_b_glu, dn_conv, dn_a_log, dn_dt_bias, dn_norm,
           ml_i_bias, ml_f_bias, ml_norm, lru_conv_w, lru_conv_b, lru_w_a, lru_b_a, lru_w_i, lru_b_i, lru_lam,
           router_w, router_bias, moe_w1, moe_w3, moe_w2):
    d = D_MODEL
    xa = jnp.concatenate([x.reshape(N_LAT, d), ctx.reshape(N_CTX, d)], axis=0)
    cond = jnp.concatenate([jax.nn.silu(c), jax.nn.silu(c_ctx)[None], jnp.zeros((16 - BATCH - 1, d), F32)], axis=0)
    cond = cond.astype(BF16)
    router = (router_w.T, router_bias.reshape(N_EXPERTS, 1))

    s5_ops = jax.vmap(s5_chunk_operators)(s5_lam_re, s5_lam_im, s5_log_step, s5_b_re, s5_b_im, s5_c_re, s5_c_im)

    for l in range(DEPTH):
        last = l == DEPTH - 1
        mod = ada_matmul(l, cond, w_ada, b_ada[:, None, :], 1024)[:BATCH + 1].reshape(BATCH + 1, 6, 1, d)
        shift1, scale1, gate1, shift2, scale2, gate2 = (mod[:, k] for k in range(6))

        h1 = norm_mod(xa, norm1[l][None], scale1, shift1, SEQ, 256)
        proj = matmul(h1, permute_proj_weight(w_in[l]), PROJ_TM, PROJ_TN)
        slabs = hybrid_mixer(
            proj,
            (tuple(op[l] for op in s5_ops), s5_d[l], s5_w_glu[l], s5_b_glu[l]),
            (dn_conv[l], dn_a_log[l], dn_dt_bias[l], dn_norm[l]),
            (ml_i_bias[l], ml_f_bias[l], ml_norm[l]),
            (lru_conv_w[l], lru_conv_b[l], lru_w_a[l], lru_b_a[l], lru_w_i[l], lru_b_i[l], lru_lam[l]))
        if last:
            xa, mix = xa[:N_LAT], [s[0] for s in slabs]
        else:
            mix = [jnp.concatenate(s, axis=0) for s in slabs]
        out_tm = 1024 if xa.shape[0] % 1024 == 0 else 512
        xa = matmul_gated_residual(mix, w_out[l].astype(BF16), xa, gate1, SEQ, out_tm, 512)

        h2, idx, wsel = norm_mod(xa, norm2[l][None], scale2, shift2, SEQ, 256, router=router)
        y0, y1 = moe_ffn(l, h2, idx, wsel, moe_w1, moe_w3, moe_w2)
        xa = gated_add(xa, y0, y1, gate2, SEQ, 256)

    return final_norm(xa[:N_LAT], norm_f[None], 256).reshape(BATCH, SEQ, d)
```

```python
import functools
import math

import numpy as np
import jax
import jax.numpy as jnp
from jax import lax
from jax.experimental import pallas as pl
from jax.experimental.pallas import tpu as pltpu

D_MODEL = 4096
BATCH = 2
SEQ = 4096
DEPTH = 2
GRID_W = 64
CTX_LEN = 256
MIX_W = D_MODEL // 4
MIX_TOTAL = 4 * MIX_W
CHUNK = 64
CONV_W = 4
CONV_PAD_L = 2
CONV_PAD_R = 1
EPS = 1e-6
S5_CH = 16
S5_G = MIX_W // S5_CH
S5_P = 64
DN_HD = 128
DN_H = MIX_W // DN_HD
ML_H = 4
ML_DV = MIX_W // ML_H
ML_DQK = ML_DV // 2
LRU_BLOCKS = 8
LRU_BD = MIX_W // LRU_BLOCKS
LRU_C = 8.0
N_EXPERTS = 16
N_GROUPS = 4
E_PER_G = N_EXPERTS // N_GROUPS
TOP_K = 2
D_FF = D_MODEL // 4
PROJ_SIZES = (MIX_W, MIX_W, MIX_W, MIX_W, MIX_W, 2 * DN_H, 2 * DN_H, ML_H * ML_DQK, ML_H * ML_DQK, MIX_W, MIX_W,
              2 * ML_H, 2 * ML_H, MIX_W, MIX_W)
PROJ_W = sum(PROJ_SIZES)

N_LAT = BATCH * SEQ
N_CTX = BATCH * CTX_LEN
VMEM_LIMIT = 56 * 1024 * 1024

F32 = jnp.float32
BF16 = jnp.bfloat16


def _cparams(sem):
    return pltpu.CompilerParams(dimension_semantics=sem, vmem_limit_bytes=VMEM_LIMIT)


def _mm_kernel(a_ref, b_ref, o_ref):
    o_ref[...] = jnp.dot(a_ref[...], b_ref[...], preferred_element_type=F32).astype(o_ref.dtype)


def matmul(a, b, tm, tn, out_dtype=F32):
    m, k = a.shape
    n = b.shape[1]
    assert m % tm == 0 and n % tn == 0
    return pl.pallas_call(
        _mm_kernel,
        grid=(m // tm, n // tn),
        in_specs=[pl.BlockSpec((tm, k), lambda i, j: (i, 0)),
                  pl.BlockSpec((k, tn), lambda i, j: (0, j))],
        out_specs=pl.BlockSpec((tm, tn), lambda i, j: (i, j)),
        out_shape=jax.ShapeDtypeStruct((m, n), out_dtype),
        compiler_params=_cparams(("parallel", "arbitrary")),
        name="mm",
    )(a, b)


def _mm_resid_kernel(a0_ref, a1_ref, a2_ref, a3_ref, b_ref, r_ref, g_ref, o_ref):
    acc = None
    for s, a_ref in enumerate((a0_ref, a1_ref, a2_ref, a3_ref)):
        part = jnp.dot(a_ref[...], b_ref[s * MIX_W:(s + 1) * MIX_W, :], preferred_element_type=F32)
        acc = part if acc is None else acc + part
    o_ref[...] = r_ref[...] + g_ref[...] * acc


def matmul_gated_residual(a_slabs, b, resid, gate, rows_per_gate, tm, tn):
    m = a_slabs[0].shape[0]
    k, n = b.shape
    assert m % tm == 0 and n % tn == 0 and rows_per_gate % tm == 0 and k == 4 * MIX_W
    return pl.pallas_call(
        _mm_resid_kernel,
        grid=(m // tm, n // tn),
        in_specs=[pl.BlockSpec((tm, MIX_W), lambda i, j: (i, 0))] * 4 + [
                  pl.BlockSpec((k, tn), lambda i, j: (0, j)),
                  pl.BlockSpec((tm, tn), lambda i, j: (i, j)),
                  pl.BlockSpec((None, 1, tn), lambda i, j: ((i * tm) // rows_per_gate, 0, j))],
        out_specs=pl.BlockSpec((tm, tn), lambda i, j: (i, j)),
        out_shape=jax.ShapeDtypeStruct((m, n), F32),
        compiler_params=_cparams(("parallel", "arbitrary")),
        name="mm_resid",
    )(*a_slabs, b, resid, gate)


def _ada_kernel(a_ref, w_ref, b_ref, o_ref):
    o_ref[...] = jnp.dot(a_ref[...], w_ref[...].astype(BF16), preferred_element_type=F32) + b_ref[...]


def ada_matmul(layer, a, w, bias, tn):
    m, k = a.shape
    n = w.shape[2]
    return pl.pallas_call(
        _ada_kernel,
        grid=(n // tn,),
        in_specs=[pl.BlockSpec((m, k), lambda j: (0, 0)),
                  pl.BlockSpec((None, k, tn), lambda j: (layer, 0, j)),
                  pl.BlockSpec((None, 1, tn), lambda j: (layer, 0, j))],
        out_specs=pl.BlockSpec((m, tn), lambda j: (0, j)),
        out_shape=jax.ShapeDtypeStruct((m, n), F32),
        compiler_params=_cparams(("arbitrary",)),
        name="ada",
    )(a, w, bias)


def _modulated_norm(x, g, sc, sh):
    ms = jnp.mean(x * x, axis=-1, keepdims=True)
    y = x * lax.rsqrt(ms + EPS) * g
    return y * (1.0 + sc) + sh


def _norm_mod_kernel(x_ref, g_ref, sc_ref, sh_ref, o_ref):
    o_ref[...] = _modulated_norm(x_ref[...], g_ref[...], sc_ref[...], sh_ref[...]).astype(o_ref.dtype)


def _pair_top2_sum(v):
    a, b, c, d = v
    return jnp.maximum(jnp.maximum(jnp.maximum(a + b, a + c), jnp.maximum(a + d, b + c)),
                       jnp.maximum(b + d, c + d))


def _norm_mod_router_kernel(x_ref, g_ref, sc_ref, sh_ref, rwt_ref, rb_ref, o_ref, idx_ref, w_ref):
    h = _modulated_norm(x_ref[...], g_ref[...], sc_ref[...], sh_ref[...])
    o_ref[...] = h.astype(o_ref.dtype)
    logits = lax.dot_general(rwt_ref[...], h, (((1,), (1,)), ((), ())),
                             precision=lax.Precision.HIGHEST, preferred_element_type=F32)
    scores = jax.nn.sigmoid(logits)
    biased = scores + rb_ref[...]
    s = [scores[e:e + 1, :] for e in range(N_EXPERTS)]
    b = [biased[e:e + 1, :] for e in range(N_EXPERTS)]
    gs = [_pair_top2_sum(b[E_PER_G * g:E_PER_G * (g + 1)]) for g in range(N_GROUPS)]
    best, gsel = gs[0], jnp.zeros_like(gs[0], dtype=jnp.int32)
    for g in range(1, N_GROUPS):
        better = gs[g] > best
        gsel = jnp.where(better, g, gsel)
        best = jnp.where(better, gs[g], best)
    vb, vs = [], []
    for k in range(E_PER_G):
        bk, sk = b[k], s[k]
        for g in range(1, N_GROUPS):
            bk = jnp.where(gsel == g, b[E_PER_G * g + k], bk)
            sk = jnp.where(gsel == g, s[E_PER_G * g + k], sk)
        vb.append(bk)
        vs.append(sk)
    m1, i1, w1 = vb[0], jnp.zeros_like(gsel), vs[0]
    for k in range(1, E_PER_G):
        better = vb[k] > m1
        i1 = jnp.where(better, k, i1)
        w1 = jnp.where(better, vs[k], w1)
        m1 = jnp.where(better, vb[k], m1)
    m2, i2, w2, have = vb[0], jnp.zeros_like(gsel), vs[0], i1 != 0
    for k in range(1, E_PER_G):
        valid = i1 != k
        better = valid & (jnp.logical_not(have) | (vb[k] > m2))
        i2 = jnp.where(better, k, i2)
        w2 = jnp.where(better, vs[k], w2)
        m2 = jnp.where(better, vb[k], m2)
        have = have | valid
    tot = w1 + w2
    idx_ref[0:1, :] = gsel * E_PER_G + i1
    idx_ref[1:2, :] = gsel * E_PER_G + i2
    w_ref[0:1, :] = w1 / tot
    w_ref[1:2, :] = w2 / tot


def norm_mod(x, g, sc, sh, rows_per_mod, tm, router=None):
    m, d = x.shape
    assert m % tm == 0 and rows_per_mod % tm == 0
    mod_spec = pl.BlockSpec((None, 1, d), lambda i: ((i * tm) // rows_per_mod, 0, 0))
    in_specs = [pl.BlockSpec((tm, d), lambda i: (i, 0)), pl.BlockSpec((1, d), lambda i: (0, 0)), mod_spec, mod_spec]
    o_spec = pl.BlockSpec((tm, d), lambda i: (i, 0))
    o_shape = jax.ShapeDtypeStruct((m, d), BF16)
    if router is None:
        return pl.pallas_call(
            _norm_mod_kernel, grid=(m // tm,), in_specs=in_specs, out_specs=o_spec, out_shape=o_shape,
            compiler_params=_cparams(("parallel",)), name="norm_mod",
        )(x, g, sc, sh)
    rwt, rb = router
    in_specs += [pl.BlockSpec((N_EXPERTS, d), lambda i: (0, 0)), pl.BlockSpec((N_EXPERTS, 1), lambda i: (0, 0))]
    sel_spec = pl.BlockSpec((TOP_K, tm), lambda i: (0, i))
    return pl.pallas_call(
        _norm_mod_router_kernel, grid=(m // tm,), in_specs=in_specs,
        out_specs=[o_spec, sel_spec, sel_spec],
        out_shape=[o_shape, jax.ShapeDtypeStruct((TOP_K, m), jnp.int32), jax.ShapeDtypeStruct((TOP_K, m), F32)],
        compiler_params=_cparams(("parallel",)), name="norm_mod_router",
    )(x, g, sc, sh, rwt, rb)


def _gated_add_kernel(x_ref, y0_ref, y1_ref, g_ref, o_ref):
    o_ref[...] = x_ref[...] + g_ref[...] * (y0_ref[...] + y1_ref[...])


def gated_add(x, y0, y1, gate, rows_per_gate, tm):
    m, d = x.shape
    assert m % tm == 0 and rows_per_gate % tm == 0
    row = pl.BlockSpec((tm, d), lambda i: (i, 0))
    return pl.pallas_call(
        _gated_add_kernel, grid=(m // tm,),
        in_specs=[row, row, row, pl.BlockSpec((None, 1, d), lambda i: ((i * tm) // rows_per_gate, 0, 0))],
        out_specs=row, out_shape=jax.ShapeDtypeStruct((m, d), F32),
        compiler_params=_cparams(("parallel",)), name="gated_add",
    )(x, y0, y1, gate)


def _final_norm_kernel(x_ref, g_ref, o_ref):
    x = x_ref[...]
    ms = jnp.mean(x * x, axis=-1, keepdims=True)
    o_ref[...] = x * lax.rsqrt(ms + EPS) * g_ref[...]


def final_norm(x, g, tm):
    m, d = x.shape
    return pl.pallas_call(
        _final_norm_kernel, grid=(m // tm,),
        in_specs=[pl.BlockSpec((tm, d), lambda i: (i, 0)), pl.BlockSpec((1, d), lambda i: (0, 0))],
        out_specs=pl.BlockSpec((tm, d), lambda i: (i, 0)),
        out_shape=jax.ShapeDtypeStruct((m, d), F32),
        compiler_params=_cparams(("parallel",)), name="final_norm",
    )(x, g)


MOE_TM = 512
MOE_KC = 1024
MOE_FC = 256
MOE_NK = D_MODEL // MOE_KC
MOE_NF = D_FF // MOE_FC
MOE_PARTS = 2


def _moe_kernel(te_ref, nu_ref, x_ref, w1_ref, w3_ref, w2_ref, g_ref, *rest, tile0, has_prev):
    o_ref, hg_s, hu_s, act_s = rest[1:] if has_prev else rest
    t = pl.program_id(0)
    s = pl.program_id(1)
    used = t + tile0 < nu_ref[0]

    @pl.when(used & (s < MOE_NK))
    def _():
        x = x_ref[...]
        pg = jnp.dot(x, w1_ref[...].astype(BF16), preferred_element_type=F32)
        pu = jnp.dot(x, w3_ref[...].astype(BF16), preferred_element_type=F32)

        @pl.when(s == 0)
        def _():
            hg_s[...] = pg
            hu_s[...] = pu

        @pl.when(s > 0)
        def _():
            hg_s[...] += pg
            hu_s[...] += pu

        @pl.when(s == MOE_NK - 1)
        def _():
            hg = hg_s[...]
            act = ((hg * jax.nn.sigmoid(hg)) * hu_s[...] * g_ref[...]).astype(BF16)
            for f in range(MOE_NF):
                act_s[f] = act[:, f * MOE_FC:(f + 1) * MOE_FC]

    @pl.when(used & (s >= MOE_NK))
    def _():
        y = jnp.dot(act_s[s - MOE_NK], w2_ref[...].astype(BF16), preferred_element_type=F32)

        @pl.when(s == MOE_NK)
        def _():
            o_ref[...] = y

        @pl.when(s > MOE_NK)
        def _():
            o_ref[...] += y

    @pl.when(jnp.logical_not(used) & (s == 0))
    def _():
        o_ref[...] = jnp.zeros_like(o_ref)


def moe_grouped(layer, tile_expert, n_used, xs, w1, w3, w2, gs, tile0, ys_prev):
    rp, d = xs.shape
    r = gs.shape[0]

    def live(t, nu):
        return t + tile0 < nu[0]

    def kidx(t, s, nu):
        return jnp.where(live(t, nu), jnp.minimum(s, MOE_NK - 1), MOE_NK - 1)

    def fidx(t, s, nu):
        return jnp.where(live(t, nu), jnp.maximum(s - MOE_NK, 0), MOE_NF - 1)

    in_specs = [pl.BlockSpec((MOE_TM, MOE_KC), lambda t, s, te, nu: (t, kidx(t, s, nu))),
                pl.BlockSpec((None, None, MOE_KC, D_FF),
                             lambda t, s, te, nu: (layer, te[t + tile0], kidx(t, s, nu), 0)),
                pl.BlockSpec((None, None, MOE_KC, D_FF),
                             lambda t, s, te, nu: (layer, te[t + tile0], kidx(t, s, nu), 0)),
                pl.BlockSpec((None, None, MOE_FC, d), lambda t, s, te, nu: (layer, te[t + tile0], fidx(t, s, nu), 0)),
                pl.BlockSpec((MOE_TM, 1), lambda t, s, te, nu: (t + tile0, 0))]
    args = [tile_expert, n_used, xs, w1, w3, w2, gs]
    aliases = {}
    if ys_prev is not None:
        in_specs.append(pl.BlockSpec(memory_space=pl.ANY))
        aliases = {len(args): 0}
        args.append(ys_prev)
    grid_spec = pltpu.PrefetchScalarGridSpec(
        num_scalar_prefetch=2,
        grid=(rp // MOE_TM, MOE_NK + MOE_NF),
        in_specs=in_specs,
        out_specs=pl.BlockSpec((MOE_TM, d), lambda t, s, te, nu: (t + tile0, 0)),
        scratch_shapes=[pltpu.VMEM((MOE_TM, D_FF), F32), pltpu.VMEM((MOE_TM, D_FF), F32),
                        pltpu.VMEM((MOE_NF, MOE_TM, MOE_FC), BF16)],
    )
    return pl.pallas_call(
        functools.partial(_moe_kernel, tile0=tile0, has_prev=ys_prev is not None), grid_spec=grid_spec,
        out_shape=jax.ShapeDtypeStruct((r, d), F32), input_output_aliases=aliases,
        compiler_params=_cparams(("arbitrary", "arbitrary")), name="moe",
    )(*args)


CUMSUM_BLK = 256


def _onehot_cumsum(onehot):
    m, e = onehot.shape
    x = onehot.reshape(m // CUMSUM_BLK, CUMSUM_BLK, e)
    tri = jnp.tril(jnp.ones((CUMSUM_BLK, CUMSUM_BLK), F32))
    within = jnp.einsum('ij,bje->bie', tri, x, precision=lax.Precision.HIGHEST)
    total = within[:, -1, :]
    return (within + (jnp.cumsum(total, axis=0) - total)[:, None, :]).reshape(m, e)


def moe_dispatch(idx, wsel):
    n = idx.shape[1]
    e_flat = idx.reshape(-1)
    onehot = (e_flat[:, None] == jnp.arange(N_EXPERTS, dtype=jnp.int32)[None, :]).astype(F32)
    csum = _onehot_cumsum(onehot)
    counts = csum[-1].astype(jnp.int32)
    rank = jnp.sum(onehot * csum, axis=1).astype(jnp.int32) - 1
    padded = ((counts + MOE_TM - 1) // MOE_TM) * MOE_TM
    seg_end = jnp.cumsum(padded)
    seg_start = seg_end - padded
    dest = jnp.sum(onehot * seg_start.astype(F32)[None, :], axis=1).astype(jnp.int32) + rank
    rows = TOP_K * n + N_EXPERTS * MOE_TM
    pair = jnp.zeros((rows,), jnp.int32).at[dest].set(jnp.arange(TOP_K * n, dtype=jnp.int32) + 1) - 1
    src_tok = jnp.maximum(pair, 0) % n
    gate = jnp.where(pair >= 0, wsel.reshape(-1).at[jnp.maximum(pair, 0)].get(mode="promise_in_bounds"), 0.0)
    n_tiles = rows // MOE_TM
    n_used = (seg_end[-1] // MOE_TM).astype(jnp.int32)
    tile_start = jnp.arange(n_tiles, dtype=jnp.int32) * MOE_TM
    te = jnp.minimum(jnp.searchsorted(seg_end, tile_start, side='right'), N_EXPERTS - 1).astype(jnp.int32)
    te = jnp.where(jnp.arange(n_tiles) < n_used, te, te[jnp.maximum(n_used - 1, 0)])
    return dest.reshape(TOP_K, n), src_tok, gate[:, None], te, n_used.reshape(1)


def _take_rows(a, rows):
    return a.at[rows].get(mode="promise_in_bounds")


def moe_ffn(layer, hb, idx, wsel, w1, w3, w2):
    dest, src_tok, gate, te, n_used = moe_dispatch(idx, wsel)
    n_tiles = src_tok.shape[0] // MOE_TM
    assert n_tiles % MOE_PARTS == 0
    part_tiles = n_tiles // MOE_PARTS
    ys = None
    for p in range(MOE_PARTS):
        rows = slice(p * part_tiles * MOE_TM, (p + 1) * part_tiles * MOE_TM)
        ys = moe_grouped(layer, te, n_used, _take_rows(hb, src_tok[rows]), w1, w3, w2, gate,
                         p * part_tiles, ys)
    return _take_rows(ys, dest[0]), _take_rows(ys, dest[1])


SEQ_ALL = CTX_LEN + SEQ
S5_T = 16
S5_NC = SEQ_ALL // S5_T
S5_NCC = CTX_LEN // S5_T
S5_W = S5_T * S5_CH


def _gelu_tanh(x):
    return x * (0.5 * (1.0 + jnp.tanh(math.sqrt(2.0 / math.pi) * (x + 0.044715 * (x * x * x)))))


def s5_chunk_operators(lam_re, lam_im, log_step, b_re, b_im, c_re, c_im):
    hp = lax.Precision.HIGHEST
    jj = jnp.arange(S5_T)[:, None]
    ii = jnp.arange(S5_T)[None, :]
    ms, es, fs, a16 = [], [], [], []
    for d in range(2):
        dt = jnp.exp(log_step[d])[:, None]
        lr, li = lam_re[d], lam_im[d]
        er = jnp.exp(lr * dt)
        abr, abi = er * jnp.cos(li * dt), er * jnp.sin(li * dt)
        nr, ni = abr - 1.0, abi
        den = lr * lr + li * li
        fr = (nr * lr + ni * li) / den
        fi = (ni * lr - nr * li) / den
        bbr = fr[..., None] * b_re[d] - fi[..., None] * b_im[d]
        bbi = fr[..., None] * b_im[d] + fi[..., None] * b_re[d]
        k = jnp.arange(S5_T + 1, dtype=F32)[:, None, None]
        mag, ang = jnp.exp(k * (lr * dt)), k * (li * dt)
        pr, pi = mag * jnp.cos(ang), mag * jnp.sin(ang)
        pbr = pr[..., None] * bbr - pi[..., None] * bbi
        pbi = pr[..., None] * bbi + pi[..., None] * bbr
        cr, ci = c_re[d], c_im[d]
        kk = (jnp.einsum('gop,lgpc->lgoc', cr, pbr[:S5_T], precision=hp)
              - jnp.einsum('gop,lgpc->lgoc', ci, pbi[:S5_T], precision=hp))
        cpr = cr[None] * pr[:, :, None, :] - ci[None] * pi[:, :, None, :]
        cpi = -(cr[None] * pi[:, :, None, :] + ci[None] * pr[:, :, None, :])
        if d == 0:
            lag, mask = ii - jj, ii >= jj
            e_pow = S5_T - 1 - jnp.arange(S5_T)
            f_pow = jnp.arange(S5_T) + 1
        else:
            lag, mask = jj - ii, jj >= ii
            e_pow = jnp.arange(S5_T)
            f_pow = S5_T - jnp.arange(S5_T)
        mm = jnp.where(mask[:, :, None, None, None], kk[jnp.clip(lag, 0, S5_T - 1)], 0.0)
        ms.append(mm.transpose(2, 0, 4, 1, 3).reshape(S5_G, S5_W, S5_W))
        for pb in (pbr, pbi):
            es.append(pb[e_pow].transpose(1, 0, 3, 2).reshape(S5_G, S5_W, S5_P))
        for cp in (cpr, cpi):
            fs.append(cp[f_pow].transpose(1, 3, 0, 2).reshape(S5_G, S5_P, S5_W))
        a16 += [pr[S5_T][:, None, :], pi[S5_T][:, None, :]]
    return (ms[0] + ms[1]).astype(BF16), jnp.stack(es).astype(BF16), jnp.stack(fs).astype(BF16), jnp.stack(a16)


def _s5_kernel(u_ref, m_ref, e_ref, f_ref, a_ref, o_ref, e_scr, s_scr):
    u = u_ref[...]
    for k in range(4):
        e_scr[k] = jnp.dot(u, e_ref[k], preferred_element_type=F32)
    arf, aif, arb, aib = (a_ref[k] for k in range(4))

    def segment(start, n, carry):
        def body(k, carry):
            out = []
            for b in range(BATCH):
                sfr, sfi, sbr, sbi = carry[4 * b:4 * b + 4]
                rf = b * S5_NC + start + k
                rb = b * S5_NC + start + n - 1 - k
                s_scr[0, pl.ds(rf, 1), :] = sfr
                s_scr[1, pl.ds(rf, 1), :] = sfi
                s_scr[2, pl.ds(rb, 1), :] = sbr
                s_scr[3, pl.ds(rb, 1), :] = sbi
                nfr = arf * sfr - aif * sfi + e_scr[0, pl.ds(rf, 1), :]
                nfi = arf * sfi + aif * sfr + e_scr[1, pl.ds(rf, 1), :]
                nbr = arb * sbr - aib * sbi + e_scr[2, pl.ds(rb, 1), :]
                nbi = arb * sbi + aib * sbr + e_scr[3, pl.ds(rb, 1), :]
                out += [nfr, nfi, nbr, nbi]
            return tuple(out)
        return lax.fori_loop(0, n, body, carry, unroll=4)

    z = jnp.zeros((1, S5_P), F32)
    carry = segment(0, S5_NCC, (z,) * (4 * BATCH))
    segment(S5_NCC, S5_NC - S5_NCC, carry)
    y = jnp.dot(u, m_ref[...], preferred_element_type=F32)
    for k in range(4):
        y += jnp.dot(s_scr[k].astype(BF16), f_ref[k], preferred_element_type=F32)
    o_ref[...] = y


def s5_scan(u, m, e, f, a16):
    g, r, w = u.shape
    return pl.pallas_call(
        _s5_kernel, grid=(g,),
        in_specs=[pl.BlockSpec((None, r, w), lambda i: (i, 0, 0)),
                  pl.BlockSpec((None, w, w), lambda i: (i, 0, 0)),
                  pl.BlockSpec((4, None, w, S5_P), lambda i: (0, i, 0, 0)),
                  pl.BlockSpec((4, None, S5_P, w), lambda i: (0, i, 0, 0)),
                  pl.BlockSpec((4, None, 1, S5_P), lambda i: (0, i, 0, 0))],
        out_specs=pl.BlockSpec((None, r, w), lambda i: (i, 0, 0)),
        out_shape=jax.ShapeDtypeStruct((g, r, w), F32),
        scratch_shapes=[pltpu.VMEM((4, r, S5_P), F32), pltpu.VMEM((4, r, S5_P), F32)],
        compiler_params=_cparams(("parallel",)), name="s5_scan",
    )(u, m, e, f, a16)


def _s5_out_kernel(y_ref, u_ref, d_ref, w_ref, b_ref, o_ref):
    g = _gelu_tanh(y_ref[...] + d_ref[...] * u_ref[...])
    z = jnp.dot(g.astype(BF16), w_ref[...], preferred_element_type=F32) + b_ref[...]
    o_ref[...] = (g * jax.nn.sigmoid(z)).astype(o_ref.dtype)


def s5_out(y, u, d_skip, w_glu, b_glu, tm):
    m, c = y.shape
    row = pl.BlockSpec((tm, c), lambda i: (i, 0))
    vec = pl.BlockSpec((1, c), lambda i: (0, 0))
    return pl.pallas_call(
        _s5_out_kernel, grid=(m // tm,),
        in_specs=[row, row, vec, pl.BlockSpec((c, c), lambda i: (0, 0)), vec],
        out_specs=row, out_shape=jax.ShapeDtypeStruct((m, c), F32),
        compiler_params=_cparams(("parallel",)), name="s5_out",
    )(y, u, d_skip, w_glu, b_glu)


S5_LG = 128 // S5_CH


def _lane_window(lane, k):
    return (lane >= k * S5_CH) & (lane < (k + 1) * S5_CH)


def _s5_pack_kernel(x_ref, o_ref):
    lane = lax.broadcasted_iota(jnp.int32, (x_ref.shape[0], 128), 1)
    tok = [x_ref[:, t, :] for t in range(S5_T)]
    for g in range(S5_LG):
        for h in range(S5_W // 128):
            acc = jnp.zeros(lane.shape, F32)
            for k in range(S5_LG):
                shift = ((k - g) * S5_CH) % 128
                src = tok[h * S5_LG + k]
                acc = jnp.where(_lane_window(lane, k), pltpu.roll(src, shift, axis=1) if shift else src, acc)
            o_ref[g, :, h * 128:(h + 1) * 128] = acc.astype(o_ref.dtype)


def _s5_unpack_kernel(y_ref, o_ref):
    lane = lax.broadcasted_iota(jnp.int32, (y_ref.shape[1], 128), 1)
    for t in range(S5_T):
        h, k = divmod(t, S5_LG)
        acc = jnp.zeros(lane.shape, F32)
        for g in range(S5_LG):
            shift = ((g - k) * S5_CH) % 128
            src = y_ref[g, :, h * 128:(h + 1) * 128]
            acc = jnp.where(_lane_window(lane, g), pltpu.roll(src, shift, axis=1) if shift else src, acc)
        o_ref[:, t, :] = acc


def s5_pack(u):
    n = u.shape[0]
    cb = n // 2
    return pl.pallas_call(
        _s5_pack_kernel, grid=(n // cb, MIX_W // 128),
        in_specs=[pl.BlockSpec((cb, S5_T, 128), lambda i, j: (i, 0, j))],
        out_specs=pl.BlockSpec((S5_LG, cb, S5_W), lambda i, j: (j, i, 0)),
        out_shape=jax.ShapeDtypeStruct((S5_G, n, S5_W), BF16),
        compiler_params=_cparams(("parallel", "parallel")), name="s5_pack",
    )(u)


def s5_unpack(y):
    n = y.shape[1]
    cb = n // 2
    return pl.pallas_call(
        _s5_unpack_kernel, grid=(n // cb, MIX_W // 128),
        in_specs=[pl.BlockSpec((S5_LG, cb, S5_W), lambda i, j: (j, i, 0))],
        out_specs=pl.BlockSpec((cb, S5_T, 128), lambda i, j: (i, 0, j)),
        out_shape=jax.ShapeDtypeStruct((n, S5_T, MIX_W), F32),
        compiler_params=_cparams(("parallel", "parallel")), name="s5_unpack",
    )(y)


def s5_mixer(u_c, u_l, operators, d_skip, w_glu, b_glu):
    bs = u_c.shape[0]
    u_all = jnp.concatenate([u_c, u_l], axis=1)
    uc = s5_pack(u_all.reshape(bs * S5_NC, S5_T, MIX_W))
    y = s5_scan(uc, *operators)
    y = s5_unpack(y).reshape(bs * SEQ_ALL, MIX_W)
    out = s5_out(y, u_all.reshape(bs * SEQ_ALL, MIX_W), d_skip[None], w_glu.astype(BF16), b_glu[None], 544)
    out = out.reshape(bs, SEQ_ALL, MIX_W)
    return out[:, :CTX_LEN], out[:, CTX_LEN:]


LRU_CB = 256
LRU_TB = 256
LRU_HALO = 8
LRU_ROWS = SEQ_ALL + 3 * LRU_HALO


def _lru_kernel(xc_ref, xl_ref, yc_ref, yl_ref, cw_ref, cb_ref, wa_ref, ba_ref, wi_ref, bi_ref, sp_ref,
                oc_ref, ol_ref, xpad, a_f, b_f, a_b, b_b):
    halo = jnp.zeros((LRU_HALO, LRU_CB), F32)
    lat0 = CTX_LEN + 2 * LRU_HALO
    xpad[0:LRU_HALO] = halo
    xpad[LRU_HALO:LRU_HALO + CTX_LEN] = xc_ref[...]
    xpad[LRU_HALO + CTX_LEN:lat0] = halo
    xpad[lat0:lat0 + SEQ] = xl_ref[...]
    xpad[lat0 + SEQ:LRU_ROWS] = halo
    cw = [cw_ref[j:j + 1, :] for j in range(CONV_W)]
    cb = cb_ref[...]

    def gates(p0, r0):
        ext = xpad[pl.ds(p0 - LRU_HALO, LRU_TB + 2 * LRU_HALO), :]
        xb = cb
        for j in range(CONV_W):
            s0 = LRU_HALO - CONV_PAD_L + j
            xb = xb + cw[j] * ext[s0:s0 + LRU_TB]
        for k in range(LRU_CB // LRU_BD):
            lanes = slice(k * LRU_BD, (k + 1) * LRU_BD)
            xk = xb[:, lanes]
            xkb = xk.astype(BF16)
            for d, (a_s, b_s) in enumerate(((a_f, b_f), (a_b, b_b))):
                r = jax.nn.sigmoid(jnp.dot(xkb, wa_ref[d, k], preferred_element_type=F32) + ba_ref[d][:, lanes])
                i = jax.nn.sigmoid(jnp.dot(xkb, wi_ref[d, k], preferred_element_type=F32) + bi_ref[d][:, lanes])
                log_a = -LRU_C * r * sp_ref[d][:, lanes]
                t = jnp.tanh(log_a)
                a_s[pl.ds(r0, LRU_TB), lanes] = jnp.exp(log_a)
                b_s[pl.ds(r0, LRU_TB), lanes] = jnp.sqrt(-2.0 * t / (1.0 - t)) * (i * xk)

    gates(LRU_HALO, 0)

    def lat_gates(k, _):
        gates(pl.multiple_of(lat0 + k * LRU_TB, 8), pl.multiple_of(CTX_LEN + k * LRU_TB, 8))
        return 0
    lax.fori_loop(0, SEQ // LRU_TB, lat_gates, 0)

    def segment(start, n, carry):
        def body(t, carry):
            hf, hb = carry
            rf = start + t
            rb = start + n - 1 - t
            hf = a_f[pl.ds(rf, 1), :] * hf + b_f[pl.ds(rf, 1), :]
            b_f[pl.ds(rf, 1), :] = hf
            hb = a_b[pl.ds(rb, 1), :] * hb + b_b[pl.ds(rb, 1), :]
            b_b[pl.ds(rb, 1), :] = hb
            return hf, hb
        return lax.fori_loop(0, n, body, carry, unroll=8)

    z = jnp.zeros((1, LRU_CB), F32)
    carry = segment(0, CTX_LEN, (z, z))
    segment(CTX_LEN, SEQ, carry)

    oc_ref[...] = (_gelu_tanh(yc_ref[...]) * (b_f[0:CTX_LEN] + b_b[0:CTX_LEN])).astype(oc_ref.dtype)

    def lat_out(k, _):
        r0 = pl.multiple_of(k * LRU_TB, 8)
        rows = pl.ds(pl.multiple_of(CTX_LEN + k * LRU_TB, 8), LRU_TB)
        h = b_f[rows, :] + b_b[rows, :]
        ol_ref[pl.ds(r0, LRU_TB), :] = (_gelu_tanh(yl_ref[pl.ds(r0, LRU_TB), :]) * h).astype(ol_ref.dtype)
        return 0
    lax.fori_loop(0, SEQ // LRU_TB, lat_out, 0)


def lru_mixer(y_c, x_c, yx_l, conv_w, conv_b, w_a, b_a, w_i, b_i, lam):
    bs = x_c.shape[0]
    nb = LRU_CB // LRU_BD
    sp = jax.nn.softplus(-lam)[:, None, :]
    ctx_spec = pl.BlockSpec((None, CTX_LEN, LRU_CB), lambda b, c: (b, 0, c))
    lat_spec = pl.BlockSpec((None, SEQ, LRU_CB), lambda b, c: (b, 0, c))
    lat_x_spec = pl.BlockSpec((None, SEQ, LRU_CB), lambda b, c: (b, 0, MIX_W // LRU_CB + c))
    vec2 = pl.BlockSpec((2, 1, LRU_CB), lambda b, c: (0, 0, c))
    wspec = pl.BlockSpec((2, nb, LRU_BD, LRU_BD), lambda b, c: (0, c, 0, 0))
    scan_buf = pltpu.VMEM((SEQ_ALL, LRU_CB), F32)
    return pl.pallas_call(
        _lru_kernel, grid=(bs, MIX_W // LRU_CB),
        in_specs=[ctx_spec, lat_x_spec, ctx_spec, lat_spec,
                  pl.BlockSpec((CONV_W, LRU_CB), lambda b, c: (0, c)),
                  pl.BlockSpec((1, LRU_CB), lambda b, c: (0, c)),
                  wspec, vec2, wspec, vec2, vec2],
        out_specs=[ctx_spec, lat_spec],
        out_shape=[jax.ShapeDtypeStruct((bs, CTX_LEN, MIX_W), F32), jax.ShapeDtypeStruct((bs, SEQ, MIX_W), F32)],
        scratch_shapes=[pltpu.VMEM((LRU_ROWS, LRU_CB), F32), scan_buf, scan_buf, scan_buf, scan_buf],
        compiler_params=_cparams(("parallel", "parallel")), name="lru",
    )(x_c, yx_l, y_c, yx_l, conv_w, conv_b[None], w_a.astype(BF16), b_a[:, None, :], w_i.astype(BF16),
      b_i[:, None, :], sp)


GT_W = 8
GT_C = 1024


def _grid_transpose_kernel(x_ref, o_ref):
    for j in range(GT_W):
        o_ref[j] = x_ref[:, j, :].astype(o_ref.dtype)


def grid_transpose(src, batch, col0, width, out_dtype):
    assert col0 % GT_C == 0 and width % GT_C == 0
    return pl.pallas_call(
        _grid_transpose_kernel, grid=(batch, GRID_W // GT_W, width // GT_C),
        in_specs=[pl.BlockSpec((GRID_W, GT_W, GT_C), lambda b, j, c: (b, j, col0 // GT_C + c))],
        out_specs=pl.BlockSpec((None, GT_W, GRID_W, GT_C), lambda b, j, c: (b, j, 0, c)),
        out_shape=jax.ShapeDtypeStruct((batch, GRID_W, GRID_W, width), out_dtype),
        compiler_params=_cparams(("parallel", "parallel", "parallel")), name="grid_transpose",
    )(src)


N_CHUNKS = SEQ_ALL // CHUNK
N_CCH = CTX_LEN // CHUNK
GATE_COLBLK = 80
MIX_TB = 256


def _dot(a, b):
    return jnp.dot(a, b, preferred_element_type=F32)


def _dot_nt(a, b):
    return lax.dot_general(a, b, (((1,), (1,)), ((), ())), preferred_element_type=F32)


def _dot_tn(a, b):
    return lax.dot_general(a, b, (((0,), (0,)), ((), ())), preferred_element_type=F32)


def _split2(a):
    hi = a.astype(BF16)
    return hi, (a - hi.astype(F32)).astype(BF16)


def _dot3(a, b):
    ah, al = _split2(a)
    bh, bl = _split2(b)
    return _dot(ah, bh) + (_dot(ah, bl) + _dot(al, bh))


def _dot_mask(m, x):
    hi = x.astype(BF16)
    r1 = x - hi.astype(F32)
    mid = r1.astype(BF16)
    lo = (r1 - mid.astype(F32)).astype(BF16)
    return _dot(m, hi) + (_dot(m, mid) + _dot(m, lo))


def _chunk_masks(backward):
    i = lax.broadcasted_iota(jnp.int32, (CHUNK, CHUNK), 0)
    j = lax.broadcasted_iota(jnp.int32, (CHUNK, CHUNK), 1)
    if backward:
        i, j = j, i
    tri, strict = j <= i, j < i
    return tri, strict, tri.astype(BF16), jnp.where(strict, 1.0, 0.0).astype(F32)


def _lane_pick(blk, lane):
    lanes = lax.broadcasted_iota(jnp.int32, blk.shape, 1)
    return jnp.sum(jnp.where(lanes == lane, blk, 0.0), axis=-1, keepdims=True)


def _softplus(x):
    return jnp.maximum(x, 0.0) + jnp.log1p(jnp.exp(-jnp.abs(x)))


def _silu(x):
    return x * jax.nn.sigmoid(x)


def _bidir_chunks(step, carry, n_ctx, n_all):
    def segment(start, n, carry):
        def body(t, carry):
            return step(start + t, start + n - 1 - t, carry)
        return lax.fori_loop(0, n, body, carry)
    carry = segment(0, n_ctx, carry)
    return segment(n_ctx, n_all - n_ctx, carry)


DN_HALO = 8
DN_ROWS = SEQ_ALL + 3 * DN_HALO
DN_C = 64
DN_GROUP = 256
DN_GC = DN_GROUP // (2 * DN_C)
DN_NCH = SEQ_ALL // DN_C
DN_NCC = CTX_LEN // DN_C
DN_LOCKSTEP = 4


def _dn_kernel(qc_ref, ql_ref, kc_ref, kl_ref, vc_ref, vl_ref, zc_ref, zl_ref, gc_ref, gl_ref, hp_ref,
               cwq_ref, cwk_ref, cwv_ref, nw_ref, ol_ref, oc_ref,
               xpad, qn, kn, vn, u_s, w_s, qg_s, kd_s, at_s, dec_s):
    h = pl.program_id(1)
    halo = jnp.zeros((DN_HALO, DN_HD), F32)
    lat0 = CTX_LEN + 2 * DN_HALO

    def conv_into(xc_ref, xl_ref, cw_ref, dst, post):
        xpad[0:DN_HALO] = halo
        xpad[DN_HALO:DN_HALO + CTX_LEN] = xc_ref[...]
        xpad[DN_HALO + CTX_LEN:lat0] = halo
        xpad[lat0:lat0 + SEQ] = xl_ref[...]
        xpad[lat0 + SEQ:DN_ROWS] = halo
        cw = [cw_ref[j:j + 1, :] for j in range(CONV_W)]

        def block(p0, r0):
            ext = xpad[pl.ds(p0 - DN_HALO, MIX_TB + 2 * DN_HALO), :]
            s0 = DN_HALO - CONV_PAD_L
            acc = cw[0] * ext[s0:s0 + MIX_TB]
            for j in range(1, CONV_W):
                acc = acc + cw[j] * ext[s0 + j:s0 + j + MIX_TB]
            dst[pl.ds(r0, MIX_TB), :] = post(_silu(acc))

        block(DN_HALO, 0)

        def lat_block(k, _):
            block(pl.multiple_of(lat0 + k * MIX_TB, 8), pl.multiple_of(CTX_LEN + k * MIX_TB, 8))
            return 0
        lax.fori_loop(0, SEQ // MIX_TB, lat_block, 0)

    def l2n(scale):
        return lambda t: t * (lax.rsqrt(jnp.sum(t * t, axis=-1, keepdims=True) + EPS) * scale)

    conv_into(qc_ref, ql_ref, cwq_ref, qn, l2n(DN_HD ** -0.5))
    conv_into(kc_ref, kl_ref, cwk_ref, kn, l2n(1.0))
    conv_into(vc_ref, vl_ref, cwv_ref, vn, lambda t: t)

    gi = lax.broadcasted_iota(jnp.int32, (DN_GROUP, DN_GROUP), 0)
    gj = lax.broadcasted_iota(jnp.int32, (DN_GROUP, DN_GROUP), 1)
    same = (gi // DN_C) == (gj // DN_C)
    blk16 = (gi // 16) == (gj // 16)
    bwd = ((gi // DN_C) % 2) == 1
    fwd = jnp.logical_not(bwd)
    li, lj = gi % DN_C, gj % DN_C
    tri = same & ((fwd & (lj <= li)) | (bwd & (lj >= li)))
    strict = same & ((fwd & (lj < li)) | (bwd & (lj > li)))
    eye = jnp.where(gi == gj, 1.0, 0.0).astype(F32)
    m16 = jnp.where(tri, 1.0, 0.0).astype(BF16)

    def both_dirs(xf, xb):
        parts = []
        for c in range(DN_GC):
            parts += [xf[c * DN_C:(c + 1) * DN_C], xb[c * DN_C:(c + 1) * DN_C]]
        return jnp.concatenate(parts, axis=0)

    def prep(ts, g_ref, row0):
        grows = DN_GC * DN_C
        n = range(len(ts))
        r0 = [pl.multiple_of(t * grows, grows) for t in ts]
        gblk = [g_ref[pl.ds(pl.multiple_of(t * grows - row0, grows), grows), :] for t in ts]
        q = [both_dirs(*[qn[pl.ds(r0[i], grows), :]] * 2) for i in n]
        k = [both_dirs(*[kn[pl.ds(r0[i], grows), :]] * 2) for i in n]
        v = [both_dirs(*[vn[pl.ds(r0[i], grows), :]] * 2) for i in n]
        g_col = [both_dirs(*[hp_ref[:, d:d + 1] * _softplus(_lane_pick(gblk[i], d * DN_H + h)
                                                            + hp_ref[:, 2 + d:3 + d]) for d in range(2)]) for i in n]
        beta = [both_dirs(*[jax.nn.sigmoid(_lane_pick(gblk[i], 2 * DN_H + d * DN_H + h)) for d in range(2)])
                for i in n]
        k16 = [k[i].astype(BF16) for i in n]
        kb = [k[i] * beta[i] for i in n]
        gb = [jnp.broadcast_to(g_col[i], (DN_GROUP, DN_HD)) for i in n]
        kk = [_dot_nt(kb[i].astype(BF16), k16[i]) for i in n]
        gcum = [_dot_mask(m16, gb[i]) for i in n]
        tot = [jnp.broadcast_to(jnp.sum(gb[i].reshape(DN_GROUP // DN_C, DN_C, DN_HD), axis=1, keepdims=True),
                                (DN_GROUP // DN_C, DN_C, DN_HD)).reshape(DN_GROUP, DN_HD) for i in n]
        qk = [_dot_nt(q[i].astype(BF16), k16[i]) for i in n]
        dlt = [gcum[i][:, 0:1] - gcum[i].T[0:1, :] for i in n]
        decay = [jnp.where(tri, jnp.exp(jnp.where(tri, dlt[i], 0.0)), 0.0) for i in n]
        lower = [jnp.where(strict, kk[i] * decay[i], 0.0) for i in n]
        x = [-jnp.where(blk16, lower[i], 0.0) for i in n]
        tinv = [eye + x[i] for i in n]
        for _ in range(3):
            x = [_dot3(x[i], x[i]) for i in n]
            tinv = [tinv[i] + _dot3(tinv[i], x[i]) for i in n]
        size = 16
        while size < DN_C:
            inner, outer = (gi // size) == (gj // size), (gi // (2 * size)) == (gj // (2 * size))
            off = [_dot3(jnp.where(outer & jnp.logical_not(inner), lower[i], 0.0), tinv[i]) for i in n]
            tinv = [tinv[i] - _dot3(tinv[i], off[i]) for i in n]
            size *= 2
        eg = [jnp.exp(gcum[i]) for i in n]
        sol = [_dot3(tinv[i], jnp.concatenate([v[i] * beta[i], kb[i] * eg[i]], axis=1)) for i in n]
        for i in n:
            attn = (qk[i] * decay[i]).astype(BF16)
            w16 = sol[i][:, DN_HD:].astype(BF16)
            qg16 = (q[i] * eg[i]).astype(BF16)
            kd16 = (k[i] * jnp.exp(tot[i] - gcum[i])).astype(BF16)
            dec = jnp.exp(tot[i])
            for p in range(DN_GROUP // DN_C):
                d, blk = p % 2, slice(p * DN_C, (p + 1) * DN_C)
                dst = pl.ds(r0[i] + (p // 2) * DN_C, DN_C)
                u_s[d, dst, :] = sol[i][blk, :DN_HD]
                w_s[d, dst, :] = w16[blk]
                at_s[d, dst, :] = attn[blk, blk]
                qg_s[d, dst, :] = qg16[blk]
                kd_s[d, dst, :] = kd16[blk]
                dec_s[d, pl.ds(pl.multiple_of((DN_GC * ts[i] + p // 2) * 8, 8), 8), :] = dec[p * DN_C:p * DN_C + 8]
        return 0

    n_ctx_groups, n_groups = DN_NCC // DN_GC, DN_NCH // DN_GC
    assert (n_groups - n_ctx_groups) % DN_LOCKSTEP == 0
    prep(list(range(n_ctx_groups)), gc_ref, 0)
    lax.fori_loop(0, (n_groups - n_ctx_groups) // DN_LOCKSTEP,
                  lambda t, _: prep([n_ctx_groups + DN_LOCKSTEP * t + i for i in range(DN_LOCKSTEP)],
                                    gl_ref, CTX_LEN), 0)
    o_dst = (qn, kn)
    o_f, o_b = o_dst

    def step(cf, cb, carry):
        dirs = range(2)
        rows = [pl.ds(pl.multiple_of(c * DN_C, DN_C), DN_C) for c in (cf, cb)]
        s16 = [carry[d].astype(BF16) for d in dirs]
        ws = [_dot(w_s[d, rows[d], :], s16[d]) for d in dirs]
        qs = [_dot(qg_s[d, rows[d], :], s16[d]) for d in dirs]
        v_new = [(u_s[d, rows[d], :] - ws[d]).astype(BF16) for d in dirs]
        kv = [_dot_tn(kd_s[d, rows[d], :], v_new[d]) for d in dirs]
        av = [_dot(at_s[d, rows[d], :], v_new[d]) for d in dirs]
        out = []
        for d, c in zip(dirs, (cf, cb)):
            o_dst[d][rows[d], :] = qs[d] + av[d]
            g_last = dec_s[d, pl.ds(pl.multiple_of(c * 8, 8), 8), :][0:1]
            out.append(carry[d] * g_last + kv[d])
        return tuple(out)

    z0 = jnp.zeros((DN_HD, DN_HD), F32)
    _bidir_chunks(step, (z0, z0), DN_NCC, DN_NCH)

    nw = nw_ref[...]

    def finish(rows, z):
        o = o_f[rows, :] + o_b[rows, :]
        o = o * lax.rsqrt(jnp.mean(o * o, axis=-1, keepdims=True) + EPS) * nw
        return (o * _silu(z)).astype(BF16)

    oc_ref[...] = finish(pl.ds(0, CTX_LEN), zc_ref[...])

    def lat_finish(k, _):
        r0 = pl.multiple_of(k * MIX_TB, 8)
        ol_ref[pl.ds(r0, MIX_TB), :] = finish(pl.ds(pl.multiple_of(CTX_LEN + k * MIX_TB, 8), MIX_TB),
                                              zl_ref[pl.ds(r0, MIX_TB), :])
        return 0
    lax.fori_loop(0, SEQ // MIX_TB, lat_finish, 0)


def deltanet_mixer(proj, conv_w, a_log, dt_bias, norm_w):
    hp = jnp.stack([-jnp.exp(a_log[0]), -jnp.exp(a_log[1]), dt_bias[0], dt_bias[1]], axis=-1)[:, None, :]
    ctx_blk0 = N_LAT // CTX_LEN

    def seg_specs(col0):
        cb = col0 // DN_HD
        return [pl.BlockSpec((CTX_LEN, DN_HD), lambda b, h: (ctx_blk0 + b, cb + h)),
                pl.BlockSpec((SEQ, DN_HD), lambda b, h: (b, cb + h))]

    def conv_spec(k):
        return pl.BlockSpec((CONV_W, DN_HD), lambda b, h: (0, k * DN_H + h))

    in_specs = (seg_specs(COL_DN_QKVZ) + seg_specs(COL_DN_QKVZ + MIX_W) + seg_specs(COL_DN_QKVZ + 2 * MIX_W)
                + seg_specs(COL_DN_QKVZ + 3 * MIX_W)
                + [pl.BlockSpec((CTX_LEN, 128), lambda b, h: (ctx_blk0 + b, GATE_COLBLK)),
                   pl.BlockSpec((SEQ, 128), lambda b, h: (b, GATE_COLBLK)),
                   pl.BlockSpec((None, 1, 4), lambda b, h: (h, 0, 0)),
                   conv_spec(0), conv_spec(1), conv_spec(2),
                   pl.BlockSpec((1, DN_HD), lambda b, h: (0, 0))])
    seq_f32 = pltpu.VMEM((SEQ_ALL, DN_HD), F32)
    dir_bf16 = pltpu.VMEM((2, SEQ_ALL, DN_HD), BF16)
    ol, oc = pl.pallas_call(
        _dn_kernel, grid=(BATCH, DN_H), in_specs=in_specs,
        out_specs=[pl.BlockSpec((SEQ, DN_HD), lambda b, h: (b, h)),
                   pl.BlockSpec((CTX_LEN, DN_HD), lambda b, h: (b, h))],
        out_shape=[jax.ShapeDtypeStruct((N_LAT, MIX_W), BF16), jax.ShapeDtypeStruct((N_CTX, MIX_W), BF16)],
        scratch_shapes=[pltpu.VMEM((DN_ROWS, DN_HD), F32), seq_f32, seq_f32, seq_f32,
                        pltpu.VMEM((2, SEQ_ALL, DN_HD), F32), dir_bf16, dir_bf16, dir_bf16,
                        pltpu.VMEM((2, SEQ_ALL, DN_C), BF16), pltpu.VMEM((2, DN_NCH * 8, DN_HD), F32)],
        compiler_params=_cparams(("parallel", "arbitrary")), name="deltanet",
    )(*([proj] * 10), hp, conv_w, conv_w, conv_w, norm_w[None])
    return ol, oc


def _ml_kernel(qc_ref, ql_ref, kc_ref, kl_ref, vc_ref, vl_ref, oc_ref, ol_ref, gc_ref, gl_ref, hp_ref, nw_ref,
               yl_ref, yc_ref, h_f, h_b):
    hh = pl.program_id(1)
    masks = [_chunk_masks(False), _chunk_masks(True)]
    ones = jnp.ones((CHUNK, CHUNK), BF16)
    eye = jnp.where(lax.broadcasted_iota(jnp.int32, (CHUNK, CHUNK), 0)
                    == lax.broadcasted_iota(jnp.int32, (CHUNK, CHUNK), 1), 1.0, 0.0).astype(F32)

    h_dst = (h_f, h_b)
    dirs = range(2)

    def gate_terms(cs, refs, row0):
        q_ref, k_ref, v_ref, g_ref = refs
        rows = [pl.ds(pl.multiple_of(c * CHUNK - row0, CHUNK), CHUNK) for c in cs]
        gblk = [g_ref[rows[d], :] for d in dirs]
        ig = [_lane_pick(gblk[d], 4 * DN_H + d * ML_H + hh) + hp_ref[:, d:d + 1] for d in dirs]
        lf = [-_softplus(-(_lane_pick(gblk[d], 4 * DN_H + 2 * ML_H + d * ML_H + hh) + hp_ref[:, 2 + d:3 + d]))
              for d in dirs]
        q16 = [(q_ref[rows[d], :] * (ML_DQK ** -0.5)).astype(BF16) for d in dirs]
        bcum = [_dot_mask(masks[d][2], jnp.broadcast_to(lf[d], (CHUNK, ML_DQK)))[:, 0:1] for d in dirs]
        dsum = [_dot_mask(masks[d][2], lf[d] * masks[d][3]) + _dot_mask(ones, ig[d] * eye) for d in dirs]
        sqk = [_dot_nt(q16[d], k_ref[rows[d], :].astype(BF16)) for d in dirs]
        dmat = [jnp.where(masks[d][0], dsum[d], -jnp.inf) for d in dirs]
        dmax = [jnp.max(dmat[d], axis=-1, keepdims=True) for d in dirs]
        tot = [jnp.sum(lf[d], axis=0, keepdims=True) for d in dirs]
        logw = [tot[d] - bcum[d] + ig[d] for d in dirs]
        logw_max = [jnp.max(logw[d], axis=0, keepdims=True) for d in dirs]
        return tuple((bcum[d], dmat[d], dmax[d], tot[d], logw[d], logw_max[d], sqk[d]) for d in dirs)

    def advance(cs, nxt, states, pre, refs, row0):
        q_ref, k_ref, v_ref, _ = refs
        rows = [pl.ds(pl.multiple_of(c * CHUNK - row0, CHUNK), CHUNK) for c in cs]
        q = [q_ref[rows[d], :] * (ML_DQK ** -0.5) for d in dirs]
        k = [k_ref[rows[d], :] for d in dirs]
        q16 = [q[d].astype(BF16) for d in dirs]
        v16 = [v_ref[rows[d], :].astype(BF16) for d in dirs]
        qc = [_dot(q16[d], states[d][0].astype(BF16)) for d in dirs]
        pre_next = gate_terms(nxt, refs, row0)
        out = []
        inter, m_t, w_inter, wmat, m_new, keep, kw = [], [], [], [], [], [], []
        for d in dirs:
            bcum, dmat, dmax, tot, logw, logw_max, sqk = pre[d]
            m = states[d][2]
            inter.append(bcum + m)
            m_t.append(jnp.maximum(inter[d], dmax))
            w_inter.append(jnp.exp(inter[d] - m_t[d]))
            wmat.append(jnp.exp(dmat - m_t[d]) * sqk)
            m_new.append(jnp.maximum(tot + m, logw_max))
            keep.append(jnp.exp(tot + m - m_new[d]))
            kw.append(k[d] * jnp.exp(logw - m_new[d]))
        wv = [_dot(wmat[d].astype(BF16), v16[d]) for d in dirs]
        kv = [_dot_tn(kw[d].astype(BF16), v16[d]) for d in dirs]
        for d in dirs:
            c_s, n_s, _ = states[d]
            num = w_inter[d] * qc[d] + wv[d]
            den = w_inter[d] * jnp.sum(q[d] * n_s, axis=-1, keepdims=True) + jnp.sum(wmat[d], axis=-1, keepdims=True)
            h_dst[d][pl.ds(pl.multiple_of(cs[d] * CHUNK, CHUNK), CHUNK), :] = (
                num / jnp.maximum(jnp.abs(den), jnp.exp(-m_t[d])))
            out.append((keep[d] * c_s + kv[d], keep[d] * n_s + jnp.sum(kw[d], axis=0, keepdims=True), m_new[d]))
        return tuple(out), pre_next

    ctx_refs = (qc_ref, kc_ref, vc_ref, gc_ref)
    lat_refs = (ql_ref, kl_ref, vl_ref, gl_ref)

    def segment(refs, start, n, states):
        chunks = lambda t: (start + t, start + n - 1 - t)

        def body(t, carry):
            states, pre = carry
            return advance(chunks(t), chunks(jnp.minimum(t + 1, n - 1)), states, pre, refs, start * CHUNK)
        return lax.fori_loop(0, n, body, (states, gate_terms(chunks(0), refs, start * CHUNK)))[0]

    s0 = (jnp.zeros((ML_DQK, ML_DV), F32), jnp.zeros((1, ML_DQK), F32), jnp.zeros((1, 1), F32))
    states = segment(ctx_refs, 0, N_CCH, (s0, s0))
    segment(lat_refs, N_CCH, N_CHUNKS - N_CCH, states)

    nw = nw_ref[...]

    def finish(rows, o):
        hsum = h_f[rows, :] + h_b[rows, :]
        hsum = hsum * lax.rsqrt(jnp.mean(hsum * hsum, axis=-1, keepdims=True) + EPS) * nw
        return (hsum * jax.nn.sigmoid(o)).astype(BF16)

    yc_ref[...] = finish(pl.ds(0, CTX_LEN), oc_ref[...])

    def lat_finish(k, _):
        r0 = pl.multiple_of(k * MIX_TB, 8)
        yl_ref[pl.ds(r0, MIX_TB), :] = finish(pl.ds(pl.multiple_of(CTX_LEN + k * MIX_TB, 8), MIX_TB),
                                              ol_ref[pl.ds(r0, MIX_TB), :])
        return 0
    lax.fori_loop(0, SEQ // MIX_TB, lat_finish, 0)


def mlstm_mixer(proj, i_bias, f_bias, norm_w):
    hp = jnp.stack([i_bias[0], i_bias[1], f_bias[0], f_bias[1]], axis=-1)[:, None, :]
    ctx_blk0 = N_LAT // CTX_LEN

    def seg_specs(col0, w):
        cb = col0 // w
        return [pl.BlockSpec((CTX_LEN, w), lambda b, h: (ctx_blk0 + b, cb + h)),
                pl.BlockSpec((SEQ, w), lambda b, h: (b, cb + h))]

    in_specs = (seg_specs(COL_ML_Q, ML_DQK) + seg_specs(COL_ML_K, ML_DQK) + seg_specs(COL_ML_V, ML_DV)
                + seg_specs(COL_ML_O, ML_DV)
                + [pl.BlockSpec((CTX_LEN, 128), lambda b, h: (ctx_blk0 + b, GATE_COLBLK)),
                   pl.BlockSpec((SEQ, 128), lambda b, h: (b, GATE_COLBLK)),
                   pl.BlockSpec((None, 1, 4), lambda b, h: (h, 0, 0)),
                   pl.BlockSpec((1, ML_DV), lambda b, h: (0, h))])
    seq_f32 = pltpu.VMEM((SEQ_ALL, ML_DV), F32)
    yl, yc = pl.pallas_call(
        _ml_kernel, grid=(BATCH, ML_H), in_specs=in_specs,
        out_specs=[pl.BlockSpec((SEQ, ML_DV), lambda b, h: (b, h)),
                   pl.BlockSpec((CTX_LEN, ML_DV), lambda b, h: (b, h))],
        out_shape=[jax.ShapeDtypeStruct((N_LAT, MIX_W), BF16), jax.ShapeDtypeStruct((N_CTX, MIX_W), BF16)],
        scratch_shapes=[seq_f32, seq_f32],
        compiler_params=_cparams(("parallel", "arbitrary")), name="mlstm",
    )(*([proj] * 10), hp, norm_w[None])
    return yl, yc


PROJ_TM = (N_LAT + N_CTX) // 8
PROJ_TN = 768
PROJ_PAD = -(-PROJ_W // PROJ_TN) * PROJ_TN
COL_S5, COL_DN_QKVZ, COL_ML_Q, COL_ML_K, COL_ML_V, COL_ML_O, COL_LRU_Y, COL_LRU_X = (
    0, 1024, 5120, 5632, 6144, 7168, 8192, 9216)
COL_DN_A, COL_DN_B, COL_ML_I, COL_ML_F = 10240, 10256, 10272, 10280


def permute_proj_weight(w):
    o = [0] + [int(v) for v in np.cumsum(PROJ_SIZES)]
    parts = [w[:, o[0]:o[5]], w[:, o[7]:o[11]], w[:, o[13]:o[15]], w[:, o[5]:o[7]], w[:, o[11]:o[13]]]
    return jnp.pad(jnp.concatenate(parts, axis=1).astype(BF16), ((0, 0), (0, PROJ_PAD - PROJ_W)))


def hybrid_mixer(proj, s5_p, dn_p, ml_p, lru_p):
    def ctx_seg(c0):
        return proj[N_LAT:, c0:c0 + MIX_W].reshape(BATCH, CTX_LEN, MIX_W)

    def to_row_major(t):
        return grid_transpose(t.reshape(BATCH * GRID_W, GRID_W, MIX_W), BATCH, 0, MIX_W, BF16).reshape(N_LAT, MIX_W)

    grid = proj.reshape((N_LAT + N_CTX) // GRID_W, GRID_W, PROJ_PAD)
    u_l = grid_transpose(grid, BATCH, COL_S5, MIX_W, F32).reshape(BATCH, SEQ, MIX_W)
    yx_l = grid_transpose(grid, BATCH, COL_LRU_Y, 2 * MIX_W, F32).reshape(BATCH, SEQ, 2 * MIX_W)
    a_c, a_l = s5_mixer(ctx_seg(COL_S5), u_l, *s5_p)
    d_c, d_l = lru_mixer(ctx_seg(COL_LRU_Y), ctx_seg(COL_LRU_X), yx_l, *lru_p)
    flat = lambda t_l, t_c: (to_row_major(t_l), t_c.reshape(N_CTX, MIX_W).astype(BF16))
    return [flat(a_l, a_c), deltanet_mixer(proj, *dn_p), mlstm_mixer(proj, *ml_p), flat(d_l, d_c)]


def kernel(x, c, ctx, c_ctx, w_ada, b_ada, norm1, norm2, norm_f, w_in, w_out, s5_lam_re, s5_lam_im, s5_log_step,
           s5_b_re, s5_b_im, s5_c_re, s5_c_im, s5_d, s5_w_glu, s5_b_glu, dn_conv, dn_a_log, dn_dt_bias, dn_norm,
           ml_i_bias, ml_f_bias, ml_norm, lru_conv_w, lru_conv_b, lru_w_a, lru_b_a, lru_w_i, lru_b_i, lru_lam,
           router_w, router_bias, moe_w1, moe_w3, moe_w2):
    d = D_MODEL
    xa = jnp.concatenate([x.reshape(N_LAT, d), ctx.reshape(N_CTX, d)], axis=0)
    cond = jnp.concatenate([jax.nn.silu(c), jax.nn.silu(c_ctx)[None], jnp.zeros((16 - BATCH - 1, d), F32)], axis=0)
    cond = cond.astype(BF16)
    router = (router_w.T, router_bias.reshape(N_EXPERTS, 1))

    s5_ops = jax.vmap(s5_chunk_operators)(s5_lam_re, s5_lam_im, s5_log_step, s5_b_re, s5_b_im, s5_c_re, s5_c_im)

    for l in range(DEPTH):
        last = l == DEPTH - 1
        mod = ada_matmul(l, cond, w_ada, b_ada[:, None, :], 1024)[:BATCH + 1].reshape(BATCH + 1, 6, 1, d)
        shift1, scale1, gate1, shift2, scale2, gate2 = (mod[:, k] for k in range(6))

        h1 = norm_mod(xa, norm1[l][None], scale1, shift1, SEQ, 256)
        proj = matmul(h1, permute_proj_weight(w_in[l]), PROJ_TM, PROJ_TN)
        slabs = hybrid_mixer(
            proj,
            (tuple(op[l] for op in s5_ops), s5_d[l], s5_w_glu[l], s5_b_glu[l]),
            (dn_conv[l], dn_a_log[l], dn_dt_bias[l], dn_norm[l]),
            (ml_i_bias[l], ml_f_bias[l], ml_norm[l]),
            (lru_conv_w[l], lru_conv_b[l], lru_w_a[l], lru_b_a[l], lru_w_i[l], lru_b_i[l], lru_lam[l]))
        if last:
            xa, mix = xa[:N_LAT], [s[0] for s in slabs]
        else:
            mix = [jnp.concatenate(s, axis=0) for s in slabs]
        out_tm = 1024 if xa.shape[0] % 1024 == 0 else 512
        xa = matmul_gated_residual(mix, w_out[l].astype(BF16), xa, gate1, SEQ, out_tm, 1024)

        h2, idx, wsel = norm_mod(xa, norm2[l][None], scale2, shift2, SEQ, 256, router=router)
        y0, y1 = moe_ffn(l, h2, idx, wsel, moe_w1, moe_w3, moe_w2)
        xa = gated_add(xa, y0, y1, gate2, SEQ, 256)

    return final_norm(xa[:N_LAT], norm_f[None], 256).reshape(BATCH, SEQ, d)
```

```python
import functools
import math

import numpy as np
import jax
import jax.numpy as jnp
from jax import lax
from jax.experimental import pallas as pl
from jax.experimental.pallas import tpu as pltpu

D_MODEL = 4096
BATCH = 2
SEQ = 4096
DEPTH = 2
GRID_W = 64
CTX_LEN = 256
MIX_W = D_MODEL // 4
MIX_TOTAL = 4 * MIX_W
CHUNK = 64
CONV_W = 4
CONV_PAD_L = 2
CONV_PAD_R = 1
EPS = 1e-6
S5_CH = 16
S5_G = MIX_W // S5_CH
S5_P = 64
DN_HD = 128
DN_H = MIX_W // DN_HD
ML_H = 4
ML_DV = MIX_W // ML_H
ML_DQK = ML_DV // 2
LRU_BLOCKS = 8
LRU_BD = MIX_W // LRU_BLOCKS
LRU_C = 8.0
N_EXPERTS = 16
N_GROUPS = 4
E_PER_G = N_EXPERTS // N_GROUPS
TOP_K = 2
D_FF = D_MODEL // 4
PROJ_SIZES = (MIX_W, MIX_W, MIX_W, MIX_W, MIX_W, 2 * DN_H, 2 * DN_H, ML_H * ML_DQK, ML_H * ML_DQK, MIX_W, MIX_W,
              2 * ML_H, 2 * ML_H, MIX_W, MIX_W)
PROJ_W = sum(PROJ_SIZES)

N_LAT = BATCH * SEQ
N_CTX = BATCH * CTX_LEN
VMEM_LIMIT = 56 * 1024 * 1024

F32 = jnp.float32
BF16 = jnp.bfloat16


def _cparams(sem):
    return pltpu.CompilerParams(dimension_semantics=sem, vmem_limit_bytes=VMEM_LIMIT)


def _mm_kernel(a_ref, b_ref, o_ref):
    o_ref[...] = jnp.dot(a_ref[...], b_ref[...], preferred_element_type=F32).astype(o_ref.dtype)


def matmul(a, b, tm, tn, out_dtype=F32):
    m, k = a.shape
    n = b.shape[1]
    assert m % tm == 0 and n % tn == 0
    return pl.pallas_call(
        _mm_kernel,
        grid=(m // tm, n // tn),
        in_specs=[pl.BlockSpec((tm, k), lambda i, j: (i, 0)),
                  pl.BlockSpec((k, tn), lambda i, j: (0, j))],
        out_specs=pl.BlockSpec((tm, tn), lambda i, j: (i, j)),
        out_shape=jax.ShapeDtypeStruct((m, n), out_dtype),
        compiler_params=_cparams(("parallel", "arbitrary")),
        name="mm",
    )(a, b)


def _mm_resid_kernel(a0_ref, a1_ref, a2_ref, a3_ref, b_ref, r_ref, g_ref, o_ref):
    acc = None
    for s, a_ref in enumerate((a0_ref, a1_ref, a2_ref, a3_ref)):
        part = jnp.dot(a_ref[...], b_ref[s * MIX_W:(s + 1) * MIX_W, :], preferred_element_type=F32)
        acc = part if acc is None else acc + part
    o_ref[...] = r_ref[...] + g_ref[...] * acc


def matmul_gated_residual(a_slabs, b, resid, gate, rows_per_gate, tm, tn):
    m = a_slabs[0].shape[0]
    k, n = b.shape
    assert m % tm == 0 and n % tn == 0 and rows_per_gate % tm == 0 and k == 4 * MIX_W
    return pl.pallas_call(
        _mm_resid_kernel,
        grid=(m // tm, n // tn),
        in_specs=[pl.BlockSpec((tm, MIX_W), lambda i, j: (i, 0))] * 4 + [
                  pl.BlockSpec((k, tn), lambda i, j: (0, j)),
                  pl.BlockSpec((tm, tn), lambda i, j: (i, j)),
                  pl.BlockSpec((None, 1, tn), lambda i, j: ((i * tm) // rows_per_gate, 0, j))],
        out_specs=pl.BlockSpec((tm, tn), lambda i, j: (i, j)),
        out_shape=jax.ShapeDtypeStruct((m, n), F32),
        compiler_params=_cparams(("parallel", "arbitrary")),
        name="mm_resid",
    )(*a_slabs, b, resid, gate)


def _ada_kernel(a_ref, w_ref, b_ref, o_ref):
    o_ref[...] = jnp.dot(a_ref[...], w_ref[...].astype(BF16), preferred_element_type=F32) + b_ref[...]


def ada_matmul(layer, a, w, bias, tn):
    m, k = a.shape
    n = w.shape[2]
    return pl.pallas_call(
        _ada_kernel,
        grid=(n // tn,),
        in_specs=[pl.BlockSpec((m, k), lambda j: (0, 0)),
                  pl.BlockSpec((None, k, tn), lambda j: (layer, 0, j)),
                  pl.BlockSpec((None, 1, tn), lambda j: (layer, 0, j))],
        out_specs=pl.BlockSpec((m, tn), lambda j: (0, j)),
        out_shape=jax.ShapeDtypeStruct((m, n), F32),
        compiler_params=_cparams(("arbitrary",)),
        name="ada",
    )(a, w, bias)


def _modulated_norm(x, g, sc, sh):
    ms = jnp.mean(x * x, axis=-1, keepdims=True)
    y = x * lax.rsqrt(ms + EPS) * g
    return y * (1.0 + sc) + sh


def _norm_mod_kernel(x_ref, g_ref, sc_ref, sh_ref, o_ref):
    o_ref[...] = _modulated_norm(x_ref[...], g_ref[...], sc_ref[...], sh_ref[...]).astype(o_ref.dtype)


def _pair_top2_sum(v):
    a, b, c, d = v
    return jnp.maximum(jnp.maximum(jnp.maximum(a + b, a + c), jnp.maximum(a + d, b + c)),
                       jnp.maximum(b + d, c + d))


def _norm_mod_router_kernel(x_ref, g_ref, sc_ref, sh_ref, rwt_ref, rb_ref, o_ref, idx_ref, w_ref):
    h = _modulated_norm(x_ref[...], g_ref[...], sc_ref[...], sh_ref[...])
    o_ref[...] = h.astype(o_ref.dtype)
    logits = lax.dot_general(rwt_ref[...], h, (((1,), (1,)), ((), ())),
                             precision=lax.Precision.HIGHEST, preferred_element_type=F32)
    scores = jax.nn.sigmoid(logits)
    biased = scores + rb_ref[...]
    s = [scores[e:e + 1, :] for e in range(N_EXPERTS)]
    b = [biased[e:e + 1, :] for e in range(N_EXPERTS)]
    gs = [_pair_top2_sum(b[E_PER_G * g:E_PER_G * (g + 1)]) for g in range(N_GROUPS)]
    best, gsel = gs[0], jnp.zeros_like(gs[0], dtype=jnp.int32)
    for g in range(1, N_GROUPS):
        better = gs[g] > best
        gsel = jnp.where(better, g, gsel)
        best = jnp.where(better, gs[g], best)
    vb, vs = [], []
    for k in range(E_PER_G):
        bk, sk = b[k], s[k]
        for g in range(1, N_GROUPS):
            bk = jnp.where(gsel == g, b[E_PER_G * g + k], bk)
            sk = jnp.where(gsel == g, s[E_PER_G * g + k], sk)
        vb.append(bk)
        vs.append(sk)
    m1, i1, w1 = vb[0], jnp.zeros_like(gsel), vs[0]
    for k in range(1, E_PER_G):
        better = vb[k] > m1
        i1 = jnp.where(better, k, i1)
        w1 = jnp.where(better, vs[k], w1)
        m1 = jnp.where(better, vb[k], m1)
    m2, i2, w2, have = vb[0], jnp.zeros_like(gsel), vs[0], i1 != 0
    for k in range(1, E_PER_G):
        valid = i1 != k
        better = valid & (jnp.logical_not(have) | (vb[k] > m2))
        i2 = jnp.where(better, k, i2)
        w2 = jnp.where(better, vs[k], w2)
        m2 = jnp.where(better, vb[k], m2)
        have = have | valid
    tot = w1 + w2
    idx_ref[0:1, :] = gsel * E_PER_G + i1
    idx_ref[1:2, :] = gsel * E_PER_G + i2
    w_ref[0:1, :] = w1 / tot
    w_ref[1:2, :] = w2 / tot


def norm_mod(x, g, sc, sh, rows_per_mod, tm, router=None):
    m, d = x.shape
    assert m % tm == 0 and rows_per_mod % tm == 0
    mod_spec = pl.BlockSpec((None, 1, d), lambda i: ((i * tm) // rows_per_mod, 0, 0))
    in_specs = [pl.BlockSpec((tm, d), lambda i: (i, 0)), pl.BlockSpec((1, d), lambda i: (0, 0)), mod_spec, mod_spec]
    o_spec = pl.BlockSpec((tm, d), lambda i: (i, 0))
    o_shape = jax.ShapeDtypeStruct((m, d), BF16)
    if router is None:
        return pl.pallas_call(
            _norm_mod_kernel, grid=(m // tm,), in_specs=in_specs, out_specs=o_spec, out_shape=o_shape,
            compiler_params=_cparams(("parallel",)), name="norm_mod",
        )(x, g, sc, sh)
    rwt, rb = router
    in_specs += [pl.BlockSpec((N_EXPERTS, d), lambda i: (0, 0)), pl.BlockSpec((N_EXPERTS, 1), lambda i: (0, 0))]
    sel_spec = pl.BlockSpec((TOP_K, tm), lambda i: (0, i))
    return pl.pallas_call(
        _norm_mod_router_kernel, grid=(m // tm,), in_specs=in_specs,
        out_specs=[o_spec, sel_spec, sel_spec],
        out_shape=[o_shape, jax.ShapeDtypeStruct((TOP_K, m), jnp.int32), jax.ShapeDtypeStruct((TOP_K, m), F32)],
        compiler_params=_cparams(("parallel",)), name="norm_mod_router",
    )(x, g, sc, sh, rwt, rb)


def _gated_add_kernel(x_ref, y0_ref, y1_ref, g_ref, o_ref):
    o_ref[...] = x_ref[...] + g_ref[...] * (y0_ref[...] + y1_ref[...])


def gated_add(x, y0, y1, gate, rows_per_gate, tm):
    m, d = x.shape
    assert m % tm == 0 and rows_per_gate % tm == 0
    row = pl.BlockSpec((tm, d), lambda i: (i, 0))
    return pl.pallas_call(
        _gated_add_kernel, grid=(m // tm,),
        in_specs=[row, row, row, pl.BlockSpec((None, 1, d), lambda i: ((i * tm) // rows_per_gate, 0, 0))],
        out_specs=row, out_shape=jax.ShapeDtypeStruct((m, d), F32),
        compiler_params=_cparams(("parallel",)), name="gated_add",
    )(x, y0, y1, gate)


def _final_norm_kernel(x_ref, g_ref, o_ref):
    x = x_ref[...]
    ms = jnp.mean(x * x, axis=-1, keepdims=True)
    o_ref[...] = x * lax.rsqrt(ms + EPS) * g_ref[...]


def final_norm(x, g, tm):
    m, d = x.shape
    return pl.pallas_call(
        _final_norm_kernel, grid=(m // tm,),
        in_specs=[pl.BlockSpec((tm, d), lambda i: (i, 0)), pl.BlockSpec((1, d), lambda i: (0, 0))],
        out_specs=pl.BlockSpec((tm, d), lambda i: (i, 0)),
        out_shape=jax.ShapeDtypeStruct((m, d), F32),
        compiler_params=_cparams(("parallel",)), name="final_norm",
    )(x, g)


MOE_TM = 512
MOE_KC = 1024
MOE_FC = 256
MOE_NK = D_MODEL // MOE_KC
MOE_NF = D_FF // MOE_FC
MOE_PARTS = 4


def _moe_kernel(te_ref, nu_ref, x_ref, w1_ref, w3_ref, w2_ref, g_ref, *rest, tile0, has_prev):
    o_ref, hg_s, hu_s, act_s = rest[1:] if has_prev else rest
    t = pl.program_id(0)
    s = pl.program_id(1)
    used = t + tile0 < nu_ref[0]

    @pl.when(used & (s < MOE_NK))
    def _():
        x = x_ref[...]
        pg = jnp.dot(x, w1_ref[...].astype(BF16), preferred_element_type=F32)
        pu = jnp.dot(x, w3_ref[...].astype(BF16), preferred_element_type=F32)

        @pl.when(s == 0)
        def _():
            hg_s[...] = pg
            hu_s[...] = pu

        @pl.when(s > 0)
        def _():
            hg_s[...] += pg
            hu_s[...] += pu

        @pl.when(s == MOE_NK - 1)
        def _():
            hg = hg_s[...]
            act = ((hg * jax.nn.sigmoid(hg)) * hu_s[...] * g_ref[...]).astype(BF16)
            for f in range(MOE_NF):
                act_s[f] = act[:, f * MOE_FC:(f + 1) * MOE_FC]

    @pl.when(used & (s >= MOE_NK))
    def _():
        y = jnp.dot(act_s[s - MOE_NK], w2_ref[...].astype(BF16), preferred_element_type=F32)

        @pl.when(s == MOE_NK)
        def _():
            o_ref[...] = y

        @pl.when(s > MOE_NK)
        def _():
            o_ref[...] += y

    @pl.when(jnp.logical_not(used) & (s == 0))
    def _():
        o_ref[...] = jnp.zeros_like(o_ref)


def moe_grouped(layer, tile_expert, n_used, xs, w1, w3, w2, gs, tile0, ys_prev):
    rp, d = xs.shape
    r = gs.shape[0]

    def live(t, nu):
        return t + tile0 < nu[0]

    def kidx(t, s, nu):
        return jnp.where(live(t, nu), jnp.minimum(s, MOE_NK - 1), MOE_NK - 1)

    def fidx(t, s, nu):
        return jnp.where(live(t, nu), jnp.maximum(s - MOE_NK, 0), MOE_NF - 1)

    in_specs = [pl.BlockSpec((MOE_TM, MOE_KC), lambda t, s, te, nu: (t, kidx(t, s, nu))),
                pl.BlockSpec((None, None, MOE_KC, D_FF),
                             lambda t, s, te, nu: (layer, te[t + tile0], kidx(t, s, nu), 0)),
                pl.BlockSpec((None, None, MOE_KC, D_FF),
                             lambda t, s, te, nu: (layer, te[t + tile0], kidx(t, s, nu), 0)),
                pl.BlockSpec((None, None, MOE_FC, d), lambda t, s, te, nu: (layer, te[t + tile0], fidx(t, s, nu), 0)),
                pl.BlockSpec((MOE_TM, 1), lambda t, s, te, nu: (t + tile0, 0))]
    args = [tile_expert, n_used, xs, w1, w3, w2, gs]
    aliases = {}
    if ys_prev is not None:
        in_specs.append(pl.BlockSpec(memory_space=pl.ANY))
        aliases = {len(args): 0}
        args.append(ys_prev)
    grid_spec = pltpu.PrefetchScalarGridSpec(
        num_scalar_prefetch=2,
        grid=(rp // MOE_TM, MOE_NK + MOE_NF),
        in_specs=in_specs,
        out_specs=pl.BlockSpec((MOE_TM, d), lambda t, s, te, nu: (t + tile0, 0)),
        scratch_shapes=[pltpu.VMEM((MOE_TM, D_FF), F32), pltpu.VMEM((MOE_TM, D_FF), F32),
                        pltpu.VMEM((MOE_NF, MOE_TM, MOE_FC), BF16)],
    )
    return pl.pallas_call(
        functools.partial(_moe_kernel, tile0=tile0, has_prev=ys_prev is not None), grid_spec=grid_spec,
        out_shape=jax.ShapeDtypeStruct((r, d), F32), input_output_aliases=aliases,
        compiler_params=_cparams(("arbitrary", "arbitrary")), name="moe",
    )(*args)


CUMSUM_BLK = 256


def _onehot_cumsum(onehot):
    m, e = onehot.shape
    x = onehot.reshape(m // CUMSUM_BLK, CUMSUM_BLK, e)
    tri = jnp.tril(jnp.ones((CUMSUM_BLK, CUMSUM_BLK), F32))
    within = jnp.einsum('ij,bje->bie', tri, x, precision=lax.Precision.HIGHEST)
    total = within[:, -1, :]
    return (within + (jnp.cumsum(total, axis=0) - total)[:, None, :]).reshape(m, e)


def moe_dispatch(idx, wsel):
    n = idx.shape[1]
    e_flat = idx.reshape(-1)
    onehot = (e_flat[:, None] == jnp.arange(N_EXPERTS, dtype=jnp.int32)[None, :]).astype(F32)
    csum = _onehot_cumsum(onehot)
    counts = csum[-1].astype(jnp.int32)
    rank = jnp.sum(onehot * csum, axis=1).astype(jnp.int32) - 1
    padded = ((counts + MOE_TM - 1) // MOE_TM) * MOE_TM
    seg_end = jnp.cumsum(padded)
    seg_start = seg_end - padded
    dest = jnp.sum(onehot * seg_start.astype(F32)[None, :], axis=1).astype(jnp.int32) + rank
    part_rows = MOE_TM * MOE_PARTS
    rows = -(-(TOP_K * n + N_EXPERTS * MOE_TM) // part_rows) * part_rows
    pair = jnp.zeros((rows,), jnp.int32).at[dest].set(jnp.arange(TOP_K * n, dtype=jnp.int32) + 1) - 1
    src_tok = jnp.maximum(pair, 0) % n
    gate = jnp.where(pair >= 0, wsel.reshape(-1).at[jnp.maximum(pair, 0)].get(mode="promise_in_bounds"), 0.0)
    n_tiles = rows // MOE_TM
    n_used = (seg_end[-1] // MOE_TM).astype(jnp.int32)
    tile_start = jnp.arange(n_tiles, dtype=jnp.int32) * MOE_TM
    te = jnp.minimum(jnp.searchsorted(seg_end, tile_start, side='right'), N_EXPERTS - 1).astype(jnp.int32)
    te = jnp.where(jnp.arange(n_tiles) < n_used, te, te[jnp.maximum(n_used - 1, 0)])
    return dest.reshape(TOP_K, n), src_tok, gate[:, None], te, n_used.reshape(1)


def _take_rows(a, rows):
    return a.at[rows].get(mode="promise_in_bounds")


def moe_ffn(layer, hb, idx, wsel, w1, w3, w2):
    dest, src_tok, gate, te, n_used = moe_dispatch(idx, wsel)
    n_tiles = src_tok.shape[0] // MOE_TM
    assert n_tiles % MOE_PARTS == 0
    part_tiles = n_tiles // MOE_PARTS
    ys = None
    for p in range(MOE_PARTS):
        rows = slice(p * part_tiles * MOE_TM, (p + 1) * part_tiles * MOE_TM)
        ys = moe_grouped(layer, te, n_used, _take_rows(hb, src_tok[rows]), w1, w3, w2, gate,
                         p * part_tiles, ys)
    return _take_rows(ys, dest[0]), _take_rows(ys, dest[1])


SEQ_ALL = CTX_LEN + SEQ
S5_T = 16
S5_NC = SEQ_ALL // S5_T
S5_NCC = CTX_LEN // S5_T
S5_W = S5_T * S5_CH


def _gelu_tanh(x):
    return x * (0.5 * (1.0 + jnp.tanh(math.sqrt(2.0 / math.pi) * (x + 0.044715 * (x * x * x)))))


def s5_chunk_operators(lam_re, lam_im, log_step, b_re, b_im, c_re, c_im):
    hp = lax.Precision.HIGHEST
    jj = jnp.arange(S5_T)[:, None]
    ii = jnp.arange(S5_T)[None, :]
    ms, es, fs, a16 = [], [], [], []
    for d in range(2):
        dt = jnp.exp(log_step[d])[:, None]
        lr, li = lam_re[d], lam_im[d]
        er = jnp.exp(lr * dt)
        abr, abi = er * jnp.cos(li * dt), er * jnp.sin(li * dt)
        nr, ni = abr - 1.0, abi
        den = lr * lr + li * li
        fr = (nr * lr + ni * li) / den
        fi = (ni * lr - nr * li) / den
        bbr = fr[..., None] * b_re[d] - fi[..., None] * b_im[d]
        bbi = fr[..., None] * b_im[d] + fi[..., None] * b_re[d]
        k = jnp.arange(S5_T + 1, dtype=F32)[:, None, None]
        mag, ang = jnp.exp(k * (lr * dt)), k * (li * dt)
        pr, pi = mag * jnp.cos(ang), mag * jnp.sin(ang)
        pbr = pr[..., None] * bbr - pi[..., None] * bbi
        pbi = pr[..., None] * bbi + pi[..., None] * bbr
        cr, ci = c_re[d], c_im[d]
        kk = (jnp.einsum('gop,lgpc->lgoc', cr, pbr[:S5_T], precision=hp)
              - jnp.einsum('gop,lgpc->lgoc', ci, pbi[:S5_T], precision=hp))
        cpr = cr[None] * pr[:, :, None, :] - ci[None] * pi[:, :, None, :]
        cpi = -(cr[None] * pi[:, :, None, :] + ci[None] * pr[:, :, None, :])
        if d == 0:
            lag, mask = ii - jj, ii >= jj
            e_pow = S5_T - 1 - jnp.arange(S5_T)
            f_pow = jnp.arange(S5_T) + 1
        else:
            lag, mask = jj - ii, jj >= ii
            e_pow = jnp.arange(S5_T)
            f_pow = S5_T - jnp.arange(S5_T)
        mm = jnp.where(mask[:, :, None, None, None], kk[jnp.clip(lag, 0, S5_T - 1)], 0.0)
        ms.append(mm.transpose(2, 0, 4, 1, 3).reshape(S5_G, S5_W, S5_W))
        for pb in (pbr, pbi):
            es.append(pb[e_pow].transpose(1, 0, 3, 2).reshape(S5_G, S5_W, S5_P))
        for cp in (cpr, cpi):
            fs.append(cp[f_pow].transpose(1, 3, 0, 2).reshape(S5_G, S5_P, S5_W))
        a16 += [pr[S5_T][:, None, :], pi[S5_T][:, None, :]]
    return (ms[0] + ms[1]).astype(BF16), jnp.stack(es).astype(BF16), jnp.stack(fs).astype(BF16), jnp.stack(a16)


def _s5_kernel(u_ref, m_ref, e_ref, f_ref, a_ref, o_ref, e_scr, s_scr):
    u = u_ref[...]
    for k in range(4):
        e_scr[k] = jnp.dot(u, e_ref[k], preferred_element_type=F32)
    arf, aif, arb, aib = (a_ref[k] for k in range(4))

    def segment(start, n, carry):
        def body(k, carry):
            out = []
            for b in range(BATCH):
                sfr, sfi, sbr, sbi = carry[4 * b:4 * b + 4]
                rf = b * S5_NC + start + k
                rb = b * S5_NC + start + n - 1 - k
                s_scr[0, pl.ds(rf, 1), :] = sfr
                s_scr[1, pl.ds(rf, 1), :] = sfi
                s_scr[2, pl.ds(rb, 1), :] = sbr
                s_scr[3, pl.ds(rb, 1), :] = sbi
                nfr = arf * sfr - aif * sfi + e_scr[0, pl.ds(rf, 1), :]
                nfi = arf * sfi + aif * sfr + e_scr[1, pl.ds(rf, 1), :]
                nbr = arb * sbr - aib * sbi + e_scr[2, pl.ds(rb, 1), :]
                nbi = arb * sbi + aib * sbr + e_scr[3, pl.ds(rb, 1), :]
                out += [nfr, nfi, nbr, nbi]
            return tuple(out)
        return lax.fori_loop(0, n, body, carry, unroll=4)

    z = jnp.zeros((1, S5_P), F32)
    carry = segment(0, S5_NCC, (z,) * (4 * BATCH))
    segment(S5_NCC, S5_NC - S5_NCC, carry)
    y = jnp.dot(u, m_ref[...], preferred_element_type=F32)
    for k in range(4):
        y += jnp.dot(s_scr[k].astype(BF16), f_ref[k], preferred_element_type=F32)
    o_ref[...] = y


def s5_scan(u, m, e, f, a16):
    g, r, w = u.shape
    return pl.pallas_call(
        _s5_kernel, grid=(g,),
        in_specs=[pl.BlockSpec((None, r, w), lambda i: (i, 0, 0)),
                  pl.BlockSpec((None, w, w), lambda i: (i, 0, 0)),
                  pl.BlockSpec((4, None, w, S5_P), lambda i: (0, i, 0, 0)),
                  pl.BlockSpec((4, None, S5_P, w), lambda i: (0, i, 0, 0)),
                  pl.BlockSpec((4, None, 1, S5_P), lambda i: (0, i, 0, 0))],
        out_specs=pl.BlockSpec((None, r, w), lambda i: (i, 0, 0)),
        out_shape=jax.ShapeDtypeStruct((g, r, w), F32),
        scratch_shapes=[pltpu.VMEM((4, r, S5_P), F32), pltpu.VMEM((4, r, S5_P), F32)],
        compiler_params=_cparams(("parallel",)), name="s5_scan",
    )(u, m, e, f, a16)


def _s5_out_kernel(y_ref, u_ref, d_ref, w_ref, b_ref, o_ref):
    g = _gelu_tanh(y_ref[...] + d_ref[...] * u_ref[...])
    z = jnp.dot(g.astype(BF16), w_ref[...], preferred_element_type=F32) + b_ref[...]
    o_ref[...] = (g * jax.nn.sigmoid(z)).astype(o_ref.dtype)


def s5_out(y, u, d_skip, w_glu, b_glu, tm):
    m, c = y.shape
    row = pl.BlockSpec((tm, c), lambda i: (i, 0))
    vec = pl.BlockSpec((1, c), lambda i: (0, 0))
    return pl.pallas_call(
        _s5_out_kernel, grid=(m // tm,),
        in_specs=[row, row, vec, pl.BlockSpec((c, c), lambda i: (0, 0)), vec],
        out_specs=row, out_shape=jax.ShapeDtypeStruct((m, c), F32),
        compiler_params=_cparams(("parallel",)), name="s5_out",
    )(y, u, d_skip, w_glu, b_glu)


S5_LG = 128 // S5_CH


def _lane_window(lane, k):
    return (lane >= k * S5_CH) & (lane < (k + 1) * S5_CH)


def _s5_pack_kernel(x_ref, o_ref):
    lane = lax.broadcasted_iota(jnp.int32, (x_ref.shape[0], 128), 1)
    tok = [x_ref[:, t, :] for t in range(S5_T)]
    for g in range(S5_LG):
        for h in range(S5_W // 128):
            acc = jnp.zeros(lane.shape, F32)
            for k in range(S5_LG):
                shift = ((k - g) * S5_CH) % 128
                src = tok[h * S5_LG + k]
                acc = jnp.where(_lane_window(lane, k), pltpu.roll(src, shift, axis=1) if shift else src, acc)
            o_ref[g, :, h * 128:(h + 1) * 128] = acc.astype(o_ref.dtype)


def _s5_unpack_kernel(y_ref, o_ref):
    lane = lax.broadcasted_iota(jnp.int32, (y_ref.shape[1], 128), 1)
    for t in range(S5_T):
        h, k = divmod(t, S5_LG)
        acc = jnp.zeros(lane.shape, F32)
        for g in range(S5_LG):
            shift = ((g - k) * S5_CH) % 128
            src = y_ref[g, :, h * 128:(h + 1) * 128]
            acc = jnp.where(_lane_window(lane, g), pltpu.roll(src, shift, axis=1) if shift else src, acc)
        o_ref[:, t, :] = acc


def s5_pack(u):
    n = u.shape[0]
    cb = n // 2
    return pl.pallas_call(
        _s5_pack_kernel, grid=(n // cb, MIX_W // 128),
        in_specs=[pl.BlockSpec((cb, S5_T, 128), lambda i, j: (i, 0, j))],
        out_specs=pl.BlockSpec((S5_LG, cb, S5_W), lambda i, j: (j, i, 0)),
        out_shape=jax.ShapeDtypeStruct((S5_G, n, S5_W), BF16),
        compiler_params=_cparams(("parallel", "parallel")), name="s5_pack",
    )(u)


def s5_unpack(y):
    n = y.shape[1]
    cb = n // 2
    return pl.pallas_call(
        _s5_unpack_kernel, grid=(n // cb, MIX_W // 128),
        in_specs=[pl.BlockSpec((S5_LG, cb, S5_W), lambda i, j: (j, i, 0))],
        out_specs=pl.BlockSpec((cb, S5_T, 128), lambda i, j: (i, 0, j)),
        out_shape=jax.ShapeDtypeStruct((n, S5_T, MIX_W), F32),
        compiler_params=_cparams(("parallel", "parallel")), name="s5_unpack",
    )(y)


def s5_mixer(u_c, u_l, operators, d_skip, w_glu, b_glu):
    bs = u_c.shape[0]
    u_all = jnp.concatenate([u_c, u_l], axis=1)
    uc = s5_pack(u_all.reshape(bs * S5_NC, S5_T, MIX_W))
    y = s5_scan(uc, *operators)
    y = s5_unpack(y).reshape(bs * SEQ_ALL, MIX_W)
    out = s5_out(y, u_all.reshape(bs * SEQ_ALL, MIX_W), d_skip[None], w_glu.astype(BF16), b_glu[None], 544)
    out = out.reshape(bs, SEQ_ALL, MIX_W)
    return out[:, :CTX_LEN], out[:, CTX_LEN:]


LRU_CB = 256
LRU_TB = 256
LRU_HALO = 8
LRU_ROWS = SEQ_ALL + 3 * LRU_HALO


def _lru_kernel(xc_ref, xl_ref, yc_ref, yl_ref, cw_ref, cb_ref, wa_ref, ba_ref, wi_ref, bi_ref, sp_ref,
                oc_ref, ol_ref, xpad, a_f, b_f, a_b, b_b):
    halo = jnp.zeros((LRU_HALO, LRU_CB), F32)
    lat0 = CTX_LEN + 2 * LRU_HALO
    xpad[0:LRU_HALO] = halo
    xpad[LRU_HALO:LRU_HALO + CTX_LEN] = xc_ref[...]
    xpad[LRU_HALO + CTX_LEN:lat0] = halo
    xpad[lat0:lat0 + SEQ] = xl_ref[...]
    xpad[lat0 + SEQ:LRU_ROWS] = halo
    cw = [cw_ref[j:j + 1, :] for j in range(CONV_W)]
    cb = cb_ref[...]

    def gates(p0, r0):
        ext = xpad[pl.ds(p0 - LRU_HALO, LRU_TB + 2 * LRU_HALO), :]
        xb = cb
        for j in range(CONV_W):
            s0 = LRU_HALO - CONV_PAD_L + j
            xb = xb + cw[j] * ext[s0:s0 + LRU_TB]
        for k in range(LRU_CB // LRU_BD):
            lanes = slice(k * LRU_BD, (k + 1) * LRU_BD)
            xk = xb[:, lanes]
            xkb = xk.astype(BF16)
            for d, (a_s, b_s) in enumerate(((a_f, b_f), (a_b, b_b))):
                r = jax.nn.sigmoid(jnp.dot(xkb, wa_ref[d, k], preferred_element_type=F32) + ba_ref[d][:, lanes])
                i = jax.nn.sigmoid(jnp.dot(xkb, wi_ref[d, k], preferred_element_type=F32) + bi_ref[d][:, lanes])
                log_a = -LRU_C * r * sp_ref[d][:, lanes]
                t = jnp.tanh(log_a)
                a_s[pl.ds(r0, LRU_TB), lanes] = jnp.exp(log_a)
                b_s[pl.ds(r0, LRU_TB), lanes] = jnp.sqrt(-2.0 * t / (1.0 - t)) * (i * xk)

    gates(LRU_HALO, 0)

    def lat_gates(k, _):
        gates(pl.multiple_of(lat0 + k * LRU_TB, 8), pl.multiple_of(CTX_LEN + k * LRU_TB, 8))
        return 0
    lax.fori_loop(0, SEQ // LRU_TB, lat_gates, 0)

    def segment(start, n, carry):
        def body(t, carry):
            hf, hb = carry
            rf = start + t
            rb = start + n - 1 - t
            hf = a_f[pl.ds(rf, 1), :] * hf + b_f[pl.ds(rf, 1), :]
            b_f[pl.ds(rf, 1), :] = hf
            hb = a_b[pl.ds(rb, 1), :] * hb + b_b[pl.ds(rb, 1), :]
            b_b[pl.ds(rb, 1), :] = hb
            return hf, hb
        return lax.fori_loop(0, n, body, carry, unroll=8)

    z = jnp.zeros((1, LRU_CB), F32)
    carry = segment(0, CTX_LEN, (z, z))
    segment(CTX_LEN, SEQ, carry)

    oc_ref[...] = (_gelu_tanh(yc_ref[...]) * (b_f[0:CTX_LEN] + b_b[0:CTX_LEN])).astype(oc_ref.dtype)

    def lat_out(k, _):
        r0 = pl.multiple_of(k * LRU_TB, 8)
        rows = pl.ds(pl.multiple_of(CTX_LEN + k * LRU_TB, 8), LRU_TB)
        h = b_f[rows, :] + b_b[rows, :]
        ol_ref[pl.ds(r0, LRU_TB), :] = (_gelu_tanh(yl_ref[pl.ds(r0, LRU_TB), :]) * h).astype(ol_ref.dtype)
        return 0
    lax.fori_loop(0, SEQ // LRU_TB, lat_out, 0)


def lru_mixer(y_c, x_c, yx_l, conv_w, conv_b, w_a, b_a, w_i, b_i, lam):
    bs = x_c.shape[0]
    nb = LRU_CB // LRU_BD
    sp = jax.nn.softplus(-lam)[:, None, :]
    ctx_spec = pl.BlockSpec((None, CTX_LEN, LRU_CB), lambda b, c: (b, 0, c))
    lat_spec = pl.BlockSpec((None, SEQ, LRU_CB), lambda b, c: (b, 0, c))
    lat_x_spec = pl.BlockSpec((None, SEQ, LRU_CB), lambda b, c: (b, 0, MIX_W // LRU_CB + c))
    vec2 = pl.BlockSpec((2, 1, LRU_CB), lambda b, c: (0, 0, c))
    wspec = pl.BlockSpec((2, nb, LRU_BD, LRU_BD), lambda b, c: (0, c, 0, 0))
    scan_buf = pltpu.VMEM((SEQ_ALL, LRU_CB), F32)
    return pl.pallas_call(
        _lru_kernel, grid=(bs, MIX_W // LRU_CB),
        in_specs=[ctx_spec, lat_x_spec, ctx_spec, lat_spec,
                  pl.BlockSpec((CONV_W, LRU_CB), lambda b, c: (0, c)),
                  pl.BlockSpec((1, LRU_CB), lambda b, c: (0, c)),
                  wspec, vec2, wspec, vec2, vec2],
        out_specs=[ctx_spec, lat_spec],
        out_shape=[jax.ShapeDtypeStruct((bs, CTX_LEN, MIX_W), F32), jax.ShapeDtypeStruct((bs, SEQ, MIX_W), F32)],
        scratch_shapes=[pltpu.VMEM((LRU_ROWS, LRU_CB), F32), scan_buf, scan_buf, scan_buf, scan_buf],
        compiler_params=_cparams(("parallel", "parallel")), name="lru",
    )(x_c, yx_l, y_c, yx_l, conv_w, conv_b[None], w_a.astype(BF16), b_a[:, None, :], w_i.astype(BF16),
      b_i[:, None, :], sp)


GT_W = 8
GT_C = 1024


def _grid_transpose_kernel(x_ref, o_ref):
    for j in range(GT_W):
        o_ref[j] = x_ref[:, j, :].astype(o_ref.dtype)


def grid_transpose(src, batch, col0, width, out_dtype):
    assert col0 % GT_C == 0 and width % GT_C == 0
    return pl.pallas_call(
        _grid_transpose_kernel, grid=(batch, GRID_W // GT_W, width // GT_C),
        in_specs=[pl.BlockSpec((GRID_W, GT_W, GT_C), lambda b, j, c: (b, j, col0 // GT_C + c))],
        out_specs=pl.BlockSpec((None, GT_W, GRID_W, GT_C), lambda b, j, c: (b, j, 0, c)),
        out_shape=jax.ShapeDtypeStruct((batch, GRID_W, GRID_W, width), out_dtype),
        compiler_params=_cparams(("parallel", "parallel", "parallel")), name="grid_transpose",
    )(src)


N_CHUNKS = SEQ_ALL // CHUNK
N_CCH = CTX_LEN // CHUNK
GATE_COLBLK = 80
MIX_TB = 256


def _dot(a, b):
    return jnp.dot(a, b, preferred_element_type=F32)


def _dot_nt(a, b):
    return lax.dot_general(a, b, (((1,), (1,)), ((), ())), preferred_element_type=F32)


def _dot_tn(a, b):
    return lax.dot_general(a, b, (((0,), (0,)), ((), ())), preferred_element_type=F32)


def _split2(a):
    hi = a.astype(BF16)
    return hi, (a - hi.astype(F32)).astype(BF16)


def _dot3(a, b):
    ah, al = _split2(a)
    bh, bl = _split2(b)
    return _dot(ah, bh) + (_dot(ah, bl) + _dot(al, bh))


def _dot_mask(m, x):
    hi = x.astype(BF16)
    r1 = x - hi.astype(F32)
    mid = r1.astype(BF16)
    lo = (r1 - mid.astype(F32)).astype(BF16)
    return _dot(m, hi) + (_dot(m, mid) + _dot(m, lo))


def _chunk_masks(backward):
    i = lax.broadcasted_iota(jnp.int32, (CHUNK, CHUNK), 0)
    j = lax.broadcasted_iota(jnp.int32, (CHUNK, CHUNK), 1)
    if backward:
        i, j = j, i
    tri, strict = j <= i, j < i
    return tri, strict, tri.astype(BF16), jnp.where(strict, 1.0, 0.0).astype(F32)


def _lane_pick(blk, lane):
    lanes = lax.broadcasted_iota(jnp.int32, blk.shape, 1)
    return jnp.sum(jnp.where(lanes == lane, blk, 0.0), axis=-1, keepdims=True)


def _softplus(x):
    return jnp.maximum(x, 0.0) + jnp.log1p(jnp.exp(-jnp.abs(x)))


def _silu(x):
    return x * jax.nn.sigmoid(x)


def _bidir_chunks(step, carry, n_ctx, n_all):
    def segment(start, n, carry):
        def body(t, carry):
            return step(start + t, start + n - 1 - t, carry)
        return lax.fori_loop(0, n, body, carry)
    carry = segment(0, n_ctx, carry)
    return segment(n_ctx, n_all - n_ctx, carry)


DN_HALO = 8
DN_ROWS = SEQ_ALL + 3 * DN_HALO
DN_C = 64
DN_GROUP = 256
DN_GC = DN_GROUP // (2 * DN_C)
DN_NCH = SEQ_ALL // DN_C
DN_NCC = CTX_LEN // DN_C
DN_LOCKSTEP = 4


def _dn_kernel(qc_ref, ql_ref, kc_ref, kl_ref, vc_ref, vl_ref, zc_ref, zl_ref, gc_ref, gl_ref, hp_ref,
               cwq_ref, cwk_ref, cwv_ref, nw_ref, ol_ref, oc_ref,
               xpad, qn, kn, vn, u_s, w_s, qg_s, kd_s, at_s, dec_s):
    h = pl.program_id(1)
    halo = jnp.zeros((DN_HALO, DN_HD), F32)
    lat0 = CTX_LEN + 2 * DN_HALO

    def conv_into(xc_ref, xl_ref, cw_ref, dst, post):
        xpad[0:DN_HALO] = halo
        xpad[DN_HALO:DN_HALO + CTX_LEN] = xc_ref[...]
        xpad[DN_HALO + CTX_LEN:lat0] = halo
        xpad[lat0:lat0 + SEQ] = xl_ref[...]
        xpad[lat0 + SEQ:DN_ROWS] = halo
        cw = [cw_ref[j:j + 1, :] for j in range(CONV_W)]

        def block(p0, r0):
            ext = xpad[pl.ds(p0 - DN_HALO, MIX_TB + 2 * DN_HALO), :]
            s0 = DN_HALO - CONV_PAD_L
            acc = cw[0] * ext[s0:s0 + MIX_TB]
            for j in range(1, CONV_W):
                acc = acc + cw[j] * ext[s0 + j:s0 + j + MIX_TB]
            dst[pl.ds(r0, MIX_TB), :] = post(_silu(acc))

        block(DN_HALO, 0)

        def lat_block(k, _):
            block(pl.multiple_of(lat0 + k * MIX_TB, 8), pl.multiple_of(CTX_LEN + k * MIX_TB, 8))
            return 0
        lax.fori_loop(0, SEQ // MIX_TB, lat_block, 0)

    def l2n(scale):
        return lambda t: t * (lax.rsqrt(jnp.sum(t * t, axis=-1, keepdims=True) + EPS) * scale)

    conv_into(qc_ref, ql_ref, cwq_ref, qn, l2n(DN_HD ** -0.5))
    conv_into(kc_ref, kl_ref, cwk_ref, kn, l2n(1.0))
    conv_into(vc_ref, vl_ref, cwv_ref, vn, lambda t: t)

    gi = lax.broadcasted_iota(jnp.int32, (DN_GROUP, DN_GROUP), 0)
    gj = lax.broadcasted_iota(jnp.int32, (DN_GROUP, DN_GROUP), 1)
    same = (gi // DN_C) == (gj // DN_C)
    blk16 = (gi // 16) == (gj // 16)
    bwd = ((gi // DN_C) % 2) == 1
    fwd = jnp.logical_not(bwd)
    li, lj = gi % DN_C, gj % DN_C
    tri = same & ((fwd & (lj <= li)) | (bwd & (lj >= li)))
    strict = same & ((fwd & (lj < li)) | (bwd & (lj > li)))
    eye = jnp.where(gi == gj, 1.0, 0.0).astype(F32)
    m16 = jnp.where(tri, 1.0, 0.0).astype(BF16)

    def both_dirs(xf, xb):
        parts = []
        for c in range(DN_GC):
            parts += [xf[c * DN_C:(c + 1) * DN_C], xb[c * DN_C:(c + 1) * DN_C]]
        return jnp.concatenate(parts, axis=0)

    def prep(ts, g_ref, row0):
        grows = DN_GC * DN_C
        n = range(len(ts))
        r0 = [pl.multiple_of(t * grows, grows) for t in ts]
        gblk = [g_ref[pl.ds(pl.multiple_of(t * grows - row0, grows), grows), :] for t in ts]
        q = [both_dirs(*[qn[pl.ds(r0[i], grows), :]] * 2) for i in n]
        k = [both_dirs(*[kn[pl.ds(r0[i], grows), :]] * 2) for i in n]
        v = [both_dirs(*[vn[pl.ds(r0[i], grows), :]] * 2) for i in n]
        g_col = [both_dirs(*[hp_ref[:, d:d + 1] * _softplus(_lane_pick(gblk[i], d * DN_H + h)
                                                            + hp_ref[:, 2 + d:3 + d]) for d in range(2)]) for i in n]
        beta = [both_dirs(*[jax.nn.sigmoid(_lane_pick(gblk[i], 2 * DN_H + d * DN_H + h)) for d in range(2)])
                for i in n]
        k16 = [k[i].astype(BF16) for i in n]
        kb = [k[i] * beta[i] for i in n]
        gb = [jnp.broadcast_to(g_col[i], (DN_GROUP, DN_HD)) for i in n]
        kk = [_dot_nt(kb[i].astype(BF16), k16[i]) for i in n]
        gcum = [_dot_mask(m16, gb[i]) for i in n]
        tot = [jnp.broadcast_to(jnp.sum(gb[i].reshape(DN_GROUP // DN_C, DN_C, DN_HD), axis=1, keepdims=True),
                                (DN_GROUP // DN_C, DN_C, DN_HD)).reshape(DN_GROUP, DN_HD) for i in n]
        qk = [_dot_nt(q[i].astype(BF16), k16[i]) for i in n]
        dlt = [gcum[i][:, 0:1] - gcum[i].T[0:1, :] for i in n]
        decay = [jnp.where(tri, jnp.exp(jnp.where(tri, dlt[i], 0.0)), 0.0) for i in n]
        lower = [jnp.where(strict, kk[i] * decay[i], 0.0) for i in n]
        x = [-jnp.where(blk16, lower[i], 0.0) for i in n]
        tinv = [eye + x[i] for i in n]
        for _ in range(3):
            x = [_dot3(x[i], x[i]) for i in n]
            tinv = [tinv[i] + _dot3(tinv[i], x[i]) for i in n]
        size = 16
        while size < DN_C:
            inner, outer = (gi // size) == (gj // size), (gi // (2 * size)) == (gj // (2 * size))
            off = [_dot3(jnp.where(outer & jnp.logical_not(inner), lower[i], 0.0), tinv[i]) for i in n]
            tinv = [tinv[i] - _dot3(tinv[i], off[i]) for i in n]
            size *= 2
        eg = [jnp.exp(gcum[i]) for i in n]
        sol = [_dot3(tinv[i], jnp.concatenate([v[i] * beta[i], kb[i] * eg[i]], axis=1)) for i in n]
        for i in n:
            attn = (qk[i] * decay[i]).astype(BF16)
            w16 = sol[i][:, DN_HD:].astype(BF16)
            qg16 = (q[i] * eg[i]).astype(BF16)
            kd16 = (k[i] * jnp.exp(tot[i] - gcum[i])).astype(BF16)
            dec = jnp.exp(tot[i])
            for p in range(DN_GROUP // DN_C):
                d, blk = p % 2, slice(p * DN_C, (p + 1) * DN_C)
                dst = pl.ds(r0[i] + (p // 2) * DN_C, DN_C)
                u_s[d, dst, :] = sol[i][blk, :DN_HD]
                w_s[d, dst, :] = w16[blk]
                at_s[d, dst, :] = attn[blk, blk]
                qg_s[d, dst, :] = qg16[blk]
                kd_s[d, dst, :] = kd16[blk]
                dec_s[d, pl.ds(pl.multiple_of((DN_GC * ts[i] + p // 2) * 8, 8), 8), :] = dec[p * DN_C:p * DN_C + 8]
        return 0

    n_ctx_groups, n_groups = DN_NCC // DN_GC, DN_NCH // DN_GC
    assert (n_groups - n_ctx_groups) % DN_LOCKSTEP == 0
    prep(list(range(n_ctx_groups)), gc_ref, 0)
    lax.fori_loop(0, (n_groups - n_ctx_groups) // DN_LOCKSTEP,
                  lambda t, _: prep([n_ctx_groups + DN_LOCKSTEP * t + i for i in range(DN_LOCKSTEP)],
                                    gl_ref, CTX_LEN), 0)
    o_dst = (qn, kn)
    o_f, o_b = o_dst

    def step(cf, cb, carry):
        dirs = range(2)
        rows = [pl.ds(pl.multiple_of(c * DN_C, DN_C), DN_C) for c in (cf, cb)]
        s16 = [carry[d].astype(BF16) for d in dirs]
        ws = [_dot(w_s[d, rows[d], :], s16[d]) for d in dirs]
        qs = [_dot(qg_s[d, rows[d], :], s16[d]) for d in dirs]
        v_new = [(u_s[d, rows[d], :] - ws[d]).astype(BF16) for d in dirs]
        kv = [_dot_tn(kd_s[d, rows[d], :], v_new[d]) for d in dirs]
        av = [_dot(at_s[d, rows[d], :], v_new[d]) for d in dirs]
        out = []
        for d, c in zip(dirs, (cf, cb)):
            o_dst[d][rows[d], :] = qs[d] + av[d]
            g_last = dec_s[d, pl.ds(pl.multiple_of(c * 8, 8), 8), :][0:1]
            out.append(carry[d] * g_last + kv[d])
        return tuple(out)

    z0 = jnp.zeros((DN_HD, DN_HD), F32)
    _bidir_chunks(step, (z0, z0), DN_NCC, DN_NCH)

    nw = nw_ref[...]

    def finish(rows, z):
        o = o_f[rows, :] + o_b[rows, :]
        o = o * lax.rsqrt(jnp.mean(o * o, axis=-1, keepdims=True) + EPS) * nw
        return (o * _silu(z)).astype(BF16)

    oc_ref[...] = finish(pl.ds(0, CTX_LEN), zc_ref[...])

    def lat_finish(k, _):
        r0 = pl.multiple_of(k * MIX_TB, 8)
        ol_ref[pl.ds(r0, MIX_TB), :] = finish(pl.ds(pl.multiple_of(CTX_LEN + k * MIX_TB, 8), MIX_TB),
                                              zl_ref[pl.ds(r0, MIX_TB), :])
        return 0
    lax.fori_loop(0, SEQ // MIX_TB, lat_finish, 0)


def deltanet_mixer(proj, conv_w, a_log, dt_bias, norm_w):
    hp = jnp.stack([-jnp.exp(a_log[0]), -jnp.exp(a_log[1]), dt_bias[0], dt_bias[1]], axis=-1)[:, None, :]
    ctx_blk0 = N_LAT // CTX_LEN

    def seg_specs(col0):
        cb = col0 // DN_HD
        return [pl.BlockSpec((CTX_LEN, DN_HD), lambda b, h: (ctx_blk0 + b, cb + h)),
                pl.BlockSpec((SEQ, DN_HD), lambda b, h: (b, cb + h))]

    def conv_spec(k):
        return pl.BlockSpec((CONV_W, DN_HD), lambda b, h: (0, k * DN_H + h))

    in_specs = (seg_specs(COL_DN_QKVZ) + seg_specs(COL_DN_QKVZ + MIX_W) + seg_specs(COL_DN_QKVZ + 2 * MIX_W)
                + seg_specs(COL_DN_QKVZ + 3 * MIX_W)
                + [pl.BlockSpec((CTX_LEN, 128), lambda b, h: (ctx_blk0 + b, GATE_COLBLK)),
                   pl.BlockSpec((SEQ, 128), lambda b, h: (b, GATE_COLBLK)),
                   pl.BlockSpec((None, 1, 4), lambda b, h: (h, 0, 0)),
                   conv_spec(0), conv_spec(1), conv_spec(2),
                   pl.BlockSpec((1, DN_HD), lambda b, h: (0, 0))])
    seq_f32 = pltpu.VMEM((SEQ_ALL, DN_HD), F32)
    dir_bf16 = pltpu.VMEM((2, SEQ_ALL, DN_HD), BF16)
    ol, oc = pl.pallas_call(
        _dn_kernel, grid=(BATCH, DN_H), in_specs=in_specs,
        out_specs=[pl.BlockSpec((SEQ, DN_HD), lambda b, h: (b, h)),
                   pl.BlockSpec((CTX_LEN, DN_HD), lambda b, h: (b, h))],
        out_shape=[jax.ShapeDtypeStruct((N_LAT, MIX_W), BF16), jax.ShapeDtypeStruct((N_CTX, MIX_W), BF16)],
        scratch_shapes=[pltpu.VMEM((DN_ROWS, DN_HD), F32), seq_f32, seq_f32, seq_f32,
                        pltpu.VMEM((2, SEQ_ALL, DN_HD), F32), dir_bf16, dir_bf16, dir_bf16,
                        pltpu.VMEM((2, SEQ_ALL, DN_C), BF16), pltpu.VMEM((2, DN_NCH * 8, DN_HD), F32)],
        compiler_params=_cparams(("parallel", "arbitrary")), name="deltanet",
    )(*([proj] * 10), hp, conv_w, conv_w, conv_w, norm_w[None])
    return ol, oc


def _ml_kernel(qc_ref, ql_ref, kc_ref, kl_ref, vc_ref, vl_ref, oc_ref, ol_ref, gc_ref, gl_ref, hp_ref, nw_ref,
               yl_ref, yc_ref, h_f, h_b):
    hh = pl.program_id(1)
    masks = [_chunk_masks(False), _chunk_masks(True)]
    ones = jnp.ones((CHUNK, CHUNK), BF16)
    eye = jnp.where(lax.broadcasted_iota(jnp.int32, (CHUNK, CHUNK), 0)
                    == lax.broadcasted_iota(jnp.int32, (CHUNK, CHUNK), 1), 1.0, 0.0).astype(F32)

    h_dst = (h_f, h_b)
    dirs = range(2)

    def gate_terms(cs, refs, row0):
        q_ref, k_ref, v_ref, g_ref = refs
        rows = [pl.ds(pl.multiple_of(c * CHUNK - row0, CHUNK), CHUNK) for c in cs]
        gblk = [g_ref[rows[d], :] for d in dirs]
        ig = [_lane_pick(gblk[d], 4 * DN_H + d * ML_H + hh) + hp_ref[:, d:d + 1] for d in dirs]
        lf = [-_softplus(-(_lane_pick(gblk[d], 4 * DN_H + 2 * ML_H + d * ML_H + hh) + hp_ref[:, 2 + d:3 + d]))
              for d in dirs]
        q16 = [(q_ref[rows[d], :] * (ML_DQK ** -0.5)).astype(BF16) for d in dirs]
        bcum = [_dot_mask(masks[d][2], jnp.broadcast_to(lf[d], (CHUNK, ML_DQK)))[:, 0:1] for d in dirs]
        dsum = [_dot_mask(masks[d][2], lf[d] * masks[d][3]) + _dot_mask(ones, ig[d] * eye) for d in dirs]
        sqk = [_dot_nt(q16[d], k_ref[rows[d], :].astype(BF16)) for d in dirs]
        dmat = [jnp.where(masks[d][0], dsum[d], -jnp.inf) for d in dirs]
        dmax = [jnp.max(dmat[d], axis=-1, keepdims=True) for d in dirs]
        tot = [jnp.sum(lf[d], axis=0, keepdims=True) for d in dirs]
        logw = [tot[d] - bcum[d] + ig[d] for d in dirs]
        logw_max = [jnp.max(logw[d], axis=0, keepdims=True) for d in dirs]
        return tuple((bcum[d], dmat[d], dmax[d], tot[d], logw[d], logw_max[d], sqk[d]) for d in dirs)

    def advance(cs, nxt, states, pre, refs, row0):
        q_ref, k_ref, v_ref, _ = refs
        rows = [pl.ds(pl.multiple_of(c * CHUNK - row0, CHUNK), CHUNK) for c in cs]
        q = [q_ref[rows[d], :] * (ML_DQK ** -0.5) for d in dirs]
        k = [k_ref[rows[d], :] for d in dirs]
        q16 = [q[d].astype(BF16) for d in dirs]
        v16 = [v_ref[rows[d], :].astype(BF16) for d in dirs]
        qc = [_dot(q16[d], states[d][0].astype(BF16)) for d in dirs]
        pre_next = gate_terms(nxt, refs, row0)
        out = []
        inter, m_t, w_inter, wmat, m_new, keep, kw = [], [], [], [], [], [], []
        for d in dirs:
            bcum, dmat, dmax, tot, logw, logw_max, sqk = pre[d]
            m = states[d][2]
            inter.append(bcum + m)
            m_t.append(jnp.maximum(inter[d], dmax))
            w_inter.append(jnp.exp(inter[d] - m_t[d]))
            wmat.append(jnp.exp(dmat - m_t[d]) * sqk)
            m_new.append(jnp.maximum(tot + m, logw_max))
            keep.append(jnp.exp(tot + m - m_new[d]))
            kw.append(k[d] * jnp.exp(logw - m_new[d]))
        wv = [_dot(wmat[d].astype(BF16), v16[d]) for d in dirs]
        kv = [_dot_tn(kw[d].astype(BF16), v16[d]) for d in dirs]
        for d in dirs:
            c_s, n_s, _ = states[d]
            num = w_inter[d] * qc[d] + wv[d]
            den = w_inter[d] * jnp.sum(q[d] * n_s, axis=-1, keepdims=True) + jnp.sum(wmat[d], axis=-1, keepdims=True)
            h_dst[d][pl.ds(pl.multiple_of(cs[d] * CHUNK, CHUNK), CHUNK), :] = (
                num / jnp.maximum(jnp.abs(den), jnp.exp(-m_t[d])))
            out.append((keep[d] * c_s + kv[d], keep[d] * n_s + jnp.sum(kw[d], axis=0, keepdims=True), m_new[d]))
        return tuple(out), pre_next

    ctx_refs = (qc_ref, kc_ref, vc_ref, gc_ref)
    lat_refs = (ql_ref, kl_ref, vl_ref, gl_ref)

    def segment(refs, start, n, states):
        chunks = lambda t: (start + t, start + n - 1 - t)

        def body(t, carry):
            states, pre = carry
            return advance(chunks(t), chunks(jnp.minimum(t + 1, n - 1)), states, pre, refs, start * CHUNK)
        return lax.fori_loop(0, n, body, (states, gate_terms(chunks(0), refs, start * CHUNK)))[0]

    s0 = (jnp.zeros((ML_DQK, ML_DV), F32), jnp.zeros((1, ML_DQK), F32), jnp.zeros((1, 1), F32))
    states = segment(ctx_refs, 0, N_CCH, (s0, s0))
    segment(lat_refs, N_CCH, N_CHUNKS - N_CCH, states)

    nw = nw_ref[...]

    def finish(rows, o):
        hsum = h_f[rows, :] + h_b[rows, :]
        hsum = hsum * lax.rsqrt(jnp.mean(hsum * hsum, axis=-1, keepdims=True) + EPS) * nw
        return (hsum * jax.nn.sigmoid(o)).astype(BF16)

    yc_ref[...] = finish(pl.ds(0, CTX_LEN), oc_ref[...])

    def lat_finish(k, _):
        r0 = pl.multiple_of(k * MIX_TB, 8)
        yl_ref[pl.ds(r0, MIX_TB), :] = finish(pl.ds(pl.multiple_of(CTX_LEN + k * MIX_TB, 8), MIX_TB),
                                              ol_ref[pl.ds(r0, MIX_TB), :])
        return 0
    lax.fori_loop(0, SEQ // MIX_TB, lat_finish, 0)


def mlstm_mixer(proj, i_bias, f_bias, norm_w):
    hp = jnp.stack([i_bias[0], i_bias[1], f_bias[0], f_bias[1]], axis=-1)[:, None, :]
    ctx_blk0 = N_LAT // CTX_LEN

    def seg_specs(col0, w):
        cb = col0 // w
        return [pl.BlockSpec((CTX_LEN, w), lambda b, h: (ctx_blk0 + b, cb + h)),
                pl.BlockSpec((SEQ, w), lambda b, h: (b, cb + h))]

    in_specs = (seg_specs(COL_ML_Q, ML_DQK) + seg_specs(COL_ML_K, ML_DQK) + seg_specs(COL_ML_V, ML_DV)
                + seg_specs(COL_ML_O, ML_DV)
                + [pl.BlockSpec((CTX_LEN, 128), lambda b, h: (ctx_blk0 + b, GATE_COLBLK)),
                   pl.BlockSpec((SEQ, 128), lambda b, h: (b, GATE_COLBLK)),
                   pl.BlockSpec((None, 1, 4), lambda b, h: (h, 0, 0)),
                   pl.BlockSpec((1, ML_DV), lambda b, h: (0, h))])
    seq_f32 = pltpu.VMEM((SEQ_ALL, ML_DV), F32)
    yl, yc = pl.pallas_call(
        _ml_kernel, grid=(BATCH, ML_H), in_specs=in_specs,
        out_specs=[pl.BlockSpec((SEQ, ML_DV), lambda b, h: (b, h)),
                   pl.BlockSpec((CTX_LEN, ML_DV), lambda b, h: (b, h))],
        out_shape=[jax.ShapeDtypeStruct((N_LAT, MIX_W), BF16), jax.ShapeDtypeStruct((N_CTX, MIX_W), BF16)],
        scratch_shapes=[seq_f32, seq_f32],
        compiler_params=_cparams(("parallel", "arbitrary")), name="mlstm",
    )(*([proj] * 10), hp, norm_w[None])
    return yl, yc


PROJ_TM = (N_LAT + N_CTX) // 8
PROJ_TN = 768
PROJ_PAD = -(-PROJ_W // PROJ_TN) * PROJ_TN
COL_S5, COL_DN_QKVZ, COL_ML_Q, COL_ML_K, COL_ML_V, COL_ML_O, COL_LRU_Y, COL_LRU_X = (
    0, 1024, 5120, 5632, 6144, 7168, 8192, 9216)
COL_DN_A, COL_DN_B, COL_ML_I, COL_ML_F = 10240, 10256, 10272, 10280


def permute_proj_weight(w):
    o = [0] + [int(v) for v in np.cumsum(PROJ_SIZES)]
    parts = [w[:, o[0]:o[5]], w[:, o[7]:o[11]], w[:, o[13]:o[15]], w[:, o[5]:o[7]], w[:, o[11]:o[13]]]
    return jnp.pad(jnp.concatenate(parts, axis=1).astype(BF16), ((0, 0), (0, PROJ_PAD - PROJ_W)))


def hybrid_mixer(proj, s5_p, dn_p, ml_p, lru_p):
    def ctx_seg(c0):
        return proj[N_LAT:, c0:c0 + MIX_W].reshape(BATCH, CTX_LEN, MIX_W)

    def to_row_major(t):
        return grid_transpose(t.reshape(BATCH * GRID_W, GRID_W, MIX_W), BATCH, 0, MIX_W, BF16).reshape(N_LAT, MIX_W)

    grid = proj.reshape((N_LAT + N_CTX) // GRID_W, GRID_W, PROJ_PAD)
    u_l = grid_transpose(grid, BATCH, COL_S5, MIX_W, F32).reshape(BATCH, SEQ, MIX_W)
    yx_l = grid_transpose(grid, BATCH, COL_LRU_Y, 2 * MIX_W, F32).reshape(BATCH, SEQ, 2 * MIX_W)
    a_c, a_l = s5_mixer(ctx_seg(COL_S5), u_l, *s5_p)
    d_c, d_l = lru_mixer(ctx_seg(COL_LRU_Y), ctx_seg(COL_LRU_X), yx_l, *lru_p)
    flat = lambda t_l, t_c: (to_row_major(t_l), t_c.reshape(N_CTX, MIX_W).astype(BF16))
    return [flat(a_l, a_c), deltanet_mixer(proj, *dn_p), mlstm_mixer(proj, *ml_p), flat(d_l, d_c)]


def kernel(x, c, ctx, c_ctx, w_ada, b_ada, norm1, norm2, norm_f, w_in, w_out, s5_lam_re, s5_lam_im, s5_log_step,
           s5_b_re, s5_b_im, s5_c_re, s5_c_im, s5_d, s5_w_glu, s5_b_glu, dn_conv, dn_a_log, dn_dt_bias, dn_norm,
           ml_i_bias, ml_f_bias, ml_norm, lru_conv_w, lru_conv_b, lru_w_a, lru_b_a, lru_w_i, lru_b_i, lru_lam,
           router_w, router_bias, moe_w1, moe_w3, moe_w2):
    d = D_MODEL
    xa = jnp.concatenate([x.reshape(N_LAT, d), ctx.reshape(N_CTX, d)], axis=0)
    cond = jnp.concatenate([jax.nn.silu(c), jax.nn.silu(c_ctx)[None], jnp.zeros((16 - BATCH - 1, d), F32)], axis=0)
    cond = cond.astype(BF16)
    router = (router_w.T, router_bias.reshape(N_EXPERTS, 1))

    s5_ops = jax.vmap(s5_chunk_operators)(s5_lam_re, s5_lam_im, s5_log_step, s5_b_re, s5_b_im, s5_c_re, s5_c_im)

    for l in range(DEPTH):
        last = l == DEPTH - 1
        mod = ada_matmul(l, cond, w_ada, b_ada[:, None, :], 1024)[:BATCH + 1].reshape(BATCH + 1, 6, 1, d)
        shift1, scale1, gate1, shift2, scale2, gate2 = (mod[:, k] for k in range(6))

        h1 = norm_mod(xa, norm1[l][None], scale1, shift1, SEQ, 256)
        proj = matmul(h1, permute_proj_weight(w_in[l]), PROJ_TM, PROJ_TN)
        slabs = hybrid_mixer(
            proj,
            (tuple(op[l] for op in s5_ops), s5_d[l], s5_w_glu[l], s5_b_glu[l]),
            (dn_conv[l], dn_a_log[l], dn_dt_bias[l], dn_norm[l]),
            (ml_i_bias[l], ml_f_bias[l], ml_norm[l]),
            (lru_conv_w[l], lru_conv_b[l], lru_w_a[l], lru_b_a[l], lru_w_i[l], lru_b_i[l], lru_lam[l]))
        if last:
            mix = [s[0] for s in slabs]
        else:
            mix = [jnp.concatenate(s, axis=0) for s in slabs]
        out_tm = 1024 if mix[0].shape[0] % 1024 == 0 else 512
        xa = matmul_gated_residual(mix, w_out[l].astype(BF16), xa, gate1, SEQ, out_tm, 1024)

        h2, idx, wsel = norm_mod(xa, norm2[l][None], scale2, shift2, SEQ, 256, router=router)
        y0, y1 = moe_ffn(l, h2, idx, wsel, moe_w1, moe_w3, moe_w2)
        xa = gated_add(xa, y0, y1, gate2, SEQ, 256)

    return final_norm(xa[:N_LAT], norm_f[None], 256).reshape(BATCH, SEQ, d)
```

```python
import functools
import math

import numpy as np
import jax
import jax.numpy as jnp
from jax import lax
from jax.experimental import pallas as pl
from jax.experimental.pallas import tpu as pltpu

D_MODEL = 4096
BATCH = 2
SEQ = 4096
DEPTH = 2
GRID_W = 64
CTX_LEN = 256
MIX_W = D_MODEL // 4
MIX_TOTAL = 4 * MIX_W
CHUNK = 64
CONV_W = 4
CONV_PAD_L = 2
CONV_PAD_R = 1
EPS = 1e-6
S5_CH = 16
S5_G = MIX_W // S5_CH
S5_P = 64
DN_HD = 128
DN_H = MIX_W // DN_HD
ML_H = 4
ML_DV = MIX_W // ML_H
ML_DQK = ML_DV // 2
LRU_BLOCKS = 8
LRU_BD = MIX_W // LRU_BLOCKS
LRU_C = 8.0
N_EXPERTS = 16
N_GROUPS = 4
E_PER_G = N_EXPERTS // N_GROUPS
TOP_K = 2
D_FF = D_MODEL // 4
PROJ_SIZES = (MIX_W, MIX_W, MIX_W, MIX_W, MIX_W, 2 * DN_H, 2 * DN_H, ML_H * ML_DQK, ML_H * ML_DQK, MIX_W, MIX_W,
              2 * ML_H, 2 * ML_H, MIX_W, MIX_W)
PROJ_W = sum(PROJ_SIZES)

N_LAT = BATCH * SEQ
N_CTX = BATCH * CTX_LEN
VMEM_LIMIT = 56 * 1024 * 1024

F32 = jnp.float32
BF16 = jnp.bfloat16


def _cparams(sem):
    return pltpu.CompilerParams(dimension_semantics=sem, vmem_limit_bytes=VMEM_LIMIT)


def _mm_kernel(a_ref, b_ref, o_ref):
    o_ref[...] = jnp.dot(a_ref[...], b_ref[...], preferred_element_type=F32).astype(o_ref.dtype)


def matmul(a, b, tm, tn, out_dtype=F32):
    m, k = a.shape
    n = b.shape[1]
    assert m % tm == 0 and n % tn == 0
    return pl.pallas_call(
        _mm_kernel,
        grid=(m // tm, n // tn),
        in_specs=[pl.BlockSpec((tm, k), lambda i, j: (i, 0)),
                  pl.BlockSpec((k, tn), lambda i, j: (0, j))],
        out_specs=pl.BlockSpec((tm, tn), lambda i, j: (i, j)),
        out_shape=jax.ShapeDtypeStruct((m, n), out_dtype),
        compiler_params=_cparams(("parallel", "arbitrary")),
        name="mm",
    )(a, b)


def _mm_resid_kernel(a0_ref, a1_ref, a2_ref, a3_ref, b_ref, r_ref, g_ref, o_ref):
    acc = None
    for s, a_ref in enumerate((a0_ref, a1_ref, a2_ref, a3_ref)):
        part = jnp.dot(a_ref[...], b_ref[s * MIX_W:(s + 1) * MIX_W, :], preferred_element_type=F32)
        acc = part if acc is None else acc + part
    o_ref[...] = r_ref[...] + g_ref[...] * acc


def matmul_gated_residual(a_slabs, b, resid, gate, rows_per_gate, tm, tn):
    m = a_slabs[0].shape[0]
    k, n = b.shape
    assert m % tm == 0 and n % tn == 0 and rows_per_gate % tm == 0 and k == 4 * MIX_W
    return pl.pallas_call(
        _mm_resid_kernel,
        grid=(m // tm, n // tn),
        in_specs=[pl.BlockSpec((tm, MIX_W), lambda i, j: (i, 0))] * 4 + [
                  pl.BlockSpec((k, tn), lambda i, j: (0, j)),
                  pl.BlockSpec((tm, tn), lambda i, j: (i, j)),
                  pl.BlockSpec((None, 1, tn), lambda i, j: ((i * tm) // rows_per_gate, 0, j))],
        out_specs=pl.BlockSpec((tm, tn), lambda i, j: (i, j)),
        out_shape=jax.ShapeDtypeStruct((m, n), F32),
        compiler_params=_cparams(("parallel", "arbitrary")),
        name="mm_resid",
    )(*a_slabs, b, resid, gate)


def _ada_kernel(a_ref, w_ref, b_ref, o_ref):
    o_ref[...] = jnp.dot(a_ref[...], w_ref[...].astype(BF16), preferred_element_type=F32) + b_ref[...]


def ada_matmul(layer, a, w, bias, tn):
    m, k = a.shape
    n = w.shape[2]
    return pl.pallas_call(
        _ada_kernel,
        grid=(n // tn,),
        in_specs=[pl.BlockSpec((m, k), lambda j: (0, 0)),
                  pl.BlockSpec((None, k, tn), lambda j: (layer, 0, j)),
                  pl.BlockSpec((None, 1, tn), lambda j: (layer, 0, j))],
        out_specs=pl.BlockSpec((m, tn), lambda j: (0, j)),
        out_shape=jax.ShapeDtypeStruct((m, n), F32),
        compiler_params=_cparams(("arbitrary",)),
        name="ada",
    )(a, w, bias)


def _modulated_norm(x, g, sc, sh):
    ms = jnp.mean(x * x, axis=-1, keepdims=True)
    y = x * lax.rsqrt(ms + EPS) * g
    return y * (1.0 + sc) + sh


def _norm_mod_kernel(x_ref, g_ref, sc_ref, sh_ref, o_ref):
    o_ref[...] = _modulated_norm(x_ref[...], g_ref[...], sc_ref[...], sh_ref[...]).astype(o_ref.dtype)


def _pair_top2_sum(v):
    a, b, c, d = v
    return jnp.maximum(jnp.maximum(jnp.maximum(a + b, a + c), jnp.maximum(a + d, b + c)),
                       jnp.maximum(b + d, c + d))


def _norm_mod_router_kernel(x_ref, g_ref, sc_ref, sh_ref, rwt_ref, rb_ref, o_ref, idx_ref, w_ref):
    h = _modulated_norm(x_ref[...], g_ref[...], sc_ref[...], sh_ref[...])
    o_ref[...] = h.astype(o_ref.dtype)
    logits = lax.dot_general(rwt_ref[...], h, (((1,), (1,)), ((), ())),
                             precision=lax.Precision.HIGHEST, preferred_element_type=F32)
    scores = jax.nn.sigmoid(logits)
    biased = scores + rb_ref[...]
    s = [scores[e:e + 1, :] for e in range(N_EXPERTS)]
    b = [biased[e:e + 1, :] for e in range(N_EXPERTS)]
    gs = [_pair_top2_sum(b[E_PER_G * g:E_PER_G * (g + 1)]) for g in range(N_GROUPS)]
    best, gsel = gs[0], jnp.zeros_like(gs[0], dtype=jnp.int32)
    for g in range(1, N_GROUPS):
        better = gs[g] > best
        gsel = jnp.where(better, g, gsel)
        best = jnp.where(better, gs[g], best)
    vb, vs = [], []
    for k in range(E_PER_G):
        bk, sk = b[k], s[k]
        for g in range(1, N_GROUPS):
            bk = jnp.where(gsel == g, b[E_PER_G * g + k], bk)
            sk = jnp.where(gsel == g, s[E_PER_G * g + k], sk)
        vb.append(bk)
        vs.append(sk)
    m1, i1, w1 = vb[0], jnp.zeros_like(gsel), vs[0]
    for k in range(1, E_PER_G):
        better = vb[k] > m1
        i1 = jnp.where(better, k, i1)
        w1 = jnp.where(better, vs[k], w1)
        m1 = jnp.where(better, vb[k], m1)
    m2, i2, w2, have = vb[0], jnp.zeros_like(gsel), vs[0], i1 != 0
    for k in range(1, E_PER_G):
        valid = i1 != k
        better = valid & (jnp.logical_not(have) | (vb[k] > m2))
        i2 = jnp.where(better, k, i2)
        w2 = jnp.where(better, vs[k], w2)
        m2 = jnp.where(better, vb[k], m2)
        have = have | valid
    tot = w1 + w2
    idx_ref[0:1, :] = gsel * E_PER_G + i1
    idx_ref[1:2, :] = gsel * E_PER_G + i2
    w_ref[0:1, :] = w1 / tot
    w_ref[1:2, :] = w2 / tot


def norm_mod(x, g, sc, sh, rows_per_mod, tm, router=None):
    m, d = x.shape
    assert m % tm == 0 and rows_per_mod % tm == 0
    mod_spec = pl.BlockSpec((None, 1, d), lambda i: ((i * tm) // rows_per_mod, 0, 0))
    in_specs = [pl.BlockSpec((tm, d), lambda i: (i, 0)), pl.BlockSpec((1, d), lambda i: (0, 0)), mod_spec, mod_spec]
    o_spec = pl.BlockSpec((tm, d), lambda i: (i, 0))
    o_shape = jax.ShapeDtypeStruct((m, d), BF16)
    if router is None:
        return pl.pallas_call(
            _norm_mod_kernel, grid=(m // tm,), in_specs=in_specs, out_specs=o_spec, out_shape=o_shape,
            compiler_params=_cparams(("parallel",)), name="norm_mod",
        )(x, g, sc, sh)
    rwt, rb = router
    in_specs += [pl.BlockSpec((N_EXPERTS, d), lambda i: (0, 0)), pl.BlockSpec((N_EXPERTS, 1), lambda i: (0, 0))]
    sel_spec = pl.BlockSpec((TOP_K, tm), lambda i: (0, i))
    return pl.pallas_call(
        _norm_mod_router_kernel, grid=(m // tm,), in_specs=in_specs,
        out_specs=[o_spec, sel_spec, sel_spec],
        out_shape=[o_shape, jax.ShapeDtypeStruct((TOP_K, m), jnp.int32), jax.ShapeDtypeStruct((TOP_K, m), F32)],
        compiler_params=_cparams(("parallel",)), name="norm_mod_router",
    )(x, g, sc, sh, rwt, rb)


def _gated_add_kernel(x_ref, y0_ref, y1_ref, g_ref, o_ref):
    o_ref[...] = x_ref[...] + g_ref[...] * (y0_ref[...] + y1_ref[...])


def gated_add(x, y0, y1, gate, rows_per_gate, tm):
    m, d = x.shape
    assert m % tm == 0 and rows_per_gate % tm == 0
    row = pl.BlockSpec((tm, d), lambda i: (i, 0))
    return pl.pallas_call(
        _gated_add_kernel, grid=(m // tm,),
        in_specs=[row, row, row, pl.BlockSpec((None, 1, d), lambda i: ((i * tm) // rows_per_gate, 0, 0))],
        out_specs=row, out_shape=jax.ShapeDtypeStruct((m, d), F32),
        compiler_params=_cparams(("parallel",)), name="gated_add",
    )(x, y0, y1, gate)


def _final_norm_kernel(x_ref, g_ref, o_ref):
    x = x_ref[...]
    ms = jnp.mean(x * x, axis=-1, keepdims=True)
    o_ref[...] = x * lax.rsqrt(ms + EPS) * g_ref[...]


def final_norm(x, g, tm):
    m, d = x.shape
    return pl.pallas_call(
        _final_norm_kernel, grid=(m // tm,),
        in_specs=[pl.BlockSpec((tm, d), lambda i: (i, 0)), pl.BlockSpec((1, d), lambda i: (0, 0))],
        out_specs=pl.BlockSpec((tm, d), lambda i: (i, 0)),
        out_shape=jax.ShapeDtypeStruct((m, d), F32),
        compiler_params=_cparams(("parallel",)), name="final_norm",
    )(x, g)


MOE_TM = 512
MOE_KC = 1024
MOE_FC = 256
MOE_NK = D_MODEL // MOE_KC
MOE_NF = D_FF // MOE_FC
MOE_PARTS = 1


def _moe_kernel(te_ref, nu_ref, x_ref, w1_ref, w3_ref, w2_ref, g_ref, *rest, tile0, has_prev):
    o_ref, hg_s, hu_s, act_s = rest[1:] if has_prev else rest
    t = pl.program_id(0)
    s = pl.program_id(1)
    used = t + tile0 < nu_ref[0]

    @pl.when(used & (s < MOE_NK))
    def _():
        x = x_ref[...]
        pg = jnp.dot(x, w1_ref[...].astype(BF16), preferred_element_type=F32)
        pu = jnp.dot(x, w3_ref[...].astype(BF16), preferred_element_type=F32)

        @pl.when(s == 0)
        def _():
            hg_s[...] = pg
            hu_s[...] = pu

        @pl.when(s > 0)
        def _():
            hg_s[...] += pg
            hu_s[...] += pu

        @pl.when(s == MOE_NK - 1)
        def _():
            hg = hg_s[...]
            act = ((hg * jax.nn.sigmoid(hg)) * hu_s[...] * g_ref[...]).astype(BF16)
            for f in range(MOE_NF):
                act_s[f] = act[:, f * MOE_FC:(f + 1) * MOE_FC]

    @pl.when(used & (s >= MOE_NK))
    def _():
        y = jnp.dot(act_s[s - MOE_NK], w2_ref[...].astype(BF16), preferred_element_type=F32)

        @pl.when(s == MOE_NK)
        def _():
            o_ref[...] = y

        @pl.when(s > MOE_NK)
        def _():
            o_ref[...] += y

    @pl.when(jnp.logical_not(used) & (s == 0))
    def _():
        o_ref[...] = jnp.zeros_like(o_ref)


def moe_grouped(layer, tile_expert, n_used, xs, w1, w3, w2, gs, tile0, ys_prev):
    rp, d = xs.shape
    r = gs.shape[0]

    def live(t, nu):
        return t + tile0 < nu[0]

    def kidx(t, s, nu):
        return jnp.where(live(t, nu), jnp.minimum(s, MOE_NK - 1), MOE_NK - 1)

    def fidx(t, s, nu):
        return jnp.where(live(t, nu), jnp.maximum(s - MOE_NK, 0), MOE_NF - 1)

    in_specs = [pl.BlockSpec((MOE_TM, MOE_KC), lambda t, s, te, nu: (t, kidx(t, s, nu))),
                pl.BlockSpec((None, None, MOE_KC, D_FF),
                             lambda t, s, te, nu: (layer, te[t + tile0], kidx(t, s, nu), 0)),
                pl.BlockSpec((None, None, MOE_KC, D_FF),
                             lambda t, s, te, nu: (layer, te[t + tile0], kidx(t, s, nu), 0)),
                pl.BlockSpec((None, None, MOE_FC, d), lambda t, s, te, nu: (layer, te[t + tile0], fidx(t, s, nu), 0)),
                pl.BlockSpec((MOE_TM, 1), lambda t, s, te, nu: (t + tile0, 0))]
    args = [tile_expert, n_used, xs, w1, w3, w2, gs]
    aliases = {}
    if ys_prev is not None:
        in_specs.append(pl.BlockSpec(memory_space=pl.ANY))
        aliases = {len(args): 0}
        args.append(ys_prev)
    grid_spec = pltpu.PrefetchScalarGridSpec(
        num_scalar_prefetch=2,
        grid=(rp // MOE_TM, MOE_NK + MOE_NF),
        in_specs=in_specs,
        out_specs=pl.BlockSpec((MOE_TM, d), lambda t, s, te, nu: (t + tile0, 0)),
        scratch_shapes=[pltpu.VMEM((MOE_TM, D_FF), F32), pltpu.VMEM((MOE_TM, D_FF), F32),
                        pltpu.VMEM((MOE_NF, MOE_TM, MOE_FC), BF16)],
    )
    return pl.pallas_call(
        functools.partial(_moe_kernel, tile0=tile0, has_prev=ys_prev is not None), grid_spec=grid_spec,
        out_shape=jax.ShapeDtypeStruct((r, d), F32), input_output_aliases=aliases,
        compiler_params=_cparams(("arbitrary", "arbitrary")), name="moe",
    )(*args)


CUMSUM_BLK = 256


def _onehot_cumsum(onehot):
    m, e = onehot.shape
    x = onehot.reshape(m // CUMSUM_BLK, CUMSUM_BLK, e)
    tri = jnp.tril(jnp.ones((CUMSUM_BLK, CUMSUM_BLK), F32))
    within = jnp.einsum('ij,bje->bie', tri, x, precision=lax.Precision.HIGHEST)
    total = within[:, -1, :]
    return (within + (jnp.cumsum(total, axis=0) - total)[:, None, :]).reshape(m, e)


def moe_dispatch(idx, wsel):
    n = idx.shape[1]
    e_flat = idx.reshape(-1)
    onehot = (e_flat[:, None] == jnp.arange(N_EXPERTS, dtype=jnp.int32)[None, :]).astype(F32)
    csum = _onehot_cumsum(onehot)
    counts = csum[-1].astype(jnp.int32)
    rank = jnp.sum(onehot * csum, axis=1).astype(jnp.int32) - 1
    padded = ((counts + MOE_TM - 1) // MOE_TM) * MOE_TM
    seg_end = jnp.cumsum(padded)
    seg_start = seg_end - padded
    dest = jnp.sum(onehot * seg_start.astype(F32)[None, :], axis=1).astype(jnp.int32) + rank
    part_rows = MOE_TM * MOE_PARTS
    rows = -(-(TOP_K * n + N_EXPERTS * MOE_TM) // part_rows) * part_rows
    pair = jnp.zeros((rows,), jnp.int32).at[dest].set(jnp.arange(TOP_K * n, dtype=jnp.int32) + 1) - 1
    src_tok = jnp.maximum(pair, 0) % n
    gate = jnp.where(pair >= 0, wsel.reshape(-1).at[jnp.maximum(pair, 0)].get(mode="promise_in_bounds"), 0.0)
    n_tiles = rows // MOE_TM
    n_used = (seg_end[-1] // MOE_TM).astype(jnp.int32)
    tile_start = jnp.arange(n_tiles, dtype=jnp.int32) * MOE_TM
    te = jnp.minimum(jnp.searchsorted(seg_end, tile_start, side='right'), N_EXPERTS - 1).astype(jnp.int32)
    te = jnp.where(jnp.arange(n_tiles) < n_used, te, te[jnp.maximum(n_used - 1, 0)])
    return dest.reshape(TOP_K, n), src_tok, gate[:, None], te, n_used.reshape(1)


def _take_rows(a, rows):
    return a.at[rows].get(mode="promise_in_bounds")


def moe_ffn(layer, hb, idx, wsel, w1, w3, w2):
    dest, src_tok, gate, te, n_used = moe_dispatch(idx, wsel)
    n_tiles = src_tok.shape[0] // MOE_TM
    assert n_tiles % MOE_PARTS == 0
    part_tiles = n_tiles // MOE_PARTS
    ys = None
    for p in range(MOE_PARTS):
        rows = slice(p * part_tiles * MOE_TM, (p + 1) * part_tiles * MOE_TM)
        ys = moe_grouped(layer, te, n_used, _take_rows(hb, src_tok[rows]), w1, w3, w2, gate,
                         p * part_tiles, ys)
    return _take_rows(ys, dest[0]), _take_rows(ys, dest[1])


SEQ_ALL = CTX_LEN + SEQ
S5_T = 16
S5_NC = SEQ_ALL // S5_T
S5_NCC = CTX_LEN // S5_T
S5_W = S5_T * S5_CH


def _gelu_tanh(x):
    return x * (0.5 * (1.0 + jnp.tanh(math.sqrt(2.0 / math.pi) * (x + 0.044715 * (x * x * x)))))


def s5_chunk_operators(lam_re, lam_im, log_step, b_re, b_im, c_re, c_im):
    hp = lax.Precision.HIGHEST
    jj = jnp.arange(S5_T)[:, None]
    ii = jnp.arange(S5_T)[None, :]
    ms, es, fs, a16 = [], [], [], []
    for d in range(2):
        dt = jnp.exp(log_step[d])[:, None]
        lr, li = lam_re[d], lam_im[d]
        er = jnp.exp(lr * dt)
        abr, abi = er * jnp.cos(li * dt), er * jnp.sin(li * dt)
        nr, ni = abr - 1.0, abi
        den = lr * lr + li * li
        fr = (nr * lr + ni * li) / den
        fi = (ni * lr - nr * li) / den
        bbr = fr[..., None] * b_re[d] - fi[..., None] * b_im[d]
        bbi = fr[..., None] * b_im[d] + fi[..., None] * b_re[d]
        k = jnp.arange(S5_T + 1, dtype=F32)[:, None, None]
        mag, ang = jnp.exp(k * (lr * dt)), k * (li * dt)
        pr, pi = mag * jnp.cos(ang), mag * jnp.sin(ang)
        pbr = pr[..., None] * bbr - pi[..., None] * bbi
        pbi = pr[..., None] * bbi + pi[..., None] * bbr
        cr, ci = c_re[d], c_im[d]
        kk = (jnp.einsum('gop,lgpc->lgoc', cr, pbr[:S5_T], precision=hp)
              - jnp.einsum('gop,lgpc->lgoc', ci, pbi[:S5_T], precision=hp))
        cpr = cr[None] * pr[:, :, None, :] - ci[None] * pi[:, :, None, :]
        cpi = -(cr[None] * pi[:, :, None, :] + ci[None] * pr[:, :, None, :])
        if d == 0:
            lag, mask = ii - jj, ii >= jj
            e_pow = S5_T - 1 - jnp.arange(S5_T)
            f_pow = jnp.arange(S5_T) + 1
        else:
            lag, mask = jj - ii, jj >= ii
            e_pow = jnp.arange(S5_T)
            f_pow = S5_T - jnp.arange(S5_T)
        mm = jnp.where(mask[:, :, None, None, None], kk[jnp.clip(lag, 0, S5_T - 1)], 0.0)
        ms.append(mm.transpose(2, 0, 4, 1, 3).reshape(S5_G, S5_W, S5_W))
        for pb in (pbr, pbi):
            es.append(pb[e_pow].transpose(1, 0, 3, 2).reshape(S5_G, S5_W, S5_P))
        for cp in (cpr, cpi):
            fs.append(cp[f_pow].transpose(1, 3, 0, 2).reshape(S5_G, S5_P, S5_W))
        a16 += [pr[S5_T][:, None, :], pi[S5_T][:, None, :]]
    return (ms[0] + ms[1]).astype(BF16), jnp.stack(es).astype(BF16), jnp.stack(fs).astype(BF16), jnp.stack(a16)


def _s5_kernel(u_ref, m_ref, e_ref, f_ref, a_ref, o_ref, e_scr, s_scr):
    u = u_ref[...]
    for k in range(4):
        e_scr[k] = jnp.dot(u, e_ref[k], preferred_element_type=F32)
    arf, aif, arb, aib = (a_ref[k] for k in range(4))

    def segment(start, n, carry):
        def body(k, carry):
            out = []
            for b in range(BATCH):
                sfr, sfi, sbr, sbi = carry[4 * b:4 * b + 4]
                rf = b * S5_NC + start + k
                rb = b * S5_NC + start + n - 1 - k
                s_scr[0, pl.ds(rf, 1), :] = sfr
                s_scr[1, pl.ds(rf, 1), :] = sfi
                s_scr[2, pl.ds(rb, 1), :] = sbr
                s_scr[3, pl.ds(rb, 1), :] = sbi
                nfr = arf * sfr - aif * sfi + e_scr[0, pl.ds(rf, 1), :]
                nfi = arf * sfi + aif * sfr + e_scr[1, pl.ds(rf, 1), :]
                nbr = arb * sbr - aib * sbi + e_scr[2, pl.ds(rb, 1), :]
                nbi = arb * sbi + aib * sbr + e_scr[3, pl.ds(rb, 1), :]
                out += [nfr, nfi, nbr, nbi]
            return tuple(out)
        return lax.fori_loop(0, n, body, carry, unroll=4)

    z = jnp.zeros((1, S5_P), F32)
    carry = segment(0, S5_NCC, (z,) * (4 * BATCH))
    segment(S5_NCC, S5_NC - S5_NCC, carry)
    y = jnp.dot(u, m_ref[...], preferred_element_type=F32)
    for k in range(4):
        y += jnp.dot(s_scr[k].astype(BF16), f_ref[k], preferred_element_type=F32)
    o_ref[...] = y


def s5_scan(u, m, e, f, a16):
    g, r, w = u.shape
    return pl.pallas_call(
        _s5_kernel, grid=(g,),
        in_specs=[pl.BlockSpec((None, r, w), lambda i: (i, 0, 0)),
                  pl.BlockSpec((None, w, w), lambda i: (i, 0, 0)),
                  pl.BlockSpec((4, None, w, S5_P), lambda i: (0, i, 0, 0)),
                  pl.BlockSpec((4, None, S5_P, w), lambda i: (0, i, 0, 0)),
                  pl.BlockSpec((4, None, 1, S5_P), lambda i: (0, i, 0, 0))],
        out_specs=pl.BlockSpec((None, r, w), lambda i: (i, 0, 0)),
        out_shape=jax.ShapeDtypeStruct((g, r, w), F32),
        scratch_shapes=[pltpu.VMEM((4, r, S5_P), F32), pltpu.VMEM((4, r, S5_P), F32)],
        compiler_params=_cparams(("parallel",)), name="s5_scan",
    )(u, m, e, f, a16)


def _s5_out_kernel(y_ref, u_ref, d_ref, w_ref, b_ref, o_ref):
    g = _gelu_tanh(y_ref[...] + d_ref[...] * u_ref[...])
    z = jnp.dot(g.astype(BF16), w_ref[...], preferred_element_type=F32) + b_ref[...]
    o_ref[...] = (g * jax.nn.sigmoid(z)).astype(o_ref.dtype)


def s5_out(y, u, d_skip, w_glu, b_glu, tm):
    m, c = y.shape
    row = pl.BlockSpec((tm, c), lambda i: (i, 0))
    vec = pl.BlockSpec((1, c), lambda i: (0, 0))
    return pl.pallas_call(
        _s5_out_kernel, grid=(m // tm,),
        in_specs=[row, row, vec, pl.BlockSpec((c, c), lambda i: (0, 0)), vec],
        out_specs=row, out_shape=jax.ShapeDtypeStruct((m, c), F32),
        compiler_params=_cparams(("parallel",)), name="s5_out",
    )(y, u, d_skip, w_glu, b_glu)


S5_LG = 128 // S5_CH


def _lane_window(lane, k):
    return (lane >= k * S5_CH) & (lane < (k + 1) * S5_CH)


def _s5_pack_kernel(x_ref, o_ref):
    lane = lax.broadcasted_iota(jnp.int32, (x_ref.shape[0], 128), 1)
    tok = [x_ref[:, t, :] for t in range(S5_T)]
    for g in range(S5_LG):
        for h in range(S5_W // 128):
            acc = jnp.zeros(lane.shape, F32)
            for k in range(S5_LG):
                shift = ((k - g) * S5_CH) % 128
                src = tok[h * S5_LG + k]
                acc = jnp.where(_lane_window(lane, k), pltpu.roll(src, shift, axis=1) if shift else src, acc)
            o_ref[g, :, h * 128:(h + 1) * 128] = acc.astype(o_ref.dtype)


def _s5_unpack_kernel(y_ref, o_ref):
    lane = lax.broadcasted_iota(jnp.int32, (y_ref.shape[1], 128), 1)
    for t in range(S5_T):
        h, k = divmod(t, S5_LG)
        acc = jnp.zeros(lane.shape, F32)
        for g in range(S5_LG):
            shift = ((g - k) * S5_CH) % 128
            src = y_ref[g, :, h * 128:(h + 1) * 128]
            acc = jnp.where(_lane_window(lane, g), pltpu.roll(src, shift, axis=1) if shift else src, acc)
        o_ref[:, t, :] = acc


def s5_pack(u):
    n = u.shape[0]
    cb = n // 2
    return pl.pallas_call(
        _s5_pack_kernel, grid=(n // cb, MIX_W // 128),
        in_specs=[pl.BlockSpec((cb, S5_T, 128), lambda i, j: (i, 0, j))],
        out_specs=pl.BlockSpec((S5_LG, cb, S5_W), lambda i, j: (j, i, 0)),
        out_shape=jax.ShapeDtypeStruct((S5_G, n, S5_W), BF16),
        compiler_params=_cparams(("parallel", "parallel")), name="s5_pack",
    )(u)


def s5_unpack(y):
    n = y.shape[1]
    cb = n // 2
    return pl.pallas_call(
        _s5_unpack_kernel, grid=(n // cb, MIX_W // 128),
        in_specs=[pl.BlockSpec((S5_LG, cb, S5_W), lambda i, j: (j, i, 0))],
        out_specs=pl.BlockSpec((cb, S5_T, 128), lambda i, j: (i, 0, j)),
        out_shape=jax.ShapeDtypeStruct((n, S5_T, MIX_W), F32),
        compiler_params=_cparams(("parallel", "parallel")), name="s5_unpack",
    )(y)


def s5_mixer(u_c, u_l, operators, d_skip, w_glu, b_glu):
    bs = u_c.shape[0]
    u_all = jnp.concatenate([u_c, u_l], axis=1)
    uc = s5_pack(u_all.reshape(bs * S5_NC, S5_T, MIX_W))
    y = s5_scan(uc, *operators)
    y = s5_unpack(y).reshape(bs * SEQ_ALL, MIX_W)
    out = s5_out(y, u_all.reshape(bs * SEQ_ALL, MIX_W), d_skip[None], w_glu.astype(BF16), b_glu[None], 544)
    out = out.reshape(bs, SEQ_ALL, MIX_W)
    return out[:, :CTX_LEN], out[:, CTX_LEN:]


LRU_CB = 256
LRU_TB = 256
LRU_HALO = 8
LRU_ROWS = SEQ_ALL + 3 * LRU_HALO


def _lru_kernel(xc_ref, xl_ref, yc_ref, yl_ref, cw_ref, cb_ref, wa_ref, ba_ref, wi_ref, bi_ref, sp_ref,
                oc_ref, ol_ref, xpad, a_f, b_f, a_b, b_b):
    halo = jnp.zeros((LRU_HALO, LRU_CB), F32)
    lat0 = CTX_LEN + 2 * LRU_HALO
    xpad[0:LRU_HALO] = halo
    xpad[LRU_HALO:LRU_HALO + CTX_LEN] = xc_ref[...]
    xpad[LRU_HALO + CTX_LEN:lat0] = halo
    xpad[lat0:lat0 + SEQ] = xl_ref[...]
    xpad[lat0 + SEQ:LRU_ROWS] = halo
    cw = [cw_ref[j:j + 1, :] for j in range(CONV_W)]
    cb = cb_ref[...]

    def gates(p0, r0):
        ext = xpad[pl.ds(p0 - LRU_HALO, LRU_TB + 2 * LRU_HALO), :]
        xb = cb
        for j in range(CONV_W):
            s0 = LRU_HALO - CONV_PAD_L + j
            xb = xb + cw[j] * ext[s0:s0 + LRU_TB]
        for k in range(LRU_CB // LRU_BD):
            lanes = slice(k * LRU_BD, (k + 1) * LRU_BD)
            xk = xb[:, lanes]
            xkb = xk.astype(BF16)
            for d, (a_s, b_s) in enumerate(((a_f, b_f), (a_b, b_b))):
                r = jax.nn.sigmoid(jnp.dot(xkb, wa_ref[d, k], preferred_element_type=F32) + ba_ref[d][:, lanes])
                i = jax.nn.sigmoid(jnp.dot(xkb, wi_ref[d, k], preferred_element_type=F32) + bi_ref[d][:, lanes])
                log_a = -LRU_C * r * sp_ref[d][:, lanes]
                t = jnp.tanh(log_a)
                a_s[pl.ds(r0, LRU_TB), lanes] = jnp.exp(log_a)
                b_s[pl.ds(r0, LRU_TB), lanes] = jnp.sqrt(-2.0 * t / (1.0 - t)) * (i * xk)

    gates(LRU_HALO, 0)

    def lat_gates(k, _):
        gates(pl.multiple_of(lat0 + k * LRU_TB, 8), pl.multiple_of(CTX_LEN + k * LRU_TB, 8))
        return 0
    lax.fori_loop(0, SEQ // LRU_TB, lat_gates, 0)

    def segment(start, n, carry):
        def body(t, carry):
            hf, hb = carry
            rf = start + t
            rb = start + n - 1 - t
            hf = a_f[pl.ds(rf, 1), :] * hf + b_f[pl.ds(rf, 1), :]
            b_f[pl.ds(rf, 1), :] = hf
            hb = a_b[pl.ds(rb, 1), :] * hb + b_b[pl.ds(rb, 1), :]
            b_b[pl.ds(rb, 1), :] = hb
            return hf, hb
        return lax.fori_loop(0, n, body, carry, unroll=8)

    z = jnp.zeros((1, LRU_CB), F32)
    carry = segment(0, CTX_LEN, (z, z))
    segment(CTX_LEN, SEQ, carry)

    oc_ref[...] = (_gelu_tanh(yc_ref[...]) * (b_f[0:CTX_LEN] + b_b[0:CTX_LEN])).astype(oc_ref.dtype)

    def lat_out(k, _):
        r0 = pl.multiple_of(k * LRU_TB, 8)
        rows = pl.ds(pl.multiple_of(CTX_LEN + k * LRU_TB, 8), LRU_TB)
        h = b_f[rows, :] + b_b[rows, :]
        ol_ref[pl.ds(r0, LRU_TB), :] = (_gelu_tanh(yl_ref[pl.ds(r0, LRU_TB), :]) * h).astype(ol_ref.dtype)
        return 0
    lax.fori_loop(0, SEQ // LRU_TB, lat_out, 0)


def lru_mixer(y_c, x_c, yx_l, conv_w, conv_b, w_a, b_a, w_i, b_i, lam):
    bs = x_c.shape[0]
    nb = LRU_CB // LRU_BD
    sp = jax.nn.softplus(-lam)[:, None, :]
    ctx_spec = pl.BlockSpec((None, CTX_LEN, LRU_CB), lambda b, c: (b, 0, c))
    lat_spec = pl.BlockSpec((None, SEQ, LRU_CB), lambda b, c: (b, 0, c))
    lat_x_spec = pl.BlockSpec((None, SEQ, LRU_CB), lambda b, c: (b, 0, MIX_W // LRU_CB + c))
    vec2 = pl.BlockSpec((2, 1, LRU_CB), lambda b, c: (0, 0, c))
    wspec = pl.BlockSpec((2, nb, LRU_BD, LRU_BD), lambda b, c: (0, c, 0, 0))
    scan_buf = pltpu.VMEM((SEQ_ALL, LRU_CB), F32)
    return pl.pallas_call(
        _lru_kernel, grid=(bs, MIX_W // LRU_CB),
        in_specs=[ctx_spec, lat_x_spec, ctx_spec, lat_spec,
                  pl.BlockSpec((CONV_W, LRU_CB), lambda b, c: (0, c)),
                  pl.BlockSpec((1, LRU_CB), lambda b, c: (0, c)),
                  wspec, vec2, wspec, vec2, vec2],
        out_specs=[ctx_spec, lat_spec],
        out_shape=[jax.ShapeDtypeStruct((bs, CTX_LEN, MIX_W), F32), jax.ShapeDtypeStruct((bs, SEQ, MIX_W), F32)],
        scratch_shapes=[pltpu.VMEM((LRU_ROWS, LRU_CB), F32), scan_buf, scan_buf, scan_buf, scan_buf],
        compiler_params=_cparams(("parallel", "parallel")), name="lru",
    )(x_c, yx_l, y_c, yx_l, conv_w, conv_b[None], w_a.astype(BF16), b_a[:, None, :], w_i.astype(BF16),
      b_i[:, None, :], sp)


GT_W = 8
GT_C = 1024


def _grid_transpose_kernel(x_ref, o_ref):
    for j in range(GT_W):
        o_ref[j] = x_ref[:, j, :].astype(o_ref.dtype)


def grid_transpose(src, batch, col0, width, out_dtype):
    assert col0 % GT_C == 0 and width % GT_C == 0
    return pl.pallas_call(
        _grid_transpose_kernel, grid=(batch, GRID_W // GT_W, width // GT_C),
        in_specs=[pl.BlockSpec((GRID_W, GT_W, GT_C), lambda b, j, c: (b, j, col0 // GT_C + c))],
        out_specs=pl.BlockSpec((None, GT_W, GRID_W, GT_C), lambda b, j, c: (b, j, 0, c)),
        out_shape=jax.ShapeDtypeStruct((batch, GRID_W, GRID_W, width), out_dtype),
        compiler_params=_cparams(("parallel", "parallel", "parallel")), name="grid_transpose",
    )(src)


N_CHUNKS = SEQ_ALL // CHUNK
N_CCH = CTX_LEN // CHUNK
GATE_COLBLK = 80
MIX_TB = 256


def _dot(a, b):
    return jnp.dot(a, b, preferred_element_type=F32)


def _dot_nt(a, b):
    return lax.dot_general(a, b, (((1,), (1,)), ((), ())), preferred_element_type=F32)


def _dot_tn(a, b):
    return lax.dot_general(a, b, (((0,), (0,)), ((), ())), preferred_element_type=F32)


def _split2(a):
    hi = a.astype(BF16)
    return hi, (a - hi.astype(F32)).astype(BF16)


def _dot3(a, b):
    ah, al = _split2(a)
    bh, bl = _split2(b)
    return _dot(ah, bh) + (_dot(ah, bl) + _dot(al, bh))


def _dot_mask(m, x):
    hi = x.astype(BF16)
    r1 = x - hi.astype(F32)
    mid = r1.astype(BF16)
    lo = (r1 - mid.astype(F32)).astype(BF16)
    return _dot(m, hi) + (_dot(m, mid) + _dot(m, lo))


def _chunk_masks(backward):
    i = lax.broadcasted_iota(jnp.int32, (CHUNK, CHUNK), 0)
    j = lax.broadcasted_iota(jnp.int32, (CHUNK, CHUNK), 1)
    if backward:
        i, j = j, i
    tri, strict = j <= i, j < i
    return tri, strict, tri.astype(BF16), jnp.where(strict, 1.0, 0.0).astype(F32)


def _lane_pick(blk, lane):
    lanes = lax.broadcasted_iota(jnp.int32, blk.shape, 1)
    return jnp.sum(jnp.where(lanes == lane, blk, 0.0), axis=-1, keepdims=True)


def _softplus(x):
    return jnp.maximum(x, 0.0) + jnp.log1p(jnp.exp(-jnp.abs(x)))


def _silu(x):
    return x * jax.nn.sigmoid(x)


def _bidir_chunks(step, carry, n_ctx, n_all):
    def segment(start, n, carry):
        def body(t, carry):
            return step(start + t, start + n - 1 - t, carry)
        return lax.fori_loop(0, n, body, carry)
    carry = segment(0, n_ctx, carry)
    return segment(n_ctx, n_all - n_ctx, carry)


DN_HALO = 8
DN_ROWS = SEQ_ALL + 3 * DN_HALO
DN_C = 64
DN_GROUP = 256
DN_GC = DN_GROUP // (2 * DN_C)
DN_NCH = SEQ_ALL // DN_C
DN_NCC = CTX_LEN // DN_C
DN_LOCKSTEP = 4


def _dn_kernel(qc_ref, ql_ref, kc_ref, kl_ref, vc_ref, vl_ref, zc_ref, zl_ref, gc_ref, gl_ref, hp_ref,
               cwq_ref, cwk_ref, cwv_ref, nw_ref, ol_ref, oc_ref,
               xpad, qn, kn, vn, u_s, w_s, qg_s, kd_s, at_s, dec_s):
    h = pl.program_id(1)
    halo = jnp.zeros((DN_HALO, DN_HD), F32)
    lat0 = CTX_LEN + 2 * DN_HALO

    def conv_into(xc_ref, xl_ref, cw_ref, dst, post):
        xpad[0:DN_HALO] = halo
        xpad[DN_HALO:DN_HALO + CTX_LEN] = xc_ref[...]
        xpad[DN_HALO + CTX_LEN:lat0] = halo
        xpad[lat0:lat0 + SEQ] = xl_ref[...]
        xpad[lat0 + SEQ:DN_ROWS] = halo
        cw = [cw_ref[j:j + 1, :] for j in range(CONV_W)]

        def block(p0, r0):
            ext = xpad[pl.ds(p0 - DN_HALO, MIX_TB + 2 * DN_HALO), :]
            s0 = DN_HALO - CONV_PAD_L
            acc = cw[0] * ext[s0:s0 + MIX_TB]
            for j in range(1, CONV_W):
                acc = acc + cw[j] * ext[s0 + j:s0 + j + MIX_TB]
            dst[pl.ds(r0, MIX_TB), :] = post(_silu(acc))

        block(DN_HALO, 0)

        def lat_block(k, _):
            block(pl.multiple_of(lat0 + k * MIX_TB, 8), pl.multiple_of(CTX_LEN + k * MIX_TB, 8))
            return 0
        lax.fori_loop(0, SEQ // MIX_TB, lat_block, 0)

    def l2n(scale):
        return lambda t: t * (lax.rsqrt(jnp.sum(t * t, axis=-1, keepdims=True) + EPS) * scale)

    conv_into(qc_ref, ql_ref, cwq_ref, qn, l2n(DN_HD ** -0.5))
    conv_into(kc_ref, kl_ref, cwk_ref, kn, l2n(1.0))
    conv_into(vc_ref, vl_ref, cwv_ref, vn, lambda t: t)

    gi = lax.broadcasted_iota(jnp.int32, (DN_GROUP, DN_GROUP), 0)
    gj = lax.broadcasted_iota(jnp.int32, (DN_GROUP, DN_GROUP), 1)
    same = (gi // DN_C) == (gj // DN_C)
    blk16 = (gi // 16) == (gj // 16)
    bwd = ((gi // DN_C) % 2) == 1
    fwd = jnp.logical_not(bwd)
    li, lj = gi % DN_C, gj % DN_C
    tri = same & ((fwd & (lj <= li)) | (bwd & (lj >= li)))
    strict = same & ((fwd & (lj < li)) | (bwd & (lj > li)))
    eye = jnp.where(gi == gj, 1.0, 0.0).astype(F32)
    m16 = jnp.where(tri, 1.0, 0.0).astype(BF16)

    def both_dirs(xf, xb):
        parts = []
        for c in range(DN_GC):
            parts += [xf[c * DN_C:(c + 1) * DN_C], xb[c * DN_C:(c + 1) * DN_C]]
        return jnp.concatenate(parts, axis=0)

    def prep(ts, g_ref, row0):
        grows = DN_GC * DN_C
        n = range(len(ts))
        r0 = [pl.multiple_of(t * grows, grows) for t in ts]
        gblk = [g_ref[pl.ds(pl.multiple_of(t * grows - row0, grows), grows), :] for t in ts]
        q = [both_dirs(*[qn[pl.ds(r0[i], grows), :]] * 2) for i in n]
        k = [both_dirs(*[kn[pl.ds(r0[i], grows), :]] * 2) for i in n]
        v = [both_dirs(*[vn[pl.ds(r0[i], grows), :]] * 2) for i in n]
        g_col = [both_dirs(*[hp_ref[:, d:d + 1] * _softplus(_lane_pick(gblk[i], d * DN_H + h)
                                                            + hp_ref[:, 2 + d:3 + d]) for d in range(2)]) for i in n]
        beta = [both_dirs(*[jax.nn.sigmoid(_lane_pick(gblk[i], 2 * DN_H + d * DN_H + h)) for d in range(2)])
                for i in n]
        k16 = [k[i].astype(BF16) for i in n]
        kb = [k[i] * beta[i] for i in n]
        gb = [jnp.broadcast_to(g_col[i], (DN_GROUP, DN_HD)) for i in n]
        kk = [_dot_nt(kb[i].astype(BF16), k16[i]) for i in n]
        gcum = [_dot_mask(m16, gb[i]) for i in n]
        tot = [jnp.broadcast_to(jnp.sum(gb[i].reshape(DN_GROUP // DN_C, DN_C, DN_HD), axis=1, keepdims=True),
                                (DN_GROUP // DN_C, DN_C, DN_HD)).reshape(DN_GROUP, DN_HD) for i in n]
        qk = [_dot_nt(q[i].astype(BF16), k16[i]) for i in n]
        dlt = [gcum[i][:, 0:1] - gcum[i].T[0:1, :] for i in n]
        decay = [jnp.where(tri, jnp.exp(jnp.where(tri, dlt[i], 0.0)), 0.0) for i in n]
        lower = [jnp.where(strict, kk[i] * decay[i], 0.0) for i in n]
        x = [-jnp.where(blk16, lower[i], 0.0) for i in n]
        tinv = [eye + x[i] for i in n]
        for _ in range(3):
            x = [_dot3(x[i], x[i]) for i in n]
            tinv = [tinv[i] + _dot3(tinv[i], x[i]) for i in n]
        size = 16
        while size < DN_C:
            inner, outer = (gi // size) == (gj // size), (gi // (2 * size)) == (gj // (2 * size))
            off = [_dot3(jnp.where(outer & jnp.logical_not(inner), lower[i], 0.0), tinv[i]) for i in n]
            tinv = [tinv[i] - _dot3(tinv[i], off[i]) for i in n]
            size *= 2
        eg = [jnp.exp(gcum[i]) for i in n]
        sol = [_dot3(tinv[i], jnp.concatenate([v[i] * beta[i], kb[i] * eg[i]], axis=1)) for i in n]
        for i in n:
            attn = (qk[i] * decay[i]).astype(BF16)
            w16 = sol[i][:, DN_HD:].astype(BF16)
            qg16 = (q[i] * eg[i]).astype(BF16)
            kd16 = (k[i] * jnp.exp(tot[i] - gcum[i])).astype(BF16)
            dec = jnp.exp(tot[i])
            for p in range(DN_GROUP // DN_C):
                d, blk = p % 2, slice(p * DN_C, (p + 1) * DN_C)
                dst = pl.ds(r0[i] + (p // 2) * DN_C, DN_C)
                u_s[d, dst, :] = sol[i][blk, :DN_HD]
                w_s[d, dst, :] = w16[blk]
                at_s[d, dst, :] = attn[blk, blk]
                qg_s[d, dst, :] = qg16[blk]
                kd_s[d, dst, :] = kd16[blk]
                dec_s[d, pl.ds(pl.multiple_of((DN_GC * ts[i] + p // 2) * 8, 8), 8), :] = dec[p * DN_C:p * DN_C + 8]
        return 0

    n_ctx_groups, n_groups = DN_NCC // DN_GC, DN_NCH // DN_GC
    assert (n_groups - n_ctx_groups) % DN_LOCKSTEP == 0
    prep(list(range(n_ctx_groups)), gc_ref, 0)
    lax.fori_loop(0, (n_groups - n_ctx_groups) // DN_LOCKSTEP,
                  lambda t, _: prep([n_ctx_groups + DN_LOCKSTEP * t + i for i in range(DN_LOCKSTEP)],
                                    gl_ref, CTX_LEN), 0)
    o_dst = (qn, kn)
    o_f, o_b = o_dst

    def step(cf, cb, carry):
        dirs = range(2)
        rows = [pl.ds(pl.multiple_of(c * DN_C, DN_C), DN_C) for c in (cf, cb)]
        s16 = [carry[d].astype(BF16) for d in dirs]
        ws = [_dot(w_s[d, rows[d], :], s16[d]) for d in dirs]
        qs = [_dot(qg_s[d, rows[d], :], s16[d]) for d in dirs]
        v_new = [(u_s[d, rows[d], :] - ws[d]).astype(BF16) for d in dirs]
        kv = [_dot_tn(kd_s[d, rows[d], :], v_new[d]) for d in dirs]
        av = [_dot(at_s[d, rows[d], :], v_new[d]) for d in dirs]
        out = []
        for d, c in zip(dirs, (cf, cb)):
            o_dst[d][rows[d], :] = qs[d] + av[d]
            g_last = dec_s[d, pl.ds(pl.multiple_of(c * 8, 8), 8), :][0:1]
            out.append(carry[d] * g_last + kv[d])
        return tuple(out)

    z0 = jnp.zeros((DN_HD, DN_HD), F32)
    _bidir_chunks(step, (z0, z0), DN_NCC, DN_NCH)

    nw = nw_ref[...]

    def finish(rows, z):
        o = o_f[rows, :] + o_b[rows, :]
        o = o * lax.rsqrt(jnp.mean(o * o, axis=-1, keepdims=True) + EPS) * nw
        return (o * _silu(z)).astype(BF16)

    oc_ref[...] = finish(pl.ds(0, CTX_LEN), zc_ref[...])

    def lat_finish(k, _):
        r0 = pl.multiple_of(k * MIX_TB, 8)
        ol_ref[pl.ds(r0, MIX_TB), :] = finish(pl.ds(pl.multiple_of(CTX_LEN + k * MIX_TB, 8), MIX_TB),
                                              zl_ref[pl.ds(r0, MIX_TB), :])
        return 0
    lax.fori_loop(0, SEQ // MIX_TB, lat_finish, 0)


def deltanet_mixer(proj, conv_w, a_log, dt_bias, norm_w):
    hp = jnp.stack([-jnp.exp(a_log[0]), -jnp.exp(a_log[1]), dt_bias[0], dt_bias[1]], axis=-1)[:, None, :]
    ctx_blk0 = N_LAT // CTX_LEN

    def seg_specs(col0):
        cb = col0 // DN_HD
        return [pl.BlockSpec((CTX_LEN, DN_HD), lambda b, h: (ctx_blk0 + b, cb + h)),
                pl.BlockSpec((SEQ, DN_HD), lambda b, h: (b, cb + h))]

    def conv_spec(k):
        return pl.BlockSpec((CONV_W, DN_HD), lambda b, h: (0, k * DN_H + h))

    in_specs = (seg_specs(COL_DN_QKVZ) + seg_specs(COL_DN_QKVZ + MIX_W) + seg_specs(COL_DN_QKVZ + 2 * MIX_W)
                + seg_specs(COL_DN_QKVZ + 3 * MIX_W)
                + [pl.BlockSpec((CTX_LEN, 128), lambda b, h: (ctx_blk0 + b, GATE_COLBLK)),
                   pl.BlockSpec((SEQ, 128), lambda b, h: (b, GATE_COLBLK)),
                   pl.BlockSpec((None, 1, 4), lambda b, h: (h, 0, 0)),
                   conv_spec(0), conv_spec(1), conv_spec(2),
                   pl.BlockSpec((1, DN_HD), lambda b, h: (0, 0))])
    seq_f32 = pltpu.VMEM((SEQ_ALL, DN_HD), F32)
    dir_bf16 = pltpu.VMEM((2, SEQ_ALL, DN_HD), BF16)
    ol, oc = pl.pallas_call(
        _dn_kernel, grid=(BATCH, DN_H), in_specs=in_specs,
        out_specs=[pl.BlockSpec((SEQ, DN_HD), lambda b, h: (b, h)),
                   pl.BlockSpec((CTX_LEN, DN_HD), lambda b, h: (b, h))],
        out_shape=[jax.ShapeDtypeStruct((N_LAT, MIX_W), BF16), jax.ShapeDtypeStruct((N_CTX, MIX_W), BF16)],
        scratch_shapes=[pltpu.VMEM((DN_ROWS, DN_HD), F32), seq_f32, seq_f32, seq_f32,
                        pltpu.VMEM((2, SEQ_ALL, DN_HD), F32), dir_bf16, dir_bf16, dir_bf16,
                        pltpu.VMEM((2, SEQ_ALL, DN_C), BF16), pltpu.VMEM((2, DN_NCH * 8, DN_HD), F32)],
        compiler_params=_cparams(("parallel", "arbitrary")), name="deltanet",
    )(*([proj] * 10), hp, conv_w, conv_w, conv_w, norm_w[None])
    return ol, oc


def _ml_kernel(qc_ref, ql_ref, kc_ref, kl_ref, vc_ref, vl_ref, oc_ref, ol_ref, gc_ref, gl_ref, hp_ref, nw_ref,
               yl_ref, yc_ref, h_f, h_b):
    hh = pl.program_id(1)
    masks = [_chunk_masks(False), _chunk_masks(True)]
    ones = jnp.ones((CHUNK, CHUNK), BF16)
    eye = jnp.where(lax.broadcasted_iota(jnp.int32, (CHUNK, CHUNK), 0)
                    == lax.broadcasted_iota(jnp.int32, (CHUNK, CHUNK), 1), 1.0, 0.0).astype(F32)

    h_dst = (h_f, h_b)
    dirs = range(2)

    def gate_terms(cs, refs, row0):
        q_ref, k_ref, v_ref, g_ref = refs
        rows = [pl.ds(pl.multiple_of(c * CHUNK - row0, CHUNK), CHUNK) for c in cs]
        gblk = [g_ref[rows[d], :] for d in dirs]
        ig = [_lane_pick(gblk[d], 4 * DN_H + d * ML_H + hh) + hp_ref[:, d:d + 1] for d in dirs]
        lf = [-_softplus(-(_lane_pick(gblk[d], 4 * DN_H + 2 * ML_H + d * ML_H + hh) + hp_ref[:, 2 + d:3 + d]))
              for d in dirs]
        q16 = [(q_ref[rows[d], :] * (ML_DQK ** -0.5)).astype(BF16) for d in dirs]
        bcum = [_dot_mask(masks[d][2], jnp.broadcast_to(lf[d], (CHUNK, ML_DQK)))[:, 0:1] for d in dirs]
        dsum = [_dot_mask(masks[d][2], lf[d] * masks[d][3]) + _dot_mask(ones, ig[d] * eye) for d in dirs]
        sqk = [_dot_nt(q16[d], k_ref[rows[d], :].astype(BF16)) for d in dirs]
        dmat = [jnp.where(masks[d][0], dsum[d], -jnp.inf) for d in dirs]
        dmax = [jnp.max(dmat[d], axis=-1, keepdims=True) for d in dirs]
        tot = [jnp.sum(lf[d], axis=0, keepdims=True) for d in dirs]
        logw = [tot[d] - bcum[d] + ig[d] for d in dirs]
        logw_max = [jnp.max(logw[d], axis=0, keepdims=True) for d in dirs]
        return tuple((bcum[d], dmat[d], dmax[d], tot[d], logw[d], logw_max[d], sqk[d]) for d in dirs)

    def advance(cs, nxt, states, pre, refs, row0):
        q_ref, k_ref, v_ref, _ = refs
        rows = [pl.ds(pl.multiple_of(c * CHUNK - row0, CHUNK), CHUNK) for c in cs]
        q = [q_ref[rows[d], :] * (ML_DQK ** -0.5) for d in dirs]
        k = [k_ref[rows[d], :] for d in dirs]
        q16 = [q[d].astype(BF16) for d in dirs]
        v16 = [v_ref[rows[d], :].astype(BF16) for d in dirs]
        qc = [_dot(q16[d], states[d][0].astype(BF16)) for d in dirs]
        pre_next = gate_terms(nxt, refs, row0)
        out = []
        inter, m_t, w_inter, wmat, m_new, keep, kw = [], [], [], [], [], [], []
        for d in dirs:
            bcum, dmat, dmax, tot, logw, logw_max, sqk = pre[d]
            m = states[d][2]
            inter.append(bcum + m)
            m_t.append(jnp.maximum(inter[d], dmax))
            w_inter.append(jnp.exp(inter[d] - m_t[d]))
            wmat.append(jnp.exp(dmat - m_t[d]) * sqk)
            m_new.append(jnp.maximum(tot + m, logw_max))
            keep.append(jnp.exp(tot + m - m_new[d]))
            kw.append(k[d] * jnp.exp(logw - m_new[d]))
        wv = [_dot(wmat[d].astype(BF16), v16[d]) for d in dirs]
        kv = [_dot_tn(kw[d].astype(BF16), v16[d]) for d in dirs]
        for d in dirs:
            c_s, n_s, _ = states[d]
            num = w_inter[d] * qc[d] + wv[d]
            den = w_inter[d] * jnp.sum(q[d] * n_s, axis=-1, keepdims=True) + jnp.sum(wmat[d], axis=-1, keepdims=True)
            h_dst[d][pl.ds(pl.multiple_of(cs[d] * CHUNK, CHUNK), CHUNK), :] = (
                num / jnp.maximum(jnp.abs(den), jnp.exp(-m_t[d])))
            out.append((keep[d] * c_s + kv[d], keep[d] * n_s + jnp.sum(kw[d], axis=0, keepdims=True), m_new[d]))
        return tuple(out), pre_next

    ctx_refs = (qc_ref, kc_ref, vc_ref, gc_ref)
    lat_refs = (ql_ref, kl_ref, vl_ref, gl_ref)

    def segment(refs, start, n, states):
        chunks = lambda t: (start + t, start + n - 1 - t)

        def body(t, carry):
            states, pre = carry
            return advance(chunks(t), chunks(jnp.minimum(t + 1, n - 1)), states, pre, refs, start * CHUNK)
        return lax.fori_loop(0, n, body, (states, gate_terms(chunks(0), refs, start * CHUNK)))[0]

    s0 = (jnp.zeros((ML_DQK, ML_DV), F32), jnp.zeros((1, ML_DQK), F32), jnp.zeros((1, 1), F32))
    states = segment(ctx_refs, 0, N_CCH, (s0, s0))
    segment(lat_refs, N_CCH, N_CHUNKS - N_CCH, states)

    nw = nw_ref[...]

    def finish(rows, o):
        hsum = h_f[rows, :] + h_b[rows, :]
        hsum = hsum * lax.rsqrt(jnp.mean(hsum * hsum, axis=-1, keepdims=True) + EPS) * nw
        return (hsum * jax.nn.sigmoid(o)).astype(BF16)

    yc_ref[...] = finish(pl.ds(0, CTX_LEN), oc_ref[...])

    def lat_finish(k, _):
        r0 = pl.multiple_of(k * MIX_TB, 8)
        yl_ref[pl.ds(r0, MIX_TB), :] = finish(pl.ds(pl.multiple_of(CTX_LEN + k * MIX_TB, 8), MIX_TB),
                                              ol_ref[pl.ds(r0, MIX_TB), :])
        return 0
    lax.fori_loop(0, SEQ // MIX_TB, lat_finish, 0)


def mlstm_mixer(proj, i_bias, f_bias, norm_w):
    hp = jnp.stack([i_bias[0], i_bias[1], f_bias[0], f_bias[1]], axis=-1)[:, None, :]
    ctx_blk0 = N_LAT // CTX_LEN

    def seg_specs(col0, w):
        cb = col0 // w
        return [pl.BlockSpec((CTX_LEN, w), lambda b, h: (ctx_blk0 + b, cb + h)),
                pl.BlockSpec((SEQ, w), lambda b, h: (b, cb + h))]

    in_specs = (seg_specs(COL_ML_Q, ML_DQK) + seg_specs(COL_ML_K, ML_DQK) + seg_specs(COL_ML_V, ML_DV)
                + seg_specs(COL_ML_O, ML_DV)
                + [pl.BlockSpec((CTX_LEN, 128), lambda b, h: (ctx_blk0 + b, GATE_COLBLK)),
                   pl.BlockSpec((SEQ, 128), lambda b, h: (b, GATE_COLBLK)),
                   pl.BlockSpec((None, 1, 4), lambda b, h: (h, 0, 0)),
                   pl.BlockSpec((1, ML_DV), lambda b, h: (0, h))])
    seq_f32 = pltpu.VMEM((SEQ_ALL, ML_DV), F32)
    yl, yc = pl.pallas_call(
        _ml_kernel, grid=(BATCH, ML_H), in_specs=in_specs,
        out_specs=[pl.BlockSpec((SEQ, ML_DV), lambda b, h: (b, h)),
                   pl.BlockSpec((CTX_LEN, ML_DV), lambda b, h: (b, h))],
        out_shape=[jax.ShapeDtypeStruct((N_LAT, MIX_W), BF16), jax.ShapeDtypeStruct((N_CTX, MIX_W), BF16)],
        scratch_shapes=[seq_f32, seq_f32],
        compiler_params=_cparams(("parallel", "arbitrary")), name="mlstm",
    )(*([proj] * 10), hp, norm_w[None])
    return yl, yc


PROJ_TM = (N_LAT + N_CTX) // 8
PROJ_TN = 768
PROJ_PAD = -(-PROJ_W // PROJ_TN) * PROJ_TN
COL_S5, COL_DN_QKVZ, COL_ML_Q, COL_ML_K, COL_ML_V, COL_ML_O, COL_LRU_Y, COL_LRU_X = (
    0, 1024, 5120, 5632, 6144, 7168, 8192, 9216)
COL_DN_A, COL_DN_B, COL_ML_I, COL_ML_F = 10240, 10256, 10272, 10280


def permute_proj_weight(w):
    o = [0] + [int(v) for v in np.cumsum(PROJ_SIZES)]
    parts = [w[:, o[0]:o[5]], w[:, o[7]:o[11]], w[:, o[13]:o[15]], w[:, o[5]:o[7]], w[:, o[11]:o[13]]]
    return jnp.pad(jnp.concatenate(parts, axis=1).astype(BF16), ((0, 0), (0, PROJ_PAD - PROJ_W)))


def hybrid_mixer(proj, s5_p, dn_p, ml_p, lru_p):
    def ctx_seg(c0):
        return proj[N_LAT:, c0:c0 + MIX_W].reshape(BATCH, CTX_LEN, MIX_W)

    def to_row_major(t):
        return grid_transpose(t.reshape(BATCH * GRID_W, GRID_W, MIX_W), BATCH, 0, MIX_W, BF16).reshape(N_LAT, MIX_W)

    grid = proj.reshape((N_LAT + N_CTX) // GRID_W, GRID_W, PROJ_PAD)
    u_l = grid_transpose(grid, BATCH, COL_S5, MIX_W, F32).reshape(BATCH, SEQ, MIX_W)
    yx_l = grid_transpose(grid, BATCH, COL_LRU_Y, 2 * MIX_W, F32).reshape(BATCH, SEQ, 2 * MIX_W)
    a_c, a_l = s5_mixer(ctx_seg(COL_S5), u_l, *s5_p)
    d_c, d_l = lru_mixer(ctx_seg(COL_LRU_Y), ctx_seg(COL_LRU_X), yx_l, *lru_p)
    flat = lambda t_l, t_c: (to_row_major(t_l), t_c.reshape(N_CTX, MIX_W).astype(BF16))
    return [flat(a_l, a_c), deltanet_mixer(proj, *dn_p), mlstm_mixer(proj, *ml_p), flat(d_l, d_c)]


def kernel(x, c, ctx, c_ctx, w_ada, b_ada, norm1, norm2, norm_f, w_in, w_out, s5_lam_re, s5_lam_im, s5_log_step,
           s5_b_re, s5_b_im, s5_c_re, s5_c_im, s5_d, s5_w_glu, s5_b_glu, dn_conv, dn_a_log, dn_dt_bias, dn_norm,
           ml_i_bias, ml_f_bias, ml_norm, lru_conv_w, lru_conv_b, lru_w_a, lru_b_a, lru_w_i, lru_b_i, lru_lam,
           router_w, router_bias, moe_w1, moe_w3, moe_w2):
    d = D_MODEL
    xa = jnp.concatenate([x.reshape(N_LAT, d), ctx.reshape(N_CTX, d)], axis=0)
    cond = jnp.concatenate([jax.nn.silu(c), jax.nn.silu(c_ctx)[None], jnp.zeros((16 - BATCH - 1, d), F32)], axis=0)
    cond = cond.astype(BF16)
    router = (router_w.T, router_bias.reshape(N_EXPERTS, 1))

    s5_ops = jax.vmap(s5_chunk_operators)(s5_lam_re, s5_lam_im, s5_log_step, s5_b_re, s5_b_im, s5_c_re, s5_c_im)

    for l in range(DEPTH):
        last = l == DEPTH - 1
        mod = ada_matmul(l, cond, w_ada, b_ada[:, None, :], 1024)[:BATCH + 1].reshape(BATCH + 1, 6, 1, d)
        shift1, scale1, gate1, shift2, scale2, gate2 = (mod[:, k] for k in range(6))

        h1 = norm_mod(xa, norm1[l][None], scale1, shift1, SEQ, 256)
        proj = matmul(h1, permute_proj_weight(w_in[l]), PROJ_TM, PROJ_TN)
        slabs = hybrid_mixer(
            proj,
            (tuple(op[l] for op in s5_ops), s5_d[l], s5_w_glu[l], s5_b_glu[l]),
            (dn_conv[l], dn_a_log[l], dn_dt_bias[l], dn_norm[l]),
            (ml_i_bias[l], ml_f_bias[l], ml_norm[l]),
            (lru_conv_w[l], lru_conv_b[l], lru_w_a[l], lru_b_a[l], lru_w_i[l], lru_b_i[l], lru_lam[l]))
        if last:
            mix = [s[0] for s in slabs]
        else:
            mix = [jnp.concatenate(s, axis=0) for s in slabs]
        out_tm = 1024 if mix[0].shape[0] % 1024 == 0 else 512
        xa = matmul_gated_residual(mix, w_out[l].astype(BF16), xa, gate1, SEQ, out_tm, 1024)

        h2, idx, wsel = norm_mod(xa, norm2[l][None], scale2, shift2, SEQ, 256, router=router)
        y0, y1 = moe_ffn(l, h2, idx, wsel, moe_w1, moe_w3, moe_w2)
        xa = gated_add(xa, y0, y1, gate2, SEQ, 256)

    return final_norm(xa[:N_LAT], norm_f[None], 256).reshape(BATCH, SEQ, d)
```
